```python
import math
import jax, jax.numpy as jnp
from jax import lax
import numpy as np

D_MODEL = 1024
BATCH = 4
SEQ = 4096
DEPTH = 1

D_MIX = D_MODEL
D_RNN = D_MIX // 2
RNN_BLOCKS = 8
RNN_BW = D_RNN // RNN_BLOCKS
CONV_W = 4
RG_C = 8.0
D_ATT = D_MIX - D_RNN
HEAD_DIM = 64
N_HEADS = D_ATT // HEAD_DIM
Q_BLOCK = 128
D_FF = 2816
N_IN = 2 * D_RNN + 3 * D_ATT
EPS = 1e-6

kernel_name = "hybrid_rglru_stickbreaking_macaron"


def _rms_norm(x, g):
    xf = x.astype(jnp.float32)
    r = lax.rsqrt(jnp.mean(xf * xf, axis=-1, keepdims=True) + EPS)
    return (xf * r * g.astype(jnp.float32)).astype(x.dtype)


def _swiglu(x, w_gate, w_up, w_down):
    return (jax.nn.silu(x @ w_gate) * (x @ w_up)) @ w_down


def _lin_combine(left, right):
    a1, b1 = left
    a2, b2 = right
    return a1 * a2, a2 * b1 + b2


def _rg_lru_group(xr, gate, conv_w, conv_b, w_a, b_a, w_x, b_x, lam):
    bsz, seq, _ = xr.shape
    kern = conv_w.astype(xr.dtype)[:, None, :]
    xc = lax.conv_general_dilated(
        xr, kern, window_strides=(1,), padding=[(CONV_W - 1, 0)],
        dimension_numbers=("NWC", "WIO", "NWC"), feature_group_count=D_RNN,
    ) + conv_b
    xb = xc.reshape(bsz, seq, RNN_BLOCKS, RNN_BW)
    r = jax.nn.sigmoid(jnp.einsum("bsnc,ncd->bsnd", xb, w_a).reshape(bsz, seq, D_RNN) + b_a)
    i = jax.nn.sigmoid(jnp.einsum("bsnc,ncd->bsnd", xb, w_x).reshape(bsz, seq, D_RNN) + b_x)
    log_a = RG_C * r.astype(jnp.float32) * jax.nn.log_sigmoid(lam.astype(jnp.float32))
    a = jnp.exp(log_a)
    mult = jnp.sqrt(-jnp.expm1(2.0 * log_a))
    b = mult * (i * xc).astype(jnp.float32)
    _, h = lax.associative_scan(_lin_combine, (a, b), axis=1)
    return h.astype(xr.dtype) * jax.nn.gelu(gate)


def _stick_breaking(q, k, v):
    seq = q.shape[2]
    k_pos = jnp.arange(seq)
    kf = k.astype(jnp.float32)
    vf = v.astype(jnp.float32)

    def block(i):
        start = i * Q_BLOCK
        qb = lax.dynamic_slice_in_dim(q, start, Q_BLOCK, axis=2).astype(jnp.float32)
        z = jnp.einsum("bhqd,bhkd->bhqk", qb, kf)
        q_pos = start + jnp.arange(Q_BLOCK)
        causal = k_pos[None, :] < q_pos[:, None]
        log_beta = jax.nn.log_sigmoid(z)
        log_1m = jnp.where(causal, jax.nn.log_sigmoid(-z), 0.0)
        tail = lax.cumsum(log_1m, axis=log_1m.ndim - 1, reverse=True) - log_1m
        w = jnp.where(causal, jnp.exp(log_beta + tail), 0.0)
        return jnp.einsum("bhqk,bhkd->bhqd", w, vf)

    out = lax.map(block, jnp.arange(seq // Q_BLOCK))
    nb, bsz, nh, qb, dh = out.shape
    out = jnp.transpose(out, (1, 0, 3, 2, 4)).reshape(bsz, seq, nh * dh)
    return out.astype(q.dtype)


def setup_inputs(seed: int = 0) -> dict:
    key = jax.random.key(seed)
    ks = jax.random.split(key, 24)
    f32 = jnp.float32

    def nrm(k, shape, scale):
        return jax.random.normal(k, shape, f32) * scale

    def gain(k, n):
        return 1.0 + 0.02 * jax.random.normal(k, (DEPTH, n), f32)

    a_base = jax.random.uniform(ks[13], (DEPTH, D_RNN), f32, 0.9, 0.999)
    s = a_base ** (1.0 / RG_C)
    rg_lambda = jnp.log(s) - jnp.log1p(-s)
    return {
        "x": nrm(ks[0], (BATCH, SEQ, D_MODEL), 1.0),
        "ffn1_norm": gain(ks[1], D_MODEL),
        "ffn1_w_gate": nrm(ks[2], (DEPTH, D_MODEL, D_FF), D_MODEL ** -0.5),
        "ffn1_w_up": nrm(ks[3], (DEPTH, D_MODEL, D_FF), D_MODEL ** -0.5),
        "ffn1_w_down": nrm(ks[4], (DEPTH, D_FF, D_MODEL), D_FF ** -0.5),
        "mix_norm": gain(ks[5], D_MODEL),
        "w_in": nrm(ks[6], (DEPTH, D_MODEL, N_IN), D_MODEL ** -0.5),
        "conv_w": nrm(ks[7], (DEPTH, CONV_W, D_RNN), CONV_W ** -0.5),
        "conv_b": nrm(ks[8], (DEPTH, D_RNN), 0.01),
        "rg_w_a": nrm(ks[9], (DEPTH, RNN_BLOCKS, RNN_BW, RNN_BW), RNN_BW ** -0.5),
        "rg_b_a": nrm(ks[10], (DEPTH, D_RNN), 0.01),
        "rg_w_x": nrm(ks[11], (DEPTH, RNN_BLOCKS, RNN_BW, RNN_BW), RNN_BW ** -0.5),
        "rg_b_x": nrm(ks[12], (DEPTH, D_RNN), 0.01),
        "rg_lambda": rg_lambda,
        "q_norm": gain(ks[14], HEAD_DIM),
        "k_norm": gain(ks[15], HEAD_DIM),
        "rnn_out_norm": gain(ks[16], D_RNN),
        "attn_out_norm": gain(ks[17], D_ATT),
        "w_out": nrm(ks[18], (DEPTH, D_MIX, D_MODEL), D_MIX ** -0.5),
        "ffn2_norm": gain(ks[19], D_MODEL),
        "ffn2_w_gate": nrm(ks[20], (DEPTH, D_MODEL, D_FF), D_MODEL ** -0.5),
        "ffn2_w_up": nrm(ks[21], (DEPTH, D_MODEL, D_FF), D_MODEL ** -0.5),
        "ffn2_w_down": nrm(ks[22], (DEPTH, D_FF, D_MODEL), D_FF ** -0.5),
    }


def reference(x, ffn1_norm, ffn1_w_gate, ffn1_w_up, ffn1_w_down, mix_norm, w_in,
              conv_w, conv_b, rg_w_a, rg_b_a, rg_w_x, rg_b_x, rg_lambda,
              q_norm, k_norm, rnn_out_norm, attn_out_norm, w_out,
              ffn2_norm, ffn2_w_gate, ffn2_w_up, ffn2_w_down):
    bsz, seq, _ = x.shape
    scale = 1.0 / math.sqrt(HEAD_DIM)
    for l in range(DEPTH):
        x = x + 0.5 * _swiglu(_rms_norm(x, ffn1_norm[l]), ffn1_w_gate[l], ffn1_w_up[l], ffn1_w_down[l])

        h = _rms_norm(x, mix_norm[l])
        proj = h @ w_in[l]
        xr, gate, q, k, v = jnp.split(
            proj, [D_RNN, 2 * D_RNN, 2 * D_RNN + D_ATT, 2 * D_RNN + 2 * D_ATT], axis=-1)

        y_rnn = _rg_lru_group(xr, gate, conv_w[l], conv_b[l], rg_w_a[l], rg_b_a[l],
                              rg_w_x[l], rg_b_x[l], rg_lambda[l])

        def heads(t):
            return jnp.transpose(t.reshape(bsz, seq, N_HEADS, HEAD_DIM), (0, 2, 1, 3))
        qh = _rms_norm(heads(q), q_norm[l]) * scale
        kh = _rms_norm(heads(k), k_norm[l])
        y_att = _stick_breaking(qh, kh, heads(v))

        y = jnp.concatenate([_rms_norm(y_rnn, rnn_out_norm[l]),
                             _rms_norm(y_att, attn_out_norm[l])], axis=-1)
        x = x + y @ w_out[l]

        x = x + 0.5 * _swiglu(_rms_norm(x, ffn2_norm[l]), ffn2_w_gate[l], ffn2_w_up[l], ffn2_w_down[l])
    return x
```

```python
import functools
import math

import jax
import jax.numpy as jnp
from jax import lax
from jax.experimental import pallas as pl
from jax.experimental.pallas import tpu as pltpu

D_MODEL = 1024
D_RNN = 512
RNN_BLOCKS = 8
RNN_BW = D_RNN // RNN_BLOCKS
CONV_W = 4
RG_C = 8.0
D_ATT = 512
HEAD_DIM = 64
N_HEADS = D_ATT // HEAD_DIM
D_FF = 2816
N_IN = 2 * D_RNN + 3 * D_ATT
EPS = 1e-6

LANES = 128
HEADS_PER_STEP = LANES // HEAD_DIM
VMEM_LIMIT = 56 * 1024 * 1024

TM = 512
FF_CHUNK = 256
TS = 512
TQ = 256
TK = 128

f32 = jnp.float32
bf16 = jnp.bfloat16


def _rms(xf, g):
    r = lax.rsqrt(jnp.mean(xf * xf, axis=-1, keepdims=True) + EPS)
    return xf * r * g


def _dot(a, b):
    return jnp.dot(a, b, preferred_element_type=f32)


def _split_bf16(x):
    hi = x.astype(bf16)
    lo = (x - hi.astype(f32)).astype(bf16)
    return hi, lo


def _const_spec(shape):
    nd = len(shape)
    return pl.BlockSpec(shape, lambda *_: (0,) * nd, pipeline_mode=pl.Buffered(1))


def _ffn_kernel(x_ref, g_ref, wg_ref, wu_ref, wd_ref, o_ref, h_ref):
    x = x_ref[...]
    xn = _rms(x, g_ref[...]).astype(bf16)
    for c in range(0, D_FF, FF_CHUNK):
        gate = _dot(xn, wg_ref[:, c:c + FF_CHUNK])
        up = _dot(xn, wu_ref[:, c:c + FF_CHUNK])
        h_ref[:, c:c + FF_CHUNK] = (jax.nn.silu(gate) * up).astype(bf16)
    o_ref[...] = x + 0.5 * _dot(h_ref[...], wd_ref[...])


def _ffn(x2d, g, wg, wu, wd):
    n_tok = x2d.shape[0]
    tok_spec = pl.BlockSpec((TM, D_MODEL), lambda i: (i, 0))
    return pl.pallas_call(
        _ffn_kernel,
        out_shape=jax.ShapeDtypeStruct((n_tok, D_MODEL), f32),
        grid=(n_tok // TM,),
        in_specs=[tok_spec, _const_spec((1, D_MODEL)), _const_spec((D_MODEL, D_FF)),
                  _const_spec((D_MODEL, D_FF)), _const_spec((D_FF, D_MODEL))],
        out_specs=tok_spec,
        scratch_shapes=[pltpu.VMEM((TM, D_FF), bf16)],
        compiler_params=pltpu.CompilerParams(
            dimension_semantics=("parallel",), vmem_limit_bytes=VMEM_LIMIT),
        name="ffn",
    )(x2d, g, wg, wu, wd)


def _head_norm(t, gain_tiled, seg_ref):
    hi, lo = _split_bf16(t * t)
    ss = _dot(hi, seg_ref[...]) + _dot(lo, seg_ref[...])
    r = lax.rsqrt(ss * (1.0 / HEAD_DIM) + EPS)
    return t * r * gain_tiled


def _mix_proj_kernel(x_ref, g_ref, w_ref, qg_ref, kg_ref, seg_ref,
                     xr_ref, gate_ref, q_ref, k_ref, v_ref):
    h = _rms(x_ref[...], g_ref[...]).astype(bf16)
    xr_ref[...] = _dot(h, w_ref[:, 0:D_RNN])
    gate_ref[...] = _dot(h, w_ref[:, D_RNN:2 * D_RNN])
    o = 2 * D_RNN
    q = _dot(h, w_ref[:, o:o + D_ATT])
    k = _dot(h, w_ref[:, o + D_ATT:o + 2 * D_ATT])
    v = _dot(h, w_ref[:, o + 2 * D_ATT:o + 3 * D_ATT])
    scale = 1.0 / math.sqrt(HEAD_DIM)
    q_ref[...] = (_head_norm(q, qg_ref[...], seg_ref) * scale).astype(bf16)
    k_ref[...] = _head_norm(k, kg_ref[...], seg_ref).astype(bf16)
    v_ref[...] = v.astype(bf16)


def _mix_proj(x2d, g, w_in, qg, kg, seg):
    n_tok = x2d.shape[0]
    tok = lambda d: pl.BlockSpec((TM, d), lambda i: (i, 0))
    return pl.pallas_call(
        _mix_proj_kernel,
        out_shape=(jax.ShapeDtypeStruct((n_tok, D_RNN), f32),
                   jax.ShapeDtypeStruct((n_tok, D_RNN), f32),
                   jax.ShapeDtypeStruct((n_tok, D_ATT), bf16),
                   jax.ShapeDtypeStruct((n_tok, D_ATT), bf16),
                   jax.ShapeDtypeStruct((n_tok, D_ATT), bf16)),
        grid=(n_tok // TM,),
        in_specs=[tok(D_MODEL), _const_spec((1, D_MODEL)), _const_spec((D_MODEL, N_IN)),
                  _const_spec((1, D_ATT)), _const_spec((1, D_ATT)), _const_spec((D_ATT, D_ATT))],
        out_specs=(tok(D_RNN), tok(D_RNN), tok(D_ATT), tok(D_ATT), tok(D_ATT)),
        compiler_params=pltpu.CompilerParams(
            dimension_semantics=("parallel",), vmem_limit_bytes=VMEM_LIMIT),
        name="mix_proj",
    )(x2d, g, w_in, qg, kg, seg)


HIST = 8


def _rglru_kernel(xr_ref, gate_ref, cw_ref, cb_ref, wa_ref, ba_ref, wx_ref, bx_ref,
                  lam_ref, g_ref, o_ref, xbuf_ref, hc_ref):
    @pl.when(pl.program_id(1) == 0)
    def _():
        xbuf_ref[0:HIST, :] = jnp.zeros((HIST, D_RNN), f32)
        hc_ref[...] = jnp.zeros((1, D_RNN), f32)

    x = xr_ref[0]
    xbuf_ref[HIST:HIST + TS, :] = x
    xc = cb_ref[...] + cw_ref[CONV_W - 1:CONV_W, :] * x
    for j in range(CONV_W - 1):
        off = HIST - (CONV_W - 1) + j
        xc = xc + cw_ref[j:j + 1, :] * xbuf_ref[off:off + TS, :]
    xbuf_ref[0:HIST, :] = x[TS - HIST:TS, :]

    xcb = xc.astype(bf16)
    r = jax.nn.sigmoid(_dot(xcb, wa_ref[...]) + ba_ref[...])
    i = jax.nn.sigmoid(_dot(xcb, wx_ref[...]) + bx_ref[...])
    lam = lam_ref[...]
    log_sig_lam = jnp.minimum(lam, 0.0) - jnp.log1p(jnp.exp(-jnp.abs(lam)))
    log_a = RG_C * r * log_sig_lam
    a = jnp.exp(log_a)
    th = jnp.tanh(log_a)
    mult = jnp.sqrt(-2.0 * th / (1.0 - th))
    b = mult * (i * xc)

    row = lax.broadcasted_iota(jnp.int32, (TS, D_RNN), 0)
    d = 1
    while d < TS:
        a_sh = pltpu.roll(a, d, axis=0)
        b_sh = pltpu.roll(b, d, axis=0)
        m = row >= d
        b = jnp.where(m, a * b_sh, 0.0) + b
        a = jnp.where(m, a * a_sh, a)
        d *= 2
    h = a * hc_ref[...] + b
    hc_ref[...] = h[TS - 1:TS, :]

    y = h * jax.nn.gelu(gate_ref[0])
    o_ref[0] = _rms(y, g_ref[...]).astype(bf16)


def _rglru(xr, gate, cw, cb, wa, ba, wx, bx, lam, g):
    bsz, seq, _ = xr.shape
    seq_spec = pl.BlockSpec((1, TS, D_RNN), lambda b, s: (b, s, 0))
    vec = _const_spec((1, D_RNN))
    return pl.pallas_call(
        _rglru_kernel,
        out_shape=jax.ShapeDtypeStruct((bsz, seq, D_RNN), bf16),
        grid=(bsz, seq // TS),
        in_specs=[seq_spec, seq_spec, _const_spec((CONV_W, D_RNN)), vec,
                  _const_spec((D_RNN, D_RNN)), vec, _const_spec((D_RNN, D_RNN)), vec, vec, vec],
        out_specs=seq_spec,
        scratch_shapes=[pltpu.VMEM((HIST + TS, D_RNN), f32), pltpu.VMEM((1, D_RNN), f32)],
        compiler_params=pltpu.CompilerParams(
            dimension_semantics=("parallel", "arbitrary"), vmem_limit_bytes=VMEM_LIMIT),
        name="rglru",
    )(xr, gate, cw, cb, wa, ba, wx, bx, lam, g)


def _attn_kernel(q_ref, k_ref, v_ref, tri_ref, o_ref, acc_ref, c_ref):
    qi = pl.program_id(2)
    q = q_ref[0]
    lane = lax.broadcasted_iota(jnp.int32, (TQ, LANES), 1)
    qh = [jnp.where((lane // HEAD_DIM) == h, q, jnp.zeros_like(q)) for h in range(HEADS_PER_STEP)]
    acc_ref[...] = jnp.zeros_like(acc_ref)
    c_ref[...] = jnp.zeros_like(c_ref)
    row_pos = qi * TQ + lax.broadcasted_iota(jnp.int32, (TQ, TK), 0)
    col_off = lax.broadcasted_iota(jnp.int32, (TQ, TK), 1)

    def step(kt, masked):
        ks = pl.multiple_of(kt * TK, TK)
        k = k_ref[0, pl.ds(ks, TK), :]
        v = v_ref[0, pl.ds(ks, TK), :]
        if masked:
            valid = (col_off + ks) < row_pos
        for h in range(HEADS_PER_STEP):
            z = lax.dot_general(qh[h], k, (((1,), (1,)), ((), ())), preferred_element_type=f32)
            lb = jnp.minimum(z, 0.0) - jnp.log1p(jnp.exp(-jnp.abs(z)))
            l1m = lb - z
            if masked:
                l1m = jnp.where(valid, l1m, 0.0)
            hi, lo = _split_bf16(l1m)
            p = _dot(jnp.concatenate([hi, lo], axis=1), tri_ref[...])
            w = jnp.exp(lb + p[:, :TK] + c_ref[h])
            if masked:
                w = jnp.where(valid, w, 0.0)
            acc_ref[h] += _dot(w.astype(bf16), v)
            c_ref[h] += p[:, TK:]

    n_diag = TQ // TK
    for j in range(n_diag - 1, -1, -1):
        step(qi * n_diag + j, True)

    n_full = qi * n_diag

    def body(i, carry):
        step(n_full - 1 - i, False)
        return carry

    lax.fori_loop(0, n_full, body, 0)

    out = acc_ref[0]
    for h in range(1, HEADS_PER_STEP):
        out = jnp.where((lane // HEAD_DIM) == h, acc_ref[h], out)
    o_ref[0] = out


def _attention(q, k, v, tri):
    bsz, seq, _ = q.shape
    q_spec = pl.BlockSpec((1, TQ, LANES), lambda b, p, i: (b, i, p))
    kv_spec = pl.BlockSpec((1, seq, LANES), lambda b, p, i: (b, 0, p))
    return pl.pallas_call(
        _attn_kernel,
        out_shape=jax.ShapeDtypeStruct((bsz, seq, D_ATT), f32),
        grid=(bsz, D_ATT // LANES, seq // TQ),
        in_specs=[q_spec, kv_spec, kv_spec, _const_spec((2 * TK, 2 * TK))],
        out_specs=q_spec,
        scratch_shapes=[pltpu.VMEM((HEADS_PER_STEP, TQ, LANES), f32),
                        pltpu.VMEM((HEADS_PER_STEP, TQ, LANES), f32)],
        compiler_params=pltpu.CompilerParams(
            dimension_semantics=("parallel", "parallel", "parallel"),
            vmem_limit_bytes=VMEM_LIMIT),
        name="stickbreak",
    )(q, k, v, tri)


def _out_proj_kernel(x_ref, yr_ref, ya_ref, g_ref, w_ref, o_ref):
    ya = _rms(ya_ref[...], g_ref[...]).astype(bf16)
    o_ref[...] = (x_ref[...] + _dot(yr_ref[...], w_ref[0:D_RNN, :])
                  + _dot(ya, w_ref[D_RNN:D_RNN + D_ATT, :]))


def _out_proj(x2d, yr, ya, g, w_out):
    n_tok = x2d.shape[0]
    tok = lambda d: pl.BlockSpec((TM, d), lambda i: (i, 0))
    return pl.pallas_call(
        _out_proj_kernel,
        out_shape=jax.ShapeDtypeStruct((n_tok, D_MODEL), f32),
        grid=(n_tok // TM,),
        in_specs=[tok(D_MODEL), tok(D_RNN), tok(D_ATT), _const_spec((1, D_ATT)),
                  _const_spec((D_RNN + D_ATT, D_MODEL))],
        out_specs=tok(D_MODEL),
        compiler_params=pltpu.CompilerParams(
            dimension_semantics=("parallel",), vmem_limit_bytes=VMEM_LIMIT),
        name="out_proj",
    )(x2d, yr, ya, g, w_out)


def _block_diag(w):
    eye = jnp.eye(RNN_BLOCKS, dtype=w.dtype)
    return jnp.einsum("ncd,nm->ncmd", w, eye).reshape(D_RNN, D_RNN)


def _scan_matrix():
    j = jnp.arange(TK)
    strict = (j[:, None] > j[None, :]).astype(bf16)
    half = jnp.concatenate([strict, jnp.ones((TK, TK), bf16)], axis=1)
    return jnp.concatenate([half, half], axis=0)


def kernel(x, ffn1_norm, ffn1_w_gate, ffn1_w_up, ffn1_w_down, mix_norm, w_in, conv_w, conv_b,
           rg_w_a, rg_b_a, rg_w_x, rg_b_x, rg_lambda, q_norm, k_norm, rnn_out_norm,
           attn_out_norm, w_out, ffn2_norm, ffn2_w_gate, ffn2_w_up, ffn2_w_down):
    bsz, seq, _ = x.shape
    depth = ffn1_norm.shape[0]
    head_id = jnp.arange(D_ATT) // HEAD_DIM
    seg = (head_id[:, None] == head_id[None, :]).astype(bf16)
    tri = _scan_matrix()
    x2d = x.reshape(bsz * seq, D_MODEL)
    for l in range(depth):
        x2d = _ffn(x2d, ffn1_norm[l][None], ffn1_w_gate[l].astype(bf16),
                   ffn1_w_up[l].astype(bf16), ffn1_w_down[l].astype(bf16))
        xr, gate, q, k, v = _mix_proj(
            x2d, mix_norm[l][None], w_in[l].astype(bf16),
            jnp.tile(q_norm[l], N_HEADS)[None], jnp.tile(k_norm[l], N_HEADS)[None], seg)
        y_rnn = _rglru(
            xr.reshape(bsz, seq, D_RNN), gate.reshape(bsz, seq, D_RNN),
            conv_w[l], conv_b[l][None], _block_diag(rg_w_a[l]).astype(bf16), rg_b_a[l][None],
            _block_diag(rg_w_x[l]).astype(bf16), rg_b_x[l][None], rg_lambda[l][None],
            rnn_out_norm[l][None])
        y_att = _attention(q.reshape(bsz, seq, D_ATT), k.reshape(bsz, seq, D_ATT),
                           v.reshape(bsz, seq, D_ATT), tri)
        x2d = _out_proj(x2d, y_rnn.reshape(bsz * seq, D_RNN), y_att.reshape(bsz * seq, D_ATT),
                        attn_out_norm[l][None], w_out[l].astype(bf16))
        x2d = _ffn(x2d, ffn2_norm[l][None], ffn2_w_gate[l].astype(bf16),
                   ffn2_w_up[l].astype(bf16), ffn2_w_down[l].astype(bf16))
    return x2d.reshape(bsz, seq, D_MODEL)
```

```python
import math

import jax
import jax.numpy as jnp
from jax import lax
from jax.experimental import pallas as pl
from jax.experimental.pallas import tpu as pltpu

D_MODEL = 1024
D_RNN = 512
RNN_BLOCKS = 8
RNN_BW = D_RNN // RNN_BLOCKS
CONV_W = 4
RG_C = 8.0
D_ATT = 512
HEAD_DIM = 64
N_HEADS = D_ATT // HEAD_DIM
D_FF = 2816
N_IN = 2 * D_RNN + 3 * D_ATT
EPS = 1e-6

LANES = 128
HEADS_PER_STEP = LANES // HEAD_DIM
VMEM_LIMIT = 56 * 1024 * 1024

TM = 512
FF_CHUNK = 256
TS = 512
TQ = 256
TK = 128

f32 = jnp.float32
bf16 = jnp.bfloat16


def _rms(xf, g):
    r = lax.rsqrt(jnp.mean(xf * xf, axis=-1, keepdims=True) + EPS)
    return xf * r * g


def _dot(a, b):
    return jnp.dot(a, b, preferred_element_type=f32)


def _split_bf16(x):
    hi = x.astype(bf16)
    lo = (x - hi.astype(f32)).astype(bf16)
    return hi, lo


def _const_spec(shape):
    nd = len(shape)
    return pl.BlockSpec(shape, lambda *_: (0,) * nd, pipeline_mode=pl.Buffered(1))


def _ffn_kernel(x_ref, g_ref, wg_ref, wu_ref, wd_ref, o_ref, h_ref):
    x = x_ref[...]
    xn = _rms(x, g_ref[...]).astype(bf16)
    for c in range(0, D_FF, FF_CHUNK):
        gate = _dot(xn, wg_ref[:, c:c + FF_CHUNK])
        up = _dot(xn, wu_ref[:, c:c + FF_CHUNK])
        h_ref[:, c:c + FF_CHUNK] = (jax.nn.silu(gate) * up).astype(bf16)
    o_ref[...] = x + 0.5 * _dot(h_ref[...], wd_ref[...])


def _ffn(x2d, g, wg, wu, wd):
    n_tok = x2d.shape[0]
    tok_spec = pl.BlockSpec((TM, D_MODEL), lambda i: (i, 0))
    return pl.pallas_call(
        _ffn_kernel,
        out_shape=jax.ShapeDtypeStruct((n_tok, D_MODEL), f32),
        grid=(n_tok // TM,),
        in_specs=[tok_spec, _const_spec((1, D_MODEL)), _const_spec((D_MODEL, D_FF)),
                  _const_spec((D_MODEL, D_FF)), _const_spec((D_FF, D_MODEL))],
        out_specs=tok_spec,
        scratch_shapes=[pltpu.VMEM((TM, D_FF), bf16)],
        compiler_params=pltpu.CompilerParams(
            dimension_semantics=("parallel",), vmem_limit_bytes=VMEM_LIMIT),
        name="ffn",
    )(x2d, g, wg, wu, wd)


def _head_norm(t, gain_tiled, seg_ref):
    hi, lo = _split_bf16(t * t)
    ss = _dot(hi, seg_ref[...]) + _dot(lo, seg_ref[...])
    r = lax.rsqrt(ss * (1.0 / HEAD_DIM) + EPS)
    return t * r * gain_tiled


def _mix_proj_kernel(x_ref, g_ref, w_ref, qg_ref, kg_ref, seg_ref,
                     xr_ref, gate_ref, q_ref, k_ref, v_ref):
    h = _rms(x_ref[...], g_ref[...]).astype(bf16)
    xr_ref[...] = _dot(h, w_ref[:, 0:D_RNN])
    gate_ref[...] = _dot(h, w_ref[:, D_RNN:2 * D_RNN])
    o = 2 * D_RNN
    q = _dot(h, w_ref[:, o:o + D_ATT])
    k = _dot(h, w_ref[:, o + D_ATT:o + 2 * D_ATT])
    v = _dot(h, w_ref[:, o + 2 * D_ATT:o + 3 * D_ATT])
    scale = 1.0 / math.sqrt(HEAD_DIM)
    q_ref[...] = (_head_norm(q, qg_ref[...], seg_ref) * scale).astype(bf16)
    k_ref[...] = _head_norm(k, kg_ref[...], seg_ref).astype(bf16)
    v_ref[...] = v.astype(bf16)


def _mix_proj(x2d, g, w_in, qg, kg, seg):
    n_tok = x2d.shape[0]
    tok = lambda d: pl.BlockSpec((TM, d), lambda i: (i, 0))
    return pl.pallas_call(
        _mix_proj_kernel,
        out_shape=(jax.ShapeDtypeStruct((n_tok, D_RNN), f32),
                   jax.ShapeDtypeStruct((n_tok, D_RNN), f32),
                   jax.ShapeDtypeStruct((n_tok, D_ATT), bf16),
                   jax.ShapeDtypeStruct((n_tok, D_ATT), bf16),
                   jax.ShapeDtypeStruct((n_tok, D_ATT), bf16)),
        grid=(n_tok // TM,),
        in_specs=[tok(D_MODEL), _const_spec((1, D_MODEL)), _const_spec((D_MODEL, N_IN)),
                  _const_spec((1, D_ATT)), _const_spec((1, D_ATT)), _const_spec((D_ATT, D_ATT))],
        out_specs=(tok(D_RNN), tok(D_RNN), tok(D_ATT), tok(D_ATT), tok(D_ATT)),
        compiler_params=pltpu.CompilerParams(
            dimension_semantics=("parallel",), vmem_limit_bytes=VMEM_LIMIT),
        name="mix_proj",
    )(x2d, g, w_in, qg, kg, seg)


HIST = 8


def _rglru_kernel(xr_ref, gate_ref, cw_ref, cb_ref, wa_ref, ba_ref, wx_ref, bx_ref,
                  lam_ref, g_ref, o_ref, xbuf_ref, hc_ref):
    @pl.when(pl.program_id(1) == 0)
    def _():
        xbuf_ref[0:HIST, :] = jnp.zeros((HIST, D_RNN), f32)
        hc_ref[...] = jnp.zeros((1, D_RNN), f32)

    x = xr_ref[0]
    xbuf_ref[HIST:HIST + TS, :] = x
    xc = cb_ref[...] + cw_ref[CONV_W - 1:CONV_W, :] * x
    for j in range(CONV_W - 1):
        off = HIST - (CONV_W - 1) + j
        xc = xc + cw_ref[j:j + 1, :] * xbuf_ref[off:off + TS, :]
    xbuf_ref[0:HIST, :] = x[TS - HIST:TS, :]

    xcb = xc.astype(bf16)
    r = jax.nn.sigmoid(_dot(xcb, wa_ref[...]) + ba_ref[...])
    i = jax.nn.sigmoid(_dot(xcb, wx_ref[...]) + bx_ref[...])
    lam = lam_ref[...]
    log_sig_lam = jnp.minimum(lam, 0.0) - jnp.log1p(jnp.exp(-jnp.abs(lam)))
    log_a = RG_C * r * log_sig_lam
    a = jnp.exp(log_a)
    th = jnp.tanh(log_a)
    mult = jnp.sqrt(-2.0 * th / (1.0 - th))
    b = mult * (i * xc)

    row = lax.broadcasted_iota(jnp.int32, (TS, D_RNN), 0)
    d = 1
    while d < TS:
        a_sh = pltpu.roll(a, d, axis=0)
        b_sh = pltpu.roll(b, d, axis=0)
        m = row >= d
        b = jnp.where(m, a * b_sh, 0.0) + b
        a = jnp.where(m, a * a_sh, a)
        d *= 2
    h = a * hc_ref[...] + b
    hc_ref[...] = h[TS - 1:TS, :]

    y = h * jax.nn.gelu(gate_ref[0])
    o_ref[0] = _rms(y, g_ref[...]).astype(bf16)


def _rglru(xr, gate, cw, cb, wa, ba, wx, bx, lam, g):
    bsz, seq, _ = xr.shape
    seq_spec = pl.BlockSpec((1, TS, D_RNN), lambda b, s: (b, s, 0))
    vec = _const_spec((1, D_RNN))
    return pl.pallas_call(
        _rglru_kernel,
        out_shape=jax.ShapeDtypeStruct((bsz, seq, D_RNN), bf16),
        grid=(bsz, seq // TS),
        in_specs=[seq_spec, seq_spec, _const_spec((CONV_W, D_RNN)), vec,
                  _const_spec((D_RNN, D_RNN)), vec, _const_spec((D_RNN, D_RNN)), vec, vec, vec],
        out_specs=seq_spec,
        scratch_shapes=[pltpu.VMEM((HIST + TS, D_RNN), f32), pltpu.VMEM((1, D_RNN), f32)],
        compiler_params=pltpu.CompilerParams(
            dimension_semantics=("parallel", "arbitrary"), vmem_limit_bytes=VMEM_LIMIT),
        name="rglru",
    )(xr, gate, cw, cb, wa, ba, wx, bx, lam, g)


def _attn_kernel(q_ref, k_ref, v_ref, tri_ref, o_ref,
                 acc_ref, c_ref, z_ref, zc_ref, p_ref, d_ref):
    qi = pl.program_id(2)
    heads = range(HEADS_PER_STEP)
    q = q_ref[0]
    lane = lax.broadcasted_iota(jnp.int32, (TQ, LANES), 1)
    qh = [jnp.where((lane // HEAD_DIM) == h, q, jnp.zeros_like(q)) for h in heads]
    acc_ref[...] = jnp.zeros_like(acc_ref)
    c_ref[...] = jnp.zeros_like(c_ref)
    d_ref[...] = (lax.broadcasted_iota(jnp.int32, (TQ, TK), 1)
                  - lax.broadcasted_iota(jnp.int32, (TQ, TK), 0))

    def key_start(t):
        return pl.multiple_of(t * TK, TK)

    def visible(t):
        return d_ref[...] < (qi * TQ - t * TK)

    def stage_a(t):
        k = k_ref[0, pl.ds(key_start(t), TK), :]
        for h in heads:
            z_ref[h] = lax.dot_general(qh[h], k, (((1,), (1,)), ((), ())),
                                       preferred_element_type=f32)

    def stage_b(t):
        valid = visible(t)
        for h in heads:
            z = z_ref[h]
            l1m = -jnp.log(1.0 + jnp.exp(-jnp.abs(z))) - jnp.maximum(z, 0.0)
            l1m = jnp.where(valid, l1m, 0.0)
            p_ref[h] = _dot(jnp.concatenate(_split_bf16(l1m), axis=1), tri_ref[...])
            zc_ref[h] = z

    def stage_c(t):
        v = v_ref[0, pl.ds(key_start(t), TK), :]
        valid = visible(t)
        for h in heads:
            w = jnp.exp(zc_ref[h] + p_ref[h, :, 0:TK] + c_ref[h])
            w = jnp.where(valid, w, 0.0)
            acc_ref[h] += _dot(w.astype(bf16), v)
            c_ref[h] += p_ref[h, :, TK:2 * TK]

    n_tiles = (qi + 1) * (TQ // TK)
    last = n_tiles - 1
    stage_a(last)
    stage_b(last)
    stage_a(last - 1)

    def body(i, carry):
        t = last - i
        stage_c(t)
        stage_b(jnp.maximum(t - 1, 0))
        stage_a(jnp.maximum(t - 2, 0))
        return carry

    lax.fori_loop(0, n_tiles, body, 0)

    out = acc_ref[0]
    for h in range(1, HEADS_PER_STEP):
        out = jnp.where((lane // HEAD_DIM) == h, acc_ref[h], out)
    o_ref[0] = out


def _attention(q, k, v, tri):
    bsz, seq, _ = q.shape
    q_spec = pl.BlockSpec((1, TQ, LANES), lambda b, p, i: (b, i, p))
    kv_spec = pl.BlockSpec((1, seq, LANES), lambda b, p, i: (b, 0, p))
    per_head = lambda n: pltpu.VMEM((HEADS_PER_STEP, TQ, n), f32)
    return pl.pallas_call(
        _attn_kernel,
        out_shape=jax.ShapeDtypeStruct((bsz, seq, D_ATT), f32),
        grid=(bsz, D_ATT // LANES, seq // TQ),
        in_specs=[q_spec, kv_spec, kv_spec, _const_spec((2 * TK, 2 * TK))],
        out_specs=q_spec,
        scratch_shapes=[per_head(LANES), per_head(LANES), per_head(TK), per_head(TK),
                        per_head(2 * TK), pltpu.VMEM((TQ, TK), jnp.int32)],
        compiler_params=pltpu.CompilerParams(
            dimension_semantics=("parallel", "parallel", "parallel"),
            vmem_limit_bytes=VMEM_LIMIT),
        name="stickbreak",
    )(q, k, v, tri)


def _out_proj_kernel(x_ref, yr_ref, ya_ref, g_ref, w_ref, o_ref):
    ya = _rms(ya_ref[...], g_ref[...]).astype(bf16)
    o_ref[...] = (x_ref[...] + _dot(yr_ref[...], w_ref[0:D_RNN, :])
                  + _dot(ya, w_ref[D_RNN:D_RNN + D_ATT, :]))


def _out_proj(x2d, yr, ya, g, w_out):
    n_tok = x2d.shape[0]
    tok = lambda d: pl.BlockSpec((TM, d), lambda i: (i, 0))
    return pl.pallas_call(
        _out_proj_kernel,
        out_shape=jax.ShapeDtypeStruct((n_tok, D_MODEL), f32),
        grid=(n_tok // TM,),
        in_specs=[tok(D_MODEL), tok(D_RNN), tok(D_ATT), _const_spec((1, D_ATT)),
                  _const_spec((D_RNN + D_ATT, D_MODEL))],
        out_specs=tok(D_MODEL),
        compiler_params=pltpu.CompilerParams(
            dimension_semantics=("parallel",), vmem_limit_bytes=VMEM_LIMIT),
        name="out_proj",
    )(x2d, yr, ya, g, w_out)


def _block_diag(w):
    eye = jnp.eye(RNN_BLOCKS, dtype=w.dtype)
    return jnp.einsum("ncd,nm->ncmd", w, eye).reshape(D_RNN, D_RNN)


def _scan_matrix():
    j = jnp.arange(TK)
    suffix = (j[:, None] >= j[None, :]).astype(bf16)
    half = jnp.concatenate([suffix, jnp.ones((TK, TK), bf16)], axis=1)
    return jnp.concatenate([half, half], axis=0)


def kernel(x, ffn1_norm, ffn1_w_gate, ffn1_w_up, ffn1_w_down, mix_norm, w_in, conv_w, conv_b,
           rg_w_a, rg_b_a, rg_w_x, rg_b_x, rg_lambda, q_norm, k_norm, rnn_out_norm,
           attn_out_norm, w_out, ffn2_norm, ffn2_w_gate, ffn2_w_up, ffn2_w_down):
    bsz, seq, _ = x.shape
    depth = ffn1_norm.shape[0]
    head_id = jnp.arange(D_ATT) // HEAD_DIM
    seg = (head_id[:, None] == head_id[None, :]).astype(bf16)
    tri = _scan_matrix()
    x2d = x.reshape(bsz * seq, D_MODEL)
    for l in range(depth):
        x2d = _ffn(x2d, ffn1_norm[l][None], ffn1_w_gate[l].astype(bf16),
                   ffn1_w_up[l].astype(bf16), ffn1_w_down[l].astype(bf16))
        xr, gate, q, k, v = _mix_proj(
            x2d, mix_norm[l][None], w_in[l].astype(bf16),
            jnp.tile(q_norm[l], N_HEADS)[None], jnp.tile(k_norm[l], N_HEADS)[None], seg)
        y_rnn = _rglru(
            xr.reshape(bsz, seq, D_RNN), gate.reshape(bsz, seq, D_RNN),
            conv_w[l], conv_b[l][None], _block_diag(rg_w_a[l]).astype(bf16), rg_b_a[l][None],
            _block_diag(rg_w_x[l]).astype(bf16), rg_b_x[l][None], rg_lambda[l][None],
            rnn_out_norm[l][None])
        y_att = _attention(q.reshape(bsz, seq, D_ATT), k.reshape(bsz, seq, D_ATT),
                           v.reshape(bsz, seq, D_ATT), tri)
        x2d = _out_proj(x2d, y_rnn.reshape(bsz * seq, D_RNN), y_att.reshape(bsz * seq, D_ATT),
                        attn_out_norm[l][None], w_out[l].astype(bf16))
        x2d = _ffn(x2d, ffn2_norm[l][None], ffn2_w_gate[l].astype(bf16),
                   ffn2_w_up[l].astype(bf16), ffn2_w_down[l].astype(bf16))
    return x2d.reshape(bsz, seq, D_MODEL)
```

```python
import math

import jax
import jax.numpy as jnp
from jax import lax
from jax.experimental import pallas as pl
from jax.experimental.pallas import tpu as pltpu

D_MODEL = 1024
D_RNN = 512
RNN_BLOCKS = 8
RNN_BW = D_RNN // RNN_BLOCKS
CONV_W = 4
RG_C = 8.0
D_ATT = 512
HEAD_DIM = 64
N_HEADS = D_ATT // HEAD_DIM
D_FF = 2816
N_IN = 2 * D_RNN + 3 * D_ATT
EPS = 1e-6

LANES = 128
HEADS_PER_STEP = LANES // HEAD_DIM
VMEM_LIMIT = 56 * 1024 * 1024

TM = 512
FF_CHUNK = 256
TS = 512
TQ = 256
TK = 256

f32 = jnp.float32
bf16 = jnp.bfloat16


def _rms(xf, g):
    r = lax.rsqrt(jnp.mean(xf * xf, axis=-1, keepdims=True) + EPS)
    return xf * r * g


def _dot(a, b):
    return jnp.dot(a, b, preferred_element_type=f32)


def _split_bf16(x):
    hi = x.astype(bf16)
    lo = (x - hi.astype(f32)).astype(bf16)
    return hi, lo


def _const_spec(shape):
    nd = len(shape)
    return pl.BlockSpec(shape, lambda *_: (0,) * nd, pipeline_mode=pl.Buffered(1))


def _ffn_kernel(x_ref, g_ref, wg_ref, wu_ref, wd_ref, o_ref, h_ref):
    x = x_ref[...]
    xn = _rms(x, g_ref[...]).astype(bf16)
    for c in range(0, D_FF, FF_CHUNK):
        gate = _dot(xn, wg_ref[:, c:c + FF_CHUNK])
        up = _dot(xn, wu_ref[:, c:c + FF_CHUNK])
        h_ref[:, c:c + FF_CHUNK] = (jax.nn.silu(gate) * up).astype(bf16)
    o_ref[...] = x + 0.5 * _dot(h_ref[...], wd_ref[...])


def _ffn(x2d, g, wg, wu, wd):
    n_tok = x2d.shape[0]
    tok_spec = pl.BlockSpec((TM, D_MODEL), lambda i: (i, 0))
    return pl.pallas_call(
        _ffn_kernel,
        out_shape=jax.ShapeDtypeStruct((n_tok, D_MODEL), f32),
        grid=(n_tok // TM,),
        in_specs=[tok_spec, _const_spec((1, D_MODEL)), _const_spec((D_MODEL, D_FF)),
                  _const_spec((D_MODEL, D_FF)), _const_spec((D_FF, D_MODEL))],
        out_specs=tok_spec,
        scratch_shapes=[pltpu.VMEM((TM, D_FF), bf16)],
        compiler_params=pltpu.CompilerParams(
            dimension_semantics=("parallel",), vmem_limit_bytes=VMEM_LIMIT),
        name="ffn",
    )(x2d, g, wg, wu, wd)


def _head_norm(t, gain_tiled, seg_ref):
    hi, lo = _split_bf16(t * t)
    ss = _dot(hi, seg_ref[...]) + _dot(lo, seg_ref[...])
    r = lax.rsqrt(ss * (1.0 / HEAD_DIM) + EPS)
    return t * r * gain_tiled


def _mix_proj_kernel(x_ref, g_ref, w_ref, qg_ref, kg_ref, seg_ref,
                     xr_ref, gate_ref, q_ref, k_ref, v_ref):
    h = _rms(x_ref[...], g_ref[...]).astype(bf16)
    xr_ref[...] = _dot(h, w_ref[:, 0:D_RNN])
    gate_ref[...] = _dot(h, w_ref[:, D_RNN:2 * D_RNN])
    o = 2 * D_RNN
    q = _dot(h, w_ref[:, o:o + D_ATT])
    k = _dot(h, w_ref[:, o + D_ATT:o + 2 * D_ATT])
    v = _dot(h, w_ref[:, o + 2 * D_ATT:o + 3 * D_ATT])
    scale = 1.0 / math.sqrt(HEAD_DIM)
    q_ref[...] = (_head_norm(q, qg_ref[...], seg_ref) * scale).astype(bf16)
    k_ref[...] = _head_norm(k, kg_ref[...], seg_ref).astype(bf16)
    v_ref[...] = v.astype(bf16)


def _mix_proj(x2d, g, w_in, qg, kg, seg):
    n_tok = x2d.shape[0]
    tok = lambda d: pl.BlockSpec((TM, d), lambda i: (i, 0))
    return pl.pallas_call(
        _mix_proj_kernel,
        out_shape=(jax.ShapeDtypeStruct((n_tok, D_RNN), f32),
                   jax.ShapeDtypeStruct((n_tok, D_RNN), f32),
                   jax.ShapeDtypeStruct((n_tok, D_ATT), bf16),
                   jax.ShapeDtypeStruct((n_tok, D_ATT), bf16),
                   jax.ShapeDtypeStruct((n_tok, D_ATT), bf16)),
        grid=(n_tok // TM,),
        in_specs=[tok(D_MODEL), _const_spec((1, D_MODEL)), _const_spec((D_MODEL, N_IN)),
                  _const_spec((1, D_ATT)), _const_spec((1, D_ATT)), _const_spec((D_ATT, D_ATT))],
        out_specs=(tok(D_RNN), tok(D_RNN), tok(D_ATT), tok(D_ATT), tok(D_ATT)),
        compiler_params=pltpu.CompilerParams(
            dimension_semantics=("parallel",), vmem_limit_bytes=VMEM_LIMIT),
        name="mix_proj",
    )(x2d, g, w_in, qg, kg, seg)


HIST = 8


def _rglru_kernel(xr_ref, gate_ref, cw_ref, cb_ref, wa_ref, ba_ref, wx_ref, bx_ref,
                  lam_ref, g_ref, o_ref, xbuf_ref, hc_ref):
    @pl.when(pl.program_id(1) == 0)
    def _():
        xbuf_ref[0:HIST, :] = jnp.zeros((HIST, D_RNN), f32)
        hc_ref[...] = jnp.zeros((1, D_RNN), f32)

    x = xr_ref[0]
    xbuf_ref[HIST:HIST + TS, :] = x
    xc = cb_ref[...] + cw_ref[CONV_W - 1:CONV_W, :] * x
    for j in range(CONV_W - 1):
        off = HIST - (CONV_W - 1) + j
        xc = xc + cw_ref[j:j + 1, :] * xbuf_ref[off:off + TS, :]
    xbuf_ref[0:HIST, :] = x[TS - HIST:TS, :]

    xcb = xc.astype(bf16)
    r = jax.nn.sigmoid(_dot(xcb, wa_ref[...]) + ba_ref[...])
    i = jax.nn.sigmoid(_dot(xcb, wx_ref[...]) + bx_ref[...])
    lam = lam_ref[...]
    log_sig_lam = jnp.minimum(lam, 0.0) - jnp.log1p(jnp.exp(-jnp.abs(lam)))
    log_a = RG_C * r * log_sig_lam
    a = jnp.exp(log_a)
    th = jnp.tanh(log_a)
    mult = jnp.sqrt(-2.0 * th / (1.0 - th))
    b = mult * (i * xc)

    row = lax.broadcasted_iota(jnp.int32, (TS, D_RNN), 0)
    d = 1
    while d < TS:
        a_sh = pltpu.roll(a, d, axis=0)
        b_sh = pltpu.roll(b, d, axis=0)
        m = row >= d
        b = jnp.where(m, a * b_sh, 0.0) + b
        a = jnp.where(m, a * a_sh, a)
        d *= 2
    h = a * hc_ref[...] + b
    hc_ref[...] = h[TS - 1:TS, :]

    y = h * jax.nn.gelu(gate_ref[0])
    o_ref[0] = _rms(y, g_ref[...]).astype(bf16)


def _rglru(xr, gate, cw, cb, wa, ba, wx, bx, lam, g):
    bsz, seq, _ = xr.shape
    seq_spec = pl.BlockSpec((1, TS, D_RNN), lambda b, s: (b, s, 0))
    vec = _const_spec((1, D_RNN))
    return pl.pallas_call(
        _rglru_kernel,
        out_shape=jax.ShapeDtypeStruct((bsz, seq, D_RNN), bf16),
        grid=(bsz, seq // TS),
        in_specs=[seq_spec, seq_spec, _const_spec((CONV_W, D_RNN)), vec,
                  _const_spec((D_RNN, D_RNN)), vec, _const_spec((D_RNN, D_RNN)), vec, vec, vec],
        out_specs=seq_spec,
        scratch_shapes=[pltpu.VMEM((HIST + TS, D_RNN), f32), pltpu.VMEM((1, D_RNN), f32)],
        compiler_params=pltpu.CompilerParams(
            dimension_semantics=("parallel", "arbitrary"), vmem_limit_bytes=VMEM_LIMIT),
        name="rglru",
    )(xr, gate, cw, cb, wa, ba, wx, bx, lam, g)


PIPE_DEPTH = 4
BLK_PER_TILE = TK // LANES
LOG2E = 1.4426950408889634


MASKED = -1e30


def _attn_kernel(qt_ref, kt_ref, q_ref, k_ref, v_ref, tri_ref, o_ref,
                 c_ref, z_ref, split_ref, p_ref, w_ref, d_ref):
    n_items = qt_ref.shape[0]
    heads = range(HEADS_PER_STEP)
    for ref in (o_ref, c_ref, z_ref, split_ref, p_ref, w_ref):
        ref[...] = jnp.zeros_like(ref)
    d_ref[...] = (lax.broadcasted_iota(jnp.int32, (TQ, TK), 1)
                  - lax.broadcasted_iota(jnp.int32, (TQ, TK), 0))
    lane_kv = lax.broadcasted_iota(jnp.int32, (TK, LANES), 1) // HEAD_DIM

    def item(j):
        j = jnp.clip(j, 0, n_items - 1)
        return qt_ref[j], kt_ref[j]

    def per_head_rows(x):
        return jnp.concatenate([jnp.where(lane_kv == h, x, jnp.zeros_like(x)) for h in heads], axis=0)

    def col(h, b):
        return pl.ds((h * BLK_PER_TILE + b) * LANES, LANES)

    def stage_c2(qi, t):
        rows = pl.ds(pl.multiple_of(qi * TQ, TQ), TQ)
        v = v_ref[0, pl.ds(pl.multiple_of(t * TK, TK), TK), :]
        acc = jnp.where(t == qi, 0.0, o_ref[0, rows, :])
        o_ref[0, rows, :] = acc + _dot(w_ref[...], per_head_rows(v))

    def stage_c1(qi, t):
        for h in heads:
            c = jnp.where(t == qi, 0.0, c_ref[h])
            for b in range(BLK_PER_TILE - 1, -1, -1):
                blk = h * BLK_PER_TILE + b
                w_ref[:, col(h, b)] = jnp.exp(p_ref[blk, :, 0:LANES] + c).astype(bf16)
                c = c + p_ref[blk, :, LANES:2 * LANES]
            c_ref[h] = c

    def stage_b2(slot):
        for h in heads:
            for b in range(BLK_PER_TILE):
                blk = h * BLK_PER_TILE + b
                r = _dot(split_ref[:, blk * 2 * LANES:(blk + 1) * 2 * LANES], tri_ref[...])
                p_ref[blk, :, 0:LANES] = r[:, 0:LANES] + z_ref[slot, :, col(h, b)]
                p_ref[blk, :, LANES:2 * LANES] = r[:, LANES:2 * LANES]

    def stage_b1(slot):
        for h in heads:
            for b in range(BLK_PER_TILE):
                blk = h * BLK_PER_TILE + b
                z = z_ref[slot, :, col(h, b)]
                sp = jnp.log(1.0 + jnp.exp2(jnp.abs(z) * (-LOG2E))) + jnp.maximum(z, 0.0)
                hi, lo = _split_bf16(sp)
                split_ref[:, pl.ds(blk * 2 * LANES, LANES)] = hi
                split_ref[:, pl.ds(blk * 2 * LANES + LANES, LANES)] = lo

    def stage_a(qi, t, slot):
        q = q_ref[0, pl.ds(pl.multiple_of(qi * TQ, TQ), TQ), :]
        k = k_ref[0, pl.ds(pl.multiple_of(t * TK, TK), TK), :]
        z = lax.dot_general(q, per_head_rows(k), (((1,), (1,)), ((), ())),
                            preferred_element_type=f32)
        for b in range(BLK_PER_TILE):
            visible = d_ref[:, b * LANES:(b + 1) * LANES] < (qi - t) * TQ
            for h in heads:
                cols = (h * BLK_PER_TILE + b) * LANES
                z_ref[slot, :, col(h, b)] = jnp.where(visible, z[:, cols:cols + LANES], MASKED)

    def iteration(i, parity):
        stage_c2(*item(i - PIPE_DEPTH))
        stage_c1(*item(i - 3))
        stage_b2(parity)
        stage_b1(1 - parity)
        stage_a(*item(i), parity)

    def body(m, carry):
        iteration(2 * m, 0)
        iteration(2 * m + 1, 1)
        return carry

    lax.fori_loop(0, (n_items + PIPE_DEPTH) // 2, body, 0)


def _attention(q, k, v, tri):
    bsz, seq, _ = q.shape
    n_qt = seq // TQ
    qt = jnp.asarray([qi for qi in range(n_qt) for _ in range(qi + 1)], jnp.int32)
    kt = jnp.asarray([t for qi in range(n_qt) for t in range(qi, -1, -1)], jnp.int32)
    assert (qt.shape[0] + PIPE_DEPTH) % 2 == 0
    seq_spec = pl.BlockSpec((1, seq, LANES), lambda b, p, *_: (b, 0, p))
    tri_spec = pl.BlockSpec((2 * LANES, 2 * LANES), lambda b, p, *_: (0, 0),
                            pipeline_mode=pl.Buffered(1))
    n_col = HEADS_PER_STEP * TK
    return pl.pallas_call(
        _attn_kernel,
        out_shape=jax.ShapeDtypeStruct((bsz, seq, D_ATT), f32),
        grid_spec=pltpu.PrefetchScalarGridSpec(
            num_scalar_prefetch=2,
            grid=(bsz, D_ATT // LANES),
            in_specs=[seq_spec, seq_spec, seq_spec, tri_spec],
            out_specs=seq_spec,
            scratch_shapes=[
                pltpu.VMEM((HEADS_PER_STEP, TQ, LANES), f32),
                pltpu.VMEM((2, TQ, n_col), f32),
                pltpu.VMEM((TQ, 2 * n_col), bf16),
                pltpu.VMEM((n_col // LANES, TQ, 2 * LANES), f32),
                pltpu.VMEM((TQ, n_col), bf16),
                pltpu.VMEM((TQ, TK), jnp.int32),
            ]),
        compiler_params=pltpu.CompilerParams(
            dimension_semantics=("parallel", "parallel"), vmem_limit_bytes=VMEM_LIMIT),
        name="stickbreak",
    )(qt, kt, q, k, v, tri)


def _out_proj_kernel(x_ref, yr_ref, ya_ref, g_ref, w_ref, o_ref):
    ya = _rms(ya_ref[...], g_ref[...]).astype(bf16)
    o_ref[...] = (x_ref[...] + _dot(yr_ref[...], w_ref[0:D_RNN, :])
                  + _dot(ya, w_ref[D_RNN:D_RNN + D_ATT, :]))


def _out_proj(x2d, yr, ya, g, w_out):
    n_tok = x2d.shape[0]
    tok = lambda d: pl.BlockSpec((TM, d), lambda i: (i, 0))
    return pl.pallas_call(
        _out_proj_kernel,
        out_shape=jax.ShapeDtypeStruct((n_tok, D_MODEL), f32),
        grid=(n_tok // TM,),
        in_specs=[tok(D_MODEL), tok(D_RNN), tok(D_ATT), _const_spec((1, D_ATT)),
                  _const_spec((D_RNN + D_ATT, D_MODEL))],
        out_specs=tok(D_MODEL),
        compiler_params=pltpu.CompilerParams(
            dimension_semantics=("parallel",), vmem_limit_bytes=VMEM_LIMIT),
        name="out_proj",
    )(x2d, yr, ya, g, w_out)


def _block_diag(w):
    eye = jnp.eye(RNN_BLOCKS, dtype=w.dtype)
    return jnp.einsum("ncd,nm->ncmd", w, eye).reshape(D_RNN, D_RNN)


def _scan_matrix():
    j = jnp.arange(LANES)
    suffix = (j[:, None] >= j[None, :]).astype(bf16)
    half = -jnp.concatenate([suffix, jnp.ones((LANES, LANES), bf16)], axis=1)
    return jnp.concatenate([half, half], axis=0)


def kernel(x, ffn1_norm, ffn1_w_gate, ffn1_w_up, ffn1_w_down, mix_norm, w_in, conv_w, conv_b,
           rg_w_a, rg_b_a, rg_w_x, rg_b_x, rg_lambda, q_norm, k_norm, rnn_out_norm,
           attn_out_norm, w_out, ffn2_norm, ffn2_w_gate, ffn2_w_up, ffn2_w_down):
    bsz, seq, _ = x.shape
    depth = ffn1_norm.shape[0]
    head_id = jnp.arange(D_ATT) // HEAD_DIM
    seg = (head_id[:, None] == head_id[None, :]).astype(bf16)
    tri = _scan_matrix()
    x2d = x.reshape(bsz * seq, D_MODEL)
    for l in range(depth):
        x2d = _ffn(x2d, ffn1_norm[l][None], ffn1_w_gate[l].astype(bf16),
                   ffn1_w_up[l].astype(bf16), ffn1_w_down[l].astype(bf16))
        xr, gate, q, k, v = _mix_proj(
            x2d, mix_norm[l][None], w_in[l].astype(bf16),
            jnp.tile(q_norm[l], N_HEADS)[None], jnp.tile(k_norm[l], N_HEADS)[None], seg)
        y_rnn = _rglru(
            xr.reshape(bsz, seq, D_RNN), gate.reshape(bsz, seq, D_RNN),
            conv_w[l], conv_b[l][None], _block_diag(rg_w_a[l]).astype(bf16), rg_b_a[l][None],
            _block_diag(rg_w_x[l]).astype(bf16), rg_b_x[l][None], rg_lambda[l][None],
            rnn_out_norm[l][None])
        y_att = _attention(q.reshape(bsz, seq, D_ATT), k.reshape(bsz, seq, D_ATT),
                           v.reshape(bsz, seq, D_ATT), tri)
        x2d = _out_proj(x2d, y_rnn.reshape(bsz * seq, D_RNN), y_att.reshape(bsz * seq, D_ATT),
                        attn_out_norm[l][None], w_out[l].astype(bf16))
        x2d = _ffn(x2d, ffn2_norm[l][None], ffn2_w_gate[l].astype(bf16),
                   ffn2_w_up[l].astype(bf16), ffn2_w_down[l].astype(bf16))
    return x2d.reshape(bsz, seq, D_MODEL)
```

```python
import math

import jax
import jax.numpy as jnp
from jax import lax
from jax.experimental import pallas as pl
from jax.experimental.pallas import tpu as pltpu

D_MODEL = 1024
D_RNN = 512
RNN_BLOCKS = 8
RNN_BW = D_RNN // RNN_BLOCKS
CONV_W = 4
RG_C = 8.0
D_ATT = 512
HEAD_DIM = 64
N_HEADS = D_ATT // HEAD_DIM
D_FF = 2816
N_IN = 2 * D_RNN + 3 * D_ATT
EPS = 1e-6

LANES = 128
HEADS_PER_STEP = LANES // HEAD_DIM
VMEM_LIMIT = 56 * 1024 * 1024

TM = 512
FF_CHUNK = 256
TS = 512
TQ = 256
TK = 256

f32 = jnp.float32
bf16 = jnp.bfloat16


def _rms(xf, g):
    r = lax.rsqrt(jnp.mean(xf * xf, axis=-1, keepdims=True) + EPS)
    return xf * r * g


def _dot(a, b):
    return jnp.dot(a, b, preferred_element_type=f32)


def _split_bf16(x):
    hi = x.astype(bf16)
    lo = (x - hi.astype(f32)).astype(bf16)
    return hi, lo


def _const_spec(shape):
    nd = len(shape)
    return pl.BlockSpec(shape, lambda *_: (0,) * nd, pipeline_mode=pl.Buffered(1))


def _ffn_kernel(x_ref, g_ref, wg_ref, wu_ref, wd_ref, o_ref, h_ref):
    x = x_ref[...]
    xn = _rms(x, g_ref[...]).astype(bf16)
    for c in range(0, D_FF, FF_CHUNK):
        gate = _dot(xn, wg_ref[:, c:c + FF_CHUNK])
        up = _dot(xn, wu_ref[:, c:c + FF_CHUNK])
        h_ref[:, c:c + FF_CHUNK] = (jax.nn.silu(gate) * up).astype(bf16)
    o_ref[...] = x + 0.5 * _dot(h_ref[...], wd_ref[...])


def _ffn(x2d, g, wg, wu, wd):
    n_tok = x2d.shape[0]
    tok_spec = pl.BlockSpec((TM, D_MODEL), lambda i: (i, 0))
    return pl.pallas_call(
        _ffn_kernel,
        out_shape=jax.ShapeDtypeStruct((n_tok, D_MODEL), f32),
        grid=(n_tok // TM,),
        in_specs=[tok_spec, _const_spec((1, D_MODEL)), _const_spec((D_MODEL, D_FF)),
                  _const_spec((D_MODEL, D_FF)), _const_spec((D_FF, D_MODEL))],
        out_specs=tok_spec,
        scratch_shapes=[pltpu.VMEM((TM, D_FF), bf16)],
        compiler_params=pltpu.CompilerParams(
            dimension_semantics=("parallel",), vmem_limit_bytes=VMEM_LIMIT),
        name="ffn",
    )(x2d, g, wg, wu, wd)


def _head_norm(t, gain_tiled, seg_ref):
    hi, lo = _split_bf16(t * t)
    ss = _dot(hi, seg_ref[...]) + _dot(lo, seg_ref[...])
    r = lax.rsqrt(ss * (1.0 / HEAD_DIM) + EPS)
    return t * r * gain_tiled


def _mix_proj_kernel(x_ref, g_ref, w_ref, qg_ref, kg_ref, seg_ref,
                     xr_ref, gate_ref, q_ref, k_ref, v_ref):
    h = _rms(x_ref[...], g_ref[...]).astype(bf16)
    xr_ref[...] = _dot(h, w_ref[:, 0:D_RNN])
    gate_ref[...] = _dot(h, w_ref[:, D_RNN:2 * D_RNN])
    o = 2 * D_RNN
    q = _dot(h, w_ref[:, o:o + D_ATT])
    k = _dot(h, w_ref[:, o + D_ATT:o + 2 * D_ATT])
    v = _dot(h, w_ref[:, o + 2 * D_ATT:o + 3 * D_ATT])
    scale = 1.0 / math.sqrt(HEAD_DIM)
    q_ref[...] = (_head_norm(q, qg_ref[...], seg_ref) * scale).astype(bf16)
    k_ref[...] = _head_norm(k, kg_ref[...], seg_ref).astype(bf16)
    v_ref[...] = v.astype(bf16)


def _mix_proj(x2d, g, w_in, qg, kg, seg):
    n_tok = x2d.shape[0]
    tok = lambda d: pl.BlockSpec((TM, d), lambda i: (i, 0))
    return pl.pallas_call(
        _mix_proj_kernel,
        out_shape=(jax.ShapeDtypeStruct((n_tok, D_RNN), f32),
                   jax.ShapeDtypeStruct((n_tok, D_RNN), f32),
                   jax.ShapeDtypeStruct((n_tok, D_ATT), bf16),
                   jax.ShapeDtypeStruct((n_tok, D_ATT), bf16),
                   jax.ShapeDtypeStruct((n_tok, D_ATT), bf16)),
        grid=(n_tok // TM,),
        in_specs=[tok(D_MODEL), _const_spec((1, D_MODEL)), _const_spec((D_MODEL, N_IN)),
                  _const_spec((1, D_ATT)), _const_spec((1, D_ATT)), _const_spec((D_ATT, D_ATT))],
        out_specs=(tok(D_RNN), tok(D_RNN), tok(D_ATT), tok(D_ATT), tok(D_ATT)),
        compiler_params=pltpu.CompilerParams(
            dimension_semantics=("parallel",), vmem_limit_bytes=VMEM_LIMIT),
        name="mix_proj",
    )(x2d, g, w_in, qg, kg, seg)


HIST = 8


def _rglru_kernel(xr_ref, gate_ref, cw_ref, cb_ref, wa_ref, ba_ref, wx_ref, bx_ref,
                  lam_ref, g_ref, o_ref, xbuf_ref, hc_ref):
    @pl.when(pl.program_id(1) == 0)
    def _():
        xbuf_ref[0:HIST, :] = jnp.zeros((HIST, D_RNN), f32)
        hc_ref[...] = jnp.zeros((1, D_RNN), f32)

    x = xr_ref[0]
    xbuf_ref[HIST:HIST + TS, :] = x
    xc = cb_ref[...] + cw_ref[CONV_W - 1:CONV_W, :] * x
    for j in range(CONV_W - 1):
        off = HIST - (CONV_W - 1) + j
        xc = xc + cw_ref[j:j + 1, :] * xbuf_ref[off:off + TS, :]
    xbuf_ref[0:HIST, :] = x[TS - HIST:TS, :]

    xcb = xc.astype(bf16)
    r = jax.nn.sigmoid(_dot(xcb, wa_ref[...]) + ba_ref[...])
    i = jax.nn.sigmoid(_dot(xcb, wx_ref[...]) + bx_ref[...])
    lam = lam_ref[...]
    log_sig_lam = jnp.minimum(lam, 0.0) - jnp.log1p(jnp.exp(-jnp.abs(lam)))
    log_a = RG_C * r * log_sig_lam
    a = jnp.exp(log_a)
    th = jnp.tanh(log_a)
    mult = jnp.sqrt(-2.0 * th / (1.0 - th))
    b = mult * (i * xc)

    row = lax.broadcasted_iota(jnp.int32, (TS, D_RNN), 0)
    d = 1
    while d < TS:
        a_sh = pltpu.roll(a, d, axis=0)
        b_sh = pltpu.roll(b, d, axis=0)
        m = row >= d
        b = jnp.where(m, a * b_sh, 0.0) + b
        a = jnp.where(m, a * a_sh, a)
        d *= 2
    h = a * hc_ref[...] + b
    hc_ref[...] = h[TS - 1:TS, :]

    y = h * jax.nn.gelu(gate_ref[0])
    o_ref[0] = _rms(y, g_ref[...]).astype(bf16)


def _rglru(xr, gate, cw, cb, wa, ba, wx, bx, lam, g):
    bsz, seq, _ = xr.shape
    seq_spec = pl.BlockSpec((1, TS, D_RNN), lambda b, s: (b, s, 0))
    vec = _const_spec((1, D_RNN))
    return pl.pallas_call(
        _rglru_kernel,
        out_shape=jax.ShapeDtypeStruct((bsz, seq, D_RNN), bf16),
        grid=(bsz, seq // TS),
        in_specs=[seq_spec, seq_spec, _const_spec((CONV_W, D_RNN)), vec,
                  _const_spec((D_RNN, D_RNN)), vec, _const_spec((D_RNN, D_RNN)), vec, vec, vec],
        out_specs=seq_spec,
        scratch_shapes=[pltpu.VMEM((HIST + TS, D_RNN), f32), pltpu.VMEM((1, D_RNN), f32)],
        compiler_params=pltpu.CompilerParams(
            dimension_semantics=("parallel", "arbitrary"), vmem_limit_bytes=VMEM_LIMIT),
        name="rglru",
    )(xr, gate, cw, cb, wa, ba, wx, bx, lam, g)


PIPE_DEPTH = 4
BLK_PER_TILE = TK // LANES
N_BLK = HEADS_PER_STEP * BLK_PER_TILE
LOG2E = 1.4426950408889634


MASKED = -1e30


def _attn_kernel(qt_ref, kt_ref, q_ref, k_ref, v_ref, tri_ref, o_ref,
                 c_ref, z_ref, split_ref, p_ref, w_ref, d_ref):
    n_items = qt_ref.shape[0]
    heads = range(HEADS_PER_STEP)
    for ref in (o_ref, c_ref, z_ref, split_ref, p_ref, w_ref):
        ref[...] = jnp.zeros_like(ref)
    for b in range(BLK_PER_TILE):
        d_ref[b] = (lax.broadcasted_iota(jnp.int32, (TQ, LANES), 1) + b * LANES
                    - lax.broadcasted_iota(jnp.int32, (TQ, LANES), 0))
    lane_kv = lax.broadcasted_iota(jnp.int32, (TK, LANES), 1) // HEAD_DIM

    def item(j):
        j = jnp.clip(j, 0, n_items - 1)
        return qt_ref[j], kt_ref[j]

    def per_head_rows(x):
        return jnp.concatenate([jnp.where(lane_kv == h, x, jnp.zeros_like(x)) for h in heads], axis=0)

    def stage_c2(qi, t):
        rows = pl.ds(pl.multiple_of(qi * TQ, TQ), TQ)
        v = v_ref[0, pl.ds(pl.multiple_of(t * TK, TK), TK), :]
        acc = jnp.where(t == qi, 0.0, o_ref[0, rows, :])
        w = jnp.concatenate([w_ref[blk] for blk in range(N_BLK)], axis=1)
        o_ref[0, rows, :] = acc + _dot(w, per_head_rows(v))

    def stage_c1(qi, t):
        for h in heads:
            c = jnp.where(t == qi, 0.0, c_ref[h])
            for b in range(BLK_PER_TILE - 1, -1, -1):
                blk = h * BLK_PER_TILE + b
                w_ref[blk] = jnp.exp(p_ref[blk, 0] + c).astype(bf16)
                c = c + p_ref[blk, 1]
            c_ref[h] = c

    def stage_b2(slot):
        for blk in range(N_BLK):
            r = _dot(jnp.concatenate([split_ref[blk, 0], split_ref[blk, 1]], axis=1), tri_ref[...])
            p_ref[blk, 0] = r[:, 0:LANES] + z_ref[slot, blk]
            p_ref[blk, 1] = r[:, LANES:2 * LANES]

    def stage_b1(slot):
        for blk in range(N_BLK):
            z = z_ref[slot, blk]
            sp = jnp.log(1.0 + jnp.exp2(jnp.abs(z) * (-LOG2E))) + jnp.maximum(z, 0.0)
            split_ref[blk, 0], split_ref[blk, 1] = _split_bf16(sp)

    def stage_a(qi, t, slot):
        q = q_ref[0, pl.ds(pl.multiple_of(qi * TQ, TQ), TQ), :]
        k = k_ref[0, pl.ds(pl.multiple_of(t * TK, TK), TK), :]
        z = lax.dot_general(q, per_head_rows(k), (((1,), (1,)), ((), ())),
                            preferred_element_type=f32)
        for b in range(BLK_PER_TILE):
            visible = d_ref[b] < (qi - t) * TQ
            for h in heads:
                blk = h * BLK_PER_TILE + b
                z_ref[slot, blk] = jnp.where(visible, z[:, blk * LANES:(blk + 1) * LANES], MASKED)

    def iteration(i, parity):
        stage_c2(*item(i - PIPE_DEPTH))
        stage_c1(*item(i - 3))
        stage_b2(parity)
        stage_b1(1 - parity)
        stage_a(*item(i), parity)

    def body(m, carry):
        iteration(2 * m, 0)
        iteration(2 * m + 1, 1)
        return carry

    lax.fori_loop(0, (n_items + PIPE_DEPTH) // 2, body, 0)


def _attention(q, k, v, tri):
    bsz, seq, _ = q.shape
    n_qt = seq // TQ
    qt = jnp.asarray([qi for qi in range(n_qt) for _ in range(qi + 1)], jnp.int32)
    kt = jnp.asarray([t for qi in range(n_qt) for t in range(qi, -1, -1)], jnp.int32)
    assert (qt.shape[0] + PIPE_DEPTH) % 2 == 0
    seq_spec = pl.BlockSpec((1, seq, LANES), lambda b, p, *_: (b, 0, p))
    tri_spec = pl.BlockSpec((2 * LANES, 2 * LANES), lambda b, p, *_: (0, 0),
                            pipeline_mode=pl.Buffered(1))
    return pl.pallas_call(
        _attn_kernel,
        out_shape=jax.ShapeDtypeStruct((bsz, seq, D_ATT), f32),
        grid_spec=pltpu.PrefetchScalarGridSpec(
            num_scalar_prefetch=2,
            grid=(bsz, D_ATT // LANES),
            in_specs=[seq_spec, seq_spec, seq_spec, tri_spec],
            out_specs=seq_spec,
            scratch_shapes=[
                pltpu.VMEM((HEADS_PER_STEP, TQ, LANES), f32),
                pltpu.VMEM((2, N_BLK, TQ, LANES), f32),
                pltpu.VMEM((N_BLK, 2, TQ, LANES), bf16),
                pltpu.VMEM((N_BLK, 2, TQ, LANES), f32),
                pltpu.VMEM((N_BLK, TQ, LANES), bf16),
                pltpu.VMEM((BLK_PER_TILE, TQ, LANES), jnp.int32),
            ]),
        compiler_params=pltpu.CompilerParams(
            dimension_semantics=("parallel", "parallel"), vmem_limit_bytes=VMEM_LIMIT),
        name="stickbreak",
    )(qt, kt, q, k, v, tri)


def _out_proj_kernel(x_ref, yr_ref, ya_ref, g_ref, w_ref, o_ref):
    ya = _rms(ya_ref[...], g_ref[...]).astype(bf16)
    o_ref[...] = (x_ref[...] + _dot(yr_ref[...], w_ref[0:D_RNN, :])
                  + _dot(ya, w_ref[D_RNN:D_RNN + D_ATT, :]))


def _out_proj(x2d, yr, ya, g, w_out):
    n_tok = x2d.shape[0]
    tok = lambda d: pl.BlockSpec((TM, d), lambda i: (i, 0))
    return pl.pallas_call(
        _out_proj_kernel,
        out_shape=jax.ShapeDtypeStruct((n_tok, D_MODEL), f32),
        grid=(n_tok // TM,),
        in_specs=[tok(D_MODEL), tok(D_RNN), tok(D_ATT), _const_spec((1, D_ATT)),
                  _const_spec((D_RNN + D_ATT, D_MODEL))],
        out_specs=tok(D_MODEL),
        compiler_params=pltpu.CompilerParams(
            dimension_semantics=("parallel",), vmem_limit_bytes=VMEM_LIMIT),
        name="out_proj",
    )(x2d, yr, ya, g, w_out)


def _block_diag(w):
    eye = jnp.eye(RNN_BLOCKS, dtype=w.dtype)
    return jnp.einsum("ncd,nm->ncmd", w, eye).reshape(D_RNN, D_RNN)


def _scan_matrix():
    j = jnp.arange(LANES)
    suffix = (j[:, None] >= j[None, :]).astype(bf16)
    half = -jnp.concatenate([suffix, jnp.ones((LANES, LANES), bf16)], axis=1)
    return jnp.concatenate([half, half], axis=0)


def kernel(x, ffn1_norm, ffn1_w_gate, ffn1_w_up, ffn1_w_down, mix_norm, w_in, conv_w, conv_b,
           rg_w_a, rg_b_a, rg_w_x, rg_b_x, rg_lambda, q_norm, k_norm, rnn_out_norm,
           attn_out_norm, w_out, ffn2_norm, ffn2_w_gate, ffn2_w_up, ffn2_w_down):
    bsz, seq, _ = x.shape
    depth = ffn1_norm.shape[0]
    head_id = jnp.arange(D_ATT) // HEAD_DIM
    seg = (head_id[:, None] == head_id[None, :]).astype(bf16)
    tri = _scan_matrix()
    x2d = x.reshape(bsz * seq, D_MODEL)
    for l in range(depth):
        x2d = _ffn(x2d, ffn1_norm[l][None], ffn1_w_gate[l].astype(bf16),
                   ffn1_w_up[l].astype(bf16), ffn1_w_down[l].astype(bf16))
        xr, gate, q, k, v = _mix_proj(
            x2d, mix_norm[l][None], w_in[l].astype(bf16),
            jnp.tile(q_norm[l], N_HEADS)[None], jnp.tile(k_norm[l], N_HEADS)[None], seg)
        y_rnn = _rglru(
            xr.reshape(bsz, seq, D_RNN), gate.reshape(bsz, seq, D_RNN),
            conv_w[l], conv_b[l][None], _block_diag(rg_w_a[l]).astype(bf16), rg_b_a[l][None],
            _block_diag(rg_w_x[l]).astype(bf16), rg_b_x[l][None], rg_lambda[l][None],
            rnn_out_norm[l][None])
        y_att = _attention(q.reshape(bsz, seq, D_ATT), k.reshape(bsz, seq, D_ATT),
                           v.reshape(bsz, seq, D_ATT), tri)
        x2d = _out_proj(x2d, y_rnn.reshape(bsz * seq, D_RNN), y_att.reshape(bsz * seq, D_ATT),
                        attn_out_norm[l][None], w_out[l].astype(bf16))
        x2d = _ffn(x2d, ffn2_norm[l][None], ffn2_w_gate[l].astype(bf16),
                   ffn2_w_up[l].astype(bf16), ffn2_w_down[l].astype(bf16))
    return x2d.reshape(bsz, seq, D_MODEL)
```

```python
import math

import jax
import jax.numpy as jnp
from jax import lax
from jax.experimental import pallas as pl
from jax.experimental.pallas import tpu as pltpu

D_MODEL = 1024
D_RNN = 512
RNN_BLOCKS = 8
RNN_BW = D_RNN // RNN_BLOCKS
CONV_W = 4
RG_C = 8.0
D_ATT = 512
HEAD_DIM = 64
N_HEADS = D_ATT // HEAD_DIM
D_FF = 2816
N_IN = 2 * D_RNN + 3 * D_ATT
EPS = 1e-6

LANES = 128
HEADS_PER_STEP = LANES // HEAD_DIM
VMEM_LIMIT = 56 * 1024 * 1024

TM = 512
FF_CHUNK = 256
TS = 512
TQ = 256
TK = 256

f32 = jnp.float32
bf16 = jnp.bfloat16


def _rms(xf, g):
    r = lax.rsqrt(jnp.mean(xf * xf, axis=-1, keepdims=True) + EPS)
    return xf * r * g


def _dot(a, b):
    return jnp.dot(a, b, preferred_element_type=f32)


def _split_bf16(x):
    hi = x.astype(bf16)
    lo = (x - hi.astype(f32)).astype(bf16)
    return hi, lo


def _const_spec(shape):
    nd = len(shape)
    return pl.BlockSpec(shape, lambda *_: (0,) * nd, pipeline_mode=pl.Buffered(1))


def _ffn_kernel(x_ref, g_ref, wg_ref, wu_ref, wd_ref, o_ref, h_ref):
    x = x_ref[...]
    xn = _rms(x, g_ref[...]).astype(bf16)
    for c in range(0, D_FF, FF_CHUNK):
        gate = _dot(xn, wg_ref[:, c:c + FF_CHUNK])
        up = _dot(xn, wu_ref[:, c:c + FF_CHUNK])
        h_ref[:, c:c + FF_CHUNK] = (jax.nn.silu(gate) * up).astype(bf16)
    o_ref[...] = x + 0.5 * _dot(h_ref[...], wd_ref[...])


def _ffn(x2d, g, wg, wu, wd):
    n_tok = x2d.shape[0]
    tok_spec = pl.BlockSpec((TM, D_MODEL), lambda i: (i, 0))
    return pl.pallas_call(
        _ffn_kernel,
        out_shape=jax.ShapeDtypeStruct((n_tok, D_MODEL), f32),
        grid=(n_tok // TM,),
        in_specs=[tok_spec, _const_spec((1, D_MODEL)), _const_spec((D_MODEL, D_FF)),
                  _const_spec((D_MODEL, D_FF)), _const_spec((D_FF, D_MODEL))],
        out_specs=tok_spec,
        scratch_shapes=[pltpu.VMEM((TM, D_FF), bf16)],
        compiler_params=pltpu.CompilerParams(
            dimension_semantics=("parallel",), vmem_limit_bytes=VMEM_LIMIT),
        name="ffn",
    )(x2d, g, wg, wu, wd)


def _head_norm(t, gain_tiled, seg_ref):
    hi, lo = _split_bf16(t * t)
    ss = _dot(hi, seg_ref[...]) + _dot(lo, seg_ref[...])
    r = lax.rsqrt(ss * (1.0 / HEAD_DIM) + EPS)
    return t * r * gain_tiled


def _mix_proj_kernel(x_ref, g_ref, w_ref, qg_ref, kg_ref, seg_ref,
                     xr_ref, gate_ref, q_ref, k_ref, v_ref):
    h = _rms(x_ref[...], g_ref[...]).astype(bf16)
    xr_ref[...] = _dot(h, w_ref[:, 0:D_RNN])
    gate_ref[...] = _dot(h, w_ref[:, D_RNN:2 * D_RNN])
    o = 2 * D_RNN
    q = _dot(h, w_ref[:, o:o + D_ATT])
    k = _dot(h, w_ref[:, o + D_ATT:o + 2 * D_ATT])
    v = _dot(h, w_ref[:, o + 2 * D_ATT:o + 3 * D_ATT])
    scale = 1.0 / math.sqrt(HEAD_DIM)
    q_ref[...] = (_head_norm(q, qg_ref[...], seg_ref) * scale).astype(bf16)
    k_ref[...] = _head_norm(k, kg_ref[...], seg_ref).astype(bf16)
    v_ref[...] = v.astype(bf16)


def _mix_proj(x2d, g, w_in, qg, kg, seg):
    n_tok = x2d.shape[0]
    tok = lambda d: pl.BlockSpec((TM, d), lambda i: (i, 0))
    return pl.pallas_call(
        _mix_proj_kernel,
        out_shape=(jax.ShapeDtypeStruct((n_tok, D_RNN), f32),
                   jax.ShapeDtypeStruct((n_tok, D_RNN), f32),
                   jax.ShapeDtypeStruct((n_tok, D_ATT), bf16),
                   jax.ShapeDtypeStruct((n_tok, D_ATT), bf16),
                   jax.ShapeDtypeStruct((n_tok, D_ATT), bf16)),
        grid=(n_tok // TM,),
        in_specs=[tok(D_MODEL), _const_spec((1, D_MODEL)), _const_spec((D_MODEL, N_IN)),
                  _const_spec((1, D_ATT)), _const_spec((1, D_ATT)), _const_spec((D_ATT, D_ATT))],
        out_specs=(tok(D_RNN), tok(D_RNN), tok(D_ATT), tok(D_ATT), tok(D_ATT)),
        compiler_params=pltpu.CompilerParams(
            dimension_semantics=("parallel",), vmem_limit_bytes=VMEM_LIMIT),
        name="mix_proj",
    )(x2d, g, w_in, qg, kg, seg)


HIST = 8


def _rglru_kernel(xr_ref, gate_ref, cw_ref, cb_ref, wa_ref, ba_ref, wx_ref, bx_ref,
                  lam_ref, g_ref, o_ref, xbuf_ref, hc_ref):
    @pl.when(pl.program_id(1) == 0)
    def _():
        xbuf_ref[0:HIST, :] = jnp.zeros((HIST, D_RNN), f32)
        hc_ref[...] = jnp.zeros((1, D_RNN), f32)

    x = xr_ref[0]
    xbuf_ref[HIST:HIST + TS, :] = x
    xc = cb_ref[...] + cw_ref[CONV_W - 1:CONV_W, :] * x
    for j in range(CONV_W - 1):
        off = HIST - (CONV_W - 1) + j
        xc = xc + cw_ref[j:j + 1, :] * xbuf_ref[off:off + TS, :]
    xbuf_ref[0:HIST, :] = x[TS - HIST:TS, :]

    xcb = xc.astype(bf16)
    r = jax.nn.sigmoid(_dot(xcb, wa_ref[...]) + ba_ref[...])
    i = jax.nn.sigmoid(_dot(xcb, wx_ref[...]) + bx_ref[...])
    lam = lam_ref[...]
    log_sig_lam = jnp.minimum(lam, 0.0) - jnp.log1p(jnp.exp(-jnp.abs(lam)))
    log_a = RG_C * r * log_sig_lam
    a = jnp.exp(log_a)
    th = jnp.tanh(log_a)
    mult = jnp.sqrt(-2.0 * th / (1.0 - th))
    b = mult * (i * xc)

    row = lax.broadcasted_iota(jnp.int32, (TS, D_RNN), 0)
    d = 1
    while d < TS:
        a_sh = pltpu.roll(a, d, axis=0)
        b_sh = pltpu.roll(b, d, axis=0)
        m = row >= d
        b = jnp.where(m, a * b_sh, 0.0) + b
        a = jnp.where(m, a * a_sh, a)
        d *= 2
    h = a * hc_ref[...] + b
    hc_ref[...] = h[TS - 1:TS, :]

    y = h * jax.nn.gelu(gate_ref[0])
    o_ref[0] = _rms(y, g_ref[...]).astype(bf16)


def _rglru(xr, gate, cw, cb, wa, ba, wx, bx, lam, g):
    bsz, seq, _ = xr.shape
    seq_spec = pl.BlockSpec((1, TS, D_RNN), lambda b, s: (b, s, 0))
    vec = _const_spec((1, D_RNN))
    return pl.pallas_call(
        _rglru_kernel,
        out_shape=jax.ShapeDtypeStruct((bsz, seq, D_RNN), bf16),
        grid=(bsz, seq // TS),
        in_specs=[seq_spec, seq_spec, _const_spec((CONV_W, D_RNN)), vec,
                  _const_spec((D_RNN, D_RNN)), vec, _const_spec((D_RNN, D_RNN)), vec, vec, vec],
        out_specs=seq_spec,
        scratch_shapes=[pltpu.VMEM((HIST + TS, D_RNN), f32), pltpu.VMEM((1, D_RNN), f32)],
        compiler_params=pltpu.CompilerParams(
            dimension_semantics=("parallel", "arbitrary"), vmem_limit_bytes=VMEM_LIMIT),
        name="rglru",
    )(xr, gate, cw, cb, wa, ba, wx, bx, lam, g)


PIPE_DEPTH = 4
BLK_PER_TILE = TK // LANES
N_BLK = HEADS_PER_STEP * BLK_PER_TILE
ITER_PER_TRIP = 4
LOG2E = 1.4426950408889634


MASKED = -1e30


def _attn_kernel(qt_ref, kt_ref, q_ref, k_ref, v_ref, tri_ref, o_ref,
                 c_ref, z_ref, split_ref, p_ref, w_ref, d_ref):
    n_items = qt_ref.shape[0]
    heads = range(HEADS_PER_STEP)
    for ref in (o_ref, c_ref, z_ref, split_ref, p_ref, w_ref):
        ref[...] = jnp.zeros_like(ref)
    for b in range(BLK_PER_TILE):
        d_ref[b] = (lax.broadcasted_iota(jnp.int32, (TQ, LANES), 1) + b * LANES
                    - lax.broadcasted_iota(jnp.int32, (TQ, LANES), 0))
    lane_kv = lax.broadcasted_iota(jnp.int32, (TK, LANES), 1) // HEAD_DIM

    def item(j):
        j = jnp.clip(j, 0, n_items - 1)
        return qt_ref[j], kt_ref[j]

    def per_head_rows(x):
        return jnp.concatenate([jnp.where(lane_kv == h, x, jnp.zeros_like(x)) for h in heads], axis=0)

    def stage_c2(qi, t):
        rows = pl.ds(pl.multiple_of(qi * TQ, TQ), TQ)
        v = v_ref[0, pl.ds(pl.multiple_of(t * TK, TK), TK), :]
        acc = jnp.where(t == qi, 0.0, o_ref[0, rows, :])
        w = jnp.concatenate([w_ref[blk] for blk in range(N_BLK)], axis=1)
        o_ref[0, rows, :] = acc + _dot(w, per_head_rows(v))

    def stage_c1(qi, t):
        for h in heads:
            c = jnp.where(t == qi, 0.0, c_ref[h])
            for b in range(BLK_PER_TILE - 1, -1, -1):
                blk = h * BLK_PER_TILE + b
                w_ref[blk] = jnp.exp(p_ref[blk, 0] + c).astype(bf16)
                c = c + p_ref[blk, 1]
            c_ref[h] = c

    def stage_b2(slot):
        for blk in range(N_BLK):
            r = _dot(jnp.concatenate([split_ref[blk, 0], split_ref[blk, 1]], axis=1), tri_ref[...])
            p_ref[blk, 0] = r[:, 0:LANES] + z_ref[slot, blk]
            p_ref[blk, 1] = r[:, LANES:2 * LANES]

    def stage_b1(slot):
        for blk in range(N_BLK):
            z = z_ref[slot, blk]
            sp = jnp.log(1.0 + jnp.exp2(jnp.abs(z) * (-LOG2E))) + jnp.maximum(z, 0.0)
            split_ref[blk, 0], split_ref[blk, 1] = _split_bf16(sp)

    def stage_a(qi, t, slot):
        q = q_ref[0, pl.ds(pl.multiple_of(qi * TQ, TQ), TQ), :]
        k = k_ref[0, pl.ds(pl.multiple_of(t * TK, TK), TK), :]
        z = lax.dot_general(q, per_head_rows(k), (((1,), (1,)), ((), ())),
                            preferred_element_type=f32)
        for b in range(BLK_PER_TILE):
            visible = d_ref[b] < (qi - t) * TQ
            for h in heads:
                blk = h * BLK_PER_TILE + b
                z_ref[slot, blk] = jnp.where(visible, z[:, blk * LANES:(blk + 1) * LANES], MASKED)

    def iteration(i, parity):
        stage_c2(*item(i - PIPE_DEPTH))
        stage_c1(*item(i - 3))
        stage_b2(parity)
        stage_b1(1 - parity)
        stage_a(*item(i), parity)

    def body(m, carry):
        for u in range(ITER_PER_TRIP):
            iteration(ITER_PER_TRIP * m + u, u % 2)
        return carry

    lax.fori_loop(0, (n_items + PIPE_DEPTH) // ITER_PER_TRIP, body, 0)


def _attention(q, k, v, tri):
    bsz, seq, _ = q.shape
    n_qt = seq // TQ
    qt = jnp.asarray([qi for qi in range(n_qt) for _ in range(qi + 1)], jnp.int32)
    kt = jnp.asarray([t for qi in range(n_qt) for t in range(qi, -1, -1)], jnp.int32)
    assert (qt.shape[0] + PIPE_DEPTH) % ITER_PER_TRIP == 0
    seq_spec = pl.BlockSpec((1, seq, LANES), lambda b, p, *_: (b, 0, p))
    tri_spec = pl.BlockSpec((2 * LANES, 2 * LANES), lambda b, p, *_: (0, 0),
                            pipeline_mode=pl.Buffered(1))
    return pl.pallas_call(
        _attn_kernel,
        out_shape=jax.ShapeDtypeStruct((bsz, seq, D_ATT), f32),
        grid_spec=pltpu.PrefetchScalarGridSpec(
            num_scalar_prefetch=2,
            grid=(bsz, D_ATT // LANES),
            in_specs=[seq_spec, seq_spec, seq_spec, tri_spec],
            out_specs=seq_spec,
            scratch_shapes=[
                pltpu.VMEM((HEADS_PER_STEP, TQ, LANES), f32),
                pltpu.VMEM((2, N_BLK, TQ, LANES), f32),
                pltpu.VMEM((N_BLK, 2, TQ, LANES), bf16),
                pltpu.VMEM((N_BLK, 2, TQ, LANES), f32),
                pltpu.VMEM((N_BLK, TQ, LANES), bf16),
                pltpu.VMEM((BLK_PER_TILE, TQ, LANES), jnp.int32),
            ]),
        compiler_params=pltpu.CompilerParams(
            dimension_semantics=("parallel", "parallel"), vmem_limit_bytes=VMEM_LIMIT),
        name="stickbreak",
    )(qt, kt, q, k, v, tri)


def _out_proj_kernel(x_ref, yr_ref, ya_ref, g_ref, w_ref, o_ref):
    ya = _rms(ya_ref[...], g_ref[...]).astype(bf16)
    o_ref[...] = (x_ref[...] + _dot(yr_ref[...], w_ref[0:D_RNN, :])
                  + _dot(ya, w_ref[D_RNN:D_RNN + D_ATT, :]))


def _out_proj(x2d, yr, ya, g, w_out):
    n_tok = x2d.shape[0]
    tok = lambda d: pl.BlockSpec((TM, d), lambda i: (i, 0))
    return pl.pallas_call(
        _out_proj_kernel,
        out_shape=jax.ShapeDtypeStruct((n_tok, D_MODEL), f32),
        grid=(n_tok // TM,),
        in_specs=[tok(D_MODEL), tok(D_RNN), tok(D_ATT), _const_spec((1, D_ATT)),
                  _const_spec((D_RNN + D_ATT, D_MODEL))],
        out_specs=tok(D_MODEL),
        compiler_params=pltpu.CompilerParams(
            dimension_semantics=("parallel",), vmem_limit_bytes=VMEM_LIMIT),
        name="out_proj",
    )(x2d, yr, ya, g, w_out)


def _block_diag(w):
    eye = jnp.eye(RNN_BLOCKS, dtype=w.dtype)
    return jnp.einsum("ncd,nm->ncmd", w, eye).reshape(D_RNN, D_RNN)


def _scan_matrix():
    j = jnp.arange(LANES)
    suffix = (j[:, None] >= j[None, :]).astype(bf16)
    half = -jnp.concatenate([suffix, jnp.ones((LANES, LANES), bf16)], axis=1)
    return jnp.concatenate([half, half], axis=0)


def kernel(x, ffn1_norm, ffn1_w_gate, ffn1_w_up, ffn1_w_down, mix_norm, w_in, conv_w, conv_b,
           rg_w_a, rg_b_a, rg_w_x, rg_b_x, rg_lambda, q_norm, k_norm, rnn_out_norm,
           attn_out_norm, w_out, ffn2_norm, ffn2_w_gate, ffn2_w_up, ffn2_w_down):
    bsz, seq, _ = x.shape
    depth = ffn1_norm.shape[0]
    head_id = jnp.arange(D_ATT) // HEAD_DIM
    seg = (head_id[:, None] == head_id[None, :]).astype(bf16)
    tri = _scan_matrix()
    x2d = x.reshape(bsz * seq, D_MODEL)
    for l in range(depth):
        x2d = _ffn(x2d, ffn1_norm[l][None], ffn1_w_gate[l].astype(bf16),
                   ffn1_w_up[l].astype(bf16), ffn1_w_down[l].astype(bf16))
        xr, gate, q, k, v = _mix_proj(
            x2d, mix_norm[l][None], w_in[l].astype(bf16),
            jnp.tile(q_norm[l], N_HEADS)[None], jnp.tile(k_norm[l], N_HEADS)[None], seg)
        y_rnn = _rglru(
            xr.reshape(bsz, seq, D_RNN), gate.reshape(bsz, seq, D_RNN),
            conv_w[l], conv_b[l][None], _block_diag(rg_w_a[l]).astype(bf16), rg_b_a[l][None],
            _block_diag(rg_w_x[l]).astype(bf16), rg_b_x[l][None], rg_lambda[l][None],
            rnn_out_norm[l][None])
        y_att = _attention(q.reshape(bsz, seq, D_ATT), k.reshape(bsz, seq, D_ATT),
                           v.reshape(bsz, seq, D_ATT), tri)
        x2d = _out_proj(x2d, y_rnn.reshape(bsz * seq, D_RNN), y_att.reshape(bsz * seq, D_ATT),
                        attn_out_norm[l][None], w_out[l].astype(bf16))
        x2d = _ffn(x2d, ffn2_norm[l][None], ffn2_w_gate[l].astype(bf16),
                   ffn2_w_up[l].astype(bf16), ffn2_w_down[l].astype(bf16))
    return x2d.reshape(bsz, seq, D_MODEL)
```

```python
import math

import jax
import jax.numpy as jnp
from jax import lax
from jax.experimental import pallas as pl
from jax.experimental.pallas import tpu as pltpu

D_MODEL = 1024
D_RNN = 512
RNN_BLOCKS = 8
RNN_BW = D_RNN // RNN_BLOCKS
CONV_W = 4
RG_C = 8.0
D_ATT = 512
HEAD_DIM = 64
N_HEADS = D_ATT // HEAD_DIM
D_FF = 2816
N_IN = 2 * D_RNN + 3 * D_ATT
EPS = 1e-6

LANES = 128
HEADS_PER_STEP = LANES // HEAD_DIM
VMEM_LIMIT = 56 * 1024 * 1024

TM = 512
FF_CHUNK = 256
TS = 512
TQ = 256
TK = 256

f32 = jnp.float32
bf16 = jnp.bfloat16


def _rms(xf, g):
    r = lax.rsqrt(jnp.mean(xf * xf, axis=-1, keepdims=True) + EPS)
    return xf * r * g


def _dot(a, b):
    return jnp.dot(a, b, preferred_element_type=f32)


def _split_bf16(x):
    hi = x.astype(bf16)
    lo = (x - hi.astype(f32)).astype(bf16)
    return hi, lo


def _const_spec(shape):
    nd = len(shape)
    return pl.BlockSpec(shape, lambda *_: (0,) * nd, pipeline_mode=pl.Buffered(1))


def _ffn_kernel(x_ref, g_ref, wg_ref, wu_ref, wd_ref, o_ref, h_ref):
    x = x_ref[...]
    xn = _rms(x, g_ref[...]).astype(bf16)
    for c in range(0, D_FF, FF_CHUNK):
        gate = _dot(xn, wg_ref[:, c:c + FF_CHUNK])
        up = _dot(xn, wu_ref[:, c:c + FF_CHUNK])
        h_ref[:, c:c + FF_CHUNK] = (jax.nn.silu(gate) * up).astype(bf16)
    o_ref[...] = x + 0.5 * _dot(h_ref[...], wd_ref[...])


def _ffn(x2d, g, wg, wu, wd):
    n_tok = x2d.shape[0]
    tok_spec = pl.BlockSpec((TM, D_MODEL), lambda i: (i, 0))
    return pl.pallas_call(
        _ffn_kernel,
        out_shape=jax.ShapeDtypeStruct((n_tok, D_MODEL), f32),
        grid=(n_tok // TM,),
        in_specs=[tok_spec, _const_spec((1, D_MODEL)), _const_spec((D_MODEL, D_FF)),
                  _const_spec((D_MODEL, D_FF)), _const_spec((D_FF, D_MODEL))],
        out_specs=tok_spec,
        scratch_shapes=[pltpu.VMEM((TM, D_FF), bf16)],
        compiler_params=pltpu.CompilerParams(
            dimension_semantics=("parallel",), vmem_limit_bytes=VMEM_LIMIT),
        name="ffn",
    )(x2d, g, wg, wu, wd)


def _head_norm(t, gain_tiled, seg_ref):
    hi, lo = _split_bf16(t * t)
    ss = _dot(hi, seg_ref[...]) + _dot(lo, seg_ref[...])
    r = lax.rsqrt(ss * (1.0 / HEAD_DIM) + EPS)
    return t * r * gain_tiled


def _mix_proj_kernel(x_ref, g_ref, w_ref, qg_ref, kg_ref, seg_ref,
                     xr_ref, gate_ref, q_ref, k_ref, v_ref):
    h = _rms(x_ref[...], g_ref[...]).astype(bf16)
    xr_ref[...] = _dot(h, w_ref[:, 0:D_RNN])
    gate_ref[...] = _dot(h, w_ref[:, D_RNN:2 * D_RNN])
    o = 2 * D_RNN
    q = _dot(h, w_ref[:, o:o + D_ATT])
    k = _dot(h, w_ref[:, o + D_ATT:o + 2 * D_ATT])
    v = _dot(h, w_ref[:, o + 2 * D_ATT:o + 3 * D_ATT])
    scale = 1.0 / math.sqrt(HEAD_DIM)
    q_ref[...] = (_head_norm(q, qg_ref[...], seg_ref) * scale).astype(bf16)
    k_ref[...] = _head_norm(k, kg_ref[...], seg_ref).astype(bf16)
    v_ref[...] = v.astype(bf16)


def _mix_proj(x2d, g, w_in, qg, kg, seg):
    n_tok = x2d.shape[0]
    tok = lambda d: pl.BlockSpec((TM, d), lambda i: (i, 0))
    return pl.pallas_call(
        _mix_proj_kernel,
        out_shape=(jax.ShapeDtypeStruct((n_tok, D_RNN), f32),
                   jax.ShapeDtypeStruct((n_tok, D_RNN), f32),
                   jax.ShapeDtypeStruct((n_tok, D_ATT), bf16),
                   jax.ShapeDtypeStruct((n_tok, D_ATT), bf16),
                   jax.ShapeDtypeStruct((n_tok, D_ATT), bf16)),
        grid=(n_tok // TM,),
        in_specs=[tok(D_MODEL), _const_spec((1, D_MODEL)), _const_spec((D_MODEL, N_IN)),
                  _const_spec((1, D_ATT)), _const_spec((1, D_ATT)), _const_spec((D_ATT, D_ATT))],
        out_specs=(tok(D_RNN), tok(D_RNN), tok(D_ATT), tok(D_ATT), tok(D_ATT)),
        compiler_params=pltpu.CompilerParams(
            dimension_semantics=("parallel",), vmem_limit_bytes=VMEM_LIMIT),
        name="mix_proj",
    )(x2d, g, w_in, qg, kg, seg)


HIST = 8


def _rglru_kernel(xr_ref, gate_ref, cw_ref, cb_ref, wa_ref, ba_ref, wx_ref, bx_ref,
                  lam_ref, g_ref, o_ref, xbuf_ref, hc_ref):
    @pl.when(pl.program_id(1) == 0)
    def _():
        xbuf_ref[0:HIST, :] = jnp.zeros((HIST, D_RNN), f32)
        hc_ref[...] = jnp.zeros((1, D_RNN), f32)

    x = xr_ref[0]
    xbuf_ref[HIST:HIST + TS, :] = x
    xc = cb_ref[...] + cw_ref[CONV_W - 1:CONV_W, :] * x
    for j in range(CONV_W - 1):
        off = HIST - (CONV_W - 1) + j
        xc = xc + cw_ref[j:j + 1, :] * xbuf_ref[off:off + TS, :]
    xbuf_ref[0:HIST, :] = x[TS - HIST:TS, :]

    xcb = xc.astype(bf16)
    r = jax.nn.sigmoid(_dot(xcb, wa_ref[...]) + ba_ref[...])
    i = jax.nn.sigmoid(_dot(xcb, wx_ref[...]) + bx_ref[...])
    lam = lam_ref[...]
    log_sig_lam = jnp.minimum(lam, 0.0) - jnp.log1p(jnp.exp(-jnp.abs(lam)))
    log_a = RG_C * r * log_sig_lam
    a = jnp.exp(log_a)
    th = jnp.tanh(log_a)
    mult = jnp.sqrt(-2.0 * th / (1.0 - th))
    b = mult * (i * xc)

    row = lax.broadcasted_iota(jnp.int32, (TS, D_RNN), 0)
    d = 1
    while d < TS:
        a_sh = pltpu.roll(a, d, axis=0)
        b_sh = pltpu.roll(b, d, axis=0)
        m = row >= d
        b = jnp.where(m, a * b_sh, 0.0) + b
        a = jnp.where(m, a * a_sh, a)
        d *= 2
    h = a * hc_ref[...] + b
    hc_ref[...] = h[TS - 1:TS, :]

    y = h * jax.nn.gelu(gate_ref[0])
    o_ref[0] = _rms(y, g_ref[...]).astype(bf16)


def _rglru(xr, gate, cw, cb, wa, ba, wx, bx, lam, g):
    bsz, seq, _ = xr.shape
    seq_spec = pl.BlockSpec((1, TS, D_RNN), lambda b, s: (b, s, 0))
    vec = _const_spec((1, D_RNN))
    return pl.pallas_call(
        _rglru_kernel,
        out_shape=jax.ShapeDtypeStruct((bsz, seq, D_RNN), bf16),
        grid=(bsz, seq // TS),
        in_specs=[seq_spec, seq_spec, _const_spec((CONV_W, D_RNN)), vec,
                  _const_spec((D_RNN, D_RNN)), vec, _const_spec((D_RNN, D_RNN)), vec, vec, vec],
        out_specs=seq_spec,
        scratch_shapes=[pltpu.VMEM((HIST + TS, D_RNN), f32), pltpu.VMEM((1, D_RNN), f32)],
        compiler_params=pltpu.CompilerParams(
            dimension_semantics=("parallel", "arbitrary"), vmem_limit_bytes=VMEM_LIMIT),
        name="rglru",
    )(xr, gate, cw, cb, wa, ba, wx, bx, lam, g)


PIPE_DEPTH = 4
BLK_PER_TILE = TK // LANES
N_BLK = HEADS_PER_STEP * BLK_PER_TILE
ITER_PER_TRIP = 4
LOG2E = 1.4426950408889634


MASKED = -1e30


def _attn_kernel(qt_ref, kt_ref, q_ref, k_ref, v_ref, tri_ref, o_ref,
                 c_ref, z_ref, split_ref, p_ref, w_ref, d_ref, kh_ref, vh_ref):
    n_items = qt_ref.shape[0]
    seq = k_ref.shape[1]
    heads = range(HEADS_PER_STEP)
    for ref in (o_ref, c_ref, z_ref, split_ref, p_ref, w_ref):
        ref[...] = jnp.zeros_like(ref)
    for b in range(BLK_PER_TILE):
        d_ref[b] = (lax.broadcasted_iota(jnp.int32, (TQ, LANES), 1) + b * LANES
                    - lax.broadcasted_iota(jnp.int32, (TQ, LANES), 0))
    lane_head = lax.broadcasted_iota(jnp.int32, (seq, LANES), 1) // HEAD_DIM
    for h in heads:
        kh_ref[h] = jnp.where(lane_head == h, k_ref[0], jnp.zeros_like(k_ref[0]))
        vh_ref[h] = jnp.where(lane_head == h, v_ref[0], jnp.zeros_like(v_ref[0]))

    def item(j):
        j = jnp.clip(j, 0, n_items - 1)
        return qt_ref[j], kt_ref[j]

    def per_head_rows(ref, t):
        rows = pl.ds(pl.multiple_of(t * TK, TK), TK)
        return jnp.concatenate([ref[h, rows, :] for h in heads], axis=0)

    def stage_c2(qi, t):
        rows = pl.ds(pl.multiple_of(qi * TQ, TQ), TQ)
        acc = jnp.where(t == qi, 0.0, o_ref[0, rows, :])
        w = jnp.concatenate([w_ref[blk] for blk in range(N_BLK)], axis=1)
        o_ref[0, rows, :] = acc + _dot(w, per_head_rows(vh_ref, t))

    def stage_c1(qi, t, slot):
        for h in heads:
            first = h * BLK_PER_TILE
            c = jnp.where(t == qi, 0.0, c_ref[h])
            for blk in range(first, first + BLK_PER_TILE):
                w_ref[blk] = jnp.exp(z_ref[slot, blk] + p_ref[blk] + c).astype(bf16)
            c_ref[h] = c + jnp.broadcast_to(p_ref[first, :, 0:1], (TQ, LANES))

    def stage_b2():
        for h in heads:
            first = h * BLK_PER_TILE
            parts = [split_ref[blk, part] for blk in range(first, first + BLK_PER_TILE)
                     for part in range(2)]
            r = _dot(jnp.concatenate(parts, axis=1), tri_ref[...])
            for b in range(BLK_PER_TILE):
                p_ref[first + b] = r[:, b * LANES:(b + 1) * LANES]

    def stage_b1(slot):
        for blk in range(N_BLK):
            z = z_ref[slot, blk]
            sp = jnp.log(1.0 + jnp.exp2(jnp.abs(z) * (-LOG2E))) + jnp.maximum(z, 0.0)
            split_ref[blk, 0], split_ref[blk, 1] = _split_bf16(sp)

    def stage_a(qi, t, slot):
        q = q_ref[0, pl.ds(pl.multiple_of(qi * TQ, TQ), TQ), :]
        z = lax.dot_general(q, per_head_rows(kh_ref, t), (((1,), (1,)), ((), ())),
                            preferred_element_type=f32)
        for b in range(BLK_PER_TILE):
            visible = d_ref[b] < (qi - t) * TQ
            for h in heads:
                blk = h * BLK_PER_TILE + b
                z_ref[slot, blk] = jnp.where(visible, z[:, blk * LANES:(blk + 1) * LANES], MASKED)

    def iteration(i, u):
        stage_c2(*item(i - PIPE_DEPTH))
        stage_c1(*item(i - 3), (u - 3) % ITER_PER_TRIP)
        stage_b2()
        stage_b1((u - 1) % ITER_PER_TRIP)
        stage_a(*item(i), u)

    def body(m, carry):
        for u in range(ITER_PER_TRIP):
            iteration(ITER_PER_TRIP * m + u, u)
        return carry

    lax.fori_loop(0, (n_items + PIPE_DEPTH) // ITER_PER_TRIP, body, 0)


def _attention(q, k, v, tri):
    bsz, seq, _ = q.shape
    n_qt = seq // TQ
    qt = jnp.asarray([qi for qi in range(n_qt) for _ in range(qi + 1)], jnp.int32)
    kt = jnp.asarray([t for qi in range(n_qt) for t in range(qi, -1, -1)], jnp.int32)
    assert (qt.shape[0] + PIPE_DEPTH) % ITER_PER_TRIP == 0
    seq_spec = pl.BlockSpec((1, seq, LANES), lambda b, p, *_: (b, 0, p))
    tri_spec = pl.BlockSpec((2 * TK, TK), lambda b, p, *_: (0, 0),
                            pipeline_mode=pl.Buffered(1))
    return pl.pallas_call(
        _attn_kernel,
        out_shape=jax.ShapeDtypeStruct((bsz, seq, D_ATT), f32),
        grid_spec=pltpu.PrefetchScalarGridSpec(
            num_scalar_prefetch=2,
            grid=(bsz, D_ATT // LANES),
            in_specs=[seq_spec, seq_spec, seq_spec, tri_spec],
            out_specs=seq_spec,
            scratch_shapes=[
                pltpu.VMEM((HEADS_PER_STEP, TQ, LANES), f32),
                pltpu.VMEM((ITER_PER_TRIP, N_BLK, TQ, LANES), f32),
                pltpu.VMEM((N_BLK, 2, TQ, LANES), bf16),
                pltpu.VMEM((N_BLK, TQ, LANES), f32),
                pltpu.VMEM((N_BLK, TQ, LANES), bf16),
                pltpu.VMEM((BLK_PER_TILE, TQ, LANES), jnp.int32),
                pltpu.VMEM((HEADS_PER_STEP, seq, LANES), bf16),
                pltpu.VMEM((HEADS_PER_STEP, seq, LANES), bf16),
            ]),
        compiler_params=pltpu.CompilerParams(
            dimension_semantics=("parallel", "parallel"), vmem_limit_bytes=VMEM_LIMIT),
        name="stickbreak",
    )(qt, kt, q, k, v, tri)


def _out_proj_kernel(x_ref, yr_ref, ya_ref, g_ref, w_ref, o_ref):
    ya = _rms(ya_ref[...], g_ref[...]).astype(bf16)
    o_ref[...] = (x_ref[...] + _dot(yr_ref[...], w_ref[0:D_RNN, :])
                  + _dot(ya, w_ref[D_RNN:D_RNN + D_ATT, :]))


def _out_proj(x2d, yr, ya, g, w_out):
    n_tok = x2d.shape[0]
    tok = lambda d: pl.BlockSpec((TM, d), lambda i: (i, 0))
    return pl.pallas_call(
        _out_proj_kernel,
        out_shape=jax.ShapeDtypeStruct((n_tok, D_MODEL), f32),
        grid=(n_tok // TM,),
        in_specs=[tok(D_MODEL), tok(D_RNN), tok(D_ATT), _const_spec((1, D_ATT)),
                  _const_spec((D_RNN + D_ATT, D_MODEL))],
        out_specs=tok(D_MODEL),
        compiler_params=pltpu.CompilerParams(
            dimension_semantics=("parallel",), vmem_limit_bytes=VMEM_LIMIT),
        name="out_proj",
    )(x2d, yr, ya, g, w_out)


def _block_diag(w):
    eye = jnp.eye(RNN_BLOCKS, dtype=w.dtype)
    return jnp.einsum("ncd,nm->ncmd", w, eye).reshape(D_RNN, D_RNN)


def _scan_matrix():
    j = jnp.arange(TK)
    minus_suffix = -(j[:, None] >= j[None, :]).astype(bf16)
    blocks = minus_suffix.reshape(BLK_PER_TILE, LANES, TK)
    return jnp.concatenate([blocks[b] for b in range(BLK_PER_TILE) for _ in range(2)], axis=0)


def kernel(x, ffn1_norm, ffn1_w_gate, ffn1_w_up, ffn1_w_down, mix_norm, w_in, conv_w, conv_b,
           rg_w_a, rg_b_a, rg_w_x, rg_b_x, rg_lambda, q_norm, k_norm, rnn_out_norm,
           attn_out_norm, w_out, ffn2_norm, ffn2_w_gate, ffn2_w_up, ffn2_w_down):
    bsz, seq, _ = x.shape
    depth = ffn1_norm.shape[0]
    head_id = jnp.arange(D_ATT) // HEAD_DIM
    seg = (head_id[:, None] == head_id[None, :]).astype(bf16)
    tri = _scan_matrix()
    x2d = x.reshape(bsz * seq, D_MODEL)
    for l in range(depth):
        x2d = _ffn(x2d, ffn1_norm[l][None], ffn1_w_gate[l].astype(bf16),
                   ffn1_w_up[l].astype(bf16), ffn1_w_down[l].astype(bf16))
        xr, gate, q, k, v = _mix_proj(
            x2d, mix_norm[l][None], w_in[l].astype(bf16),
            jnp.tile(q_norm[l], N_HEADS)[None], jnp.tile(k_norm[l], N_HEADS)[None], seg)
        y_rnn = _rglru(
            xr.reshape(bsz, seq, D_RNN), gate.reshape(bsz, seq, D_RNN),
            conv_w[l], conv_b[l][None], _block_diag(rg_w_a[l]).astype(bf16), rg_b_a[l][None],
            _block_diag(rg_w_x[l]).astype(bf16), rg_b_x[l][None], rg_lambda[l][None],
            rnn_out_norm[l][None])
        y_att = _attention(q.reshape(bsz, seq, D_ATT), k.reshape(bsz, seq, D_ATT),
                           v.reshape(bsz, seq, D_ATT), tri)
        x2d = _out_proj(x2d, y_rnn.reshape(bsz * seq, D_RNN), y_att.reshape(bsz * seq, D_ATT),
                        attn_out_norm[l][None], w_out[l].astype(bf16))
        x2d = _ffn(x2d, ffn2_norm[l][None], ffn2_w_gate[l].astype(bf16),
                   ffn2_w_up[l].astype(bf16), ffn2_w_down[l].astype(bf16))
    return x2d.reshape(bsz, seq, D_MODEL)
```

```python
import math

import jax
import jax.numpy as jnp
from jax import lax
from jax.experimental import pallas as pl
from jax.experimental.pallas import tpu as pltpu

D_MODEL = 1024
D_RNN = 512
RNN_BLOCKS = 8
RNN_BW = D_RNN // RNN_BLOCKS
CONV_W = 4
RG_C = 8.0
D_ATT = 512
HEAD_DIM = 64
N_HEADS = D_ATT // HEAD_DIM
D_FF = 2816
N_IN = 2 * D_RNN + 3 * D_ATT
EPS = 1e-6

LANES = 128
HEADS_PER_STEP = LANES // HEAD_DIM
VMEM_LIMIT = 56 * 1024 * 1024

TM = 512
FF_CHUNK = 256
TS = 512
TQ = 256
TK = 256

f32 = jnp.float32
bf16 = jnp.bfloat16


def _rms(xf, g):
    r = lax.rsqrt(jnp.mean(xf * xf, axis=-1, keepdims=True) + EPS)
    return xf * r * g


def _dot(a, b):
    return jnp.dot(a, b, preferred_element_type=f32)


def _split_bf16(x):
    hi = x.astype(bf16)
    lo = (x - hi.astype(f32)).astype(bf16)
    return hi, lo


def _const_spec(shape):
    nd = len(shape)
    return pl.BlockSpec(shape, lambda *_: (0,) * nd, pipeline_mode=pl.Buffered(1))


def _ffn_kernel(x_ref, g_ref, wg_ref, wu_ref, wd_ref, o_ref, h_ref):
    x = x_ref[...]
    xn = _rms(x, g_ref[...]).astype(bf16)
    for c in range(0, D_FF, FF_CHUNK):
        gate = _dot(xn, wg_ref[:, c:c + FF_CHUNK])
        up = _dot(xn, wu_ref[:, c:c + FF_CHUNK])
        h_ref[:, c:c + FF_CHUNK] = (jax.nn.silu(gate) * up).astype(bf16)
    o_ref[...] = x + 0.5 * _dot(h_ref[...], wd_ref[...])


def _ffn(x2d, g, wg, wu, wd):
    n_tok = x2d.shape[0]
    tok_spec = pl.BlockSpec((TM, D_MODEL), lambda i: (i, 0))
    return pl.pallas_call(
        _ffn_kernel,
        out_shape=jax.ShapeDtypeStruct((n_tok, D_MODEL), f32),
        grid=(n_tok // TM,),
        in_specs=[tok_spec, _const_spec((1, D_MODEL)), _const_spec((D_MODEL, D_FF)),
                  _const_spec((D_MODEL, D_FF)), _const_spec((D_FF, D_MODEL))],
        out_specs=tok_spec,
        scratch_shapes=[pltpu.VMEM((TM, D_FF), bf16)],
        compiler_params=pltpu.CompilerParams(
            dimension_semantics=("parallel",), vmem_limit_bytes=VMEM_LIMIT),
        name="ffn",
    )(x2d, g, wg, wu, wd)


def _head_norm(t, gain_tiled, seg_ref):
    hi, lo = _split_bf16(t * t)
    ss = _dot(hi, seg_ref[...]) + _dot(lo, seg_ref[...])
    r = lax.rsqrt(ss * (1.0 / HEAD_DIM) + EPS)
    return t * r * gain_tiled


def _mix_proj_kernel(x_ref, g_ref, w_ref, qg_ref, kg_ref, seg_ref,
                     xr_ref, gate_ref, q_ref, k_ref, v_ref):
    h = _rms(x_ref[...], g_ref[...]).astype(bf16)
    xr_ref[...] = _dot(h, w_ref[:, 0:D_RNN])
    gate_ref[...] = _dot(h, w_ref[:, D_RNN:2 * D_RNN])
    o = 2 * D_RNN
    q = _dot(h, w_ref[:, o:o + D_ATT])
    k = _dot(h, w_ref[:, o + D_ATT:o + 2 * D_ATT])
    v = _dot(h, w_ref[:, o + 2 * D_ATT:o + 3 * D_ATT])
    scale = 1.0 / math.sqrt(HEAD_DIM)
    q_ref[...] = (_head_norm(q, qg_ref[...], seg_ref) * scale).astype(bf16)
    k_ref[...] = _head_norm(k, kg_ref[...], seg_ref).astype(bf16)
    v_ref[...] = v.astype(bf16)


def _mix_proj(x2d, g, w_in, qg, kg, seg):
    n_tok = x2d.shape[0]
    tok = lambda d: pl.BlockSpec((TM, d), lambda i: (i, 0))
    return pl.pallas_call(
        _mix_proj_kernel,
        out_shape=(jax.ShapeDtypeStruct((n_tok, D_RNN), f32),
                   jax.ShapeDtypeStruct((n_tok, D_RNN), f32),
                   jax.ShapeDtypeStruct((n_tok, D_ATT), bf16),
                   jax.ShapeDtypeStruct((n_tok, D_ATT), bf16),
                   jax.ShapeDtypeStruct((n_tok, D_ATT), bf16)),
        grid=(n_tok // TM,),
        in_specs=[tok(D_MODEL), _const_spec((1, D_MODEL)), _const_spec((D_MODEL, N_IN)),
                  _const_spec((1, D_ATT)), _const_spec((1, D_ATT)), _const_spec((D_ATT, D_ATT))],
        out_specs=(tok(D_RNN), tok(D_RNN), tok(D_ATT), tok(D_ATT), tok(D_ATT)),
        compiler_params=pltpu.CompilerParams(
            dimension_semantics=("parallel",), vmem_limit_bytes=VMEM_LIMIT),
        name="mix_proj",
    )(x2d, g, w_in, qg, kg, seg)


HIST = 8


def _rglru_kernel(xr_ref, gate_ref, cw_ref, cb_ref, wa_ref, ba_ref, wx_ref, bx_ref,
                  lam_ref, g_ref, o_ref, xbuf_ref, hc_ref):
    @pl.when(pl.program_id(1) == 0)
    def _():
        xbuf_ref[0:HIST, :] = jnp.zeros((HIST, D_RNN), f32)
        hc_ref[...] = jnp.zeros((1, D_RNN), f32)

    x = xr_ref[0]
    xbuf_ref[HIST:HIST + TS, :] = x
    xc = cb_ref[...] + cw_ref[CONV_W - 1:CONV_W, :] * x
    for j in range(CONV_W - 1):
        off = HIST - (CONV_W - 1) + j
        xc = xc + cw_ref[j:j + 1, :] * xbuf_ref[off:off + TS, :]
    xbuf_ref[0:HIST, :] = x[TS - HIST:TS, :]

    xcb = xc.astype(bf16)
    r = jax.nn.sigmoid(_dot(xcb, wa_ref[...]) + ba_ref[...])
    i = jax.nn.sigmoid(_dot(xcb, wx_ref[...]) + bx_ref[...])
    lam = lam_ref[...]
    log_sig_lam = jnp.minimum(lam, 0.0) - jnp.log1p(jnp.exp(-jnp.abs(lam)))
    log_a = RG_C * r * log_sig_lam
    a = jnp.exp(log_a)
    th = jnp.tanh(log_a)
    mult = jnp.sqrt(-2.0 * th / (1.0 - th))
    b = mult * (i * xc)

    row = lax.broadcasted_iota(jnp.int32, (TS, D_RNN), 0)
    d = 1
    while d < TS:
        a_sh = pltpu.roll(a, d, axis=0)
        b_sh = pltpu.roll(b, d, axis=0)
        m = row >= d
        b = jnp.where(m, a * b_sh, 0.0) + b
        a = jnp.where(m, a * a_sh, a)
        d *= 2
    h = a * hc_ref[...] + b
    hc_ref[...] = h[TS - 1:TS, :]

    y = h * jax.nn.gelu(gate_ref[0])
    o_ref[0] = _rms(y, g_ref[...]).astype(bf16)


def _rglru(xr, gate, cw, cb, wa, ba, wx, bx, lam, g):
    bsz, seq, _ = xr.shape
    seq_spec = pl.BlockSpec((1, TS, D_RNN), lambda b, s: (b, s, 0))
    vec = _const_spec((1, D_RNN))
    return pl.pallas_call(
        _rglru_kernel,
        out_shape=jax.ShapeDtypeStruct((bsz, seq, D_RNN), bf16),
        grid=(bsz, seq // TS),
        in_specs=[seq_spec, seq_spec, _const_spec((CONV_W, D_RNN)), vec,
                  _const_spec((D_RNN, D_RNN)), vec, _const_spec((D_RNN, D_RNN)), vec, vec, vec],
        out_specs=seq_spec,
        scratch_shapes=[pltpu.VMEM((HIST + TS, D_RNN), f32), pltpu.VMEM((1, D_RNN), f32)],
        compiler_params=pltpu.CompilerParams(
            dimension_semantics=("parallel", "arbitrary"), vmem_limit_bytes=VMEM_LIMIT),
        name="rglru",
    )(xr, gate, cw, cb, wa, ba, wx, bx, lam, g)


PIPE_DEPTH = 4
BLK_PER_TILE = TK // LANES
N_BLK = HEADS_PER_STEP * BLK_PER_TILE
ITER_PER_TRIP = 4
LOG2E = 1.4426950408889634


MASKED = -1e30


def _attn_kernel(qt_ref, kt_ref, q_ref, k_ref, v_ref, tri_ref, o_ref,
                 c_ref, z_ref, split_ref, p_ref, w_ref, d_ref, kh_ref, vh_ref):
    n_items = qt_ref.shape[0]
    seq = k_ref.shape[1]
    heads = range(HEADS_PER_STEP)
    for ref in (o_ref, c_ref, z_ref, split_ref, p_ref, w_ref):
        ref[...] = jnp.zeros_like(ref)
    for b in range(BLK_PER_TILE):
        d_ref[b] = (lax.broadcasted_iota(jnp.int32, (TQ, LANES), 1) + b * LANES
                    - lax.broadcasted_iota(jnp.int32, (TQ, LANES), 0))
    lane_head = lax.broadcasted_iota(jnp.int32, (seq, LANES), 1) // HEAD_DIM
    for h in heads:
        kh_ref[h] = jnp.where(lane_head == h, k_ref[0], jnp.zeros_like(k_ref[0]))
        vh_ref[h] = jnp.where(lane_head == h, v_ref[0], jnp.zeros_like(v_ref[0]))

    def item(j):
        j = jnp.clip(j, 0, n_items - 1)
        return qt_ref[j], kt_ref[j]

    def per_head_rows(ref, t):
        rows = pl.ds(pl.multiple_of(t * TK, TK), TK)
        return jnp.concatenate([ref[h, rows, :] for h in heads], axis=0)

    def stage_c2(qi, t):
        rows = pl.ds(pl.multiple_of(qi * TQ, TQ), TQ)
        acc = jnp.where(t == qi, 0.0, o_ref[0, rows, :])
        w = jnp.concatenate([w_ref[blk] for blk in range(N_BLK)], axis=1)
        o_ref[0, rows, :] = acc + _dot(w, per_head_rows(vh_ref, t))

    def stage_c1(qi, t, slot):
        for h in heads:
            first = h * BLK_PER_TILE
            c = jnp.where(t == qi, 0.0, c_ref[h])
            for blk in range(first, first + BLK_PER_TILE):
                w_ref[blk] = jnp.exp(z_ref[slot, blk] + p_ref[blk] + c).astype(bf16)
            c_ref[h] = c + jnp.broadcast_to(p_ref[first, :, 0:1], (TQ, LANES))

    def stage_b2():
        for h in heads:
            first = h * BLK_PER_TILE
            parts = [split_ref[blk] for blk in range(first, first + BLK_PER_TILE)]
            r = _dot(jnp.concatenate(parts, axis=1), tri_ref[...])
            for b in range(BLK_PER_TILE):
                p_ref[first + b] = r[:, b * LANES:(b + 1) * LANES]

    def stage_b1(slot):
        for blk in range(N_BLK):
            z = z_ref[slot, blk]
            sp = jnp.log(1.0 + jnp.exp2(jnp.abs(z) * (-LOG2E))) + jnp.maximum(z, 0.0)
            split_ref[blk] = sp.astype(bf16)

    def stage_a(qi, t, slot):
        q = q_ref[0, pl.ds(pl.multiple_of(qi * TQ, TQ), TQ), :]
        z = lax.dot_general(q, per_head_rows(kh_ref, t), (((1,), (1,)), ((), ())),
                            preferred_element_type=f32)
        for b in range(BLK_PER_TILE):
            visible = d_ref[b] < (qi - t) * TQ
            for h in heads:
                blk = h * BLK_PER_TILE + b
                z_ref[slot, blk] = jnp.where(visible, z[:, blk * LANES:(blk + 1) * LANES], MASKED)

    def iteration(i, u):
        stage_c2(*item(i - PIPE_DEPTH))
        stage_c1(*item(i - 3), (u - 3) % ITER_PER_TRIP)
        stage_b2()
        stage_b1((u - 1) % ITER_PER_TRIP)
        stage_a(*item(i), u)

    def body(m, carry):
        for u in range(ITER_PER_TRIP):
            iteration(ITER_PER_TRIP * m + u, u)
        return carry

    lax.fori_loop(0, (n_items + PIPE_DEPTH) // ITER_PER_TRIP, body, 0)


def _attention(q, k, v, tri):
    bsz, seq, _ = q.shape
    n_qt = seq // TQ
    qt = jnp.asarray([qi for qi in range(n_qt) for _ in range(qi + 1)], jnp.int32)
    kt = jnp.asarray([t for qi in range(n_qt) for t in range(qi, -1, -1)], jnp.int32)
    assert (qt.shape[0] + PIPE_DEPTH) % ITER_PER_TRIP == 0
    seq_spec = pl.BlockSpec((1, seq, LANES), lambda b, p, *_: (b, 0, p))
    tri_spec = pl.BlockSpec((TK, TK), lambda b, p, *_: (0, 0),
                            pipeline_mode=pl.Buffered(1))
    return pl.pallas_call(
        _attn_kernel,
        out_shape=jax.ShapeDtypeStruct((bsz, seq, D_ATT), f32),
        grid_spec=pltpu.PrefetchScalarGridSpec(
            num_scalar_prefetch=2,
            grid=(bsz, D_ATT // LANES),
            in_specs=[seq_spec, seq_spec, seq_spec, tri_spec],
            out_specs=seq_spec,
            scratch_shapes=[
                pltpu.VMEM((HEADS_PER_STEP, TQ, LANES), f32),
                pltpu.VMEM((ITER_PER_TRIP, N_BLK, TQ, LANES), f32),
                pltpu.VMEM((N_BLK, TQ, LANES), bf16),
                pltpu.VMEM((N_BLK, TQ, LANES), f32),
                pltpu.VMEM((N_BLK, TQ, LANES), bf16),
                pltpu.VMEM((BLK_PER_TILE, TQ, LANES), jnp.int32),
                pltpu.VMEM((HEADS_PER_STEP, seq, LANES), bf16),
                pltpu.VMEM((HEADS_PER_STEP, seq, LANES), bf16),
            ]),
        compiler_params=pltpu.CompilerParams(
            dimension_semantics=("parallel", "parallel"), vmem_limit_bytes=VMEM_LIMIT),
        name="stickbreak",
    )(qt, kt, q, k, v, tri)


def _out_proj_kernel(x_ref, yr_ref, ya_ref, g_ref, w_ref, o_ref):
    ya = _rms(ya_ref[...], g_ref[...]).astype(bf16)
    o_ref[...] = (x_ref[...] + _dot(yr_ref[...], w_ref[0:D_RNN, :])
                  + _dot(ya, w_ref[D_RNN:D_RNN + D_ATT, :]))


def _out_proj(x2d, yr, ya, g, w_out):
    n_tok = x2d.shape[0]
    tok = lambda d: pl.BlockSpec((TM, d), lambda i: (i, 0))
    return pl.pallas_call(
        _out_proj_kernel,
        out_shape=jax.ShapeDtypeStruct((n_tok, D_MODEL), f32),
        grid=(n_tok // TM,),
        in_specs=[tok(D_MODEL), tok(D_RNN), tok(D_ATT), _const_spec((1, D_ATT)),
                  _const_spec((D_RNN + D_ATT, D_MODEL))],
        out_specs=tok(D_MODEL),
        compiler_params=pltpu.CompilerParams(
            dimension_semantics=("parallel",), vmem_limit_bytes=VMEM_LIMIT),
        name="out_proj",
    )(x2d, yr, ya, g, w_out)


def _block_diag(w):
    eye = jnp.eye(RNN_BLOCKS, dtype=w.dtype)
    return jnp.einsum("ncd,nm->ncmd", w, eye).reshape(D_RNN, D_RNN)


def _scan_matrix():
    j = jnp.arange(TK)
    return -(j[:, None] >= j[None, :]).astype(bf16)


def kernel(x, ffn1_norm, ffn1_w_gate, ffn1_w_up, ffn1_w_down, mix_norm, w_in, conv_w, conv_b,
           rg_w_a, rg_b_a, rg_w_x, rg_b_x, rg_lambda, q_norm, k_norm, rnn_out_norm,
           attn_out_norm, w_out, ffn2_norm, ffn2_w_gate, ffn2_w_up, ffn2_w_down):
    bsz, seq, _ = x.shape
    depth = ffn1_norm.shape[0]
    head_id = jnp.arange(D_ATT) // HEAD_DIM
    seg = (head_id[:, None] == head_id[None, :]).astype(bf16)
    tri = _scan_matrix()
    x2d = x.reshape(bsz * seq, D_MODEL)
    for l in range(depth):
        x2d = _ffn(x2d, ffn1_norm[l][None], ffn1_w_gate[l].astype(bf16),
                   ffn1_w_up[l].astype(bf16), ffn1_w_down[l].astype(bf16))
        xr, gate, q, k, v = _mix_proj(
            x2d, mix_norm[l][None], w_in[l].astype(bf16),
            jnp.tile(q_norm[l], N_HEADS)[None], jnp.tile(k_norm[l], N_HEADS)[None], seg)
        y_rnn = _rglru(
            xr.reshape(bsz, seq, D_RNN), gate.reshape(bsz, seq, D_RNN),
            conv_w[l], conv_b[l][None], _block_diag(rg_w_a[l]).astype(bf16), rg_b_a[l][None],
            _block_diag(rg_w_x[l]).astype(bf16), rg_b_x[l][None], rg_lambda[l][None],
            rnn_out_norm[l][None])
        y_att = _attention(q.reshape(bsz, seq, D_ATT), k.reshape(bsz, seq, D_ATT),
                           v.reshape(bsz, seq, D_ATT), tri)
        x2d = _out_proj(x2d, y_rnn.reshape(bsz * seq, D_RNN), y_att.reshape(bsz * seq, D_ATT),
                        attn_out_norm[l][None], w_out[l].astype(bf16))
        x2d = _ffn(x2d, ffn2_norm[l][None], ffn2_w_gate[l].astype(bf16),
                   ffn2_w_up[l].astype(bf16), ffn2_w_down[l].astype(bf16))
    return x2d.reshape(bsz, seq, D_MODEL)
```

```python
import math

import jax
import jax.numpy as jnp
from jax import lax
from jax.experimental import pallas as pl
from jax.experimental.pallas import tpu as pltpu

D_MODEL = 1024
D_RNN = 512
RNN_BLOCKS = 8
RNN_BW = D_RNN // RNN_BLOCKS
CONV_W = 4
RG_C = 8.0
D_ATT = 512
HEAD_DIM = 64
N_HEADS = D_ATT // HEAD_DIM
D_FF = 2816
N_IN = 2 * D_RNN + 3 * D_ATT
EPS = 1e-6

LANES = 128
HEADS_PER_STEP = LANES // HEAD_DIM
VMEM_LIMIT = 56 * 1024 * 1024

TM = 512
FF_CHUNK = 256
TS = 512
TQ = 256
TK = 256

f32 = jnp.float32
bf16 = jnp.bfloat16


def _rms(xf, g):
    r = lax.rsqrt(jnp.mean(xf * xf, axis=-1, keepdims=True) + EPS)
    return xf * r * g


def _dot(a, b):
    return jnp.dot(a, b, preferred_element_type=f32)


def _split_bf16(x):
    hi = x.astype(bf16)
    lo = (x - hi.astype(f32)).astype(bf16)
    return hi, lo


def _const_spec(shape):
    nd = len(shape)
    return pl.BlockSpec(shape, lambda *_: (0,) * nd, pipeline_mode=pl.Buffered(1))


def _ffn_kernel(x_ref, g_ref, wg_ref, wu_ref, wd_ref, o_ref, h_ref):
    x = x_ref[...]
    xn = _rms(x, g_ref[...]).astype(bf16)
    for c in range(0, D_FF, FF_CHUNK):
        gate = _dot(xn, wg_ref[:, c:c + FF_CHUNK])
        up = _dot(xn, wu_ref[:, c:c + FF_CHUNK])
        h_ref[:, c:c + FF_CHUNK] = (jax.nn.silu(gate) * up).astype(bf16)
    o_ref[...] = x + 0.5 * _dot(h_ref[...], wd_ref[...])


def _ffn(x2d, g, wg, wu, wd):
    n_tok = x2d.shape[0]
    tok_spec = pl.BlockSpec((TM, D_MODEL), lambda i: (i, 0))
    return pl.pallas_call(
        _ffn_kernel,
        out_shape=jax.ShapeDtypeStruct((n_tok, D_MODEL), f32),
        grid=(n_tok // TM,),
        in_specs=[tok_spec, _const_spec((1, D_MODEL)), _const_spec((D_MODEL, D_FF)),
                  _const_spec((D_MODEL, D_FF)), _const_spec((D_FF, D_MODEL))],
        out_specs=tok_spec,
        scratch_shapes=[pltpu.VMEM((TM, D_FF), bf16)],
        compiler_params=pltpu.CompilerParams(
            dimension_semantics=("parallel",), vmem_limit_bytes=VMEM_LIMIT),
        name="ffn",
    )(x2d, g, wg, wu, wd)


def _head_norm(t, gain_tiled, seg_ref):
    hi, lo = _split_bf16(t * t)
    ss = _dot(hi, seg_ref[...]) + _dot(lo, seg_ref[...])
    r = lax.rsqrt(ss * (1.0 / HEAD_DIM) + EPS)
    return t * r * gain_tiled


def _mix_proj_kernel(x_ref, g_ref, w_ref, qg_ref, kg_ref, seg_ref,
                     xr_ref, gate_ref, q_ref, k_ref, v_ref):
    h = _rms(x_ref[...], g_ref[...]).astype(bf16)
    xr_ref[...] = _dot(h, w_ref[:, 0:D_RNN])
    gate_ref[...] = _dot(h, w_ref[:, D_RNN:2 * D_RNN])
    o = 2 * D_RNN
    q = _dot(h, w_ref[:, o:o + D_ATT])
    k = _dot(h, w_ref[:, o + D_ATT:o + 2 * D_ATT])
    v = _dot(h, w_ref[:, o + 2 * D_ATT:o + 3 * D_ATT])
    scale = 1.0 / math.sqrt(HEAD_DIM)
    q_ref[...] = (_head_norm(q, qg_ref[...], seg_ref) * scale).astype(bf16)
    k_ref[...] = _head_norm(k, kg_ref[...], seg_ref).astype(bf16)
    v_ref[...] = v.astype(bf16)


def _mix_proj(x2d, g, w_in, qg, kg, seg):
    n_tok = x2d.shape[0]
    tok = lambda d: pl.BlockSpec((TM, d), lambda i: (i, 0))
    return pl.pallas_call(
        _mix_proj_kernel,
        out_shape=(jax.ShapeDtypeStruct((n_tok, D_RNN), f32),
                   jax.ShapeDtypeStruct((n_tok, D_RNN), f32),
                   jax.ShapeDtypeStruct((n_tok, D_ATT), bf16),
                   jax.ShapeDtypeStruct((n_tok, D_ATT), bf16),
                   jax.ShapeDtypeStruct((n_tok, D_ATT), bf16)),
        grid=(n_tok // TM,),
        in_specs=[tok(D_MODEL), _const_spec((1, D_MODEL)), _const_spec((D_MODEL, N_IN)),
                  _const_spec((1, D_ATT)), _const_spec((1, D_ATT)), _const_spec((D_ATT, D_ATT))],
        out_specs=(tok(D_RNN), tok(D_RNN), tok(D_ATT), tok(D_ATT), tok(D_ATT)),
        compiler_params=pltpu.CompilerParams(
            dimension_semantics=("parallel",), vmem_limit_bytes=VMEM_LIMIT),
        name="mix_proj",
    )(x2d, g, w_in, qg, kg, seg)


HIST = 8


def _rglru_kernel(xr_ref, gate_ref, cw_ref, cb_ref, wa_ref, ba_ref, wx_ref, bx_ref,
                  lam_ref, g_ref, o_ref, xbuf_ref, hc_ref):
    @pl.when(pl.program_id(1) == 0)
    def _():
        xbuf_ref[0:HIST, :] = jnp.zeros((HIST, D_RNN), f32)
        hc_ref[...] = jnp.zeros((1, D_RNN), f32)

    x = xr_ref[0]
    xbuf_ref[HIST:HIST + TS, :] = x
    xc = cb_ref[...] + cw_ref[CONV_W - 1:CONV_W, :] * x
    for j in range(CONV_W - 1):
        off = HIST - (CONV_W - 1) + j
        xc = xc + cw_ref[j:j + 1, :] * xbuf_ref[off:off + TS, :]
    xbuf_ref[0:HIST, :] = x[TS - HIST:TS, :]

    xcb = xc.astype(bf16)
    r = jax.nn.sigmoid(_dot(xcb, wa_ref[...]) + ba_ref[...])
    i = jax.nn.sigmoid(_dot(xcb, wx_ref[...]) + bx_ref[...])
    lam = lam_ref[...]
    log_sig_lam = jnp.minimum(lam, 0.0) - jnp.log1p(jnp.exp(-jnp.abs(lam)))
    log_a = RG_C * r * log_sig_lam
    a = jnp.exp(log_a)
    th = jnp.tanh(log_a)
    mult = jnp.sqrt(-2.0 * th / (1.0 - th))
    b = mult * (i * xc)

    row = lax.broadcasted_iota(jnp.int32, (TS, D_RNN), 0)
    d = 1
    while d < TS:
        a_sh = pltpu.roll(a, d, axis=0)
        b_sh = pltpu.roll(b, d, axis=0)
        m = row >= d
        b = jnp.where(m, a * b_sh, 0.0) + b
        a = jnp.where(m, a * a_sh, a)
        d *= 2
    h = a * hc_ref[...] + b
    hc_ref[...] = h[TS - 1:TS, :]

    y = h * jax.nn.gelu(gate_ref[0])
    o_ref[0] = _rms(y, g_ref[...]).astype(bf16)


def _rglru(xr, gate, cw, cb, wa, ba, wx, bx, lam, g):
    bsz, seq, _ = xr.shape
    seq_spec = pl.BlockSpec((1, TS, D_RNN), lambda b, s: (b, s, 0))
    vec = _const_spec((1, D_RNN))
    return pl.pallas_call(
        _rglru_kernel,
        out_shape=jax.ShapeDtypeStruct((bsz, seq, D_RNN), bf16),
        grid=(bsz, seq // TS),
        in_specs=[seq_spec, seq_spec, _const_spec((CONV_W, D_RNN)), vec,
                  _const_spec((D_RNN, D_RNN)), vec, _const_spec((D_RNN, D_RNN)), vec, vec, vec],
        out_specs=seq_spec,
        scratch_shapes=[pltpu.VMEM((HIST + TS, D_RNN), f32), pltpu.VMEM((1, D_RNN), f32)],
        compiler_params=pltpu.CompilerParams(
            dimension_semantics=("parallel", "arbitrary"), vmem_limit_bytes=VMEM_LIMIT),
        name="rglru",
    )(xr, gate, cw, cb, wa, ba, wx, bx, lam, g)


PIPE_DEPTH = 4
BLK_PER_TILE = TK // LANES
N_BLK = HEADS_PER_STEP * BLK_PER_TILE
ITER_PER_TRIP = 4
LOG2E = 1.4426950408889634


MASKED = -1e30


def _attn_kernel(qt_ref, kt_ref, q_ref, k_ref, v_ref, tri_ref, o_ref,
                 c_ref, z_ref, split_ref, tsum_ref, p_ref, w_ref, d_ref, kh_ref, vh_ref):
    n_items = qt_ref.shape[0]
    seq = k_ref.shape[1]
    heads = range(HEADS_PER_STEP)
    for ref in (o_ref, c_ref, z_ref, split_ref, tsum_ref, p_ref, w_ref):
        ref[...] = jnp.zeros_like(ref)
    for b in range(BLK_PER_TILE):
        d_ref[b] = (lax.broadcasted_iota(jnp.int32, (TQ, LANES), 1) + b * LANES
                    - lax.broadcasted_iota(jnp.int32, (TQ, LANES), 0))
    lane_head = lax.broadcasted_iota(jnp.int32, (seq, LANES), 1) // HEAD_DIM
    for h in heads:
        kh_ref[h] = jnp.where(lane_head == h, k_ref[0], jnp.zeros_like(k_ref[0]))
        vh_ref[h] = jnp.where(lane_head == h, v_ref[0], jnp.zeros_like(v_ref[0]))

    def item(j):
        j = jnp.clip(j, 0, n_items - 1)
        return qt_ref[j], kt_ref[j]

    def per_head_rows(ref, t):
        rows = pl.ds(pl.multiple_of(t * TK, TK), TK)
        return jnp.concatenate([ref[h, rows, :] for h in heads], axis=0)

    def stage_c2(qi, t):
        rows = pl.ds(pl.multiple_of(qi * TQ, TQ), TQ)
        acc = jnp.where(t == qi, 0.0, o_ref[0, rows, :])
        w = jnp.concatenate([w_ref[blk] for blk in range(N_BLK)], axis=1)
        o_ref[0, rows, :] = acc + _dot(w, per_head_rows(vh_ref, t))

    def stage_c1(qi, t, slot):
        for h in heads:
            first = h * BLK_PER_TILE
            c = jnp.where(t == qi, 0.0, c_ref[h])
            for blk in range(first, first + BLK_PER_TILE):
                w_ref[blk] = jnp.exp(z_ref[slot, blk] + p_ref[blk] + c).astype(bf16)
            c_ref[h] = c - tsum_ref[slot % 2, h]

    def stage_b2():
        for h in heads:
            first = h * BLK_PER_TILE
            parts = [split_ref[blk] for blk in range(first, first + BLK_PER_TILE)]
            r = _dot(jnp.concatenate(parts, axis=1), tri_ref[...])
            for b in range(BLK_PER_TILE):
                p_ref[first + b] = r[:, b * LANES:(b + 1) * LANES]

    def stage_b1(slot):
        for h in heads:
            tile_sum = None
            for blk in range(h * BLK_PER_TILE, (h + 1) * BLK_PER_TILE):
                z = z_ref[slot, blk]
                sp = jnp.log(1.0 + jnp.exp2(jnp.abs(z) * (-LOG2E))) + jnp.maximum(z, 0.0)
                split_ref[blk] = sp.astype(bf16)
                tile_sum = sp if tile_sum is None else tile_sum + sp
            tsum_ref[slot % 2, h] = jnp.broadcast_to(
                jnp.sum(tile_sum, axis=1, keepdims=True), (TQ, LANES))

    def stage_a(qi, t, slot):
        q = q_ref[0, pl.ds(pl.multiple_of(qi * TQ, TQ), TQ), :]
        z = lax.dot_general(q, per_head_rows(kh_ref, t), (((1,), (1,)), ((), ())),
                            preferred_element_type=f32)
        for b in range(BLK_PER_TILE):
            visible = d_ref[b] < (qi - t) * TQ
            for h in heads:
                blk = h * BLK_PER_TILE + b
                z_ref[slot, blk] = jnp.where(visible, z[:, blk * LANES:(blk + 1) * LANES], MASKED)

    def iteration(i, u):
        stage_c2(*item(i - PIPE_DEPTH))
        stage_c1(*item(i - 3), (u - 3) % ITER_PER_TRIP)
        stage_b2()
        stage_b1((u - 1) % ITER_PER_TRIP)
        stage_a(*item(i), u)

    def body(m, carry):
        for u in range(ITER_PER_TRIP):
            iteration(ITER_PER_TRIP * m + u, u)
        return carry

    lax.fori_loop(0, (n_items + PIPE_DEPTH) // ITER_PER_TRIP, body, 0)


def _attention(q, k, v, tri):
    bsz, seq, _ = q.shape
    n_qt = seq // TQ
    qt = jnp.asarray([qi for qi in range(n_qt) for _ in range(qi + 1)], jnp.int32)
    kt = jnp.asarray([t for qi in range(n_qt) for t in range(qi, -1, -1)], jnp.int32)
    assert (qt.shape[0] + PIPE_DEPTH) % ITER_PER_TRIP == 0
    seq_spec = pl.BlockSpec((1, seq, LANES), lambda b, p, *_: (b, 0, p))
    tri_spec = pl.BlockSpec((TK, TK), lambda b, p, *_: (0, 0),
                            pipeline_mode=pl.Buffered(1))
    return pl.pallas_call(
        _attn_kernel,
        out_shape=jax.ShapeDtypeStruct((bsz, seq, D_ATT), f32),
        grid_spec=pltpu.PrefetchScalarGridSpec(
            num_scalar_prefetch=2,
            grid=(bsz, D_ATT // LANES),
            in_specs=[seq_spec, seq_spec, seq_spec, tri_spec],
            out_specs=seq_spec,
            scratch_shapes=[
                pltpu.VMEM((HEADS_PER_STEP, TQ, LANES), f32),
                pltpu.VMEM((ITER_PER_TRIP, N_BLK, TQ, LANES), f32),
                pltpu.VMEM((N_BLK, TQ, LANES), bf16),
                pltpu.VMEM((2, HEADS_PER_STEP, TQ, LANES), f32),
                pltpu.VMEM((N_BLK, TQ, LANES), f32),
                pltpu.VMEM((N_BLK, TQ, LANES), bf16),
                pltpu.VMEM((BLK_PER_TILE, TQ, LANES), jnp.int32),
                pltpu.VMEM((HEADS_PER_STEP, seq, LANES), bf16),
                pltpu.VMEM((HEADS_PER_STEP, seq, LANES), bf16),
            ]),
        compiler_params=pltpu.CompilerParams(
            dimension_semantics=("parallel", "parallel"), vmem_limit_bytes=VMEM_LIMIT),
        name="stickbreak",
    )(qt, kt, q, k, v, tri)


def _out_proj_kernel(x_ref, yr_ref, ya_ref, g_ref, w_ref, o_ref):
    ya = _rms(ya_ref[...], g_ref[...]).astype(bf16)
    o_ref[...] = (x_ref[...] + _dot(yr_ref[...], w_ref[0:D_RNN, :])
                  + _dot(ya, w_ref[D_RNN:D_RNN + D_ATT, :]))


def _out_proj(x2d, yr, ya, g, w_out):
    n_tok = x2d.shape[0]
    tok = lambda d: pl.BlockSpec((TM, d), lambda i: (i, 0))
    return pl.pallas_call(
        _out_proj_kernel,
        out_shape=jax.ShapeDtypeStruct((n_tok, D_MODEL), f32),
        grid=(n_tok // TM,),
        in_specs=[tok(D_MODEL), tok(D_RNN), tok(D_ATT), _const_spec((1, D_ATT)),
                  _const_spec((D_RNN + D_ATT, D_MODEL))],
        out_specs=tok(D_MODEL),
        compiler_params=pltpu.CompilerParams(
            dimension_semantics=("parallel",), vmem_limit_bytes=VMEM_LIMIT),
        name="out_proj",
    )(x2d, yr, ya, g, w_out)


def _block_diag(w):
    eye = jnp.eye(RNN_BLOCKS, dtype=w.dtype)
    return jnp.einsum("ncd,nm->ncmd", w, eye).reshape(D_RNN, D_RNN)


def _scan_matrix():
    j = jnp.arange(TK)
    return -(j[:, None] >= j[None, :]).astype(bf16)


def kernel(x, ffn1_norm, ffn1_w_gate, ffn1_w_up, ffn1_w_down, mix_norm, w_in, conv_w, conv_b,
           rg_w_a, rg_b_a, rg_w_x, rg_b_x, rg_lambda, q_norm, k_norm, rnn_out_norm,
           attn_out_norm, w_out, ffn2_norm, ffn2_w_gate, ffn2_w_up, ffn2_w_down):
    bsz, seq, _ = x.shape
    depth = ffn1_norm.shape[0]
    head_id = jnp.arange(D_ATT) // HEAD_DIM
    seg = (head_id[:, None] == head_id[None, :]).astype(bf16)
    tri = _scan_matrix()
    x2d = x.reshape(bsz * seq, D_MODEL)
    for l in range(depth):
        x2d = _ffn(x2d, ffn1_norm[l][None], ffn1_w_gate[l].astype(bf16),
                   ffn1_w_up[l].astype(bf16), ffn1_w_down[l].astype(bf16))
        xr, gate, q, k, v = _mix_proj(
            x2d, mix_norm[l][None], w_in[l].astype(bf16),
            jnp.tile(q_norm[l], N_HEADS)[None], jnp.tile(k_norm[l], N_HEADS)[None], seg)
        y_rnn = _rglru(
            xr.reshape(bsz, seq, D_RNN), gate.reshape(bsz, seq, D_RNN),
            conv_w[l], conv_b[l][None], _block_diag(rg_w_a[l]).astype(bf16), rg_b_a[l][None],
            _block_diag(rg_w_x[l]).astype(bf16), rg_b_x[l][None], rg_lambda[l][None],
            rnn_out_norm[l][None])
        y_att = _attention(q.reshape(bsz, seq, D_ATT), k.reshape(bsz, seq, D_ATT),
                           v.reshape(bsz, seq, D_ATT), tri)
        x2d = _out_proj(x2d, y_rnn.reshape(bsz * seq, D_RNN), y_att.reshape(bsz * seq, D_ATT),
                        attn_out_norm[l][None], w_out[l].astype(bf16))
        x2d = _ffn(x2d, ffn2_norm[l][None], ffn2_w_gate[l].astype(bf16),
                   ffn2_w_up[l].astype(bf16), ffn2_w_down[l].astype(bf16))
    return x2d.reshape(bsz, seq, D_MODEL)
```

```python
import math

import jax
import jax.numpy as jnp
from jax import lax
from jax.experimental import pallas as pl
from jax.experimental.pallas import tpu as pltpu

D_MODEL = 1024
D_RNN = 512
RNN_BLOCKS = 8
RNN_BW = D_RNN // RNN_BLOCKS
CONV_W = 4
RG_C = 8.0
D_ATT = 512
HEAD_DIM = 64
N_HEADS = D_ATT // HEAD_DIM
D_FF = 2816
N_IN = 2 * D_RNN + 3 * D_ATT
EPS = 1e-6

LANES = 128
SUBLANES = 8
HEADS_PER_STEP = LANES // HEAD_DIM
VMEM_LIMIT = 56 * 1024 * 1024

TM = 512
FF_CHUNK = 256
TS = 512
TQ = 256
TK = 256

f32 = jnp.float32
bf16 = jnp.bfloat16


def _rms(xf, g):
    r = lax.rsqrt(jnp.mean(xf * xf, axis=-1, keepdims=True) + EPS)
    return xf * r * g


def _dot(a, b):
    return jnp.dot(a, b, preferred_element_type=f32)


def _split_bf16(x):
    hi = x.astype(bf16)
    lo = (x - hi.astype(f32)).astype(bf16)
    return hi, lo


def _const_spec(shape):
    nd = len(shape)
    return pl.BlockSpec(shape, lambda *_: (0,) * nd, pipeline_mode=pl.Buffered(1))


def _ffn_tile(x, g_ref, wg_ref, wu_ref, wd_ref, h_ref):
    xn = _rms(x, g_ref[...]).astype(bf16)
    for c in range(0, D_FF, FF_CHUNK):
        gate = _dot(xn, wg_ref[:, c:c + FF_CHUNK])
        up = _dot(xn, wu_ref[:, c:c + FF_CHUNK])
        h_ref[:, c:c + FF_CHUNK] = (jax.nn.silu(gate) * up).astype(bf16)
    return x + 0.5 * _dot(h_ref[...], wd_ref[...])


def _ffn_specs():
    return [_const_spec((1, D_MODEL)), _const_spec((D_MODEL, D_FF)),
            _const_spec((D_MODEL, D_FF)), _const_spec((D_FF, D_MODEL))]


def _tok_spec(d):
    return pl.BlockSpec((TM, d), lambda i: (i, 0))


def _head_norm(t, gain_tiled, seg_ref):
    hi, lo = _split_bf16(t * t)
    ss = _dot(hi, seg_ref[...]) + _dot(lo, seg_ref[...])
    r = lax.rsqrt(ss * (1.0 / HEAD_DIM) + EPS)
    return t * r * gain_tiled


def _ffn_in_kernel(x_ref, g1_ref, wg_ref, wu_ref, wd_ref, g_ref, w_ref, qg_ref, kg_ref, seg_ref,
                   o_ref, xr_ref, gate_ref, q_ref, k_ref, v_ref, h_ref):
    x1 = _ffn_tile(x_ref[...], g1_ref, wg_ref, wu_ref, wd_ref, h_ref)
    o_ref[...] = x1
    h = _rms(x1, g_ref[...]).astype(bf16)
    xr_ref[...] = _dot(h, w_ref[:, 0:D_RNN])
    gate_ref[...] = _dot(h, w_ref[:, D_RNN:2 * D_RNN])
    o = 2 * D_RNN
    q = _dot(h, w_ref[:, o:o + D_ATT])
    k = _dot(h, w_ref[:, o + D_ATT:o + 2 * D_ATT])
    v = _dot(h, w_ref[:, o + 2 * D_ATT:o + 3 * D_ATT])
    scale = 1.0 / math.sqrt(HEAD_DIM)
    q_ref[...] = (_head_norm(q, qg_ref[...], seg_ref) * scale).astype(bf16)
    k_ref[...] = _head_norm(k, kg_ref[...], seg_ref).astype(bf16)
    v_ref[...] = v.astype(bf16)


def _ffn_in(x2d, g1, wg, wu, wd, g, w_in, qg, kg, seg):
    n_tok = x2d.shape[0]
    return pl.pallas_call(
        _ffn_in_kernel,
        out_shape=(jax.ShapeDtypeStruct((n_tok, D_MODEL), f32),
                   jax.ShapeDtypeStruct((n_tok, D_RNN), f32),
                   jax.ShapeDtypeStruct((n_tok, D_RNN), f32),
                   jax.ShapeDtypeStruct((n_tok, D_ATT), bf16),
                   jax.ShapeDtypeStruct((n_tok, D_ATT), bf16),
                   jax.ShapeDtypeStruct((n_tok, D_ATT), bf16)),
        grid=(n_tok // TM,),
        in_specs=[_tok_spec(D_MODEL)] + _ffn_specs() + [
            _const_spec((1, D_MODEL)), _const_spec((D_MODEL, N_IN)),
            _const_spec((1, D_ATT)), _const_spec((1, D_ATT)), _const_spec((D_ATT, D_ATT))],
        out_specs=(_tok_spec(D_MODEL), _tok_spec(D_RNN), _tok_spec(D_RNN),
                   _tok_spec(D_ATT), _tok_spec(D_ATT), _tok_spec(D_ATT)),
        scratch_shapes=[pltpu.VMEM((TM, D_FF), bf16)],
        compiler_params=pltpu.CompilerParams(
            dimension_semantics=("parallel",), vmem_limit_bytes=VMEM_LIMIT),
        name="ffn_in",
    )(x2d, g1, wg, wu, wd, g, w_in, qg, kg, seg)


HIST = 8


def _rglru_kernel(xr_ref, gate_ref, cw_ref, cb_ref, wa_ref, ba_ref, wx_ref, bx_ref,
                  lam_ref, g_ref, o_ref, xbuf_ref, hc_ref):
    @pl.when(pl.program_id(1) == 0)
    def _():
        xbuf_ref[0:HIST, :] = jnp.zeros((HIST, D_RNN), f32)
        hc_ref[...] = jnp.zeros((1, D_RNN), f32)

    x = xr_ref[0]
    xbuf_ref[HIST:HIST + TS, :] = x
    xc = cb_ref[...] + cw_ref[CONV_W - 1:CONV_W, :] * x
    for j in range(CONV_W - 1):
        off = HIST - (CONV_W - 1) + j
        xc = xc + cw_ref[j:j + 1, :] * xbuf_ref[off:off + TS, :]
    xbuf_ref[0:HIST, :] = x[TS - HIST:TS, :]

    xcb = xc.astype(bf16)
    r = jax.nn.sigmoid(_dot(xcb, wa_ref[...]) + ba_ref[...])
    i = jax.nn.sigmoid(_dot(xcb, wx_ref[...]) + bx_ref[...])
    lam = lam_ref[...]
    log_sig_lam = jnp.minimum(lam, 0.0) - jnp.log1p(jnp.exp(-jnp.abs(lam)))
    log_a = RG_C * r * log_sig_lam
    a = jnp.exp(log_a)
    th = jnp.tanh(log_a)
    mult = jnp.sqrt(-2.0 * th / (1.0 - th))
    b = mult * (i * xc)

    n_grp = TS // SUBLANES
    a = a.reshape(n_grp, SUBLANES, D_RNN)
    b = b.reshape(n_grp, SUBLANES, D_RNN)
    row = lax.broadcasted_iota(jnp.int32, (n_grp, SUBLANES, D_RNN), 1)
    d = 1
    while d < SUBLANES:
        a_sh = pltpu.roll(a, d, axis=1)
        b_sh = pltpu.roll(b, d, axis=1)
        m = row >= d
        b = jnp.where(m, a * b_sh, 0.0) + b
        a = jnp.where(m, a * a_sh, a)
        d *= 2
    carry = jnp.broadcast_to(hc_ref[...], (SUBLANES, D_RNN))
    groups = []
    for g in range(n_grp):
        h_g = a[g] * carry + b[g]
        groups.append(h_g)
        carry = jnp.broadcast_to(h_g[SUBLANES - 1:SUBLANES, :], (SUBLANES, D_RNN))
    h = jnp.concatenate(groups, axis=0)
    hc_ref[...] = groups[-1][SUBLANES - 1:SUBLANES, :]

    y = h * jax.nn.gelu(gate_ref[0])
    o_ref[0] = _rms(y, g_ref[...]).astype(bf16)


def _rglru(xr, gate, cw, cb, wa, ba, wx, bx, lam, g):
    bsz, seq, _ = xr.shape
    seq_spec = pl.BlockSpec((1, TS, D_RNN), lambda b, s: (b, s, 0))
    vec = _const_spec((1, D_RNN))
    return pl.pallas_call(
        _rglru_kernel,
        out_shape=jax.ShapeDtypeStruct((bsz, seq, D_RNN), bf16),
        grid=(bsz, seq // TS),
        in_specs=[seq_spec, seq_spec, _const_spec((CONV_W, D_RNN)), vec,
                  _const_spec((D_RNN, D_RNN)), vec, _const_spec((D_RNN, D_RNN)), vec, vec, vec],
        out_specs=seq_spec,
        scratch_shapes=[pltpu.VMEM((HIST + TS, D_RNN), f32), pltpu.VMEM((1, D_RNN), f32)],
        compiler_params=pltpu.CompilerParams(
            dimension_semantics=("parallel", "arbitrary"), vmem_limit_bytes=VMEM_LIMIT),
        name="rglru",
    )(xr, gate, cw, cb, wa, ba, wx, bx, lam, g)


PIPE_DEPTH = 4
BLK_PER_TILE = TK // LANES
N_BLK = HEADS_PER_STEP * BLK_PER_TILE
ITER_PER_TRIP = 4
LOG2E = 1.4426950408889634


MASKED = -1e30


def _attn_kernel(qt_ref, kt_ref, q_ref, k_ref, v_ref, tri_ref, o_ref,
                 c_ref, z_ref, split_ref, p_ref, w_ref, d_ref, kh_ref, vh_ref):
    n_items = qt_ref.shape[0]
    seq = k_ref.shape[1]
    heads = range(HEADS_PER_STEP)
    for ref in (o_ref, c_ref, z_ref, split_ref, p_ref, w_ref):
        ref[...] = jnp.zeros_like(ref)
    for b in range(BLK_PER_TILE):
        d_ref[b] = (lax.broadcasted_iota(jnp.int32, (TQ, LANES), 1) + b * LANES
                    - lax.broadcasted_iota(jnp.int32, (TQ, LANES), 0))
    lane_head = lax.broadcasted_iota(jnp.int32, (seq, LANES), 1) // HEAD_DIM
    for h in heads:
        kh_ref[h] = jnp.where(lane_head == h, k_ref[0], jnp.zeros_like(k_ref[0]))
        vh_ref[h] = jnp.where(lane_head == h, v_ref[0], jnp.zeros_like(v_ref[0]))

    def item(j):
        j = jnp.clip(j, 0, n_items - 1)
        return qt_ref[j], kt_ref[j]

    def per_head_rows(ref, t):
        rows = pl.ds(pl.multiple_of(t * TK, TK), TK)
        return jnp.concatenate([ref[h, rows, :] for h in heads], axis=0)

    def stage_c2(qi, t):
        rows = pl.ds(pl.multiple_of(qi * TQ, TQ), TQ)
        acc = jnp.where(t == qi, 0.0, o_ref[0, rows, :])
        w = jnp.concatenate([w_ref[blk] for blk in range(N_BLK)], axis=1)
        o_ref[0, rows, :] = acc + _dot(w, per_head_rows(vh_ref, t))

    def stage_c1(qi, t, slot):
        for h in heads:
            first = h * BLK_PER_TILE
            c = jnp.where(t == qi, 0.0, c_ref[h])
            for blk in range(first, first + BLK_PER_TILE):
                w_ref[blk] = jnp.exp(z_ref[slot, blk] + p_ref[blk] + c).astype(bf16)
            c_ref[h] = c + jnp.broadcast_to(p_ref[first, :, 0:1], (TQ, LANES))

    def stage_b2():
        for h in heads:
            first = h * BLK_PER_TILE
            parts = [split_ref[blk] for blk in range(first, first + BLK_PER_TILE)]
            r = _dot(jnp.concatenate(parts, axis=1), tri_ref[...])
            for b in range(BLK_PER_TILE):
                p_ref[first + b] = r[:, b * LANES:(b + 1) * LANES]

    def stage_b1(slot):
        for blk in range(N_BLK):
            z = z_ref[slot, blk]
            sp = jnp.log(1.0 + jnp.exp2(jnp.abs(z) * (-LOG2E))) + jnp.maximum(z, 0.0)
            split_ref[blk] = sp.astype(bf16)

    def stage_a(qi, t, slot):
        q = q_ref[0, pl.ds(pl.multiple_of(qi * TQ, TQ), TQ), :]
        z = lax.dot_general(q, per_head_rows(kh_ref, t), (((1,), (1,)), ((), ())),
                            preferred_element_type=f32)
        for b in range(BLK_PER_TILE):
            visible = d_ref[b] < (qi - t) * TQ
            for h in heads:
                blk = h * BLK_PER_TILE + b
                z_ref[slot, blk] = jnp.where(visible, z[:, blk * LANES:(blk + 1) * LANES], MASKED)

    def iteration(i, u):
        stage_c2(*item(i - PIPE_DEPTH))
        stage_c1(*item(i - 3), (u - 3) % ITER_PER_TRIP)
        stage_b2()
        stage_b1((u - 1) % ITER_PER_TRIP)
        stage_a(*item(i), u)

    def body(m, carry):
        for u in range(ITER_PER_TRIP):
            iteration(ITER_PER_TRIP * m + u, u)
        return carry

    lax.fori_loop(0, (n_items + PIPE_DEPTH) // ITER_PER_TRIP, body, 0)


def _attention(q, k, v, tri):
    bsz, seq, _ = q.shape
    n_qt = seq // TQ
    qt = jnp.asarray([qi for qi in range(n_qt) for _ in range(qi + 1)], jnp.int32)
    kt = jnp.asarray([t for qi in range(n_qt) for t in range(qi, -1, -1)], jnp.int32)
    assert (qt.shape[0] + PIPE_DEPTH) % ITER_PER_TRIP == 0
    seq_spec = pl.BlockSpec((1, seq, LANES), lambda b, p, *_: (b, 0, p))
    tri_spec = pl.BlockSpec((TK, TK), lambda b, p, *_: (0, 0),
                            pipeline_mode=pl.Buffered(1))
    return pl.pallas_call(
        _attn_kernel,
        out_shape=jax.ShapeDtypeStruct((bsz, seq, D_ATT), f32),
        grid_spec=pltpu.PrefetchScalarGridSpec(
            num_scalar_prefetch=2,
            grid=(bsz, D_ATT // LANES),
            in_specs=[seq_spec, seq_spec, seq_spec, tri_spec],
            out_specs=seq_spec,
            scratch_shapes=[
                pltpu.VMEM((HEADS_PER_STEP, TQ, LANES), f32),
                pltpu.VMEM((ITER_PER_TRIP, N_BLK, TQ, LANES), f32),
                pltpu.VMEM((N_BLK, TQ, LANES), bf16),
                pltpu.VMEM((N_BLK, TQ, LANES), f32),
                pltpu.VMEM((N_BLK, TQ, LANES), bf16),
                pltpu.VMEM((BLK_PER_TILE, TQ, LANES), jnp.int32),
                pltpu.VMEM((HEADS_PER_STEP, seq, LANES), bf16),
                pltpu.VMEM((HEADS_PER_STEP, seq, LANES), bf16),
            ]),
        compiler_params=pltpu.CompilerParams(
            dimension_semantics=("parallel", "parallel"), vmem_limit_bytes=VMEM_LIMIT),
        name="stickbreak",
    )(qt, kt, q, k, v, tri)


def _ffn_out_kernel(x_ref, yr_ref, ya_ref, ga_ref, wo_ref, g2_ref, wg_ref, wu_ref, wd_ref,
                    o_ref, h_ref):
    ya = _rms(ya_ref[...], ga_ref[...]).astype(bf16)
    x2 = (x_ref[...] + _dot(yr_ref[...], wo_ref[0:D_RNN, :])
          + _dot(ya, wo_ref[D_RNN:D_RNN + D_ATT, :]))
    o_ref[...] = _ffn_tile(x2, g2_ref, wg_ref, wu_ref, wd_ref, h_ref)


def _ffn_out(x2d, yr, ya, ga, w_out, g2, wg, wu, wd):
    n_tok = x2d.shape[0]
    return pl.pallas_call(
        _ffn_out_kernel,
        out_shape=jax.ShapeDtypeStruct((n_tok, D_MODEL), f32),
        grid=(n_tok // TM,),
        in_specs=[_tok_spec(D_MODEL), _tok_spec(D_RNN), _tok_spec(D_ATT), _const_spec((1, D_ATT)),
                  _const_spec((D_RNN + D_ATT, D_MODEL))] + _ffn_specs(),
        out_specs=_tok_spec(D_MODEL),
        scratch_shapes=[pltpu.VMEM((TM, D_FF), bf16)],
        compiler_params=pltpu.CompilerParams(
            dimension_semantics=("parallel",), vmem_limit_bytes=VMEM_LIMIT),
        name="ffn_out",
    )(x2d, yr, ya, ga, w_out, g2, wg, wu, wd)


def _block_diag(w):
    eye = jnp.eye(RNN_BLOCKS, dtype=w.dtype)
    return jnp.einsum("ncd,nm->ncmd", w, eye).reshape(D_RNN, D_RNN)


def _scan_matrix():
    j = jnp.arange(TK)
    return -(j[:, None] >= j[None, :]).astype(bf16)


def kernel(x, ffn1_norm, ffn1_w_gate, ffn1_w_up, ffn1_w_down, mix_norm, w_in, conv_w, conv_b,
           rg_w_a, rg_b_a, rg_w_x, rg_b_x, rg_lambda, q_norm, k_norm, rnn_out_norm,
           attn_out_norm, w_out, ffn2_norm, ffn2_w_gate, ffn2_w_up, ffn2_w_down):
    bsz, seq, _ = x.shape
    depth = ffn1_norm.shape[0]
    head_id = jnp.arange(D_ATT) // HEAD_DIM
    seg = (head_id[:, None] == head_id[None, :]).astype(bf16)
    tri = _scan_matrix()
    x2d = x.reshape(bsz * seq, D_MODEL)
    for l in range(depth):
        x2d, xr, gate, q, k, v = _ffn_in(
            x2d, ffn1_norm[l][None], ffn1_w_gate[l].astype(bf16), ffn1_w_up[l].astype(bf16),
            ffn1_w_down[l].astype(bf16), mix_norm[l][None], w_in[l].astype(bf16),
            jnp.tile(q_norm[l], N_HEADS)[None], jnp.tile(k_norm[l], N_HEADS)[None], seg)
        y_rnn = _rglru(
            xr.reshape(bsz, seq, D_RNN), gate.reshape(bsz, seq, D_RNN),
            conv_w[l], conv_b[l][None], _block_diag(rg_w_a[l]).astype(bf16), rg_b_a[l][None],
            _block_diag(rg_w_x[l]).astype(bf16), rg_b_x[l][None], rg_lambda[l][None],
            rnn_out_norm[l][None])
        y_att = _attention(q.reshape(bsz, seq, D_ATT), k.reshape(bsz, seq, D_ATT),
                           v.reshape(bsz, seq, D_ATT), tri)
        x2d = _ffn_out(x2d, y_rnn.reshape(bsz * seq, D_RNN), y_att.reshape(bsz * seq, D_ATT),
                       attn_out_norm[l][None], w_out[l].astype(bf16), ffn2_norm[l][None],
                       ffn2_w_gate[l].astype(bf16), ffn2_w_up[l].astype(bf16),
                       ffn2_w_down[l].astype(bf16))
    return x2d.reshape(bsz, seq, D_MODEL)
```

```python
import math

import jax
import jax.numpy as jnp
from jax import lax
from jax.experimental import pallas as pl
from jax.experimental.pallas import tpu as pltpu

D_MODEL = 1024
D_RNN = 512
RNN_BLOCKS = 8
RNN_BW = D_RNN // RNN_BLOCKS
CONV_W = 4
RG_C = 8.0
D_ATT = 512
HEAD_DIM = 64
N_HEADS = D_ATT // HEAD_DIM
D_FF = 2816
N_IN = 2 * D_RNN + 3 * D_ATT
EPS = 1e-6

LANES = 128
SUBLANES = 8
HEADS_PER_STEP = LANES // HEAD_DIM
VMEM_LIMIT = 56 * 1024 * 1024

TM = 512
FF_CHUNK = 256
TS = 512
TQ = 256
TK = 256

f32 = jnp.float32
bf16 = jnp.bfloat16


def _rms(xf, g):
    r = lax.rsqrt(jnp.mean(xf * xf, axis=-1, keepdims=True) + EPS)
    return xf * r * g


def _dot(a, b):
    return jnp.dot(a, b, preferred_element_type=f32)


def _split_bf16(x):
    hi = x.astype(bf16)
    lo = (x - hi.astype(f32)).astype(bf16)
    return hi, lo


def _const_spec(shape):
    nd = len(shape)
    return pl.BlockSpec(shape, lambda *_: (0,) * nd, pipeline_mode=pl.Buffered(1))


def _ffn_tile(x, g_ref, wg_ref, wu_ref, wd_ref, h_ref):
    xn = _rms(x, g_ref[...]).astype(bf16)
    for c in range(0, D_FF, FF_CHUNK):
        gate = _dot(xn, wg_ref[:, c:c + FF_CHUNK])
        up = _dot(xn, wu_ref[:, c:c + FF_CHUNK])
        h_ref[:, c:c + FF_CHUNK] = (jax.nn.silu(gate) * up).astype(bf16)
    return x + 0.5 * _dot(h_ref[...], wd_ref[...])


def _ffn_specs():
    return [_const_spec((1, D_MODEL)), _const_spec((D_MODEL, D_FF)),
            _const_spec((D_MODEL, D_FF)), _const_spec((D_FF, D_MODEL))]


def _tok_spec(d):
    return pl.BlockSpec((TM, d), lambda i: (i, 0))


def _head_norm(t, gain_tiled, seg_ref):
    hi, lo = _split_bf16(t * t)
    ss = _dot(hi, seg_ref[...]) + _dot(lo, seg_ref[...])
    r = lax.rsqrt(ss * (1.0 / HEAD_DIM) + EPS)
    return t * r * gain_tiled


def _ffn_in_kernel(x_ref, g1_ref, wg_ref, wu_ref, wd_ref, g_ref, w_ref, qg_ref, kg_ref, seg_ref,
                   o_ref, xr_ref, gate_ref, q_ref, k_ref, v_ref, h_ref):
    x1 = _ffn_tile(x_ref[...], g1_ref, wg_ref, wu_ref, wd_ref, h_ref)
    o_ref[...] = x1
    h = _rms(x1, g_ref[...]).astype(bf16)
    xr_ref[...] = _dot(h, w_ref[:, 0:D_RNN])
    gate_ref[...] = _dot(h, w_ref[:, D_RNN:2 * D_RNN])
    o = 2 * D_RNN
    q = _dot(h, w_ref[:, o:o + D_ATT])
    k = _dot(h, w_ref[:, o + D_ATT:o + 2 * D_ATT])
    v = _dot(h, w_ref[:, o + 2 * D_ATT:o + 3 * D_ATT])
    scale = 1.0 / math.sqrt(HEAD_DIM)
    q_ref[...] = (_head_norm(q, qg_ref[...], seg_ref) * scale).astype(bf16)
    k_ref[...] = _head_norm(k, kg_ref[...], seg_ref).astype(bf16)
    v_ref[...] = v.astype(bf16)


def _ffn_in(x2d, g1, wg, wu, wd, g, w_in, qg, kg, seg):
    n_tok = x2d.shape[0]
    return pl.pallas_call(
        _ffn_in_kernel,
        out_shape=(jax.ShapeDtypeStruct((n_tok, D_MODEL), f32),
                   jax.ShapeDtypeStruct((n_tok, D_RNN), f32),
                   jax.ShapeDtypeStruct((n_tok, D_RNN), f32),
                   jax.ShapeDtypeStruct((n_tok, D_ATT), bf16),
                   jax.ShapeDtypeStruct((n_tok, D_ATT), bf16),
                   jax.ShapeDtypeStruct((n_tok, D_ATT), bf16)),
        grid=(n_tok // TM,),
        in_specs=[_tok_spec(D_MODEL)] + _ffn_specs() + [
            _const_spec((1, D_MODEL)), _const_spec((D_MODEL, N_IN)),
            _const_spec((1, D_ATT)), _const_spec((1, D_ATT)), _const_spec((D_ATT, D_ATT))],
        out_specs=(_tok_spec(D_MODEL), _tok_spec(D_RNN), _tok_spec(D_RNN),
                   _tok_spec(D_ATT), _tok_spec(D_ATT), _tok_spec(D_ATT)),
        scratch_shapes=[pltpu.VMEM((TM, D_FF), bf16)],
        compiler_params=pltpu.CompilerParams(
            dimension_semantics=("parallel",), vmem_limit_bytes=VMEM_LIMIT),
        name="ffn_in",
    )(x2d, g1, wg, wu, wd, g, w_in, qg, kg, seg)


HIST = 8


def _rglru_kernel(xr_ref, gate_ref, cw_ref, cb_ref, wa_ref, ba_ref, wx_ref, bx_ref,
                  lam_ref, g_ref, o_ref, xbuf_ref, hc_ref):
    @pl.when(pl.program_id(1) == 0)
    def _():
        xbuf_ref[0:HIST, :] = jnp.zeros((HIST, D_RNN), f32)
        hc_ref[...] = jnp.zeros((1, D_RNN), f32)

    x = xr_ref[0]
    xbuf_ref[HIST:HIST + TS, :] = x
    xc = cb_ref[...] + cw_ref[CONV_W - 1:CONV_W, :] * x
    for j in range(CONV_W - 1):
        off = HIST - (CONV_W - 1) + j
        xc = xc + cw_ref[j:j + 1, :] * xbuf_ref[off:off + TS, :]
    xbuf_ref[0:HIST, :] = x[TS - HIST:TS, :]

    xcb = xc.astype(bf16)
    r = jax.nn.sigmoid(_dot(xcb, wa_ref[...]) + ba_ref[...])
    i = jax.nn.sigmoid(_dot(xcb, wx_ref[...]) + bx_ref[...])
    lam = lam_ref[...]
    log_sig_lam = jnp.minimum(lam, 0.0) - jnp.log1p(jnp.exp(-jnp.abs(lam)))
    log_a = RG_C * r * log_sig_lam
    a = jnp.exp(log_a)
    th = jnp.tanh(log_a)
    mult = jnp.sqrt(-2.0 * th / (1.0 - th))
    b = mult * (i * xc)

    n_grp = TS // SUBLANES
    a = a.reshape(n_grp, SUBLANES, D_RNN)
    b = b.reshape(n_grp, SUBLANES, D_RNN)
    row = lax.broadcasted_iota(jnp.int32, (n_grp, SUBLANES, D_RNN), 1)
    d = 1
    while d < SUBLANES:
        a_sh = pltpu.roll(a, d, axis=1)
        b_sh = pltpu.roll(b, d, axis=1)
        m = row >= d
        b = jnp.where(m, a * b_sh, 0.0) + b
        a = jnp.where(m, a * a_sh, a)
        d *= 2
    carry = jnp.broadcast_to(hc_ref[...], (SUBLANES, D_RNN))
    groups = []
    for g in range(n_grp):
        h_g = a[g] * carry + b[g]
        groups.append(h_g)
        carry = jnp.broadcast_to(h_g[SUBLANES - 1:SUBLANES, :], (SUBLANES, D_RNN))
    h = jnp.concatenate(groups, axis=0)
    hc_ref[...] = groups[-1][SUBLANES - 1:SUBLANES, :]

    y = h * jax.nn.gelu(gate_ref[0])
    o_ref[0] = _rms(y, g_ref[...]).astype(bf16)


def _rglru(xr, gate, cw, cb, wa, ba, wx, bx, lam, g):
    bsz, seq, _ = xr.shape
    seq_spec = pl.BlockSpec((1, TS, D_RNN), lambda b, s: (b, s, 0))
    vec = _const_spec((1, D_RNN))
    return pl.pallas_call(
        _rglru_kernel,
        out_shape=jax.ShapeDtypeStruct((bsz, seq, D_RNN), bf16),
        grid=(bsz, seq // TS),
        in_specs=[seq_spec, seq_spec, _const_spec((CONV_W, D_RNN)), vec,
                  _const_spec((D_RNN, D_RNN)), vec, _const_spec((D_RNN, D_RNN)), vec, vec, vec],
        out_specs=seq_spec,
        scratch_shapes=[pltpu.VMEM((HIST + TS, D_RNN), f32), pltpu.VMEM((1, D_RNN), f32)],
        compiler_params=pltpu.CompilerParams(
            dimension_semantics=("parallel", "arbitrary"), vmem_limit_bytes=VMEM_LIMIT),
        name="rglru",
    )(xr, gate, cw, cb, wa, ba, wx, bx, lam, g)


PIPE_DEPTH = 4
BLK_PER_TILE = TK // LANES
N_BLK = HEADS_PER_STEP * BLK_PER_TILE
ITER_PER_TRIP = 4
LOG2E = 1.4426950408889634
PAD_F32 = SUBLANES
PAD_BF16 = 2 * SUBLANES


MASKED = -1e30


def _attn_kernel(qt_ref, kt_ref, q_ref, k_ref, v_ref, tri_ref, o_ref,
                 c_ref, z_ref, split_ref, p_ref, w_ref, d_ref, kh_ref, vh_ref):
    n_items = qt_ref.shape[0]
    seq = k_ref.shape[1]
    heads = range(HEADS_PER_STEP)
    for ref in (o_ref, c_ref, z_ref, split_ref, p_ref, w_ref):
        ref[...] = jnp.zeros_like(ref)
    for b in range(BLK_PER_TILE):
        d_ref[b] = (lax.broadcasted_iota(jnp.int32, (TQ, LANES), 1) + b * LANES
                    - lax.broadcasted_iota(jnp.int32, (TQ, LANES), 0))
    lane_head = lax.broadcasted_iota(jnp.int32, (seq, LANES), 1) // HEAD_DIM
    for h in heads:
        kh_ref[h] = jnp.where(lane_head == h, k_ref[0], jnp.zeros_like(k_ref[0]))
        vh_ref[h] = jnp.where(lane_head == h, v_ref[0], jnp.zeros_like(v_ref[0]))

    def item(j):
        j = jnp.clip(j, 0, n_items - 1)
        return qt_ref[j], kt_ref[j]

    def per_head_rows(ref, t):
        rows = pl.ds(pl.multiple_of(t * TK, TK), TK)
        return jnp.concatenate([ref[h, rows, :] for h in heads], axis=0)

    def stage_c2(qi, t):
        rows = pl.ds(pl.multiple_of(qi * TQ, TQ), TQ)
        acc = jnp.where(t == qi, 0.0, o_ref[0, rows, :])
        w = jnp.concatenate([w_ref[blk, 0:TQ] for blk in range(N_BLK)], axis=1)
        o_ref[0, rows, :] = acc + _dot(w, per_head_rows(vh_ref, t))

    def stage_c1(qi, t, slot):
        for h in heads:
            first = h * BLK_PER_TILE
            c = jnp.where(t == qi, 0.0, c_ref[h, 0:TQ])
            for blk in range(first, first + BLK_PER_TILE):
                w_ref[blk, 0:TQ] = jnp.exp(z_ref[slot, blk, 0:TQ] + p_ref[blk, 0:TQ] + c).astype(bf16)
            c_ref[h, 0:TQ] = c + jnp.broadcast_to(p_ref[first, 0:TQ, 0:1], (TQ, LANES))

    def stage_b2():
        for h in heads:
            first = h * BLK_PER_TILE
            parts = [split_ref[blk, 0:TQ] for blk in range(first, first + BLK_PER_TILE)]
            r = _dot(jnp.concatenate(parts, axis=1), tri_ref[...])
            for b in range(BLK_PER_TILE):
                p_ref[first + b, 0:TQ] = r[:, b * LANES:(b + 1) * LANES]

    def stage_b1(slot):
        for blk in range(N_BLK):
            z = z_ref[slot, blk, 0:TQ]
            sp = jnp.log(1.0 + jnp.exp2(jnp.abs(z) * (-LOG2E))) + jnp.maximum(z, 0.0)
            split_ref[blk, 0:TQ] = sp.astype(bf16)

    def stage_a(qi, t, slot):
        q = q_ref[0, pl.ds(pl.multiple_of(qi * TQ, TQ), TQ), :]
        z = lax.dot_general(q, per_head_rows(kh_ref, t), (((1,), (1,)), ((), ())),
                            preferred_element_type=f32)
        for b in range(BLK_PER_TILE):
            visible = d_ref[b] < (qi - t) * TQ
            for h in heads:
                blk = h * BLK_PER_TILE + b
                z_ref[slot, blk, 0:TQ] = jnp.where(visible, z[:, blk * LANES:(blk + 1) * LANES], MASKED)

    def iteration(i, u):
        stage_c2(*item(i - PIPE_DEPTH))
        stage_c1(*item(i - 3), (u - 3) % ITER_PER_TRIP)
        stage_b2()
        stage_b1((u - 1) % ITER_PER_TRIP)
        stage_a(*item(i), u)

    def body(m, carry):
        for u in range(ITER_PER_TRIP):
            iteration(ITER_PER_TRIP * m + u, u)
        return carry

    lax.fori_loop(0, (n_items + PIPE_DEPTH) // ITER_PER_TRIP, body, 0)


def _attention(q, k, v, tri):
    bsz, seq, _ = q.shape
    n_qt = seq // TQ
    qt = jnp.asarray([qi for qi in range(n_qt) for _ in range(qi + 1)], jnp.int32)
    kt = jnp.asarray([t for qi in range(n_qt) for t in range(qi, -1, -1)], jnp.int32)
    assert (qt.shape[0] + PIPE_DEPTH) % ITER_PER_TRIP == 0
    seq_spec = pl.BlockSpec((1, seq, LANES), lambda b, p, *_: (b, 0, p))
    tri_spec = pl.BlockSpec((TK, TK), lambda b, p, *_: (0, 0),
                            pipeline_mode=pl.Buffered(1))
    return pl.pallas_call(
        _attn_kernel,
        out_shape=jax.ShapeDtypeStruct((bsz, seq, D_ATT), f32),
        grid_spec=pltpu.PrefetchScalarGridSpec(
            num_scalar_prefetch=2,
            grid=(bsz, D_ATT // LANES),
            in_specs=[seq_spec, seq_spec, seq_spec, tri_spec],
            out_specs=seq_spec,
            scratch_shapes=[
                pltpu.VMEM((HEADS_PER_STEP, TQ + PAD_F32, LANES), f32),
                pltpu.VMEM((ITER_PER_TRIP, N_BLK, TQ + PAD_F32, LANES), f32),
                pltpu.VMEM((N_BLK, TQ + PAD_BF16, LANES), bf16),
                pltpu.VMEM((N_BLK, TQ + PAD_F32, LANES), f32),
                pltpu.VMEM((N_BLK, TQ + PAD_BF16, LANES), bf16),
                pltpu.VMEM((BLK_PER_TILE, TQ, LANES), jnp.int32),
                pltpu.VMEM((HEADS_PER_STEP, seq, LANES), bf16),
                pltpu.VMEM((HEADS_PER_STEP, seq, LANES), bf16),
            ]),
        compiler_params=pltpu.CompilerParams(
            dimension_semantics=("parallel", "parallel"), vmem_limit_bytes=VMEM_LIMIT),
        name="stickbreak",
    )(qt, kt, q, k, v, tri)


def _ffn_out_kernel(x_ref, yr_ref, ya_ref, ga_ref, wo_ref, g2_ref, wg_ref, wu_ref, wd_ref,
                    o_ref, h_ref):
    ya = _rms(ya_ref[...], ga_ref[...]).astype(bf16)
    x2 = (x_ref[...] + _dot(yr_ref[...], wo_ref[0:D_RNN, :])
          + _dot(ya, wo_ref[D_RNN:D_RNN + D_ATT, :]))
    o_ref[...] = _ffn_tile(x2, g2_ref, wg_ref, wu_ref, wd_ref, h_ref)


def _ffn_out(x2d, yr, ya, ga, w_out, g2, wg, wu, wd):
    n_tok = x2d.shape[0]
    return pl.pallas_call(
        _ffn_out_kernel,
        out_shape=jax.ShapeDtypeStruct((n_tok, D_MODEL), f32),
        grid=(n_tok // TM,),
        in_specs=[_tok_spec(D_MODEL), _tok_spec(D_RNN), _tok_spec(D_ATT), _const_spec((1, D_ATT)),
                  _const_spec((D_RNN + D_ATT, D_MODEL))] + _ffn_specs(),
        out_specs=_tok_spec(D_MODEL),
        scratch_shapes=[pltpu.VMEM((TM, D_FF), bf16)],
        compiler_params=pltpu.CompilerParams(
            dimension_semantics=("parallel",), vmem_limit_bytes=VMEM_LIMIT),
        name="ffn_out",
    )(x2d, yr, ya, ga, w_out, g2, wg, wu, wd)


def _block_diag(w):
    eye = jnp.eye(RNN_BLOCKS, dtype=w.dtype)
    return jnp.einsum("ncd,nm->ncmd", w, eye).reshape(D_RNN, D_RNN)


def _scan_matrix():
    j = jnp.arange(TK)
    return -(j[:, None] >= j[None, :]).astype(bf16)


def kernel(x, ffn1_norm, ffn1_w_gate, ffn1_w_up, ffn1_w_down, mix_norm, w_in, conv_w, conv_b,
           rg_w_a, rg_b_a, rg_w_x, rg_b_x, rg_lambda, q_norm, k_norm, rnn_out_norm,
           attn_out_norm, w_out, ffn2_norm, ffn2_w_gate, ffn2_w_up, ffn2_w_down):
    bsz, seq, _ = x.shape
    depth = ffn1_norm.shape[0]
    head_id = jnp.arange(D_ATT) // HEAD_DIM
    seg = (head_id[:, None] == head_id[None, :]).astype(bf16)
    tri = _scan_matrix()
    x2d = x.reshape(bsz * seq, D_MODEL)
    for l in range(depth):
        x2d, xr, gate, q, k, v = _ffn_in(
            x2d, ffn1_norm[l][None], ffn1_w_gate[l].astype(bf16), ffn1_w_up[l].astype(bf16),
            ffn1_w_down[l].astype(bf16), mix_norm[l][None], w_in[l].astype(bf16),
            jnp.tile(q_norm[l], N_HEADS)[None], jnp.tile(k_norm[l], N_HEADS)[None], seg)
        y_rnn = _rglru(
            xr.reshape(bsz, seq, D_RNN), gate.reshape(bsz, seq, D_RNN),
            conv_w[l], conv_b[l][None], _block_diag(rg_w_a[l]).astype(bf16), rg_b_a[l][None],
            _block_diag(rg_w_x[l]).astype(bf16), rg_b_x[l][None], rg_lambda[l][None],
            rnn_out_norm[l][None])
        y_att = _attention(q.reshape(bsz, seq, D_ATT), k.reshape(bsz, seq, D_ATT),
                           v.reshape(bsz, seq, D_ATT), tri)
        x2d = _ffn_out(x2d, y_rnn.reshape(bsz * seq, D_RNN), y_att.reshape(bsz * seq, D_ATT),
                       attn_out_norm[l][None], w_out[l].astype(bf16), ffn2_norm[l][None],
                       ffn2_w_gate[l].astype(bf16), ffn2_w_up[l].astype(bf16),
                       ffn2_w_down[l].astype(bf16))
    return x2d.reshape(bsz, seq, D_MODEL)
```

```python
import math

import jax
import jax.numpy as jnp
from jax import lax
from jax.experimental import pallas as pl
from jax.experimental.pallas import tpu as pltpu

D_MODEL = 1024
D_RNN = 512
RNN_BLOCKS = 8
RNN_BW = D_RNN // RNN_BLOCKS
CONV_W = 4
RG_C = 8.0
D_ATT = 512
HEAD_DIM = 64
N_HEADS = D_ATT // HEAD_DIM
D_FF = 2816
N_IN = 2 * D_RNN + 3 * D_ATT
EPS = 1e-6

LANES = 128
SUBLANES = 8
HEADS_PER_STEP = LANES // HEAD_DIM
VMEM_LIMIT = 56 * 1024 * 1024

TM = 512
FF_CHUNK = 256
TS = 512
TQ = 256
TK = 256

f32 = jnp.float32
bf16 = jnp.bfloat16


def _rms(xf, g):
    r = lax.rsqrt(jnp.mean(xf * xf, axis=-1, keepdims=True) + EPS)
    return xf * r * g


def _dot(a, b):
    return jnp.dot(a, b, preferred_element_type=f32)


def _split_bf16(x):
    hi = x.astype(bf16)
    lo = (x - hi.astype(f32)).astype(bf16)
    return hi, lo


def _const_spec(shape):
    nd = len(shape)
    return pl.BlockSpec(shape, lambda *_: (0,) * nd, pipeline_mode=pl.Buffered(1))


def _ffn_tile(x, g_ref, wg_ref, wu_ref, wd_ref, h_ref):
    xn = _rms(x, g_ref[...]).astype(bf16)
    for c in range(0, D_FF, FF_CHUNK):
        gate = _dot(xn, wg_ref[:, c:c + FF_CHUNK])
        up = _dot(xn, wu_ref[:, c:c + FF_CHUNK])
        h_ref[:, c:c + FF_CHUNK] = (jax.nn.silu(gate) * up).astype(bf16)
    return x + 0.5 * _dot(h_ref[...], wd_ref[...])


def _ffn_specs():
    return [_const_spec((1, D_MODEL)), _const_spec((D_MODEL, D_FF)),
            _const_spec((D_MODEL, D_FF)), _const_spec((D_FF, D_MODEL))]


def _tok_spec(d):
    return pl.BlockSpec((TM, d), lambda i: (i, 0))


def _head_norm(t, gain_tiled, seg_ref):
    hi, lo = _split_bf16(t * t)
    ss = _dot(hi, seg_ref[...]) + _dot(lo, seg_ref[...])
    r = lax.rsqrt(ss * (1.0 / HEAD_DIM) + EPS)
    return t * r * gain_tiled


def _ffn_in_kernel(x_ref, g1_ref, wg_ref, wu_ref, wd_ref, g_ref, w_ref, qg_ref, kg_ref, seg_ref,
                   o_ref, xr_ref, gate_ref, q_ref, k_ref, v_ref, h_ref):
    x1 = _ffn_tile(x_ref[...], g1_ref, wg_ref, wu_ref, wd_ref, h_ref)
    o_ref[...] = x1
    h = _rms(x1, g_ref[...]).astype(bf16)
    xr_ref[...] = _dot(h, w_ref[:, 0:D_RNN])
    gate_ref[...] = _dot(h, w_ref[:, D_RNN:2 * D_RNN])
    o = 2 * D_RNN
    q = _dot(h, w_ref[:, o:o + D_ATT])
    k = _dot(h, w_ref[:, o + D_ATT:o + 2 * D_ATT])
    v = _dot(h, w_ref[:, o + 2 * D_ATT:o + 3 * D_ATT])
    scale = 1.0 / math.sqrt(HEAD_DIM)
    q_ref[...] = (_head_norm(q, qg_ref[...], seg_ref) * scale).astype(bf16)
    k_ref[...] = _head_norm(k, kg_ref[...], seg_ref).astype(bf16)
    v_ref[...] = v.astype(bf16)


def _ffn_in(x2d, g1, wg, wu, wd, g, w_in, qg, kg, seg):
    n_tok = x2d.shape[0]
    return pl.pallas_call(
        _ffn_in_kernel,
        out_shape=(jax.ShapeDtypeStruct((n_tok, D_MODEL), f32),
                   jax.ShapeDtypeStruct((n_tok, D_RNN), f32),
                   jax.ShapeDtypeStruct((n_tok, D_RNN), f32),
                   jax.ShapeDtypeStruct((n_tok, D_ATT), bf16),
                   jax.ShapeDtypeStruct((n_tok, D_ATT), bf16),
                   jax.ShapeDtypeStruct((n_tok, D_ATT), bf16)),
        grid=(n_tok // TM,),
        in_specs=[_tok_spec(D_MODEL)] + _ffn_specs() + [
            _const_spec((1, D_MODEL)), _const_spec((D_MODEL, N_IN)),
            _const_spec((1, D_ATT)), _const_spec((1, D_ATT)), _const_spec((D_ATT, D_ATT))],
        out_specs=(_tok_spec(D_MODEL), _tok_spec(D_RNN), _tok_spec(D_RNN),
                   _tok_spec(D_ATT), _tok_spec(D_ATT), _tok_spec(D_ATT)),
        scratch_shapes=[pltpu.VMEM((TM, D_FF), bf16)],
        compiler_params=pltpu.CompilerParams(
            dimension_semantics=("parallel",), vmem_limit_bytes=VMEM_LIMIT),
        name="ffn_in",
    )(x2d, g1, wg, wu, wd, g, w_in, qg, kg, seg)


HIST = 8


def _rglru_kernel(xr_ref, gate_ref, cw_ref, cb_ref, wa_ref, ba_ref, wx_ref, bx_ref,
                  lam_ref, g_ref, o_ref, xbuf_ref, hc_ref):
    @pl.when(pl.program_id(1) == 0)
    def _():
        xbuf_ref[0:HIST, :] = jnp.zeros((HIST, D_RNN), f32)
        hc_ref[...] = jnp.zeros((1, D_RNN), f32)

    x = xr_ref[0]
    xbuf_ref[HIST:HIST + TS, :] = x
    xc = cb_ref[...] + cw_ref[CONV_W - 1:CONV_W, :] * x
    for j in range(CONV_W - 1):
        off = HIST - (CONV_W - 1) + j
        xc = xc + cw_ref[j:j + 1, :] * xbuf_ref[off:off + TS, :]
    xbuf_ref[0:HIST, :] = x[TS - HIST:TS, :]

    xcb = xc.astype(bf16)
    r = jax.nn.sigmoid(_dot(xcb, wa_ref[...]) + ba_ref[...])
    i = jax.nn.sigmoid(_dot(xcb, wx_ref[...]) + bx_ref[...])
    lam = lam_ref[...]
    log_sig_lam = jnp.minimum(lam, 0.0) - jnp.log1p(jnp.exp(-jnp.abs(lam)))
    log_a = RG_C * r * log_sig_lam
    a = jnp.exp(log_a)
    th = jnp.tanh(log_a)
    mult = jnp.sqrt(-2.0 * th / (1.0 - th))
    b = mult * (i * xc)

    n_grp = TS // SUBLANES
    a = a.reshape(n_grp, SUBLANES, D_RNN)
    b = b.reshape(n_grp, SUBLANES, D_RNN)
    row = lax.broadcasted_iota(jnp.int32, (n_grp, SUBLANES, D_RNN), 1)
    d = 1
    while d < SUBLANES:
        a_sh = pltpu.roll(a, d, axis=1)
        b_sh = pltpu.roll(b, d, axis=1)
        m = row >= d
        b = jnp.where(m, a * b_sh, 0.0) + b
        a = jnp.where(m, a * a_sh, a)
        d *= 2
    carry = jnp.broadcast_to(hc_ref[...], (SUBLANES, D_RNN))
    groups = []
    for g in range(n_grp):
        h_g = a[g] * carry + b[g]
        groups.append(h_g)
        carry = jnp.broadcast_to(h_g[SUBLANES - 1:SUBLANES, :], (SUBLANES, D_RNN))
    h = jnp.concatenate(groups, axis=0)
    hc_ref[...] = groups[-1][SUBLANES - 1:SUBLANES, :]

    y = h * jax.nn.gelu(gate_ref[0])
    o_ref[0] = _rms(y, g_ref[...]).astype(bf16)


def _rglru(xr, gate, cw, cb, wa, ba, wx, bx, lam, g):
    bsz, seq, _ = xr.shape
    seq_spec = pl.BlockSpec((1, TS, D_RNN), lambda b, s: (b, s, 0))
    vec = _const_spec((1, D_RNN))
    return pl.pallas_call(
        _rglru_kernel,
        out_shape=jax.ShapeDtypeStruct((bsz, seq, D_RNN), bf16),
        grid=(bsz, seq // TS),
        in_specs=[seq_spec, seq_spec, _const_spec((CONV_W, D_RNN)), vec,
                  _const_spec((D_RNN, D_RNN)), vec, _const_spec((D_RNN, D_RNN)), vec, vec, vec],
        out_specs=seq_spec,
        scratch_shapes=[pltpu.VMEM((HIST + TS, D_RNN), f32), pltpu.VMEM((1, D_RNN), f32)],
        compiler_params=pltpu.CompilerParams(
            dimension_semantics=("parallel", "arbitrary"), vmem_limit_bytes=VMEM_LIMIT),
        name="rglru",
    )(xr, gate, cw, cb, wa, ba, wx, bx, lam, g)


PIPE_DEPTH = 4
BLK_PER_TILE = TK // LANES
N_BLK = HEADS_PER_STEP * BLK_PER_TILE
ITER_PER_TRIP = 4
LOG2E = 1.4426950408889634


MASKED = -1e30


def _attn_kernel(qt_ref, kt_ref, q_ref, k_ref, v_ref, tri_ref, o_ref,
                 c_ref, z_ref, split_ref, p_ref, w_ref, d_ref):
    n_items = qt_ref.shape[0]
    heads = range(HEADS_PER_STEP)
    for ref in (o_ref, c_ref, z_ref, split_ref, p_ref, w_ref):
        ref[...] = jnp.zeros_like(ref)
    for b in range(BLK_PER_TILE):
        d_ref[b] = (lax.broadcasted_iota(jnp.int32, (TQ, LANES), 1) + b * LANES
                    - lax.broadcasted_iota(jnp.int32, (TQ, LANES), 0))
    lane_head = lax.broadcasted_iota(jnp.int32, (TK, LANES), 1) // HEAD_DIM

    def item(j):
        j = jnp.clip(j, 0, n_items - 1)
        return qt_ref[j], kt_ref[j]

    def per_head_rows(ref, t):
        x = ref[0, pl.ds(pl.multiple_of(t * TK, TK), TK), :]
        return jnp.concatenate([jnp.where(lane_head == h, x, jnp.zeros_like(x)) for h in heads], axis=0)

    def stage_c2(qi, t):
        rows = pl.ds(pl.multiple_of(qi * TQ, TQ), TQ)
        acc = jnp.where(t == qi, 0.0, o_ref[0, rows, :])
        w = jnp.concatenate([w_ref[blk] for blk in range(N_BLK)], axis=1)
        o_ref[0, rows, :] = acc + _dot(w, per_head_rows(v_ref, t))

    def stage_c1(qi, t, slot):
        for h in heads:
            first = h * BLK_PER_TILE
            c = jnp.where(t == qi, 0.0, c_ref[h])
            for blk in range(first, first + BLK_PER_TILE):
                w_ref[blk] = jnp.exp(z_ref[slot, blk] + p_ref[blk] + c).astype(bf16)
            c_ref[h] = c + jnp.broadcast_to(p_ref[first, :, 0:1], (TQ, LANES))

    def stage_b2():
        for h in heads:
            first = h * BLK_PER_TILE
            parts = [split_ref[blk] for blk in range(first, first + BLK_PER_TILE)]
            r = _dot(jnp.concatenate(parts, axis=1), tri_ref[...])
            for b in range(BLK_PER_TILE):
                p_ref[first + b] = r[:, b * LANES:(b + 1) * LANES]

    def stage_b1(slot):
        for blk in range(N_BLK):
            z = z_ref[slot, blk]
            sp = jnp.log(1.0 + jnp.exp2(jnp.abs(z) * (-LOG2E))) + jnp.maximum(z, 0.0)
            split_ref[blk] = sp.astype(bf16)

    def stage_a(qi, t, slot):
        q = q_ref[0, pl.ds(pl.multiple_of(qi * TQ, TQ), TQ), :]
        z = lax.dot_general(q, per_head_rows(k_ref, t), (((1,), (1,)), ((), ())),
                            preferred_element_type=f32)
        for b in range(BLK_PER_TILE):
            visible = d_ref[b] < (qi - t) * TQ
            for h in heads:
                blk = h * BLK_PER_TILE + b
                z_ref[slot, blk] = jnp.where(visible, z[:, blk * LANES:(blk + 1) * LANES], MASKED)

    def iteration(i, u):
        stage_c2(*item(i - PIPE_DEPTH))
        stage_c1(*item(i - 3), (u - 3) % ITER_PER_TRIP)
        stage_b2()
        stage_b1((u - 1) % ITER_PER_TRIP)
        stage_a(*item(i), u)

    def body(m, carry):
        for u in range(ITER_PER_TRIP):
            iteration(ITER_PER_TRIP * m + u, u)
        return carry

    lax.fori_loop(0, (n_items + PIPE_DEPTH) // ITER_PER_TRIP, body, 0)


def _attention(q, k, v, tri):
    bsz, seq, _ = q.shape
    n_qt = seq // TQ
    qt = jnp.asarray([qi for qi in range(n_qt) for _ in range(qi + 1)], jnp.int32)
    kt = jnp.asarray([t for qi in range(n_qt) for t in range(qi, -1, -1)], jnp.int32)
    assert (qt.shape[0] + PIPE_DEPTH) % ITER_PER_TRIP == 0
    seq_spec = pl.BlockSpec((1, seq, LANES), lambda b, p, *_: (b, 0, p))
    tri_spec = pl.BlockSpec((TK, TK), lambda b, p, *_: (0, 0),
                            pipeline_mode=pl.Buffered(1))
    return pl.pallas_call(
        _attn_kernel,
        out_shape=jax.ShapeDtypeStruct((bsz, seq, D_ATT), f32),
        grid_spec=pltpu.PrefetchScalarGridSpec(
            num_scalar_prefetch=2,
            grid=(bsz, D_ATT // LANES),
            in_specs=[seq_spec, seq_spec, seq_spec, tri_spec],
            out_specs=seq_spec,
            scratch_shapes=[
                pltpu.VMEM((HEADS_PER_STEP, TQ, LANES), f32),
                pltpu.VMEM((ITER_PER_TRIP, N_BLK, TQ, LANES), f32),
                pltpu.VMEM((N_BLK, TQ, LANES), bf16),
                pltpu.VMEM((N_BLK, TQ, LANES), f32),
                pltpu.VMEM((N_BLK, TQ, LANES), bf16),
                pltpu.VMEM((BLK_PER_TILE, TQ, LANES), jnp.int32),
            ]),
        compiler_params=pltpu.CompilerParams(
            dimension_semantics=("parallel", "parallel"), vmem_limit_bytes=VMEM_LIMIT),
        name="stickbreak",
    )(qt, kt, q, k, v, tri)


def _ffn_out_kernel(x_ref, yr_ref, ya_ref, ga_ref, wo_ref, g2_ref, wg_ref, wu_ref, wd_ref,
                    o_ref, h_ref):
    ya = _rms(ya_ref[...], ga_ref[...]).astype(bf16)
    x2 = (x_ref[...] + _dot(yr_ref[...], wo_ref[0:D_RNN, :])
          + _dot(ya, wo_ref[D_RNN:D_RNN + D_ATT, :]))
    o_ref[...] = _ffn_tile(x2, g2_ref, wg_ref, wu_ref, wd_ref, h_ref)


def _ffn_out(x2d, yr, ya, ga, w_out, g2, wg, wu, wd):
    n_tok = x2d.shape[0]
    return pl.pallas_call(
        _ffn_out_kernel,
        out_shape=jax.ShapeDtypeStruct((n_tok, D_MODEL), f32),
        grid=(n_tok // TM,),
        in_specs=[_tok_spec(D_MODEL), _tok_spec(D_RNN), _tok_spec(D_ATT), _const_spec((1, D_ATT)),
                  _const_spec((D_RNN + D_ATT, D_MODEL))] + _ffn_specs(),
        out_specs=_tok_spec(D_MODEL),
        scratch_shapes=[pltpu.VMEM((TM, D_FF), bf16)],
        compiler_params=pltpu.CompilerParams(
            dimension_semantics=("parallel",), vmem_limit_bytes=VMEM_LIMIT),
        name="ffn_out",
    )(x2d, yr, ya, ga, w_out, g2, wg, wu, wd)


def _block_diag(w):
    eye = jnp.eye(RNN_BLOCKS, dtype=w.dtype)
    return jnp.einsum("ncd,nm->ncmd", w, eye).reshape(D_RNN, D_RNN)


def _scan_matrix():
    j = jnp.arange(TK)
    return -(j[:, None] >= j[None, :]).astype(bf16)


def kernel(x, ffn1_norm, ffn1_w_gate, ffn1_w_up, ffn1_w_down, mix_norm, w_in, conv_w, conv_b,
           rg_w_a, rg_b_a, rg_w_x, rg_b_x, rg_lambda, q_norm, k_norm, rnn_out_norm,
           attn_out_norm, w_out, ffn2_norm, ffn2_w_gate, ffn2_w_up, ffn2_w_down):
    bsz, seq, _ = x.shape
    depth = ffn1_norm.shape[0]
    head_id = jnp.arange(D_ATT) // HEAD_DIM
    seg = (head_id[:, None] == head_id[None, :]).astype(bf16)
    tri = _scan_matrix()
    x2d = x.reshape(bsz * seq, D_MODEL)
    for l in range(depth):
        x2d, xr, gate, q, k, v = _ffn_in(
            x2d, ffn1_norm[l][None], ffn1_w_gate[l].astype(bf16), ffn1_w_up[l].astype(bf16),
            ffn1_w_down[l].astype(bf16), mix_norm[l][None], w_in[l].astype(bf16),
            jnp.tile(q_norm[l], N_HEADS)[None], jnp.tile(k_norm[l], N_HEADS)[None], seg)
        y_rnn = _rglru(
            xr.reshape(bsz, seq, D_RNN), gate.reshape(bsz, seq, D_RNN),
            conv_w[l], conv_b[l][None], _block_diag(rg_w_a[l]).astype(bf16), rg_b_a[l][None],
            _block_diag(rg_w_x[l]).astype(bf16), rg_b_x[l][None], rg_lambda[l][None],
            rnn_out_norm[l][None])
        y_att = _attention(q.reshape(bsz, seq, D_ATT), k.reshape(bsz, seq, D_ATT),
                           v.reshape(bsz, seq, D_ATT), tri)
        x2d = _ffn_out(x2d, y_rnn.reshape(bsz * seq, D_RNN), y_att.reshape(bsz * seq, D_ATT),
                       attn_out_norm[l][None], w_out[l].astype(bf16), ffn2_norm[l][None],
                       ffn2_w_gate[l].astype(bf16), ffn2_w_up[l].astype(bf16),
                       ffn2_w_down[l].astype(bf16))
    return x2d.reshape(bsz, seq, D_MODEL)
```

```python
import math

import jax
import jax.numpy as jnp
from jax import lax
from jax.experimental import pallas as pl
from jax.experimental.pallas import tpu as pltpu

D_MODEL = 1024
D_RNN = 512
RNN_BLOCKS = 8
RNN_BW = D_RNN // RNN_BLOCKS
CONV_W = 4
RG_C = 8.0
D_ATT = 512
HEAD_DIM = 64
N_HEADS = D_ATT // HEAD_DIM
D_FF = 2816
N_IN = 2 * D_RNN + 3 * D_ATT
EPS = 1e-6

LANES = 128
SUBLANES = 8
HEADS_PER_STEP = LANES // HEAD_DIM
VMEM_LIMIT = 56 * 1024 * 1024

TM = 512
FF_CHUNK = 256
TS = 512
TQ = 512
TK = 256

f32 = jnp.float32
bf16 = jnp.bfloat16


def _rms(xf, g):
    r = lax.rsqrt(jnp.mean(xf * xf, axis=-1, keepdims=True) + EPS)
    return xf * r * g


def _dot(a, b):
    return jnp.dot(a, b, preferred_element_type=f32)


def _split_bf16(x):
    hi = x.astype(bf16)
    lo = (x - hi.astype(f32)).astype(bf16)
    return hi, lo


def _const_spec(shape):
    nd = len(shape)
    return pl.BlockSpec(shape, lambda *_: (0,) * nd, pipeline_mode=pl.Buffered(1))


def _ffn_tile(x, g_ref, wg_ref, wu_ref, wd_ref, h_ref):
    xn = _rms(x, g_ref[...]).astype(bf16)
    for c in range(0, D_FF, FF_CHUNK):
        gate = _dot(xn, wg_ref[:, c:c + FF_CHUNK])
        up = _dot(xn, wu_ref[:, c:c + FF_CHUNK])
        h_ref[:, c:c + FF_CHUNK] = (jax.nn.silu(gate) * up).astype(bf16)
    return x + 0.5 * _dot(h_ref[...], wd_ref[...])


def _ffn_specs():
    return [_const_spec((1, D_MODEL)), _const_spec((D_MODEL, D_FF)),
            _const_spec((D_MODEL, D_FF)), _const_spec((D_FF, D_MODEL))]


def _tok_spec(d):
    return pl.BlockSpec((TM, d), lambda i: (i, 0))


def _head_norm(t, gain_tiled, seg_ref):
    hi, lo = _split_bf16(t * t)
    ss = _dot(hi, seg_ref[...]) + _dot(lo, seg_ref[...])
    r = lax.rsqrt(ss * (1.0 / HEAD_DIM) + EPS)
    return t * r * gain_tiled


def _ffn_in_kernel(x_ref, g1_ref, wg_ref, wu_ref, wd_ref, g_ref, w_ref, qg_ref, kg_ref, seg_ref,
                   o_ref, xr_ref, gate_ref, q_ref, k_ref, v_ref, h_ref):
    x1 = _ffn_tile(x_ref[...], g1_ref, wg_ref, wu_ref, wd_ref, h_ref)
    o_ref[...] = x1
    h = _rms(x1, g_ref[...]).astype(bf16)
    xr_ref[...] = _dot(h, w_ref[:, 0:D_RNN])
    gate_ref[...] = _dot(h, w_ref[:, D_RNN:2 * D_RNN])
    o = 2 * D_RNN
    q = _dot(h, w_ref[:, o:o + D_ATT])
    k = _dot(h, w_ref[:, o + D_ATT:o + 2 * D_ATT])
    v = _dot(h, w_ref[:, o + 2 * D_ATT:o + 3 * D_ATT])
    scale = 1.0 / math.sqrt(HEAD_DIM)
    q_ref[...] = (_head_norm(q, qg_ref[...], seg_ref) * scale).astype(bf16)
    k_ref[...] = _head_norm(k, kg_ref[...], seg_ref).astype(bf16)
    v_ref[...] = v.astype(bf16)


def _ffn_in(x2d, g1, wg, wu, wd, g, w_in, qg, kg, seg):
    n_tok = x2d.shape[0]
    return pl.pallas_call(
        _ffn_in_kernel,
        out_shape=(jax.ShapeDtypeStruct((n_tok, D_MODEL), f32),
                   jax.ShapeDtypeStruct((n_tok, D_RNN), f32),
                   jax.ShapeDtypeStruct((n_tok, D_RNN), f32),
                   jax.ShapeDtypeStruct((n_tok, D_ATT), bf16),
                   jax.ShapeDtypeStruct((n_tok, D_ATT), bf16),
                   jax.ShapeDtypeStruct((n_tok, D_ATT), bf16)),
        grid=(n_tok // TM,),
        in_specs=[_tok_spec(D_MODEL)] + _ffn_specs() + [
            _const_spec((1, D_MODEL)), _const_spec((D_MODEL, N_IN)),
            _const_spec((1, D_ATT)), _const_spec((1, D_ATT)), _const_spec((D_ATT, D_ATT))],
        out_specs=(_tok_spec(D_MODEL), _tok_spec(D_RNN), _tok_spec(D_RNN),
                   _tok_spec(D_ATT), _tok_spec(D_ATT), _tok_spec(D_ATT)),
        scratch_shapes=[pltpu.VMEM((TM, D_FF), bf16)],
        compiler_params=pltpu.CompilerParams(
            dimension_semantics=("parallel",), vmem_limit_bytes=VMEM_LIMIT),
        name="ffn_in",
    )(x2d, g1, wg, wu, wd, g, w_in, qg, kg, seg)


HIST = 8


def _rglru_kernel(xr_ref, gate_ref, cw_ref, cb_ref, wa_ref, ba_ref, wx_ref, bx_ref,
                  lam_ref, g_ref, o_ref, xbuf_ref, hc_ref):
    @pl.when(pl.program_id(1) == 0)
    def _():
        xbuf_ref[0:HIST, :] = jnp.zeros((HIST, D_RNN), f32)
        hc_ref[...] = jnp.zeros((1, D_RNN), f32)

    x = xr_ref[0]
    xbuf_ref[HIST:HIST + TS, :] = x
    xc = cb_ref[...] + cw_ref[CONV_W - 1:CONV_W, :] * x
    for j in range(CONV_W - 1):
        off = HIST - (CONV_W - 1) + j
        xc = xc + cw_ref[j:j + 1, :] * xbuf_ref[off:off + TS, :]
    xbuf_ref[0:HIST, :] = x[TS - HIST:TS, :]

    xcb = xc.astype(bf16)
    r = jax.nn.sigmoid(_dot(xcb, wa_ref[...]) + ba_ref[...])
    i = jax.nn.sigmoid(_dot(xcb, wx_ref[...]) + bx_ref[...])
    lam = lam_ref[...]
    log_sig_lam = jnp.minimum(lam, 0.0) - jnp.log1p(jnp.exp(-jnp.abs(lam)))
    log_a = RG_C * r * log_sig_lam
    a = jnp.exp(log_a)
    th = jnp.tanh(log_a)
    mult = jnp.sqrt(-2.0 * th / (1.0 - th))
    b = mult * (i * xc)

    n_grp = TS // SUBLANES
    a = a.reshape(n_grp, SUBLANES, D_RNN)
    b = b.reshape(n_grp, SUBLANES, D_RNN)
    row = lax.broadcasted_iota(jnp.int32, (n_grp, SUBLANES, D_RNN), 1)
    d = 1
    while d < SUBLANES:
        a_sh = pltpu.roll(a, d, axis=1)
        b_sh = pltpu.roll(b, d, axis=1)
        m = row >= d
        b = jnp.where(m, a * b_sh, 0.0) + b
        a = jnp.where(m, a * a_sh, a)
        d *= 2
    carry = jnp.broadcast_to(hc_ref[...], (SUBLANES, D_RNN))
    groups = []
    for g in range(n_grp):
        h_g = a[g] * carry + b[g]
        groups.append(h_g)
        carry = jnp.broadcast_to(h_g[SUBLANES - 1:SUBLANES, :], (SUBLANES, D_RNN))
    h = jnp.concatenate(groups, axis=0)
    hc_ref[...] = groups[-1][SUBLANES - 1:SUBLANES, :]

    y = h * jax.nn.gelu(gate_ref[0])
    o_ref[0] = _rms(y, g_ref[...]).astype(bf16)


def _rglru(xr, gate, cw, cb, wa, ba, wx, bx, lam, g):
    bsz, seq, _ = xr.shape
    seq_spec = pl.BlockSpec((1, TS, D_RNN), lambda b, s: (b, s, 0))
    vec = _const_spec((1, D_RNN))
    return pl.pallas_call(
        _rglru_kernel,
        out_shape=jax.ShapeDtypeStruct((bsz, seq, D_RNN), bf16),
        grid=(bsz, seq // TS),
        in_specs=[seq_spec, seq_spec, _const_spec((CONV_W, D_RNN)), vec,
                  _const_spec((D_RNN, D_RNN)), vec, _const_spec((D_RNN, D_RNN)), vec, vec, vec],
        out_specs=seq_spec,
        scratch_shapes=[pltpu.VMEM((HIST + TS, D_RNN), f32), pltpu.VMEM((1, D_RNN), f32)],
        compiler_params=pltpu.CompilerParams(
            dimension_semantics=("parallel", "arbitrary"), vmem_limit_bytes=VMEM_LIMIT),
        name="rglru",
    )(xr, gate, cw, cb, wa, ba, wx, bx, lam, g)


PIPE_DEPTH = 4
BLK_PER_TILE = TK // LANES
KT_PER_QT = TQ // TK
N_BLK = HEADS_PER_STEP * BLK_PER_TILE
ITER_PER_TRIP = 4
LOG2E = 1.4426950408889634


MASKED = -1e30


def _attn_kernel(qt_ref, kt_ref, q_ref, k_ref, v_ref, tri_ref, o_ref,
                 c_ref, z_ref, split_ref, p_ref, w_ref, d_ref):
    n_items = qt_ref.shape[0]
    heads = range(HEADS_PER_STEP)
    for ref in (o_ref, c_ref, z_ref, split_ref, p_ref, w_ref):
        ref[...] = jnp.zeros_like(ref)
    for b in range(BLK_PER_TILE):
        d_ref[b] = (lax.broadcasted_iota(jnp.int32, (TQ, LANES), 1) + b * LANES
                    - lax.broadcasted_iota(jnp.int32, (TQ, LANES), 0))
    lane_head = lax.broadcasted_iota(jnp.int32, (TK, LANES), 1) // HEAD_DIM

    def item(j):
        j = jnp.clip(j, 0, n_items - 1)
        return qt_ref[j], kt_ref[j]

    def diag_tile(qi):
        return (qi + 1) * KT_PER_QT - 1

    def per_head_rows(ref, t):
        x = ref[0, pl.ds(pl.multiple_of(t * TK, TK), TK), :]
        return jnp.concatenate([jnp.where(lane_head == h, x, jnp.zeros_like(x)) for h in heads], axis=0)

    def stage_c2(qi, t):
        rows = pl.ds(pl.multiple_of(qi * TQ, TQ), TQ)
        acc = jnp.where(t == diag_tile(qi), 0.0, o_ref[0, rows, :])
        w = jnp.concatenate([w_ref[blk] for blk in range(N_BLK)], axis=1)
        o_ref[0, rows, :] = acc + _dot(w, per_head_rows(v_ref, t))

    def stage_c1(qi, t, slot):
        for h in heads:
            first = h * BLK_PER_TILE
            c = jnp.where(t == diag_tile(qi), 0.0, c_ref[h])
            for blk in range(first, first + BLK_PER_TILE):
                w_ref[blk] = jnp.exp(z_ref[slot, blk] + p_ref[blk] + c).astype(bf16)
            c_ref[h] = c + jnp.broadcast_to(p_ref[first, :, 0:1], (TQ, LANES))

    def stage_b2():
        for h in heads:
            first = h * BLK_PER_TILE
            parts = [split_ref[blk] for blk in range(first, first + BLK_PER_TILE)]
            r = _dot(jnp.concatenate(parts, axis=1), tri_ref[...])
            for b in range(BLK_PER_TILE):
                p_ref[first + b] = r[:, b * LANES:(b + 1) * LANES]

    def stage_b1(slot):
        for blk in range(N_BLK):
            z = z_ref[slot, blk]
            sp = jnp.log(1.0 + jnp.exp2(jnp.abs(z) * (-LOG2E))) + jnp.maximum(z, 0.0)
            split_ref[blk] = sp.astype(bf16)

    def stage_a(qi, t, slot):
        q = q_ref[0, pl.ds(pl.multiple_of(qi * TQ, TQ), TQ), :]
        z = lax.dot_general(q, per_head_rows(k_ref, t), (((1,), (1,)), ((), ())),
                            preferred_element_type=f32)
        for b in range(BLK_PER_TILE):
            visible = d_ref[b] < qi * TQ - t * TK
            for h in heads:
                blk = h * BLK_PER_TILE + b
                z_ref[slot, blk] = jnp.where(visible, z[:, blk * LANES:(blk + 1) * LANES], MASKED)

    def iteration(i, u):
        stage_c2(*item(i - PIPE_DEPTH))
        stage_c1(*item(i - 3), (u - 3) % ITER_PER_TRIP)
        stage_b2()
        stage_b1((u - 1) % ITER_PER_TRIP)
        stage_a(*item(i), u)

    def body(m, carry):
        for u in range(ITER_PER_TRIP):
            iteration(ITER_PER_TRIP * m + u, u)
        return carry

    lax.fori_loop(0, (n_items + PIPE_DEPTH) // ITER_PER_TRIP, body, 0)


def _attention(q, k, v, tri):
    bsz, seq, _ = q.shape
    n_qt = seq // TQ
    items = [(qi, t) for qi in range(n_qt) for t in range((qi + 1) * KT_PER_QT - 1, -1, -1)]
    qt = jnp.asarray([qi for qi, _ in items], jnp.int32)
    kt = jnp.asarray([t for _, t in items], jnp.int32)
    assert (qt.shape[0] + PIPE_DEPTH) % ITER_PER_TRIP == 0
    seq_spec = pl.BlockSpec((1, seq, LANES), lambda b, p, *_: (b, 0, p))
    tri_spec = pl.BlockSpec((TK, TK), lambda b, p, *_: (0, 0),
                            pipeline_mode=pl.Buffered(1))
    return pl.pallas_call(
        _attn_kernel,
        out_shape=jax.ShapeDtypeStruct((bsz, seq, D_ATT), f32),
        grid_spec=pltpu.PrefetchScalarGridSpec(
            num_scalar_prefetch=2,
            grid=(bsz, D_ATT // LANES),
            in_specs=[seq_spec, seq_spec, seq_spec, tri_spec],
            out_specs=seq_spec,
            scratch_shapes=[
                pltpu.VMEM((HEADS_PER_STEP, TQ, LANES), f32),
                pltpu.VMEM((ITER_PER_TRIP, N_BLK, TQ, LANES), f32),
                pltpu.VMEM((N_BLK, TQ, LANES), bf16),
                pltpu.VMEM((N_BLK, TQ, LANES), f32),
                pltpu.VMEM((N_BLK, TQ, LANES), bf16),
                pltpu.VMEM((BLK_PER_TILE, TQ, LANES), jnp.int32),
            ]),
        compiler_params=pltpu.CompilerParams(
            dimension_semantics=("parallel", "parallel"), vmem_limit_bytes=VMEM_LIMIT),
        name="stickbreak",
    )(qt, kt, q, k, v, tri)


def _ffn_out_kernel(x_ref, yr_ref, ya_ref, ga_ref, wo_ref, g2_ref, wg_ref, wu_ref, wd_ref,
                    o_ref, h_ref):
    ya = _rms(ya_ref[...], ga_ref[...]).astype(bf16)
    x2 = (x_ref[...] + _dot(yr_ref[...], wo_ref[0:D_RNN, :])
          + _dot(ya, wo_ref[D_RNN:D_RNN + D_ATT, :]))
    o_ref[...] = _ffn_tile(x2, g2_ref, wg_ref, wu_ref, wd_ref, h_ref)


def _ffn_out(x2d, yr, ya, ga, w_out, g2, wg, wu, wd):
    n_tok = x2d.shape[0]
    return pl.pallas_call(
        _ffn_out_kernel,
        out_shape=jax.ShapeDtypeStruct((n_tok, D_MODEL), f32),
        grid=(n_tok // TM,),
        in_specs=[_tok_spec(D_MODEL), _tok_spec(D_RNN), _tok_spec(D_ATT), _const_spec((1, D_ATT)),
                  _const_spec((D_RNN + D_ATT, D_MODEL))] + _ffn_specs(),
        out_specs=_tok_spec(D_MODEL),
        scratch_shapes=[pltpu.VMEM((TM, D_FF), bf16)],
        compiler_params=pltpu.CompilerParams(
            dimension_semantics=("parallel",), vmem_limit_bytes=VMEM_LIMIT),
        name="ffn_out",
    )(x2d, yr, ya, ga, w_out, g2, wg, wu, wd)


def _block_diag(w):
    eye = jnp.eye(RNN_BLOCKS, dtype=w.dtype)
    return jnp.einsum("ncd,nm->ncmd", w, eye).reshape(D_RNN, D_RNN)


def _scan_matrix():
    j = jnp.arange(TK)
    return -(j[:, None] >= j[None, :]).astype(bf16)


def kernel(x, ffn1_norm, ffn1_w_gate, ffn1_w_up, ffn1_w_down, mix_norm, w_in, conv_w, conv_b,
           rg_w_a, rg_b_a, rg_w_x, rg_b_x, rg_lambda, q_norm, k_norm, rnn_out_norm,
           attn_out_norm, w_out, ffn2_norm, ffn2_w_gate, ffn2_w_up, ffn2_w_down):
    bsz, seq, _ = x.shape
    depth = ffn1_norm.shape[0]
    head_id = jnp.arange(D_ATT) // HEAD_DIM
    seg = (head_id[:, None] == head_id[None, :]).astype(bf16)
    tri = _scan_matrix()
    x2d = x.reshape(bsz * seq, D_MODEL)
    for l in range(depth):
        x2d, xr, gate, q, k, v = _ffn_in(
            x2d, ffn1_norm[l][None], ffn1_w_gate[l].astype(bf16), ffn1_w_up[l].astype(bf16),
            ffn1_w_down[l].astype(bf16), mix_norm[l][None], w_in[l].astype(bf16),
            jnp.tile(q_norm[l], N_HEADS)[None], jnp.tile(k_norm[l], N_HEADS)[None], seg)
        y_rnn = _rglru(
            xr.reshape(bsz, seq, D_RNN), gate.reshape(bsz, seq, D_RNN),
            conv_w[l], conv_b[l][None], _block_diag(rg_w_a[l]).astype(bf16), rg_b_a[l][None],
            _block_diag(rg_w_x[l]).astype(bf16), rg_b_x[l][None], rg_lambda[l][None],
            rnn_out_norm[l][None])
        y_att = _attention(q.reshape(bsz, seq, D_ATT), k.reshape(bsz, seq, D_ATT),
                           v.reshape(bsz, seq, D_ATT), tri)
        x2d = _ffn_out(x2d, y_rnn.reshape(bsz * seq, D_RNN), y_att.reshape(bsz * seq, D_ATT),
                       attn_out_norm[l][None], w_out[l].astype(bf16), ffn2_norm[l][None],
                       ffn2_w_gate[l].astype(bf16), ffn2_w_up[l].astype(bf16),
                       ffn2_w_down[l].astype(bf16))
    return x2d.reshape(bsz, seq, D_MODEL)
```

```python
import math

import jax
import jax.numpy as jnp
from jax import lax
from jax.experimental import pallas as pl
from jax.experimental.pallas import tpu as pltpu

D_MODEL = 1024
D_RNN = 512
RNN_BLOCKS = 8
RNN_BW = D_RNN // RNN_BLOCKS
CONV_W = 4
RG_C = 8.0
D_ATT = 512
HEAD_DIM = 64
N_HEADS = D_ATT // HEAD_DIM
D_FF = 2816
N_IN = 2 * D_RNN + 3 * D_ATT
EPS = 1e-6

LANES = 128
SUBLANES = 8
HEADS_PER_STEP = LANES // HEAD_DIM
VMEM_LIMIT = 56 * 1024 * 1024

TM = 512
FF_CHUNK = 256
TS = 512
TQ = 512
TK = 256

f32 = jnp.float32
bf16 = jnp.bfloat16


def _rms(xf, g):
    r = lax.rsqrt(jnp.mean(xf * xf, axis=-1, keepdims=True) + EPS)
    return xf * r * g


def _dot(a, b):
    return jnp.dot(a, b, preferred_element_type=f32)


def _split_bf16(x):
    hi = x.astype(bf16)
    lo = (x - hi.astype(f32)).astype(bf16)
    return hi, lo


def _const_spec(shape):
    nd = len(shape)
    return pl.BlockSpec(shape, lambda *_: (0,) * nd, pipeline_mode=pl.Buffered(1))


def _ffn_tile(x, g_ref, wg_ref, wu_ref, wd_ref, h_ref):
    xn = _rms(x, g_ref[...]).astype(bf16)
    for c in range(0, D_FF, FF_CHUNK):
        gate = _dot(xn, wg_ref[:, c:c + FF_CHUNK])
        up = _dot(xn, wu_ref[:, c:c + FF_CHUNK])
        h_ref[:, c:c + FF_CHUNK] = (jax.nn.silu(gate) * up).astype(bf16)
    return x + 0.5 * _dot(h_ref[...], wd_ref[...])


def _ffn_specs():
    return [_const_spec((1, D_MODEL)), _const_spec((D_MODEL, D_FF)),
            _const_spec((D_MODEL, D_FF)), _const_spec((D_FF, D_MODEL))]


def _tok_spec(d):
    return pl.BlockSpec((TM, d), lambda i: (i, 0))


def _head_norm(t, gain_tiled, seg_ref):
    hi, lo = _split_bf16(t * t)
    ss = _dot(hi, seg_ref[...]) + _dot(lo, seg_ref[...])
    r = lax.rsqrt(ss * (1.0 / HEAD_DIM) + EPS)
    return t * r * gain_tiled


def _ffn_in_kernel(x_ref, g1_ref, wg_ref, wu_ref, wd_ref, g_ref, w_ref, qg_ref, kg_ref, seg_ref,
                   o_ref, xr_ref, gate_ref, q_ref, k_ref, v_ref, h_ref):
    x1 = _ffn_tile(x_ref[...], g1_ref, wg_ref, wu_ref, wd_ref, h_ref)
    o_ref[...] = x1
    h = _rms(x1, g_ref[...]).astype(bf16)
    xr_ref[...] = _dot(h, w_ref[:, 0:D_RNN])
    gate_ref[...] = _dot(h, w_ref[:, D_RNN:2 * D_RNN])
    o = 2 * D_RNN
    q = _dot(h, w_ref[:, o:o + D_ATT])
    k = _dot(h, w_ref[:, o + D_ATT:o + 2 * D_ATT])
    v = _dot(h, w_ref[:, o + 2 * D_ATT:o + 3 * D_ATT])
    scale = 1.0 / math.sqrt(HEAD_DIM)
    q_ref[...] = (_head_norm(q, qg_ref[...], seg_ref) * scale).astype(bf16)
    k_ref[...] = _head_norm(k, kg_ref[...], seg_ref).astype(bf16)
    v_ref[...] = v.astype(bf16)


def _ffn_in(x2d, g1, wg, wu, wd, g, w_in, qg, kg, seg):
    n_tok = x2d.shape[0]
    return pl.pallas_call(
        _ffn_in_kernel,
        out_shape=(jax.ShapeDtypeStruct((n_tok, D_MODEL), f32),
                   jax.ShapeDtypeStruct((n_tok, D_RNN), f32),
                   jax.ShapeDtypeStruct((n_tok, D_RNN), f32),
                   jax.ShapeDtypeStruct((n_tok, D_ATT), bf16),
                   jax.ShapeDtypeStruct((n_tok, D_ATT), bf16),
                   jax.ShapeDtypeStruct((n_tok, D_ATT), bf16)),
        grid=(n_tok // TM,),
        in_specs=[_tok_spec(D_MODEL)] + _ffn_specs() + [
            _const_spec((1, D_MODEL)), _const_spec((D_MODEL, N_IN)),
            _const_spec((1, D_ATT)), _const_spec((1, D_ATT)), _const_spec((D_ATT, D_ATT))],
        out_specs=(_tok_spec(D_MODEL), _tok_spec(D_RNN), _tok_spec(D_RNN),
                   _tok_spec(D_ATT), _tok_spec(D_ATT), _tok_spec(D_ATT)),
        scratch_shapes=[pltpu.VMEM((TM, D_FF), bf16)],
        compiler_params=pltpu.CompilerParams(
            dimension_semantics=("parallel",), vmem_limit_bytes=VMEM_LIMIT),
        name="ffn_in",
    )(x2d, g1, wg, wu, wd, g, w_in, qg, kg, seg)


HIST = 8


def _rglru_kernel(xr_ref, gate_ref, cw_ref, cb_ref, wa_ref, ba_ref, wx_ref, bx_ref,
                  lam_ref, g_ref, o_ref, xbuf_ref, hc_ref):
    @pl.when(pl.program_id(1) == 0)
    def _():
        xbuf_ref[0:HIST, :] = jnp.zeros((HIST, D_RNN), f32)
        hc_ref[...] = jnp.zeros((1, D_RNN), f32)

    x = xr_ref[0]
    xbuf_ref[HIST:HIST + TS, :] = x
    xc = cb_ref[...] + cw_ref[CONV_W - 1:CONV_W, :] * x
    for j in range(CONV_W - 1):
        off = HIST - (CONV_W - 1) + j
        xc = xc + cw_ref[j:j + 1, :] * xbuf_ref[off:off + TS, :]
    xbuf_ref[0:HIST, :] = x[TS - HIST:TS, :]

    xcb = xc.astype(bf16)
    r = jax.nn.sigmoid(_dot(xcb, wa_ref[...]) + ba_ref[...])
    i = jax.nn.sigmoid(_dot(xcb, wx_ref[...]) + bx_ref[...])
    lam = lam_ref[...]
    log_sig_lam = jnp.minimum(lam, 0.0) - jnp.log1p(jnp.exp(-jnp.abs(lam)))
    log_a = RG_C * r * log_sig_lam
    a = jnp.exp(log_a)
    th = jnp.tanh(log_a)
    mult = jnp.sqrt(-2.0 * th / (1.0 - th))
    b = mult * (i * xc)

    n_grp = TS // SUBLANES
    a = a.reshape(n_grp, SUBLANES, D_RNN)
    b = b.reshape(n_grp, SUBLANES, D_RNN)
    row = lax.broadcasted_iota(jnp.int32, (n_grp, SUBLANES, D_RNN), 1)
    d = 1
    while d < SUBLANES:
        a_sh = pltpu.roll(a, d, axis=1)
        b_sh = pltpu.roll(b, d, axis=1)
        m = row >= d
        b = jnp.where(m, a * b_sh, 0.0) + b
        a = jnp.where(m, a * a_sh, a)
        d *= 2
    carry = jnp.broadcast_to(hc_ref[...], (SUBLANES, D_RNN))
    groups = []
    for g in range(n_grp):
        h_g = a[g] * carry + b[g]
        groups.append(h_g)
        carry = jnp.broadcast_to(h_g[SUBLANES - 1:SUBLANES, :], (SUBLANES, D_RNN))
    h = jnp.concatenate(groups, axis=0)
    hc_ref[...] = groups[-1][SUBLANES - 1:SUBLANES, :]

    y = h * jax.nn.gelu(gate_ref[0])
    o_ref[0] = _rms(y, g_ref[...]).astype(bf16)


def _rglru(xr, gate, cw, cb, wa, ba, wx, bx, lam, g):
    bsz, seq, _ = xr.shape
    seq_spec = pl.BlockSpec((1, TS, D_RNN), lambda b, s: (b, s, 0))
    vec = _const_spec((1, D_RNN))
    return pl.pallas_call(
        _rglru_kernel,
        out_shape=jax.ShapeDtypeStruct((bsz, seq, D_RNN), bf16),
        grid=(bsz, seq // TS),
        in_specs=[seq_spec, seq_spec, _const_spec((CONV_W, D_RNN)), vec,
                  _const_spec((D_RNN, D_RNN)), vec, _const_spec((D_RNN, D_RNN)), vec, vec, vec],
        out_specs=seq_spec,
        scratch_shapes=[pltpu.VMEM((HIST + TS, D_RNN), f32), pltpu.VMEM((1, D_RNN), f32)],
        compiler_params=pltpu.CompilerParams(
            dimension_semantics=("parallel", "arbitrary"), vmem_limit_bytes=VMEM_LIMIT),
        name="rglru",
    )(xr, gate, cw, cb, wa, ba, wx, bx, lam, g)


PIPE_DEPTH = 4
BLK_PER_TILE = TK // LANES
KT_PER_QT = TQ // TK
N_BLK = HEADS_PER_STEP * BLK_PER_TILE
ITER_PER_TRIP = 4
LOG2E = 1.4426950408889634


MASKED = -1e30


def _attn_kernel(qt_ref, kt_ref, q_ref, k_ref, v_ref, tri_ref, o_ref,
                 c_ref, z_ref, split_ref, p_ref, w_ref, d_ref, kh_ref, vh_ref):
    n_items = qt_ref.shape[0]
    seq = k_ref.shape[1]
    heads = range(HEADS_PER_STEP)
    for ref in (o_ref, c_ref, z_ref, split_ref, p_ref, w_ref):
        ref[...] = jnp.zeros_like(ref)
    for b in range(BLK_PER_TILE):
        d_ref[b] = (lax.broadcasted_iota(jnp.int32, (TQ, LANES), 1) + b * LANES
                    - lax.broadcasted_iota(jnp.int32, (TQ, LANES), 0))
    lane_head = lax.broadcasted_iota(jnp.int32, (seq, LANES), 1) // HEAD_DIM
    for h in heads:
        kh_ref[h] = jnp.where(lane_head == h, k_ref[0], jnp.zeros_like(k_ref[0]))
        vh_ref[h] = jnp.where(lane_head == h, v_ref[0], jnp.zeros_like(v_ref[0]))

    def item(j):
        j = jnp.clip(j, 0, n_items - 1)
        return qt_ref[j], kt_ref[j]

    def diag_tile(qi):
        return (qi + 1) * KT_PER_QT - 1

    def per_head_rows(ref, t):
        rows = pl.ds(pl.multiple_of(t * TK, TK), TK)
        return jnp.concatenate([ref[h, rows, :] for h in heads], axis=0)

    def stage_c2(qi, t):
        rows = pl.ds(pl.multiple_of(qi * TQ, TQ), TQ)
        acc = jnp.where(t == diag_tile(qi), 0.0, o_ref[0, rows, :])
        w = jnp.concatenate([w_ref[blk] for blk in range(N_BLK)], axis=1)
        o_ref[0, rows, :] = acc + _dot(w, per_head_rows(vh_ref, t))

    def stage_c1(qi, t, slot):
        for h in heads:
            first = h * BLK_PER_TILE
            c = jnp.where(t == diag_tile(qi), 0.0, c_ref[h])
            for blk in range(first, first + BLK_PER_TILE):
                w_ref[blk] = jnp.exp(z_ref[slot, blk] + p_ref[blk] + c).astype(bf16)
            c_ref[h] = c + jnp.broadcast_to(p_ref[first, :, 0:1], (TQ, LANES))

    def stage_b2():
        for h in heads:
            first = h * BLK_PER_TILE
            parts = [split_ref[blk] for blk in range(first, first + BLK_PER_TILE)]
            r = _dot(jnp.concatenate(parts, axis=1), tri_ref[...])
            for b in range(BLK_PER_TILE):
                p_ref[first + b] = r[:, b * LANES:(b + 1) * LANES]

    def stage_b1(slot):
        for blk in range(N_BLK):
            z = z_ref[slot, blk]
            sp = jnp.log(1.0 + jnp.exp2(jnp.abs(z) * (-LOG2E))) + jnp.maximum(z, 0.0)
            split_ref[blk] = sp.astype(bf16)

    def stage_a(qi, t, slot):
        q = q_ref[0, pl.ds(pl.multiple_of(qi * TQ, TQ), TQ), :]
        z = lax.dot_general(q, per_head_rows(kh_ref, t), (((1,), (1,)), ((), ())),
                            preferred_element_type=f32)
        for b in range(BLK_PER_TILE):
            visible = d_ref[b] < qi * TQ - t * TK
            for h in heads:
                blk = h * BLK_PER_TILE + b
                z_ref[slot, blk] = jnp.where(visible, z[:, blk * LANES:(blk + 1) * LANES], MASKED)

    def iteration(i, u):
        stage_c2(*item(i - PIPE_DEPTH))
        stage_c1(*item(i - 3), (u - 3) % ITER_PER_TRIP)
        stage_b2()
        stage_b1((u - 1) % ITER_PER_TRIP)
        stage_a(*item(i), u)

    def body(m, carry):
        for u in range(ITER_PER_TRIP):
            iteration(ITER_PER_TRIP * m + u, u)
        return carry

    lax.fori_loop(0, (n_items + PIPE_DEPTH) // ITER_PER_TRIP, body, 0)


def _attention(q, k, v, tri):
    bsz, seq, _ = q.shape
    n_qt = seq // TQ
    items = [(qi, t) for qi in range(n_qt) for t in range((qi + 1) * KT_PER_QT - 1, -1, -1)]
    qt = jnp.asarray([qi for qi, _ in items], jnp.int32)
    kt = jnp.asarray([t for _, t in items], jnp.int32)
    assert (qt.shape[0] + PIPE_DEPTH) % ITER_PER_TRIP == 0
    seq_spec = pl.BlockSpec((1, seq, LANES), lambda b, p, *_: (b, 0, p))
    tri_spec = pl.BlockSpec((TK, TK), lambda b, p, *_: (0, 0),
                            pipeline_mode=pl.Buffered(1))
    return pl.pallas_call(
        _attn_kernel,
        out_shape=jax.ShapeDtypeStruct((bsz, seq, D_ATT), f32),
        grid_spec=pltpu.PrefetchScalarGridSpec(
            num_scalar_prefetch=2,
            grid=(bsz, D_ATT // LANES),
            in_specs=[seq_spec, seq_spec, seq_spec, tri_spec],
            out_specs=seq_spec,
            scratch_shapes=[
                pltpu.VMEM((HEADS_PER_STEP, TQ, LANES), f32),
                pltpu.VMEM((ITER_PER_TRIP, N_BLK, TQ, LANES), f32),
                pltpu.VMEM((N_BLK, TQ, LANES), bf16),
                pltpu.VMEM((N_BLK, TQ, LANES), f32),
                pltpu.VMEM((N_BLK, TQ, LANES), bf16),
                pltpu.VMEM((BLK_PER_TILE, TQ, LANES), jnp.int32),
                pltpu.VMEM((HEADS_PER_STEP, seq, LANES), bf16),
                pltpu.VMEM((HEADS_PER_STEP, seq, LANES), bf16),
            ]),
        compiler_params=pltpu.CompilerParams(
            dimension_semantics=("parallel", "parallel"), vmem_limit_bytes=VMEM_LIMIT),
        name="stickbreak",
    )(qt, kt, q, k, v, tri)


def _ffn_out_kernel(x_ref, yr_ref, ya_ref, ga_ref, wo_ref, g2_ref, wg_ref, wu_ref, wd_ref,
                    o_ref, h_ref):
    ya = _rms(ya_ref[...], ga_ref[...]).astype(bf16)
    x2 = (x_ref[...] + _dot(yr_ref[...], wo_ref[0:D_RNN, :])
          + _dot(ya, wo_ref[D_RNN:D_RNN + D_ATT, :]))
    o_ref[...] = _ffn_tile(x2, g2_ref, wg_ref, wu_ref, wd_ref, h_ref)


def _ffn_out(x2d, yr, ya, ga, w_out, g2, wg, wu, wd):
    n_tok = x2d.shape[0]
    return pl.pallas_call(
        _ffn_out_kernel,
        out_shape=jax.ShapeDtypeStruct((n_tok, D_MODEL), f32),
        grid=(n_tok // TM,),
        in_specs=[_tok_spec(D_MODEL), _tok_spec(D_RNN), _tok_spec(D_ATT), _const_spec((1, D_ATT)),
                  _const_spec((D_RNN + D_ATT, D_MODEL))] + _ffn_specs(),
        out_specs=_tok_spec(D_MODEL),
        scratch_shapes=[pltpu.VMEM((TM, D_FF), bf16)],
        compiler_params=pltpu.CompilerParams(
            dimension_semantics=("parallel",), vmem_limit_bytes=VMEM_LIMIT),
        name="ffn_out",
    )(x2d, yr, ya, ga, w_out, g2, wg, wu, wd)


def _block_diag(w):
    eye = jnp.eye(RNN_BLOCKS, dtype=w.dtype)
    return jnp.einsum("ncd,nm->ncmd", w, eye).reshape(D_RNN, D_RNN)


def _scan_matrix():
    j = jnp.arange(TK)
    return -(j[:, None] >= j[None, :]).astype(bf16)


def kernel(x, ffn1_norm, ffn1_w_gate, ffn1_w_up, ffn1_w_down, mix_norm, w_in, conv_w, conv_b,
           rg_w_a, rg_b_a, rg_w_x, rg_b_x, rg_lambda, q_norm, k_norm, rnn_out_norm,
           attn_out_norm, w_out, ffn2_norm, ffn2_w_gate, ffn2_w_up, ffn2_w_down):
    bsz, seq, _ = x.shape
    depth = ffn1_norm.shape[0]
    head_id = jnp.arange(D_ATT) // HEAD_DIM
    seg = (head_id[:, None] == head_id[None, :]).astype(bf16)
    tri = _scan_matrix()
    x2d = x.reshape(bsz * seq, D_MODEL)
    for l in range(depth):
        x2d, xr, gate, q, k, v = _ffn_in(
            x2d, ffn1_norm[l][None], ffn1_w_gate[l].astype(bf16), ffn1_w_up[l].astype(bf16),
            ffn1_w_down[l].astype(bf16), mix_norm[l][None], w_in[l].astype(bf16),
            jnp.tile(q_norm[l], N_HEADS)[None], jnp.tile(k_norm[l], N_HEADS)[None], seg)
        y_rnn = _rglru(
            xr.reshape(bsz, seq, D_RNN), gate.reshape(bsz, seq, D_RNN),
            conv_w[l], conv_b[l][None], _block_diag(rg_w_a[l]).astype(bf16), rg_b_a[l][None],
            _block_diag(rg_w_x[l]).astype(bf16), rg_b_x[l][None], rg_lambda[l][None],
            rnn_out_norm[l][None])
        y_att = _attention(q.reshape(bsz, seq, D_ATT), k.reshape(bsz, seq, D_ATT),
                           v.reshape(bsz, seq, D_ATT), tri)
        x2d = _ffn_out(x2d, y_rnn.reshape(bsz * seq, D_RNN), y_att.reshape(bsz * seq, D_ATT),
                       attn_out_norm[l][None], w_out[l].astype(bf16), ffn2_norm[l][None],
                       ffn2_w_gate[l].astype(bf16), ffn2_w_up[l].astype(bf16),
                       ffn2_w_down[l].astype(bf16))
    return x2d.reshape(bsz, seq, D_MODEL)
```

```python
import math

import jax
import jax.numpy as jnp
from jax import lax
from jax.experimental import pallas as pl
from jax.experimental.pallas import tpu as pltpu

D_MODEL = 1024
D_RNN = 512
RNN_BLOCKS = 8
RNN_BW = D_RNN // RNN_BLOCKS
CONV_W = 4
RG_C = 8.0
D_ATT = 512
HEAD_DIM = 64
N_HEADS = D_ATT // HEAD_DIM
D_FF = 2816
N_IN = 2 * D_RNN + 3 * D_ATT
EPS = 1e-6

LANES = 128
SUBLANES = 8
HEADS_PER_STEP = LANES // HEAD_DIM
VMEM_LIMIT = 56 * 1024 * 1024

TM = 512
FF_CHUNK = 256
TS = 512
TQ = 512
TK = 256

f32 = jnp.float32
bf16 = jnp.bfloat16


def _rms(xf, g):
    r = lax.rsqrt(jnp.mean(xf * xf, axis=-1, keepdims=True) + EPS)
    return xf * r * g


def _dot(a, b):
    return jnp.dot(a, b, preferred_element_type=f32)


def _split_bf16(x):
    hi = x.astype(bf16)
    lo = (x - hi.astype(f32)).astype(bf16)
    return hi, lo


def _const_spec(shape):
    nd = len(shape)
    return pl.BlockSpec(shape, lambda *_: (0,) * nd, pipeline_mode=pl.Buffered(1))


def _ffn_tile(x, g_ref, wg_ref, wu_ref, wd_ref, h_ref):
    xn = _rms(x, g_ref[...]).astype(bf16)
    for c in range(0, D_FF, FF_CHUNK):
        gate = _dot(xn, wg_ref[:, c:c + FF_CHUNK])
        up = _dot(xn, wu_ref[:, c:c + FF_CHUNK])
        h_ref[:, c:c + FF_CHUNK] = (jax.nn.silu(gate) * up).astype(bf16)
    return x + 0.5 * _dot(h_ref[...], wd_ref[...])


def _ffn_specs():
    return [_const_spec((1, D_MODEL)), _const_spec((D_MODEL, D_FF)),
            _const_spec((D_MODEL, D_FF)), _const_spec((D_FF, D_MODEL))]


def _tok_spec(d):
    return pl.BlockSpec((TM, d), lambda i: (i, 0))


def _head_norm(t, gain_tiled, seg_ref):
    hi, lo = _split_bf16(t * t)
    ss = _dot(hi, seg_ref[...]) + _dot(lo, seg_ref[...])
    r = lax.rsqrt(ss * (1.0 / HEAD_DIM) + EPS)
    return t * r * gain_tiled


def _ffn_in_kernel(x_ref, g1_ref, wg_ref, wu_ref, wd_ref, g_ref, w_ref, qg_ref, kg_ref, seg_ref,
                   o_ref, xr_ref, gate_ref, q_ref, k_ref, v_ref, h_ref):
    x1 = _ffn_tile(x_ref[...], g1_ref, wg_ref, wu_ref, wd_ref, h_ref)
    o_ref[...] = x1
    h = _rms(x1, g_ref[...]).astype(bf16)
    xr_ref[...] = _dot(h, w_ref[:, 0:D_RNN])
    gate_ref[...] = _dot(h, w_ref[:, D_RNN:2 * D_RNN])
    o = 2 * D_RNN
    q = _dot(h, w_ref[:, o:o + D_ATT])
    k = _dot(h, w_ref[:, o + D_ATT:o + 2 * D_ATT])
    v = _dot(h, w_ref[:, o + 2 * D_ATT:o + 3 * D_ATT])
    scale = 1.0 / math.sqrt(HEAD_DIM)
    q_ref[...] = (_head_norm(q, qg_ref[...], seg_ref) * scale).astype(bf16)
    k_ref[...] = _head_norm(k, kg_ref[...], seg_ref).astype(bf16)
    v_ref[...] = v.astype(bf16)


def _ffn_in(x2d, g1, wg, wu, wd, g, w_in, qg, kg, seg):
    n_tok = x2d.shape[0]
    return pl.pallas_call(
        _ffn_in_kernel,
        out_shape=(jax.ShapeDtypeStruct((n_tok, D_MODEL), f32),
                   jax.ShapeDtypeStruct((n_tok, D_RNN), f32),
                   jax.ShapeDtypeStruct((n_tok, D_RNN), f32),
                   jax.ShapeDtypeStruct((n_tok, D_ATT), bf16),
                   jax.ShapeDtypeStruct((n_tok, D_ATT), bf16),
                   jax.ShapeDtypeStruct((n_tok, D_ATT), bf16)),
        grid=(n_tok // TM,),
        in_specs=[_tok_spec(D_MODEL)] + _ffn_specs() + [
            _const_spec((1, D_MODEL)), _const_spec((D_MODEL, N_IN)),
            _const_spec((1, D_ATT)), _const_spec((1, D_ATT)), _const_spec((D_ATT, D_ATT))],
        out_specs=(_tok_spec(D_MODEL), _tok_spec(D_RNN), _tok_spec(D_RNN),
                   _tok_spec(D_ATT), _tok_spec(D_ATT), _tok_spec(D_ATT)),
        scratch_shapes=[pltpu.VMEM((TM, D_FF), bf16)],
        compiler_params=pltpu.CompilerParams(
            dimension_semantics=("parallel",), vmem_limit_bytes=VMEM_LIMIT),
        name="ffn_in",
    )(x2d, g1, wg, wu, wd, g, w_in, qg, kg, seg)


HIST = 8


def _rglru_kernel(xr_ref, gate_ref, cw_ref, cb_ref, wa_ref, ba_ref, wx_ref, bx_ref,
                  lam_ref, g_ref, o_ref, xbuf_ref, hc_ref):
    @pl.when(pl.program_id(1) == 0)
    def _():
        xbuf_ref[0:HIST, :] = jnp.zeros((HIST, D_RNN), f32)
        hc_ref[...] = jnp.zeros((1, D_RNN), f32)

    x = xr_ref[0]
    xbuf_ref[HIST:HIST + TS, :] = x
    xc = cb_ref[...] + cw_ref[CONV_W - 1:CONV_W, :] * x
    for j in range(CONV_W - 1):
        off = HIST - (CONV_W - 1) + j
        xc = xc + cw_ref[j:j + 1, :] * xbuf_ref[off:off + TS, :]
    xbuf_ref[0:HIST, :] = x[TS - HIST:TS, :]

    xcb = xc.astype(bf16)
    r = jax.nn.sigmoid(_dot(xcb, wa_ref[...]) + ba_ref[...])
    i = jax.nn.sigmoid(_dot(xcb, wx_ref[...]) + bx_ref[...])
    lam = lam_ref[...]
    log_sig_lam = jnp.minimum(lam, 0.0) - jnp.log1p(jnp.exp(-jnp.abs(lam)))
    log_a = RG_C * r * log_sig_lam
    a = jnp.exp(log_a)
    th = jnp.tanh(log_a)
    mult = jnp.sqrt(-2.0 * th / (1.0 - th))
    b = mult * (i * xc)

    n_grp = TS // SUBLANES
    a = a.reshape(n_grp, SUBLANES, D_RNN)
    b = b.reshape(n_grp, SUBLANES, D_RNN)
    row = lax.broadcasted_iota(jnp.int32, (n_grp, SUBLANES, D_RNN), 1)
    d = 1
    while d < SUBLANES:
        a_sh = pltpu.roll(a, d, axis=1)
        b_sh = pltpu.roll(b, d, axis=1)
        m = row >= d
        b = jnp.where(m, a * b_sh, 0.0) + b
        a = jnp.where(m, a * a_sh, a)
        d *= 2
    carry = jnp.broadcast_to(hc_ref[...], (SUBLANES, D_RNN))
    groups = []
    for g in range(n_grp):
        h_g = a[g] * carry + b[g]
        groups.append(h_g)
        carry = jnp.broadcast_to(h_g[SUBLANES - 1:SUBLANES, :], (SUBLANES, D_RNN))
    h = jnp.concatenate(groups, axis=0)
    hc_ref[...] = groups[-1][SUBLANES - 1:SUBLANES, :]

    y = h * jax.nn.gelu(gate_ref[0])
    o_ref[0] = _rms(y, g_ref[...]).astype(bf16)


def _rglru(xr, gate, cw, cb, wa, ba, wx, bx, lam, g):
    bsz, seq, _ = xr.shape
    seq_spec = pl.BlockSpec((1, TS, D_RNN), lambda b, s: (b, s, 0))
    vec = _const_spec((1, D_RNN))
    return pl.pallas_call(
        _rglru_kernel,
        out_shape=jax.ShapeDtypeStruct((bsz, seq, D_RNN), bf16),
        grid=(bsz, seq // TS),
        in_specs=[seq_spec, seq_spec, _const_spec((CONV_W, D_RNN)), vec,
                  _const_spec((D_RNN, D_RNN)), vec, _const_spec((D_RNN, D_RNN)), vec, vec, vec],
        out_specs=seq_spec,
        scratch_shapes=[pltpu.VMEM((HIST + TS, D_RNN), f32), pltpu.VMEM((1, D_RNN), f32)],
        compiler_params=pltpu.CompilerParams(
            dimension_semantics=("parallel", "arbitrary"), vmem_limit_bytes=VMEM_LIMIT),
        name="rglru",
    )(xr, gate, cw, cb, wa, ba, wx, bx, lam, g)


PIPE_DEPTH = 2
BLK_PER_TILE = TK // LANES
KT_PER_QT = TQ // TK
N_BLK = HEADS_PER_STEP * BLK_PER_TILE
ITER_PER_TRIP = 4
LOG2E = 1.4426950408889634


MASKED = -1e30


def _attn_kernel(qt_ref, kt_ref, q_ref, k_ref, v_ref, tri_ref, o_ref,
                 c_ref, z_ref, split_ref, w_ref, d_ref):
    n_items = qt_ref.shape[0]
    heads = range(HEADS_PER_STEP)
    for ref in (o_ref, c_ref, z_ref, split_ref, w_ref):
        ref[...] = jnp.zeros_like(ref)
    for b in range(BLK_PER_TILE):
        d_ref[b] = (lax.broadcasted_iota(jnp.int32, (TQ, LANES), 1) + b * LANES
                    - lax.broadcasted_iota(jnp.int32, (TQ, LANES), 0))
    lane_head = lax.broadcasted_iota(jnp.int32, (TK, LANES), 1) // HEAD_DIM

    def item(j):
        j = jnp.clip(j, 0, n_items - 1)
        return qt_ref[j], kt_ref[j]

    def diag_tile(qi):
        return (qi + 1) * KT_PER_QT - 1

    def per_head_rows(ref, t):
        x = ref[0, pl.ds(pl.multiple_of(t * TK, TK), TK), :]
        return jnp.concatenate([jnp.where(lane_head == h, x, jnp.zeros_like(x)) for h in heads], axis=0)

    def stage_c2(qi, t):
        rows = pl.ds(pl.multiple_of(qi * TQ, TQ), TQ)
        acc = jnp.where(t == diag_tile(qi), 0.0, o_ref[0, rows, :])
        w = jnp.concatenate([w_ref[blk] for blk in range(N_BLK)], axis=1)
        o_ref[0, rows, :] = acc + _dot(w, per_head_rows(v_ref, t))

    def stage_b2c1(qi, t, slot):
        for h in heads:
            first = h * BLK_PER_TILE
            parts = [split_ref[blk] for blk in range(first, first + BLK_PER_TILE)]
            r = _dot(jnp.concatenate(parts, axis=1), tri_ref[...])
            c = jnp.where(t == diag_tile(qi), 0.0, c_ref[h])
            for b in range(BLK_PER_TILE):
                s = z_ref[slot, first + b] + r[:, b * LANES:(b + 1) * LANES] + c
                w_ref[first + b] = jnp.exp(s).astype(bf16)
            c_ref[h] = c + jnp.broadcast_to(r[:, 0:1], (TQ, LANES))

    def stage_ab1(qi, t, slot):
        q = q_ref[0, pl.ds(pl.multiple_of(qi * TQ, TQ), TQ), :]
        z = lax.dot_general(q, per_head_rows(k_ref, t), (((1,), (1,)), ((), ())),
                            preferred_element_type=f32)
        for b in range(BLK_PER_TILE):
            visible = d_ref[b] < qi * TQ - t * TK
            for h in heads:
                blk = h * BLK_PER_TILE + b
                zm = jnp.where(visible, z[:, blk * LANES:(blk + 1) * LANES], MASKED)
                z_ref[slot, blk] = zm
                sp = jnp.log(1.0 + jnp.exp2(jnp.abs(zm) * (-LOG2E))) + jnp.maximum(zm, 0.0)
                split_ref[blk] = sp.astype(bf16)

    def iteration(i, u):
        stage_c2(*item(i - PIPE_DEPTH))
        stage_b2c1(*item(i - 1), (u - 1) % ITER_PER_TRIP)
        stage_ab1(*item(i), u)

    n_iter = n_items + PIPE_DEPTH
    pad = -n_iter % ITER_PER_TRIP

    def body(m, carry):
        for u in range(ITER_PER_TRIP):
            iteration(ITER_PER_TRIP * m + u - pad, u)
        return carry

    lax.fori_loop(0, (n_iter + pad) // ITER_PER_TRIP, body, 0)


def _attention(q, k, v, tri):
    bsz, seq, _ = q.shape
    n_qt = seq // TQ
    items = [(qi, t) for qi in range(n_qt) for t in range((qi + 1) * KT_PER_QT - 1, -1, -1)]
    qt = jnp.asarray([qi for qi, _ in items], jnp.int32)
    kt = jnp.asarray([t for _, t in items], jnp.int32)
    seq_spec = pl.BlockSpec((1, seq, LANES), lambda b, p, *_: (b, 0, p))
    tri_spec = pl.BlockSpec((TK, TK), lambda b, p, *_: (0, 0),
                            pipeline_mode=pl.Buffered(1))
    return pl.pallas_call(
        _attn_kernel,
        out_shape=jax.ShapeDtypeStruct((bsz, seq, D_ATT), f32),
        grid_spec=pltpu.PrefetchScalarGridSpec(
            num_scalar_prefetch=2,
            grid=(bsz, D_ATT // LANES),
            in_specs=[seq_spec, seq_spec, seq_spec, tri_spec],
            out_specs=seq_spec,
            scratch_shapes=[
                pltpu.VMEM((HEADS_PER_STEP, TQ, LANES), f32),
                pltpu.VMEM((ITER_PER_TRIP, N_BLK, TQ, LANES), f32),
                pltpu.VMEM((N_BLK, TQ, LANES), bf16),
                pltpu.VMEM((N_BLK, TQ, LANES), bf16),
                pltpu.VMEM((BLK_PER_TILE, TQ, LANES), jnp.int32),
            ]),
        compiler_params=pltpu.CompilerParams(
            dimension_semantics=("parallel", "parallel"), vmem_limit_bytes=VMEM_LIMIT),
        name="stickbreak",
    )(qt, kt, q, k, v, tri)


def _ffn_out_kernel(x_ref, yr_ref, ya_ref, ga_ref, wo_ref, g2_ref, wg_ref, wu_ref, wd_ref,
                    o_ref, h_ref):
    ya = _rms(ya_ref[...], ga_ref[...]).astype(bf16)
    x2 = (x_ref[...] + _dot(yr_ref[...], wo_ref[0:D_RNN, :])
          + _dot(ya, wo_ref[D_RNN:D_RNN + D_ATT, :]))
    o_ref[...] = _ffn_tile(x2, g2_ref, wg_ref, wu_ref, wd_ref, h_ref)


def _ffn_out(x2d, yr, ya, ga, w_out, g2, wg, wu, wd):
    n_tok = x2d.shape[0]
    return pl.pallas_call(
        _ffn_out_kernel,
        out_shape=jax.ShapeDtypeStruct((n_tok, D_MODEL), f32),
        grid=(n_tok // TM,),
        in_specs=[_tok_spec(D_MODEL), _tok_spec(D_RNN), _tok_spec(D_ATT), _const_spec((1, D_ATT)),
                  _const_spec((D_RNN + D_ATT, D_MODEL))] + _ffn_specs(),
        out_specs=_tok_spec(D_MODEL),
        scratch_shapes=[pltpu.VMEM((TM, D_FF), bf16)],
        compiler_params=pltpu.CompilerParams(
            dimension_semantics=("parallel",), vmem_limit_bytes=VMEM_LIMIT),
        name="ffn_out",
    )(x2d, yr, ya, ga, w_out, g2, wg, wu, wd)


def _block_diag(w):
    eye = jnp.eye(RNN_BLOCKS, dtype=w.dtype)
    return jnp.einsum("ncd,nm->ncmd", w, eye).reshape(D_RNN, D_RNN)


def _scan_matrix():
    j = jnp.arange(TK)
    return -(j[:, None] >= j[None, :]).astype(bf16)


def kernel(x, ffn1_norm, ffn1_w_gate, ffn1_w_up, ffn1_w_down, mix_norm, w_in, conv_w, conv_b,
           rg_w_a, rg_b_a, rg_w_x, rg_b_x, rg_lambda, q_norm, k_norm, rnn_out_norm,
           attn_out_norm, w_out, ffn2_norm, ffn2_w_gate, ffn2_w_up, ffn2_w_down):
    bsz, seq, _ = x.shape
    depth = ffn1_norm.shape[0]
    head_id = jnp.arange(D_ATT) // HEAD_DIM
    seg = (head_id[:, None] == head_id[None, :]).astype(bf16)
    tri = _scan_matrix()
    x2d = x.reshape(bsz * seq, D_MODEL)
    for l in range(depth):
        x2d, xr, gate, q, k, v = _ffn_in(
            x2d, ffn1_norm[l][None], ffn1_w_gate[l].astype(bf16), ffn1_w_up[l].astype(bf16),
            ffn1_w_down[l].astype(bf16), mix_norm[l][None], w_in[l].astype(bf16),
            jnp.tile(q_norm[l], N_HEADS)[None], jnp.tile(k_norm[l], N_HEADS)[None], seg)
        y_rnn = _rglru(
            xr.reshape(bsz, seq, D_RNN), gate.reshape(bsz, seq, D_RNN),
            conv_w[l], conv_b[l][None], _block_diag(rg_w_a[l]).astype(bf16), rg_b_a[l][None],
            _block_diag(rg_w_x[l]).astype(bf16), rg_b_x[l][None], rg_lambda[l][None],
            rnn_out_norm[l][None])
        y_att = _attention(q.reshape(bsz, seq, D_ATT), k.reshape(bsz, seq, D_ATT),
                           v.reshape(bsz, seq, D_ATT), tri)
        x2d = _ffn_out(x2d, y_rnn.reshape(bsz * seq, D_RNN), y_att.reshape(bsz * seq, D_ATT),
                       attn_out_norm[l][None], w_out[l].astype(bf16), ffn2_norm[l][None],
                       ffn2_w_gate[l].astype(bf16), ffn2_w_up[l].astype(bf16),
                       ffn2_w_down[l].astype(bf16))
    return x2d.reshape(bsz, seq, D_MODEL)
```

```python
import math

import jax
import jax.numpy as jnp
from jax import lax
from jax.experimental import pallas as pl
from jax.experimental.pallas import tpu as pltpu

D_MODEL = 1024
D_RNN = 512
RNN_BLOCKS = 8
RNN_BW = D_RNN // RNN_BLOCKS
CONV_W = 4
RG_C = 8.0
D_ATT = 512
HEAD_DIM = 64
N_HEADS = D_ATT // HEAD_DIM
D_FF = 2816
N_IN = 2 * D_RNN + 3 * D_ATT
EPS = 1e-6

LANES = 128
SUBLANES = 8
HEADS_PER_STEP = LANES // HEAD_DIM
VMEM_LIMIT = 56 * 1024 * 1024

TM = 512
FF_CHUNK = 256
TS = 512
TQ = 512
TK = 256

f32 = jnp.float32
bf16 = jnp.bfloat16


def _rms(xf, g):
    r = lax.rsqrt(jnp.mean(xf * xf, axis=-1, keepdims=True) + EPS)
    return xf * r * g


def _dot(a, b):
    return jnp.dot(a, b, preferred_element_type=f32)


def _split_bf16(x):
    hi = x.astype(bf16)
    lo = (x - hi.astype(f32)).astype(bf16)
    return hi, lo


def _const_spec(shape):
    nd = len(shape)
    return pl.BlockSpec(shape, lambda *_: (0,) * nd, pipeline_mode=pl.Buffered(1))


def _ffn_tile(x, g_ref, wg_ref, wu_ref, wd_ref, h_ref):
    xn = _rms(x, g_ref[...]).astype(bf16)
    for c in range(0, D_FF, FF_CHUNK):
        gate = _dot(xn, wg_ref[:, c:c + FF_CHUNK])
        up = _dot(xn, wu_ref[:, c:c + FF_CHUNK])
        h_ref[:, c:c + FF_CHUNK] = (jax.nn.silu(gate) * up).astype(bf16)
    return x + 0.5 * _dot(h_ref[...], wd_ref[...])


def _ffn_specs():
    return [_const_spec((1, D_MODEL)), _const_spec((D_MODEL, D_FF)),
            _const_spec((D_MODEL, D_FF)), _const_spec((D_FF, D_MODEL))]


def _tok_spec(d):
    return pl.BlockSpec((TM, d), lambda i: (i, 0))


def _head_norm(t, gain_tiled, seg_ref):
    hi, lo = _split_bf16(t * t)
    groups = []
    for g in range(0, t.shape[1], LANES):
        both = jnp.concatenate([hi[:, g:g + LANES], lo[:, g:g + LANES]], axis=1)
        groups.append(_dot(both, seg_ref[...]))
    ss = jnp.concatenate(groups, axis=1)
    r = lax.rsqrt(ss * (1.0 / HEAD_DIM) + EPS)
    return t * r * gain_tiled


def _ffn_in_kernel(x_ref, g1_ref, wg_ref, wu_ref, wd_ref, g_ref, w_ref, qg_ref, kg_ref, seg_ref,
                   o_ref, xr_ref, gate_ref, q_ref, k_ref, v_ref, h_ref):
    x1 = _ffn_tile(x_ref[...], g1_ref, wg_ref, wu_ref, wd_ref, h_ref)
    o_ref[...] = x1
    h = _rms(x1, g_ref[...]).astype(bf16)
    xr_ref[...] = _dot(h, w_ref[:, 0:D_RNN])
    gate_ref[...] = _dot(h, w_ref[:, D_RNN:2 * D_RNN])
    o = 2 * D_RNN
    q = _dot(h, w_ref[:, o:o + D_ATT])
    k = _dot(h, w_ref[:, o + D_ATT:o + 2 * D_ATT])
    v = _dot(h, w_ref[:, o + 2 * D_ATT:o + 3 * D_ATT])
    scale = 1.0 / math.sqrt(HEAD_DIM)
    q_ref[...] = (_head_norm(q, qg_ref[...], seg_ref) * scale).astype(bf16)
    k_ref[...] = _head_norm(k, kg_ref[...], seg_ref).astype(bf16)
    v_ref[...] = v.astype(bf16)


def _ffn_in(x2d, g1, wg, wu, wd, g, w_in, qg, kg, seg):
    n_tok = x2d.shape[0]
    return pl.pallas_call(
        _ffn_in_kernel,
        out_shape=(jax.ShapeDtypeStruct((n_tok, D_MODEL), f32),
                   jax.ShapeDtypeStruct((n_tok, D_RNN), f32),
                   jax.ShapeDtypeStruct((n_tok, D_RNN), f32),
                   jax.ShapeDtypeStruct((n_tok, D_ATT), bf16),
                   jax.ShapeDtypeStruct((n_tok, D_ATT), bf16),
                   jax.ShapeDtypeStruct((n_tok, D_ATT), bf16)),
        grid=(n_tok // TM,),
        in_specs=[_tok_spec(D_MODEL)] + _ffn_specs() + [
            _const_spec((1, D_MODEL)), _const_spec((D_MODEL, N_IN)),
            _const_spec((1, D_ATT)), _const_spec((1, D_ATT)), _const_spec((2 * LANES, LANES))],
        out_specs=(_tok_spec(D_MODEL), _tok_spec(D_RNN), _tok_spec(D_RNN),
                   _tok_spec(D_ATT), _tok_spec(D_ATT), _tok_spec(D_ATT)),
        scratch_shapes=[pltpu.VMEM((TM, D_FF), bf16)],
        compiler_params=pltpu.CompilerParams(
            dimension_semantics=("parallel",), vmem_limit_bytes=VMEM_LIMIT),
        name="ffn_in",
    )(x2d, g1, wg, wu, wd, g, w_in, qg, kg, seg)


HIST = 8


def _rglru_kernel(xr_ref, gate_ref, cw_ref, cb_ref, wa_ref, ba_ref, wx_ref, bx_ref,
                  lam_ref, g_ref, o_ref, xbuf_ref, hc_ref):
    @pl.when(pl.program_id(1) == 0)
    def _():
        xbuf_ref[0:HIST, :] = jnp.zeros((HIST, D_RNN), f32)
        hc_ref[...] = jnp.zeros((1, D_RNN), f32)

    x = xr_ref[0]
    xbuf_ref[HIST:HIST + TS, :] = x
    xc = cb_ref[...] + cw_ref[CONV_W - 1:CONV_W, :] * x
    for j in range(CONV_W - 1):
        off = HIST - (CONV_W - 1) + j
        xc = xc + cw_ref[j:j + 1, :] * xbuf_ref[off:off + TS, :]
    xbuf_ref[0:HIST, :] = x[TS - HIST:TS, :]

    xcb = xc.astype(bf16)
    r = jax.nn.sigmoid(_dot(xcb, wa_ref[...]) + ba_ref[...])
    i = jax.nn.sigmoid(_dot(xcb, wx_ref[...]) + bx_ref[...])
    lam = lam_ref[...]
    log_sig_lam = jnp.minimum(lam, 0.0) - jnp.log1p(jnp.exp(-jnp.abs(lam)))
    log_a = RG_C * r * log_sig_lam
    a = jnp.exp(log_a)
    th = jnp.tanh(log_a)
    mult = jnp.sqrt(-2.0 * th / (1.0 - th))
    b = mult * (i * xc)

    n_grp = TS // SUBLANES
    a = a.reshape(n_grp, SUBLANES, D_RNN)
    b = b.reshape(n_grp, SUBLANES, D_RNN)
    row = lax.broadcasted_iota(jnp.int32, (n_grp, SUBLANES, D_RNN), 1)
    d = 1
    while d < SUBLANES:
        a_sh = pltpu.roll(a, d, axis=1)
        b_sh = pltpu.roll(b, d, axis=1)
        m = row >= d
        b = jnp.where(m, a * b_sh, 0.0) + b
        a = jnp.where(m, a * a_sh, a)
        d *= 2
    carry = jnp.broadcast_to(hc_ref[...], (SUBLANES, D_RNN))
    groups = []
    for g in range(n_grp):
        h_g = a[g] * carry + b[g]
        groups.append(h_g)
        carry = jnp.broadcast_to(h_g[SUBLANES - 1:SUBLANES, :], (SUBLANES, D_RNN))
    h = jnp.concatenate(groups, axis=0)
    hc_ref[...] = groups[-1][SUBLANES - 1:SUBLANES, :]

    y = h * jax.nn.gelu(gate_ref[0])
    o_ref[0] = _rms(y, g_ref[...]).astype(bf16)


def _rglru(xr, gate, cw, cb, wa, ba, wx, bx, lam, g):
    bsz, seq, _ = xr.shape
    seq_spec = pl.BlockSpec((1, TS, D_RNN), lambda b, s: (b, s, 0))
    vec = _const_spec((1, D_RNN))
    return pl.pallas_call(
        _rglru_kernel,
        out_shape=jax.ShapeDtypeStruct((bsz, seq, D_RNN), bf16),
        grid=(bsz, seq // TS),
        in_specs=[seq_spec, seq_spec, _const_spec((CONV_W, D_RNN)), vec,
                  _const_spec((D_RNN, D_RNN)), vec, _const_spec((D_RNN, D_RNN)), vec, vec, vec],
        out_specs=seq_spec,
        scratch_shapes=[pltpu.VMEM((HIST + TS, D_RNN), f32), pltpu.VMEM((1, D_RNN), f32)],
        compiler_params=pltpu.CompilerParams(
            dimension_semantics=("parallel", "arbitrary"), vmem_limit_bytes=VMEM_LIMIT),
        name="rglru",
    )(xr, gate, cw, cb, wa, ba, wx, bx, lam, g)


PIPE_DEPTH = 2
BLK_PER_TILE = TK // LANES
KT_PER_QT = TQ // TK
N_BLK = HEADS_PER_STEP * BLK_PER_TILE
ITER_PER_TRIP = 4
LOG2E = 1.4426950408889634


MASKED = -1e30


def _attn_kernel(qt_ref, kt_ref, q_ref, k_ref, v_ref, tri_ref, o_ref,
                 c_ref, z_ref, split_ref, w_ref, d_ref):
    n_items = qt_ref.shape[0]
    heads = range(HEADS_PER_STEP)
    for ref in (o_ref, c_ref, z_ref, split_ref, w_ref):
        ref[...] = jnp.zeros_like(ref)
    for b in range(BLK_PER_TILE):
        d_ref[b] = (lax.broadcasted_iota(jnp.int32, (TQ, LANES), 1) + b * LANES
                    - lax.broadcasted_iota(jnp.int32, (TQ, LANES), 0))
    lane_head = lax.broadcasted_iota(jnp.int32, (TK, LANES), 1) // HEAD_DIM

    def item(j):
        j = jnp.clip(j, 0, n_items - 1)
        return qt_ref[j], kt_ref[j]

    def diag_tile(qi):
        return (qi + 1) * KT_PER_QT - 1

    def per_head_rows(ref, t):
        x = ref[0, pl.ds(pl.multiple_of(t * TK, TK), TK), :]
        return jnp.concatenate([jnp.where(lane_head == h, x, jnp.zeros_like(x)) for h in heads], axis=0)

    def stage_c2(qi, t):
        rows = pl.ds(pl.multiple_of(qi * TQ, TQ), TQ)
        acc = jnp.where(t == diag_tile(qi), 0.0, o_ref[0, rows, :])
        w = jnp.concatenate([w_ref[blk] for blk in range(N_BLK)], axis=1)
        o_ref[0, rows, :] = acc + _dot(w, per_head_rows(v_ref, t))

    def stage_b2c1(qi, t, slot):
        for h in heads:
            first = h * BLK_PER_TILE
            parts = [split_ref[blk] for blk in range(first, first + BLK_PER_TILE)]
            r = _dot(jnp.concatenate(parts, axis=1), tri_ref[...])
            c = jnp.where(t == diag_tile(qi), 0.0, c_ref[h])
            for b in range(BLK_PER_TILE):
                s = z_ref[slot, first + b] + r[:, b * LANES:(b + 1) * LANES] + c
                w_ref[first + b] = jnp.exp(s).astype(bf16)
            c_ref[h] = c + jnp.broadcast_to(r[:, 0:1], (TQ, LANES))

    def stage_ab1(qi, t, slot):
        q = q_ref[0, pl.ds(pl.multiple_of(qi * TQ, TQ), TQ), :]
        z = lax.dot_general(q, per_head_rows(k_ref, t), (((1,), (1,)), ((), ())),
                            preferred_element_type=f32)
        for b in range(BLK_PER_TILE):
            visible = d_ref[b] < qi * TQ - t * TK
            for h in heads:
                blk = h * BLK_PER_TILE + b
                zm = jnp.where(visible, z[:, blk * LANES:(blk + 1) * LANES], MASKED)
                z_ref[slot, blk] = zm
                sp = jnp.log(1.0 + jnp.exp2(jnp.abs(zm) * (-LOG2E))) + jnp.maximum(zm, 0.0)
                split_ref[blk] = sp.astype(bf16)

    def iteration(i, u):
        stage_c2(*item(i - PIPE_DEPTH))
        stage_b2c1(*item(i - 1), (u - 1) % ITER_PER_TRIP)
        stage_ab1(*item(i), u)

    n_iter = n_items + PIPE_DEPTH
    pad = -n_iter % ITER_PER_TRIP

    def body(m, carry):
        for u in range(ITER_PER_TRIP):
            iteration(ITER_PER_TRIP * m + u - pad, u)
        return carry

    lax.fori_loop(0, (n_iter + pad) // ITER_PER_TRIP, body, 0)


def _attention(q, k, v, tri):
    bsz, seq, _ = q.shape
    n_qt = seq // TQ
    items = [(qi, t) for qi in range(n_qt) for t in range((qi + 1) * KT_PER_QT - 1, -1, -1)]
    qt = jnp.asarray([qi for qi, _ in items], jnp.int32)
    kt = jnp.asarray([t for _, t in items], jnp.int32)
    seq_spec = pl.BlockSpec((1, seq, LANES), lambda b, p, *_: (b, 0, p))
    tri_spec = pl.BlockSpec((TK, TK), lambda b, p, *_: (0, 0),
                            pipeline_mode=pl.Buffered(1))
    return pl.pallas_call(
        _attn_kernel,
        out_shape=jax.ShapeDtypeStruct((bsz, seq, D_ATT), f32),
        grid_spec=pltpu.PrefetchScalarGridSpec(
            num_scalar_prefetch=2,
            grid=(bsz, D_ATT // LANES),
            in_specs=[seq_spec, seq_spec, seq_spec, tri_spec],
            out_specs=seq_spec,
            scratch_shapes=[
                pltpu.VMEM((HEADS_PER_STEP, TQ, LANES), f32),
                pltpu.VMEM((ITER_PER_TRIP, N_BLK, TQ, LANES), f32),
                pltpu.VMEM((N_BLK, TQ, LANES), bf16),
                pltpu.VMEM((N_BLK, TQ, LANES), bf16),
                pltpu.VMEM((BLK_PER_TILE, TQ, LANES), jnp.int32),
            ]),
        compiler_params=pltpu.CompilerParams(
            dimension_semantics=("parallel", "parallel"), vmem_limit_bytes=VMEM_LIMIT),
        name="stickbreak",
    )(qt, kt, q, k, v, tri)


def _ffn_out_kernel(x_ref, yr_ref, ya_ref, ga_ref, wo_ref, g2_ref, wg_ref, wu_ref, wd_ref,
                    o_ref, h_ref):
    ya = _rms(ya_ref[...], ga_ref[...]).astype(bf16)
    x2 = (x_ref[...] + _dot(yr_ref[...], wo_ref[0:D_RNN, :])
          + _dot(ya, wo_ref[D_RNN:D_RNN + D_ATT, :]))
    o_ref[...] = _ffn_tile(x2, g2_ref, wg_ref, wu_ref, wd_ref, h_ref)


def _ffn_out(x2d, yr, ya, ga, w_out, g2, wg, wu, wd):
    n_tok = x2d.shape[0]
    return pl.pallas_call(
        _ffn_out_kernel,
        out_shape=jax.ShapeDtypeStruct((n_tok, D_MODEL), f32),
        grid=(n_tok // TM,),
        in_specs=[_tok_spec(D_MODEL), _tok_spec(D_RNN), _tok_spec(D_ATT), _const_spec((1, D_ATT)),
                  _const_spec((D_RNN + D_ATT, D_MODEL))] + _ffn_specs(),
        out_specs=_tok_spec(D_MODEL),
        scratch_shapes=[pltpu.VMEM((TM, D_FF), bf16)],
        compiler_params=pltpu.CompilerParams(
            dimension_semantics=("parallel",), vmem_limit_bytes=VMEM_LIMIT),
        name="ffn_out",
    )(x2d, yr, ya, ga, w_out, g2, wg, wu, wd)


def _block_diag(w):
    eye = jnp.eye(RNN_BLOCKS, dtype=w.dtype)
    return jnp.einsum("ncd,nm->ncmd", w, eye).reshape(D_RNN, D_RNN)


def _scan_matrix():
    j = jnp.arange(TK)
    return -(j[:, None] >= j[None, :]).astype(bf16)


def kernel(x, ffn1_norm, ffn1_w_gate, ffn1_w_up, ffn1_w_down, mix_norm, w_in, conv_w, conv_b,
           rg_w_a, rg_b_a, rg_w_x, rg_b_x, rg_lambda, q_norm, k_norm, rnn_out_norm,
           attn_out_norm, w_out, ffn2_norm, ffn2_w_gate, ffn2_w_up, ffn2_w_down):
    bsz, seq, _ = x.shape
    depth = ffn1_norm.shape[0]
    row_head = (jnp.arange(2 * LANES) % LANES) // HEAD_DIM
    seg = (row_head[:, None] == (jnp.arange(LANES) // HEAD_DIM)[None, :]).astype(bf16)
    tri = _scan_matrix()
    x2d = x.reshape(bsz * seq, D_MODEL)
    for l in range(depth):
        x2d, xr, gate, q, k, v = _ffn_in(
            x2d, ffn1_norm[l][None], ffn1_w_gate[l].astype(bf16), ffn1_w_up[l].astype(bf16),
            ffn1_w_down[l].astype(bf16), mix_norm[l][None], w_in[l].astype(bf16),
            jnp.tile(q_norm[l], N_HEADS)[None], jnp.tile(k_norm[l], N_HEADS)[None], seg)
        y_rnn = _rglru(
            xr.reshape(bsz, seq, D_RNN), gate.reshape(bsz, seq, D_RNN),
            conv_w[l], conv_b[l][None], _block_diag(rg_w_a[l]).astype(bf16), rg_b_a[l][None],
            _block_diag(rg_w_x[l]).astype(bf16), rg_b_x[l][None], rg_lambda[l][None],
            rnn_out_norm[l][None])
        y_att = _attention(q.reshape(bsz, seq, D_ATT), k.reshape(bsz, seq, D_ATT),
                           v.reshape(bsz, seq, D_ATT), tri)
        x2d = _ffn_out(x2d, y_rnn.reshape(bsz * seq, D_RNN), y_att.reshape(bsz * seq, D_ATT),
                       attn_out_norm[l][None], w_out[l].astype(bf16), ffn2_norm[l][None],
                       ffn2_w_gate[l].astype(bf16), ffn2_w_up[l].astype(bf16),
                       ffn2_w_down[l].astype(bf16))
    return x2d.reshape(bsz, seq, D_MODEL)
```

```python
import math

import jax
import jax.numpy as jnp
from jax import lax
from jax.experimental import pallas as pl
from jax.experimental.pallas import tpu as pltpu

D_MODEL = 1024
D_RNN = 512
RNN_BLOCKS = 8
RNN_BW = D_RNN // RNN_BLOCKS
CONV_W = 4
RG_C = 8.0
D_ATT = 512
HEAD_DIM = 64
N_HEADS = D_ATT // HEAD_DIM
D_FF = 2816
N_IN = 2 * D_RNN + 3 * D_ATT
EPS = 1e-6

LANES = 128
SUBLANES = 8
HEADS_PER_STEP = LANES // HEAD_DIM
N_PAIR = D_ATT // LANES
VMEM_LIMIT = 56 * 1024 * 1024

TM = 512
FF_CHUNK = 256
TS = 512
TQ = 512
TK = 256

f32 = jnp.float32
bf16 = jnp.bfloat16


def _rms(xf, g):
    r = lax.rsqrt(jnp.mean(xf * xf, axis=-1, keepdims=True) + EPS)
    return xf * r * g


def _dot(a, b):
    return jnp.dot(a, b, preferred_element_type=f32)


def _split_bf16(x):
    hi = x.astype(bf16)
    lo = (x - hi.astype(f32)).astype(bf16)
    return hi, lo


def _const_spec(shape):
    nd = len(shape)
    return pl.BlockSpec(shape, lambda *_: (0,) * nd, pipeline_mode=pl.Buffered(1))


def _ffn_tile(x, g_ref, wg_ref, wu_ref, wd_ref, h_ref):
    xn = _rms(x, g_ref[...]).astype(bf16)
    for c in range(0, D_FF, FF_CHUNK):
        gate = _dot(xn, wg_ref[:, c:c + FF_CHUNK])
        up = _dot(xn, wu_ref[:, c:c + FF_CHUNK])
        h_ref[:, c:c + FF_CHUNK] = (jax.nn.silu(gate) * up).astype(bf16)
    return x + 0.5 * _dot(h_ref[...], wd_ref[...])


def _ffn_specs():
    return [_const_spec((1, D_MODEL)), _const_spec((D_MODEL, D_FF)),
            _const_spec((D_MODEL, D_FF)), _const_spec((D_FF, D_MODEL))]


def _tok_spec(d):
    return pl.BlockSpec((TM, d), lambda i: (i, 0))


def _pair_spec():
    return pl.BlockSpec((N_PAIR, TM, LANES), lambda i: (0, i, 0))


def _head_norm(t, gain_tiled, seg_ref):
    hi, lo = _split_bf16(t * t)
    groups = []
    for g in range(0, t.shape[1], LANES):
        both = jnp.concatenate([hi[:, g:g + LANES], lo[:, g:g + LANES]], axis=1)
        groups.append(_dot(both, seg_ref[...]))
    ss = jnp.concatenate(groups, axis=1)
    r = lax.rsqrt(ss * (1.0 / HEAD_DIM) + EPS)
    return t * r * gain_tiled


def _ffn_in_kernel(x_ref, g1_ref, wg_ref, wu_ref, wd_ref, g_ref, w_ref, qg_ref, kg_ref, seg_ref,
                   o_ref, xr_ref, gate_ref, q_ref, k_ref, v_ref, h_ref):
    x1 = _ffn_tile(x_ref[...], g1_ref, wg_ref, wu_ref, wd_ref, h_ref)
    o_ref[...] = x1
    h = _rms(x1, g_ref[...]).astype(bf16)
    xr_ref[...] = _dot(h, w_ref[:, 0:D_RNN])
    gate_ref[...] = _dot(h, w_ref[:, D_RNN:2 * D_RNN])
    o = 2 * D_RNN
    q = _dot(h, w_ref[:, o:o + D_ATT])
    k = _dot(h, w_ref[:, o + D_ATT:o + 2 * D_ATT])
    v = _dot(h, w_ref[:, o + 2 * D_ATT:o + 3 * D_ATT])
    scale = 1.0 / math.sqrt(HEAD_DIM)
    qn = (_head_norm(q, qg_ref[...], seg_ref) * scale).astype(bf16)
    kn = _head_norm(k, kg_ref[...], seg_ref).astype(bf16)
    vb = v.astype(bf16)
    for p in range(N_PAIR):
        lanes = slice(p * LANES, (p + 1) * LANES)
        q_ref[p] = qn[:, lanes]
        k_ref[p] = kn[:, lanes]
        v_ref[p] = vb[:, lanes]


def _ffn_in(x2d, g1, wg, wu, wd, g, w_in, qg, kg, seg):
    n_tok = x2d.shape[0]
    return pl.pallas_call(
        _ffn_in_kernel,
        out_shape=(jax.ShapeDtypeStruct((n_tok, D_MODEL), f32),
                   jax.ShapeDtypeStruct((n_tok, D_RNN), f32),
                   jax.ShapeDtypeStruct((n_tok, D_RNN), f32),
                   jax.ShapeDtypeStruct((N_PAIR, n_tok, LANES), bf16),
                   jax.ShapeDtypeStruct((N_PAIR, n_tok, LANES), bf16),
                   jax.ShapeDtypeStruct((N_PAIR, n_tok, LANES), bf16)),
        grid=(n_tok // TM,),
        in_specs=[_tok_spec(D_MODEL)] + _ffn_specs() + [
            _const_spec((1, D_MODEL)), _const_spec((D_MODEL, N_IN)),
            _const_spec((1, D_ATT)), _const_spec((1, D_ATT)), _const_spec((2 * LANES, LANES))],
        out_specs=(_tok_spec(D_MODEL), _tok_spec(D_RNN), _tok_spec(D_RNN),
                   _pair_spec(), _pair_spec(), _pair_spec()),
        scratch_shapes=[pltpu.VMEM((TM, D_FF), bf16)],
        compiler_params=pltpu.CompilerParams(
            dimension_semantics=("parallel",), vmem_limit_bytes=VMEM_LIMIT),
        name="ffn_in",
    )(x2d, g1, wg, wu, wd, g, w_in, qg, kg, seg)


HIST = 8


def _rglru_kernel(xr_ref, gate_ref, cw_ref, cb_ref, wa_ref, ba_ref, wx_ref, bx_ref,
                  lam_ref, g_ref, o_ref, xbuf_ref, hc_ref):
    @pl.when(pl.program_id(1) == 0)
    def _():
        xbuf_ref[0:HIST, :] = jnp.zeros((HIST, D_RNN), f32)
        hc_ref[...] = jnp.zeros((1, D_RNN), f32)

    x = xr_ref[0]
    xbuf_ref[HIST:HIST + TS, :] = x
    xc = cb_ref[...] + cw_ref[CONV_W - 1:CONV_W, :] * x
    for j in range(CONV_W - 1):
        off = HIST - (CONV_W - 1) + j
        xc = xc + cw_ref[j:j + 1, :] * xbuf_ref[off:off + TS, :]
    xbuf_ref[0:HIST, :] = x[TS - HIST:TS, :]

    xcb = xc.astype(bf16)
    r = jax.nn.sigmoid(_dot(xcb, wa_ref[...]) + ba_ref[...])
    i = jax.nn.sigmoid(_dot(xcb, wx_ref[...]) + bx_ref[...])
    lam = lam_ref[...]
    log_sig_lam = jnp.minimum(lam, 0.0) - jnp.log1p(jnp.exp(-jnp.abs(lam)))
    log_a = RG_C * r * log_sig_lam
    a = jnp.exp(log_a)
    th = jnp.tanh(log_a)
    mult = jnp.sqrt(-2.0 * th / (1.0 - th))
    b = mult * (i * xc)

    n_grp = TS // SUBLANES
    a = a.reshape(n_grp, SUBLANES, D_RNN)
    b = b.reshape(n_grp, SUBLANES, D_RNN)
    row = lax.broadcasted_iota(jnp.int32, (n_grp, SUBLANES, D_RNN), 1)
    d = 1
    while d < SUBLANES:
        a_sh = pltpu.roll(a, d, axis=1)
        b_sh = pltpu.roll(b, d, axis=1)
        m = row >= d
        b = jnp.where(m, a * b_sh, 0.0) + b
        a = jnp.where(m, a * a_sh, a)
        d *= 2
    carry = jnp.broadcast_to(hc_ref[...], (SUBLANES, D_RNN))
    groups = []
    for g in range(n_grp):
        h_g = a[g] * carry + b[g]
        groups.append(h_g)
        carry = jnp.broadcast_to(h_g[SUBLANES - 1:SUBLANES, :], (SUBLANES, D_RNN))
    h = jnp.concatenate(groups, axis=0)
    hc_ref[...] = groups[-1][SUBLANES - 1:SUBLANES, :]

    y = h * jax.nn.gelu(gate_ref[0])
    o_ref[0] = _rms(y, g_ref[...]).astype(bf16)


def _rglru(xr, gate, cw, cb, wa, ba, wx, bx, lam, g):
    bsz, seq, _ = xr.shape
    seq_spec = pl.BlockSpec((1, TS, D_RNN), lambda b, s: (b, s, 0))
    vec = _const_spec((1, D_RNN))
    return pl.pallas_call(
        _rglru_kernel,
        out_shape=jax.ShapeDtypeStruct((bsz, seq, D_RNN), bf16),
        grid=(bsz, seq // TS),
        in_specs=[seq_spec, seq_spec, _const_spec((CONV_W, D_RNN)), vec,
                  _const_spec((D_RNN, D_RNN)), vec, _const_spec((D_RNN, D_RNN)), vec, vec, vec],
        out_specs=seq_spec,
        scratch_shapes=[pltpu.VMEM((HIST + TS, D_RNN), f32), pltpu.VMEM((1, D_RNN), f32)],
        compiler_params=pltpu.CompilerParams(
            dimension_semantics=("parallel", "arbitrary"), vmem_limit_bytes=VMEM_LIMIT),
        name="rglru",
    )(xr, gate, cw, cb, wa, ba, wx, bx, lam, g)


PIPE_DEPTH = 2
BLK_PER_TILE = TK // LANES
KT_PER_QT = TQ // TK
N_BLK = HEADS_PER_STEP * BLK_PER_TILE
ITER_PER_TRIP = 4
LOG2E = 1.4426950408889634


MASKED = -1e30


def _attn_kernel(hp_ref, qt_ref, kt_ref, q_ref, k_ref, v_ref, tri_ref, o_ref,
                 c_ref, z_ref, split_ref, w_ref, d_ref):
    n_items = qt_ref.shape[0]
    heads = range(HEADS_PER_STEP)
    for ref in (o_ref, c_ref, z_ref, split_ref, w_ref):
        ref[...] = jnp.zeros_like(ref)
    for b in range(BLK_PER_TILE):
        d_ref[b] = (lax.broadcasted_iota(jnp.int32, (TQ, LANES), 1) + b * LANES
                    - lax.broadcasted_iota(jnp.int32, (TQ, LANES), 0))
    lane_head = lax.broadcasted_iota(jnp.int32, (TK, LANES), 1) // HEAD_DIM

    def item(j):
        j = jnp.clip(j, 0, n_items - 1)
        return hp_ref[j], qt_ref[j], kt_ref[j]

    def diag_tile(qi):
        return (qi + 1) * KT_PER_QT - 1

    def per_head_rows(ref, hp, t):
        x = ref[hp, 0, pl.ds(pl.multiple_of(t * TK, TK), TK), :]
        return jnp.concatenate([jnp.where(lane_head == h, x, jnp.zeros_like(x)) for h in heads], axis=0)

    def stage_c2(hp, qi, t):
        rows = pl.ds(pl.multiple_of(qi * TQ, TQ), TQ)
        acc = jnp.where(t == diag_tile(qi), 0.0, o_ref[hp, 0, rows, :])
        w = jnp.concatenate([w_ref[blk] for blk in range(N_BLK)], axis=1)
        o_ref[hp, 0, rows, :] = acc + _dot(w, per_head_rows(v_ref, hp, t))

    def stage_b2c1(hp, qi, t, slot):
        for h in heads:
            first = h * BLK_PER_TILE
            parts = [split_ref[blk] for blk in range(first, first + BLK_PER_TILE)]
            r = _dot(jnp.concatenate(parts, axis=1), tri_ref[...])
            c = jnp.where(t == diag_tile(qi), 0.0, c_ref[h])
            for b in range(BLK_PER_TILE):
                s = z_ref[slot, first + b] + r[:, b * LANES:(b + 1) * LANES] + c
                w_ref[first + b] = jnp.exp(s).astype(bf16)
            c_ref[h] = c + jnp.broadcast_to(r[:, 0:1], (TQ, LANES))

    def stage_ab1(hp, qi, t, slot):
        q = q_ref[hp, 0, pl.ds(pl.multiple_of(qi * TQ, TQ), TQ), :]
        z = lax.dot_general(q, per_head_rows(k_ref, hp, t), (((1,), (1,)), ((), ())),
                            preferred_element_type=f32)
        for b in range(BLK_PER_TILE):
            visible = d_ref[b] < qi * TQ - t * TK
            for h in heads:
                blk = h * BLK_PER_TILE + b
                zm = jnp.where(visible, z[:, blk * LANES:(blk + 1) * LANES], MASKED)
                z_ref[slot, blk] = zm
                sp = jnp.log(1.0 + jnp.exp2(jnp.abs(zm) * (-LOG2E))) + jnp.maximum(zm, 0.0)
                split_ref[blk] = sp.astype(bf16)

    def iteration(i, u):
        stage_c2(*item(i - PIPE_DEPTH))
        stage_b2c1(*item(i - 1), (u - 1) % ITER_PER_TRIP)
        stage_ab1(*item(i), u)

    def body(m, carry):
        for u in range(ITER_PER_TRIP):
            iteration(ITER_PER_TRIP * m + u, u)
        return carry

    n_iter = n_items + PIPE_DEPTH
    full_trips = n_iter // ITER_PER_TRIP
    lax.fori_loop(0, full_trips, body, 0)
    for u in range(n_iter % ITER_PER_TRIP):
        iteration(full_trips * ITER_PER_TRIP + u, u)


def _attention(q, k, v, tri):
    _, bsz, seq, _ = q.shape
    n_qt = seq // TQ
    items = [(hp, qi, t) for hp in range(N_PAIR) for qi in range(n_qt)
             for t in range((qi + 1) * KT_PER_QT - 1, -1, -1)]
    hp_tab, qt, kt = (jnp.asarray([it[n] for it in items], jnp.int32) for n in range(3))
    seq_spec = pl.BlockSpec((N_PAIR, 1, seq, LANES), lambda b, *_: (0, b, 0, 0))
    tri_spec = pl.BlockSpec((TK, TK), lambda b, *_: (0, 0), pipeline_mode=pl.Buffered(1))
    return pl.pallas_call(
        _attn_kernel,
        out_shape=jax.ShapeDtypeStruct((N_PAIR, bsz, seq, LANES), f32),
        grid_spec=pltpu.PrefetchScalarGridSpec(
            num_scalar_prefetch=3,
            grid=(bsz,),
            in_specs=[seq_spec, seq_spec, seq_spec, tri_spec],
            out_specs=seq_spec,
            scratch_shapes=[
                pltpu.VMEM((HEADS_PER_STEP, TQ, LANES), f32),
                pltpu.VMEM((ITER_PER_TRIP, N_BLK, TQ, LANES), f32),
                pltpu.VMEM((N_BLK, TQ, LANES), bf16),
                pltpu.VMEM((N_BLK, TQ, LANES), bf16),
                pltpu.VMEM((BLK_PER_TILE, TQ, LANES), jnp.int32),
            ]),
        compiler_params=pltpu.CompilerParams(
            dimension_semantics=("parallel",), vmem_limit_bytes=VMEM_LIMIT),
        name="stickbreak",
    )(hp_tab, qt, kt, q, k, v, tri)


def _ffn_out_kernel(x_ref, yr_ref, ya_ref, ga_ref, wo_ref, g2_ref, wg_ref, wu_ref, wd_ref,
                    o_ref, h_ref):
    ya = jnp.concatenate([ya_ref[p] for p in range(N_PAIR)], axis=1)
    ya = _rms(ya, ga_ref[...]).astype(bf16)
    x2 = (x_ref[...] + _dot(yr_ref[...], wo_ref[0:D_RNN, :])
          + _dot(ya, wo_ref[D_RNN:D_RNN + D_ATT, :]))
    o_ref[...] = _ffn_tile(x2, g2_ref, wg_ref, wu_ref, wd_ref, h_ref)


def _ffn_out(x2d, yr, ya, ga, w_out, g2, wg, wu, wd):
    n_tok = x2d.shape[0]
    return pl.pallas_call(
        _ffn_out_kernel,
        out_shape=jax.ShapeDtypeStruct((n_tok, D_MODEL), f32),
        grid=(n_tok // TM,),
        in_specs=[_tok_spec(D_MODEL), _tok_spec(D_RNN), _pair_spec(), _const_spec((1, D_ATT)),
                  _const_spec((D_RNN + D_ATT, D_MODEL))] + _ffn_specs(),
        out_specs=_tok_spec(D_MODEL),
        scratch_shapes=[pltpu.VMEM((TM, D_FF), bf16)],
        compiler_params=pltpu.CompilerParams(
            dimension_semantics=("parallel",), vmem_limit_bytes=VMEM_LIMIT),
        name="ffn_out",
    )(x2d, yr, ya, ga, w_out, g2, wg, wu, wd)


def _block_diag(w):
    eye = jnp.eye(RNN_BLOCKS, dtype=w.dtype)
    return jnp.einsum("ncd,nm->ncmd", w, eye).reshape(D_RNN, D_RNN)


def _scan_matrix():
    j = jnp.arange(TK)
    return -(j[:, None] >= j[None, :]).astype(bf16)


def kernel(x, ffn1_norm, ffn1_w_gate, ffn1_w_up, ffn1_w_down, mix_norm, w_in, conv_w, conv_b,
           rg_w_a, rg_b_a, rg_w_x, rg_b_x, rg_lambda, q_norm, k_norm, rnn_out_norm,
           attn_out_norm, w_out, ffn2_norm, ffn2_w_gate, ffn2_w_up, ffn2_w_down):
    bsz, seq, _ = x.shape
    depth = ffn1_norm.shape[0]
    row_head = (jnp.arange(2 * LANES) % LANES) // HEAD_DIM
    seg = (row_head[:, None] == (jnp.arange(LANES) // HEAD_DIM)[None, :]).astype(bf16)
    tri = _scan_matrix()
    x2d = x.reshape(bsz * seq, D_MODEL)
    for l in range(depth):
        x2d, xr, gate, q, k, v = _ffn_in(
            x2d, ffn1_norm[l][None], ffn1_w_gate[l].astype(bf16), ffn1_w_up[l].astype(bf16),
            ffn1_w_down[l].astype(bf16), mix_norm[l][None], w_in[l].astype(bf16),
            jnp.tile(q_norm[l], N_HEADS)[None], jnp.tile(k_norm[l], N_HEADS)[None], seg)
        y_rnn = _rglru(
            xr.reshape(bsz, seq, D_RNN), gate.reshape(bsz, seq, D_RNN),
            conv_w[l], conv_b[l][None], _block_diag(rg_w_a[l]).astype(bf16), rg_b_a[l][None],
            _block_diag(rg_w_x[l]).astype(bf16), rg_b_x[l][None], rg_lambda[l][None],
            rnn_out_norm[l][None])
        pairs = (N_PAIR, bsz, seq, LANES)
        y_att = _attention(q.reshape(pairs), k.reshape(pairs), v.reshape(pairs), tri)
        x2d = _ffn_out(x2d, y_rnn.reshape(bsz * seq, D_RNN), y_att.reshape(N_PAIR, bsz * seq, LANES),
                       attn_out_norm[l][None], w_out[l].astype(bf16), ffn2_norm[l][None],
                       ffn2_w_gate[l].astype(bf16), ffn2_w_up[l].astype(bf16),
                       ffn2_w_down[l].astype(bf16))
    return x2d.reshape(bsz, seq, D_MODEL)
```

```python
import math

import jax
import jax.numpy as jnp
from jax import lax
from jax.experimental import pallas as pl
from jax.experimental.pallas import tpu as pltpu

D_MODEL = 1024
D_RNN = 512
RNN_BLOCKS = 8
RNN_BW = D_RNN // RNN_BLOCKS
CONV_W = 4
RG_C = 8.0
D_ATT = 512
HEAD_DIM = 64
N_HEADS = D_ATT // HEAD_DIM
D_FF = 2816
N_IN = 2 * D_RNN + 3 * D_ATT
EPS = 1e-6

LANES = 128
SUBLANES = 8
HEADS_PER_STEP = LANES // HEAD_DIM
N_PAIR = D_ATT // LANES
VMEM_LIMIT = 56 * 1024 * 1024

TM = 512
FF_CHUNK = 256
TS = 512
TQ = 512
TK = 256

f32 = jnp.float32
bf16 = jnp.bfloat16


def _rms(xf, g):
    r = lax.rsqrt(jnp.mean(xf * xf, axis=-1, keepdims=True) + EPS)
    return xf * r * g


def _dot(a, b):
    return jnp.dot(a, b, preferred_element_type=f32)


def _split_bf16(x):
    hi = x.astype(bf16)
    lo = (x - hi.astype(f32)).astype(bf16)
    return hi, lo


def _const_spec(shape):
    nd = len(shape)
    return pl.BlockSpec(shape, lambda *_: (0,) * nd, pipeline_mode=pl.Buffered(1))


def _ffn_tile(x, g_ref, wg_ref, wu_ref, wd_ref, h_ref):
    xn = _rms(x, g_ref[...]).astype(bf16)
    for c in range(0, D_FF, FF_CHUNK):
        gate = _dot(xn, wg_ref[:, c:c + FF_CHUNK])
        up = _dot(xn, wu_ref[:, c:c + FF_CHUNK])
        h_ref[:, c:c + FF_CHUNK] = (jax.nn.silu(gate) * up).astype(bf16)
    return x + 0.5 * _dot(h_ref[...], wd_ref[...])


def _ffn_specs():
    return [_const_spec((1, D_MODEL)), _const_spec((D_MODEL, D_FF)),
            _const_spec((D_MODEL, D_FF)), _const_spec((D_FF, D_MODEL))]


def _tok_spec(d):
    return pl.BlockSpec((TM, d), lambda i: (i, 0))


def _pair_spec():
    return pl.BlockSpec((N_PAIR, TM, LANES), lambda i: (0, i, 0))


def _head_norm(t, gain_tiled, seg_ref):
    hi, lo = _split_bf16(t * t)
    groups = []
    for g in range(0, t.shape[1], LANES):
        both = jnp.concatenate([hi[:, g:g + LANES], lo[:, g:g + LANES]], axis=1)
        groups.append(_dot(both, seg_ref[...]))
    ss = jnp.concatenate(groups, axis=1)
    r = lax.rsqrt(ss * (1.0 / HEAD_DIM) + EPS)
    return t * r * gain_tiled


def _ffn_in_kernel(x_ref, g1_ref, wg_ref, wu_ref, wd_ref, g_ref, w_ref, qg_ref, kg_ref, seg_ref,
                   o_ref, xr_ref, gate_ref, q_ref, k_ref, v_ref, h_ref):
    x1 = _ffn_tile(x_ref[...], g1_ref, wg_ref, wu_ref, wd_ref, h_ref)
    o_ref[...] = x1
    h = _rms(x1, g_ref[...]).astype(bf16)
    xr_ref[...] = _dot(h, w_ref[:, 0:D_RNN])
    gate_ref[...] = _dot(h, w_ref[:, D_RNN:2 * D_RNN])
    o = 2 * D_RNN
    q = _dot(h, w_ref[:, o:o + D_ATT])
    k = _dot(h, w_ref[:, o + D_ATT:o + 2 * D_ATT])
    v = _dot(h, w_ref[:, o + 2 * D_ATT:o + 3 * D_ATT])
    scale = 1.0 / math.sqrt(HEAD_DIM)
    qn = (_head_norm(q, qg_ref[...], seg_ref) * scale).astype(bf16)
    kn = _head_norm(k, kg_ref[...], seg_ref).astype(bf16)
    vb = v.astype(bf16)
    for p in range(N_PAIR):
        lanes = slice(p * LANES, (p + 1) * LANES)
        q_ref[p] = qn[:, lanes]
        k_ref[p] = kn[:, lanes]
        v_ref[p] = vb[:, lanes]


def _ffn_in(x2d, g1, wg, wu, wd, g, w_in, qg, kg, seg):
    n_tok = x2d.shape[0]
    return pl.pallas_call(
        _ffn_in_kernel,
        out_shape=(jax.ShapeDtypeStruct((n_tok, D_MODEL), f32),
                   jax.ShapeDtypeStruct((n_tok, D_RNN), f32),
                   jax.ShapeDtypeStruct((n_tok, D_RNN), f32),
                   jax.ShapeDtypeStruct((N_PAIR, n_tok, LANES), bf16),
                   jax.ShapeDtypeStruct((N_PAIR, n_tok, LANES), bf16),
                   jax.ShapeDtypeStruct((N_PAIR, n_tok, LANES), bf16)),
        grid=(n_tok // TM,),
        in_specs=[_tok_spec(D_MODEL)] + _ffn_specs() + [
            _const_spec((1, D_MODEL)), _const_spec((D_MODEL, N_IN)),
            _const_spec((1, D_ATT)), _const_spec((1, D_ATT)), _const_spec((2 * LANES, LANES))],
        out_specs=(_tok_spec(D_MODEL), _tok_spec(D_RNN), _tok_spec(D_RNN),
                   _pair_spec(), _pair_spec(), _pair_spec()),
        scratch_shapes=[pltpu.VMEM((TM, D_FF), bf16)],
        compiler_params=pltpu.CompilerParams(
            dimension_semantics=("parallel",), vmem_limit_bytes=VMEM_LIMIT),
        name="ffn_in",
    )(x2d, g1, wg, wu, wd, g, w_in, qg, kg, seg)


HIST = 8


def _rglru_kernel(xr_ref, gate_ref, cw_ref, cb_ref, wa_ref, ba_ref, wx_ref, bx_ref,
                  lam_ref, g_ref, o_ref, xbuf_ref, hc_ref):
    @pl.when(pl.program_id(1) == 0)
    def _():
        xbuf_ref[0:HIST, :] = jnp.zeros((HIST, D_RNN), f32)
        hc_ref[...] = jnp.zeros((1, D_RNN), f32)

    x = xr_ref[0]
    xbuf_ref[HIST:HIST + TS, :] = x
    xc = cb_ref[...] + cw_ref[CONV_W - 1:CONV_W, :] * x
    for j in range(CONV_W - 1):
        off = HIST - (CONV_W - 1) + j
        xc = xc + cw_ref[j:j + 1, :] * xbuf_ref[off:off + TS, :]
    xbuf_ref[0:HIST, :] = x[TS - HIST:TS, :]

    xcb = xc.astype(bf16)
    r = jax.nn.sigmoid(_dot(xcb, wa_ref[...]) + ba_ref[...])
    i = jax.nn.sigmoid(_dot(xcb, wx_ref[...]) + bx_ref[...])
    lam = lam_ref[...]
    log_sig_lam = jnp.minimum(lam, 0.0) - jnp.log1p(jnp.exp(-jnp.abs(lam)))
    log_a = RG_C * r * log_sig_lam
    a = jnp.exp(log_a)
    th = jnp.tanh(log_a)
    mult = jnp.sqrt(-2.0 * th / (1.0 - th))
    b = mult * (i * xc)

    n_grp = TS // SUBLANES
    a = a.reshape(n_grp, SUBLANES, D_RNN)
    b = b.reshape(n_grp, SUBLANES, D_RNN)
    row = lax.broadcasted_iota(jnp.int32, (n_grp, SUBLANES, D_RNN), 1)
    d = 1
    while d < SUBLANES:
        a_sh = pltpu.roll(a, d, axis=1)
        b_sh = pltpu.roll(b, d, axis=1)
        m = row >= d
        b = jnp.where(m, a * b_sh, 0.0) + b
        a = jnp.where(m, a * a_sh, a)
        d *= 2
    carry = jnp.broadcast_to(hc_ref[...], (SUBLANES, D_RNN))
    groups = []
    for g in range(n_grp):
        h_g = a[g] * carry + b[g]
        groups.append(h_g)
        carry = jnp.broadcast_to(h_g[SUBLANES - 1:SUBLANES, :], (SUBLANES, D_RNN))
    h = jnp.concatenate(groups, axis=0)
    hc_ref[...] = groups[-1][SUBLANES - 1:SUBLANES, :]

    y = h * jax.nn.gelu(gate_ref[0])
    o_ref[0] = _rms(y, g_ref[...]).astype(bf16)


def _rglru(xr, gate, cw, cb, wa, ba, wx, bx, lam, g):
    bsz, seq, _ = xr.shape
    seq_spec = pl.BlockSpec((1, TS, D_RNN), lambda b, s: (b, s, 0))
    vec = _const_spec((1, D_RNN))
    return pl.pallas_call(
        _rglru_kernel,
        out_shape=jax.ShapeDtypeStruct((bsz, seq, D_RNN), bf16),
        grid=(bsz, seq // TS),
        in_specs=[seq_spec, seq_spec, _const_spec((CONV_W, D_RNN)), vec,
                  _const_spec((D_RNN, D_RNN)), vec, _const_spec((D_RNN, D_RNN)), vec, vec, vec],
        out_specs=seq_spec,
        scratch_shapes=[pltpu.VMEM((HIST + TS, D_RNN), f32), pltpu.VMEM((1, D_RNN), f32)],
        compiler_params=pltpu.CompilerParams(
            dimension_semantics=("parallel", "arbitrary"), vmem_limit_bytes=VMEM_LIMIT),
        name="rglru",
    )(xr, gate, cw, cb, wa, ba, wx, bx, lam, g)


PIPE_DEPTH = 2
BLK_PER_TILE = TK // LANES
KT_PER_QT = TQ // TK
N_BLK = HEADS_PER_STEP * BLK_PER_TILE
ITER_PER_TRIP = 4
LOG2E = 1.4426950408889634


MASKED = -1e30


def _attn_kernel(hp_ref, qt_ref, kt_ref, q_ref, k_ref, v_ref, tri_ref, o_ref,
                 c_ref, z_ref, split_ref, w_ref, pen_ref):
    n_items = qt_ref.shape[0]
    heads = range(HEADS_PER_STEP)
    for ref in (o_ref, c_ref, z_ref, split_ref, w_ref):
        ref[...] = jnp.zeros_like(ref)
    pen_ref[0] = jnp.zeros_like(pen_ref[0])
    for j in range(KT_PER_QT):
        for b in range(BLK_PER_TILE):
            key = lax.broadcasted_iota(jnp.int32, (TQ, LANES), 1) + (j * TK + b * LANES)
            query = lax.broadcasted_iota(jnp.int32, (TQ, LANES), 0)
            pen_ref[j + 1, b] = jnp.where(key < query, 0.0, MASKED)
    lane_head = lax.broadcasted_iota(jnp.int32, (TK, LANES), 1) // HEAD_DIM

    def item(j):
        j = jnp.clip(j, 0, n_items - 1)
        return hp_ref[j], qt_ref[j], kt_ref[j]

    def diag_tile(qi):
        return (qi + 1) * KT_PER_QT - 1

    def per_head_rows(ref, hp, t):
        x = ref[hp, 0, pl.ds(pl.multiple_of(t * TK, TK), TK), :]
        return jnp.concatenate([jnp.where(lane_head == h, x, jnp.zeros_like(x)) for h in heads], axis=0)

    def stage_c2(hp, qi, t):
        rows = pl.ds(pl.multiple_of(qi * TQ, TQ), TQ)
        acc = jnp.where(t == diag_tile(qi), 0.0, o_ref[hp, 0, rows, :])
        w = jnp.concatenate([w_ref[blk] for blk in range(N_BLK)], axis=1)
        o_ref[hp, 0, rows, :] = acc + _dot(w, per_head_rows(v_ref, hp, t))

    def stage_b2c1(hp, qi, t, slot):
        for h in heads:
            first = h * BLK_PER_TILE
            parts = [split_ref[blk] for blk in range(first, first + BLK_PER_TILE)]
            r = _dot(jnp.concatenate(parts, axis=1), tri_ref[...])
            c = jnp.where(t == diag_tile(qi), 0.0, c_ref[h])
            for b in range(BLK_PER_TILE):
                s = z_ref[slot, first + b] + r[:, b * LANES:(b + 1) * LANES] + c
                w_ref[first + b] = jnp.exp(s).astype(bf16)
            c_ref[h] = c + jnp.broadcast_to(r[:, 0:1], (TQ, LANES))

    def stage_ab1(hp, qi, t, slot):
        q = q_ref[hp, 0, pl.ds(pl.multiple_of(qi * TQ, TQ), TQ), :]
        z = lax.dot_general(q, per_head_rows(k_ref, hp, t), (((1,), (1,)), ((), ())),
                            preferred_element_type=f32)
        kind = jnp.maximum(t - qi * KT_PER_QT + 1, 0)
        for b in range(BLK_PER_TILE):
            pen = pen_ref[kind, b]
            for h in heads:
                blk = h * BLK_PER_TILE + b
                zm = z[:, blk * LANES:(blk + 1) * LANES] + pen
                z_ref[slot, blk] = zm
                sp = jnp.log(1.0 + jnp.exp2(jnp.abs(zm) * (-LOG2E))) + jnp.maximum(zm, 0.0)
                split_ref[blk] = sp.astype(bf16)

    def iteration(i, u):
        stage_c2(*item(i - PIPE_DEPTH))
        stage_b2c1(*item(i - 1), (u - 1) % ITER_PER_TRIP)
        stage_ab1(*item(i), u)

    def body(m, carry):
        for u in range(ITER_PER_TRIP):
            iteration(ITER_PER_TRIP * m + u, u)
        return carry

    n_iter = n_items + PIPE_DEPTH
    full_trips = n_iter // ITER_PER_TRIP
    lax.fori_loop(0, full_trips, body, 0)
    for u in range(n_iter % ITER_PER_TRIP):
        iteration(full_trips * ITER_PER_TRIP + u, u)


def _attention(q, k, v, tri):
    _, bsz, seq, _ = q.shape
    n_qt = seq // TQ
    items = [(hp, qi, t) for hp in range(N_PAIR) for qi in range(n_qt)
             for t in range((qi + 1) * KT_PER_QT - 1, -1, -1)]
    hp_tab, qt, kt = (jnp.asarray([it[n] for it in items], jnp.int32) for n in range(3))
    seq_spec = pl.BlockSpec((N_PAIR, 1, seq, LANES), lambda b, *_: (0, b, 0, 0))
    tri_spec = pl.BlockSpec((TK, TK), lambda b, *_: (0, 0), pipeline_mode=pl.Buffered(1))
    return pl.pallas_call(
        _attn_kernel,
        out_shape=jax.ShapeDtypeStruct((N_PAIR, bsz, seq, LANES), f32),
        grid_spec=pltpu.PrefetchScalarGridSpec(
            num_scalar_prefetch=3,
            grid=(bsz,),
            in_specs=[seq_spec, seq_spec, seq_spec, tri_spec],
            out_specs=seq_spec,
            scratch_shapes=[
                pltpu.VMEM((HEADS_PER_STEP, TQ, LANES), f32),
                pltpu.VMEM((ITER_PER_TRIP, N_BLK, TQ, LANES), f32),
                pltpu.VMEM((N_BLK, TQ, LANES), bf16),
                pltpu.VMEM((N_BLK, TQ, LANES), bf16),
                pltpu.VMEM((KT_PER_QT + 1, BLK_PER_TILE, TQ, LANES), f32),
            ]),
        compiler_params=pltpu.CompilerParams(
            dimension_semantics=("parallel",), vmem_limit_bytes=VMEM_LIMIT),
        name="stickbreak",
    )(hp_tab, qt, kt, q, k, v, tri)


def _ffn_out_kernel(x_ref, yr_ref, ya_ref, ga_ref, wo_ref, g2_ref, wg_ref, wu_ref, wd_ref,
                    o_ref, h_ref):
    ya = jnp.concatenate([ya_ref[p] for p in range(N_PAIR)], axis=1)
    ya = _rms(ya, ga_ref[...]).astype(bf16)
    x2 = (x_ref[...] + _dot(yr_ref[...], wo_ref[0:D_RNN, :])
          + _dot(ya, wo_ref[D_RNN:D_RNN + D_ATT, :]))
    o_ref[...] = _ffn_tile(x2, g2_ref, wg_ref, wu_ref, wd_ref, h_ref)


def _ffn_out(x2d, yr, ya, ga, w_out, g2, wg, wu, wd):
    n_tok = x2d.shape[0]
    return pl.pallas_call(
        _ffn_out_kernel,
        out_shape=jax.ShapeDtypeStruct((n_tok, D_MODEL), f32),
        grid=(n_tok // TM,),
        in_specs=[_tok_spec(D_MODEL), _tok_spec(D_RNN), _pair_spec(), _const_spec((1, D_ATT)),
                  _const_spec((D_RNN + D_ATT, D_MODEL))] + _ffn_specs(),
        out_specs=_tok_spec(D_MODEL),
        scratch_shapes=[pltpu.VMEM((TM, D_FF), bf16)],
        compiler_params=pltpu.CompilerParams(
            dimension_semantics=("parallel",), vmem_limit_bytes=VMEM_LIMIT),
        name="ffn_out",
    )(x2d, yr, ya, ga, w_out, g2, wg, wu, wd)


def _block_diag(w):
    eye = jnp.eye(RNN_BLOCKS, dtype=w.dtype)
    return jnp.einsum("ncd,nm->ncmd", w, eye).reshape(D_RNN, D_RNN)


def _scan_matrix():
    j = jnp.arange(TK)
    return -(j[:, None] >= j[None, :]).astype(bf16)


def kernel(x, ffn1_norm, ffn1_w_gate, ffn1_w_up, ffn1_w_down, mix_norm, w_in, conv_w, conv_b,
           rg_w_a, rg_b_a, rg_w_x, rg_b_x, rg_lambda, q_norm, k_norm, rnn_out_norm,
           attn_out_norm, w_out, ffn2_norm, ffn2_w_gate, ffn2_w_up, ffn2_w_down):
    bsz, seq, _ = x.shape
    depth = ffn1_norm.shape[0]
    row_head = (jnp.arange(2 * LANES) % LANES) // HEAD_DIM
    seg = (row_head[:, None] == (jnp.arange(LANES) // HEAD_DIM)[None, :]).astype(bf16)
    tri = _scan_matrix()
    x2d = x.reshape(bsz * seq, D_MODEL)
    for l in range(depth):
        x2d, xr, gate, q, k, v = _ffn_in(
            x2d, ffn1_norm[l][None], ffn1_w_gate[l].astype(bf16), ffn1_w_up[l].astype(bf16),
            ffn1_w_down[l].astype(bf16), mix_norm[l][None], w_in[l].astype(bf16),
            jnp.tile(q_norm[l], N_HEADS)[None], jnp.tile(k_norm[l], N_HEADS)[None], seg)
        y_rnn = _rglru(
            xr.reshape(bsz, seq, D_RNN), gate.reshape(bsz, seq, D_RNN),
            conv_w[l], conv_b[l][None], _block_diag(rg_w_a[l]).astype(bf16), rg_b_a[l][None],
            _block_diag(rg_w_x[l]).astype(bf16), rg_b_x[l][None], rg_lambda[l][None],
            rnn_out_norm[l][None])
        pairs = (N_PAIR, bsz, seq, LANES)
        y_att = _attention(q.reshape(pairs), k.reshape(pairs), v.reshape(pairs), tri)
        x2d = _ffn_out(x2d, y_rnn.reshape(bsz * seq, D_RNN), y_att.reshape(N_PAIR, bsz * seq, LANES),
                       attn_out_norm[l][None], w_out[l].astype(bf16), ffn2_norm[l][None],
                       ffn2_w_gate[l].astype(bf16), ffn2_w_up[l].astype(bf16),
                       ffn2_w_down[l].astype(bf16))
    return x2d.reshape(bsz, seq, D_MODEL)
```

```python
import math

import jax
import jax.numpy as jnp
from jax import lax
from jax.experimental import pallas as pl
from jax.experimental.pallas import tpu as pltpu

D_MODEL = 1024
D_RNN = 512
RNN_BLOCKS = 8
RNN_BW = D_RNN // RNN_BLOCKS
CONV_W = 4
RG_C = 8.0
D_ATT = 512
HEAD_DIM = 64
N_HEADS = D_ATT // HEAD_DIM
D_FF = 2816
N_IN = 2 * D_RNN + 3 * D_ATT
EPS = 1e-6

LANES = 128
SUBLANES = 8
HEADS_PER_STEP = LANES // HEAD_DIM
N_PAIR = D_ATT // LANES
VMEM_LIMIT = 56 * 1024 * 1024

TM = 512
FF_CHUNK = 256
TS = 512
TQ = 512
TK = 256

f32 = jnp.float32
bf16 = jnp.bfloat16


def _rms(xf, g):
    r = lax.rsqrt(jnp.mean(xf * xf, axis=-1, keepdims=True) + EPS)
    return xf * r * g


def _dot(a, b):
    return jnp.dot(a, b, preferred_element_type=f32)


def _split_bf16(x):
    hi = x.astype(bf16)
    lo = (x - hi.astype(f32)).astype(bf16)
    return hi, lo


def _const_spec(shape):
    nd = len(shape)
    return pl.BlockSpec(shape, lambda *_: (0,) * nd, pipeline_mode=pl.Buffered(1))


def _ffn_tile(x, g_ref, wg_ref, wu_ref, wd_ref, h_ref):
    xn = _rms(x, g_ref[...]).astype(bf16)
    for c in range(0, D_FF, FF_CHUNK):
        gate = _dot(xn, wg_ref[:, c:c + FF_CHUNK])
        up = _dot(xn, wu_ref[:, c:c + FF_CHUNK])
        h_ref[:, c:c + FF_CHUNK] = (jax.nn.silu(gate) * up).astype(bf16)
    return x + 0.5 * _dot(h_ref[...], wd_ref[...])


def _ffn_specs():
    return [_const_spec((1, D_MODEL)), _const_spec((D_MODEL, D_FF)),
            _const_spec((D_MODEL, D_FF)), _const_spec((D_FF, D_MODEL))]


def _tok_spec(d):
    return pl.BlockSpec((TM, d), lambda i: (i, 0))


def _pair_spec():
    return pl.BlockSpec((N_PAIR, TM, LANES), lambda i: (0, i, 0))


def _head_norm(t, gain_tiled, seg_ref):
    hi, lo = _split_bf16(t * t)
    groups = []
    for g in range(0, t.shape[1], LANES):
        both = jnp.concatenate([hi[:, g:g + LANES], lo[:, g:g + LANES]], axis=1)
        groups.append(_dot(both, seg_ref[...]))
    ss = jnp.concatenate(groups, axis=1)
    r = lax.rsqrt(ss * (1.0 / HEAD_DIM) + EPS)
    return t * r * gain_tiled


def _ffn_in_kernel(x_ref, g1_ref, wg_ref, wu_ref, wd_ref, g_ref, w_ref, qg_ref, kg_ref, seg_ref,
                   o_ref, xr_ref, gate_ref, q_ref, k_ref, v_ref, h_ref):
    x1 = _ffn_tile(x_ref[...], g1_ref, wg_ref, wu_ref, wd_ref, h_ref)
    o_ref[...] = x1
    h = _rms(x1, g_ref[...]).astype(bf16)
    xr_ref[...] = _dot(h, w_ref[:, 0:D_RNN])
    gate_ref[...] = _dot(h, w_ref[:, D_RNN:2 * D_RNN])
    o = 2 * D_RNN
    q = _dot(h, w_ref[:, o:o + D_ATT])
    k = _dot(h, w_ref[:, o + D_ATT:o + 2 * D_ATT])
    v = _dot(h, w_ref[:, o + 2 * D_ATT:o + 3 * D_ATT])
    scale = 1.0 / math.sqrt(HEAD_DIM)
    qn = (_head_norm(q, qg_ref[...], seg_ref) * scale).astype(bf16)
    kn = _head_norm(k, kg_ref[...], seg_ref).astype(bf16)
    vb = v.astype(bf16)
    for p in range(N_PAIR):
        lanes = slice(p * LANES, (p + 1) * LANES)
        q_ref[p] = qn[:, lanes]
        k_ref[p] = kn[:, lanes]
        v_ref[p] = vb[:, lanes]


def _ffn_in(x2d, g1, wg, wu, wd, g, w_in, qg, kg, seg):
    n_tok = x2d.shape[0]
    return pl.pallas_call(
        _ffn_in_kernel,
        out_shape=(jax.ShapeDtypeStruct((n_tok, D_MODEL), f32),
                   jax.ShapeDtypeStruct((n_tok, D_RNN), f32),
                   jax.ShapeDtypeStruct((n_tok, D_RNN), f32),
                   jax.ShapeDtypeStruct((N_PAIR, n_tok, LANES), bf16),
                   jax.ShapeDtypeStruct((N_PAIR, n_tok, LANES), bf16),
                   jax.ShapeDtypeStruct((N_PAIR, n_tok, LANES), bf16)),
        grid=(n_tok // TM,),
        in_specs=[_tok_spec(D_MODEL)] + _ffn_specs() + [
            _const_spec((1, D_MODEL)), _const_spec((D_MODEL, N_IN)),
            _const_spec((1, D_ATT)), _const_spec((1, D_ATT)), _const_spec((2 * LANES, LANES))],
        out_specs=(_tok_spec(D_MODEL), _tok_spec(D_RNN), _tok_spec(D_RNN),
                   _pair_spec(), _pair_spec(), _pair_spec()),
        scratch_shapes=[pltpu.VMEM((TM, D_FF), bf16)],
        compiler_params=pltpu.CompilerParams(
            dimension_semantics=("parallel",), vmem_limit_bytes=VMEM_LIMIT),
        name="ffn_in",
    )(x2d, g1, wg, wu, wd, g, w_in, qg, kg, seg)


HIST = 8


def _rglru_kernel(xr_ref, gate_ref, cw_ref, cb_ref, wa_ref, ba_ref, wx_ref, bx_ref,
                  lam_ref, g_ref, o_ref, xbuf_ref, hc_ref):
    @pl.when(pl.program_id(1) == 0)
    def _():
        xbuf_ref[0:HIST, :] = jnp.zeros((HIST, D_RNN), f32)
        hc_ref[...] = jnp.zeros((1, D_RNN), f32)

    x = xr_ref[0]
    xbuf_ref[HIST:HIST + TS, :] = x
    xc = cb_ref[...] + cw_ref[CONV_W - 1:CONV_W, :] * x
    for j in range(CONV_W - 1):
        off = HIST - (CONV_W - 1) + j
        xc = xc + cw_ref[j:j + 1, :] * xbuf_ref[off:off + TS, :]
    xbuf_ref[0:HIST, :] = x[TS - HIST:TS, :]

    xcb = xc.astype(bf16)
    r = jax.nn.sigmoid(_dot(xcb, wa_ref[...]) + ba_ref[...])
    i = jax.nn.sigmoid(_dot(xcb, wx_ref[...]) + bx_ref[...])
    lam = lam_ref[...]
    log_sig_lam = jnp.minimum(lam, 0.0) - jnp.log1p(jnp.exp(-jnp.abs(lam)))
    log_a = RG_C * r * log_sig_lam
    a = jnp.exp(log_a)
    th = jnp.tanh(log_a)
    mult = jnp.sqrt(-2.0 * th / (1.0 - th))
    b = mult * (i * xc)

    n_grp = TS // SUBLANES
    a = a.reshape(n_grp, SUBLANES, D_RNN)
    b = b.reshape(n_grp, SUBLANES, D_RNN)
    row = lax.broadcasted_iota(jnp.int32, (n_grp, SUBLANES, D_RNN), 1)
    d = 1
    while d < SUBLANES:
        a_sh = pltpu.roll(a, d, axis=1)
        b_sh = pltpu.roll(b, d, axis=1)
        m = row >= d
        b = jnp.where(m, a * b_sh, 0.0) + b
        a = jnp.where(m, a * a_sh, a)
        d *= 2
    carry = jnp.broadcast_to(hc_ref[...], (SUBLANES, D_RNN))
    groups = []
    for g in range(n_grp):
        h_g = a[g] * carry + b[g]
        groups.append(h_g)
        carry = jnp.broadcast_to(h_g[SUBLANES - 1:SUBLANES, :], (SUBLANES, D_RNN))
    h = jnp.concatenate(groups, axis=0)
    hc_ref[...] = groups[-1][SUBLANES - 1:SUBLANES, :]

    y = h * jax.nn.gelu(gate_ref[0])
    o_ref[0] = _rms(y, g_ref[...]).astype(bf16)


def _rglru(xr, gate, cw, cb, wa, ba, wx, bx, lam, g):
    bsz, seq, _ = xr.shape
    seq_spec = pl.BlockSpec((1, TS, D_RNN), lambda b, s: (b, s, 0))
    vec = _const_spec((1, D_RNN))
    return pl.pallas_call(
        _rglru_kernel,
        out_shape=jax.ShapeDtypeStruct((bsz, seq, D_RNN), bf16),
        grid=(bsz, seq // TS),
        in_specs=[seq_spec, seq_spec, _const_spec((CONV_W, D_RNN)), vec,
                  _const_spec((D_RNN, D_RNN)), vec, _const_spec((D_RNN, D_RNN)), vec, vec, vec],
        out_specs=seq_spec,
        scratch_shapes=[pltpu.VMEM((HIST + TS, D_RNN), f32), pltpu.VMEM((1, D_RNN), f32)],
        compiler_params=pltpu.CompilerParams(
            dimension_semantics=("parallel", "arbitrary"), vmem_limit_bytes=VMEM_LIMIT),
        name="rglru",
    )(xr, gate, cw, cb, wa, ba, wx, bx, lam, g)


PIPE_DEPTH = 2
BLK_PER_TILE = TK // LANES
KT_PER_QT = TQ // TK
N_BLK = HEADS_PER_STEP * BLK_PER_TILE
ITER_PER_TRIP = 4
LOG2E = 1.4426950408889634


MASKED = -1e30


def _attn_kernel(hp_ref, qt_ref, kt_ref, q_ref, k_ref, v_ref, tri_ref, o_ref,
                 c_ref, z_ref, split_ref, w_ref, pen_ref):
    n_items = qt_ref.shape[0]
    heads = range(HEADS_PER_STEP)
    for ref in (o_ref, c_ref, z_ref, split_ref, w_ref):
        ref[...] = jnp.zeros_like(ref)
    pen_ref[0] = jnp.zeros_like(pen_ref[0])
    for j in range(KT_PER_QT):
        for b in range(BLK_PER_TILE):
            key = lax.broadcasted_iota(jnp.int32, (TQ, LANES), 1) + (j * TK + b * LANES)
            query = lax.broadcasted_iota(jnp.int32, (TQ, LANES), 0)
            pen_ref[j + 1, b] = jnp.where(key < query, 0.0, MASKED)
    lane_head = lax.broadcasted_iota(jnp.int32, (TK, LANES), 1) // HEAD_DIM

    def item(j):
        j = jnp.clip(j, 0, n_items - 1)
        return hp_ref[j], qt_ref[j], kt_ref[j]

    def diag_tile(qi):
        return (qi + 1) * KT_PER_QT - 1

    def per_head(ref, hp, t):
        x = ref[hp, 0, pl.ds(pl.multiple_of(t * TK, TK), TK), :]
        return [jnp.where(lane_head == h, x, jnp.zeros_like(x)) for h in heads]

    def per_head_rows(ref, hp, t):
        return jnp.concatenate(per_head(ref, hp, t), axis=0)

    def stage_c2(hp, qi, t):
        rows = pl.ds(pl.multiple_of(qi * TQ, TQ), TQ)
        acc = jnp.where(t == diag_tile(qi), 0.0, o_ref[hp, 0, rows, :])
        for h, v_h in enumerate(per_head(v_ref, hp, t)):
            w_h = jnp.concatenate([w_ref[blk] for blk in range(h * BLK_PER_TILE, (h + 1) * BLK_PER_TILE)],
                                  axis=1)
            acc = acc + _dot(w_h, v_h)
        o_ref[hp, 0, rows, :] = acc

    def stage_b2c1(hp, qi, t, slot):
        for h in heads:
            first = h * BLK_PER_TILE
            parts = [split_ref[blk] for blk in range(first, first + BLK_PER_TILE)]
            r = _dot(jnp.concatenate(parts, axis=1), tri_ref[...])
            c = jnp.where(t == diag_tile(qi), 0.0, c_ref[h])
            for b in range(BLK_PER_TILE):
                s = z_ref[slot, first + b] + r[:, b * LANES:(b + 1) * LANES] + c
                w_ref[first + b] = jnp.exp(s).astype(bf16)
            c_ref[h] = c + jnp.broadcast_to(r[:, 0:1], (TQ, LANES))

    def stage_ab1(hp, qi, t, slot):
        q = q_ref[hp, 0, pl.ds(pl.multiple_of(qi * TQ, TQ), TQ), :]
        z = lax.dot_general(q, per_head_rows(k_ref, hp, t), (((1,), (1,)), ((), ())),
                            preferred_element_type=f32)
        kind = jnp.maximum(t - qi * KT_PER_QT + 1, 0)
        for b in range(BLK_PER_TILE):
            pen = pen_ref[kind, b]
            for h in heads:
                blk = h * BLK_PER_TILE + b
                zm = z[:, blk * LANES:(blk + 1) * LANES] + pen
                z_ref[slot, blk] = zm
                sp = jnp.log(1.0 + jnp.exp2(jnp.abs(zm) * (-LOG2E))) + jnp.maximum(zm, 0.0)
                split_ref[blk] = sp.astype(bf16)

    def iteration(i, u):
        stage_c2(*item(i - PIPE_DEPTH))
        stage_b2c1(*item(i - 1), (u - 1) % ITER_PER_TRIP)
        stage_ab1(*item(i), u)

    def body(m, carry):
        for u in range(ITER_PER_TRIP):
            iteration(ITER_PER_TRIP * m + u, u)
        return carry

    n_iter = n_items + PIPE_DEPTH
    full_trips = n_iter // ITER_PER_TRIP
    lax.fori_loop(0, full_trips, body, 0)
    for u in range(n_iter % ITER_PER_TRIP):
        iteration(full_trips * ITER_PER_TRIP + u, u)


def _attention(q, k, v, tri):
    _, bsz, seq, _ = q.shape
    n_qt = seq // TQ
    items = [(hp, qi, t) for hp in range(N_PAIR) for qi in range(n_qt)
             for t in range((qi + 1) * KT_PER_QT - 1, -1, -1)]
    hp_tab, qt, kt = (jnp.asarray([it[n] for it in items], jnp.int32) for n in range(3))
    seq_spec = pl.BlockSpec((N_PAIR, 1, seq, LANES), lambda b, *_: (0, b, 0, 0))
    tri_spec = pl.BlockSpec((TK, TK), lambda b, *_: (0, 0), pipeline_mode=pl.Buffered(1))
    return pl.pallas_call(
        _attn_kernel,
        out_shape=jax.ShapeDtypeStruct((N_PAIR, bsz, seq, LANES), f32),
        grid_spec=pltpu.PrefetchScalarGridSpec(
            num_scalar_prefetch=3,
            grid=(bsz,),
            in_specs=[seq_spec, seq_spec, seq_spec, tri_spec],
            out_specs=seq_spec,
            scratch_shapes=[
                pltpu.VMEM((HEADS_PER_STEP, TQ, LANES), f32),
                pltpu.VMEM((ITER_PER_TRIP, N_BLK, TQ, LANES), f32),
                pltpu.VMEM((N_BLK, TQ, LANES), bf16),
                pltpu.VMEM((N_BLK, TQ, LANES), bf16),
                pltpu.VMEM((KT_PER_QT + 1, BLK_PER_TILE, TQ, LANES), f32),
            ]),
        compiler_params=pltpu.CompilerParams(
            dimension_semantics=("parallel",), vmem_limit_bytes=VMEM_LIMIT),
        name="stickbreak",
    )(hp_tab, qt, kt, q, k, v, tri)


def _ffn_out_kernel(x_ref, yr_ref, ya_ref, ga_ref, wo_ref, g2_ref, wg_ref, wu_ref, wd_ref,
                    o_ref, h_ref):
    ya = jnp.concatenate([ya_ref[p] for p in range(N_PAIR)], axis=1)
    ya = _rms(ya, ga_ref[...]).astype(bf16)
    x2 = (x_ref[...] + _dot(yr_ref[...], wo_ref[0:D_RNN, :])
          + _dot(ya, wo_ref[D_RNN:D_RNN + D_ATT, :]))
    o_ref[...] = _ffn_tile(x2, g2_ref, wg_ref, wu_ref, wd_ref, h_ref)


def _ffn_out(x2d, yr, ya, ga, w_out, g2, wg, wu, wd):
    n_tok = x2d.shape[0]
    return pl.pallas_call(
        _ffn_out_kernel,
        out_shape=jax.ShapeDtypeStruct((n_tok, D_MODEL), f32),
        grid=(n_tok // TM,),
        in_specs=[_tok_spec(D_MODEL), _tok_spec(D_RNN), _pair_spec(), _const_spec((1, D_ATT)),
                  _const_spec((D_RNN + D_ATT, D_MODEL))] + _ffn_specs(),
        out_specs=_tok_spec(D_MODEL),
        scratch_shapes=[pltpu.VMEM((TM, D_FF), bf16)],
        compiler_params=pltpu.CompilerParams(
            dimension_semantics=("parallel",), vmem_limit_bytes=VMEM_LIMIT),
        name="ffn_out",
    )(x2d, yr, ya, ga, w_out, g2, wg, wu, wd)


def _block_diag(w):
    eye = jnp.eye(RNN_BLOCKS, dtype=w.dtype)
    return jnp.einsum("ncd,nm->ncmd", w, eye).reshape(D_RNN, D_RNN)


def _scan_matrix():
    j = jnp.arange(TK)
    return -(j[:, None] >= j[None, :]).astype(bf16)


def kernel(x, ffn1_norm, ffn1_w_gate, ffn1_w_up, ffn1_w_down, mix_norm, w_in, conv_w, conv_b,
           rg_w_a, rg_b_a, rg_w_x, rg_b_x, rg_lambda, q_norm, k_norm, rnn_out_norm,
           attn_out_norm, w_out, ffn2_norm, ffn2_w_gate, ffn2_w_up, ffn2_w_down):
    bsz, seq, _ = x.shape
    depth = ffn1_norm.shape[0]
    row_head = (jnp.arange(2 * LANES) % LANES) // HEAD_DIM
    seg = (row_head[:, None] == (jnp.arange(LANES) // HEAD_DIM)[None, :]).astype(bf16)
    tri = _scan_matrix()
    x2d = x.reshape(bsz * seq, D_MODEL)
    for l in range(depth):
        x2d, xr, gate, q, k, v = _ffn_in(
            x2d, ffn1_norm[l][None], ffn1_w_gate[l].astype(bf16), ffn1_w_up[l].astype(bf16),
            ffn1_w_down[l].astype(bf16), mix_norm[l][None], w_in[l].astype(bf16),
            jnp.tile(q_norm[l], N_HEADS)[None], jnp.tile(k_norm[l], N_HEADS)[None], seg)
        y_rnn = _rglru(
            xr.reshape(bsz, seq, D_RNN), gate.reshape(bsz, seq, D_RNN),
            conv_w[l], conv_b[l][None], _block_diag(rg_w_a[l]).astype(bf16), rg_b_a[l][None],
            _block_diag(rg_w_x[l]).astype(bf16), rg_b_x[l][None], rg_lambda[l][None],
            rnn_out_norm[l][None])
        pairs = (N_PAIR, bsz, seq, LANES)
        y_att = _attention(q.reshape(pairs), k.reshape(pairs), v.reshape(pairs), tri)
        x2d = _ffn_out(x2d, y_rnn.reshape(bsz * seq, D_RNN), y_att.reshape(N_PAIR, bsz * seq, LANES),
                       attn_out_norm[l][None], w_out[l].astype(bf16), ffn2_norm[l][None],
                       ffn2_w_gate[l].astype(bf16), ffn2_w_up[l].astype(bf16),
                       ffn2_w_down[l].astype(bf16))
    return x2d.reshape(bsz, seq, D_MODEL)
```

```python
import math

import jax
import jax.numpy as jnp
from jax import lax
from jax.experimental import pallas as pl
from jax.experimental.pallas import tpu as pltpu

D_MODEL = 1024
D_RNN = 512
RNN_BLOCKS = 8
RNN_BW = D_RNN // RNN_BLOCKS
CONV_W = 4
RG_C = 8.0
D_ATT = 512
HEAD_DIM = 64
N_HEADS = D_ATT // HEAD_DIM
D_FF = 2816
N_IN = 2 * D_RNN + 3 * D_ATT
EPS = 1e-6

LANES = 128
SUBLANES = 8
HEADS_PER_STEP = LANES // HEAD_DIM
N_PAIR = D_ATT // LANES
VMEM_LIMIT = 56 * 1024 * 1024

TM = 512
FF_CHUNK = 256
TS = 512
TQ = 512
TK = 256

f32 = jnp.float32
bf16 = jnp.bfloat16


def _rms(xf, g):
    r = lax.rsqrt(jnp.mean(xf * xf, axis=-1, keepdims=True) + EPS)
    return xf * r * g


def _dot(a, b):
    return jnp.dot(a, b, preferred_element_type=f32)


def _split_bf16(x):
    hi = x.astype(bf16)
    lo = (x - hi.astype(f32)).astype(bf16)
    return hi, lo


def _const_spec(shape):
    nd = len(shape)
    return pl.BlockSpec(shape, lambda *_: (0,) * nd, pipeline_mode=pl.Buffered(1))


def _ffn_tile(x, g_ref, wg_ref, wu_ref, wd_ref, h_ref):
    xn = _rms(x, g_ref[...]).astype(bf16)
    for c in range(0, D_FF, FF_CHUNK):
        gate = _dot(xn, wg_ref[:, c:c + FF_CHUNK])
        up = _dot(xn, wu_ref[:, c:c + FF_CHUNK])
        h_ref[:, c:c + FF_CHUNK] = (jax.nn.silu(gate) * up).astype(bf16)
    return x + 0.5 * _dot(h_ref[...], wd_ref[...])


def _ffn_specs():
    return [_const_spec((1, D_MODEL)), _const_spec((D_MODEL, D_FF)),
            _const_spec((D_MODEL, D_FF)), _const_spec((D_FF, D_MODEL))]


def _tok_spec(d):
    return pl.BlockSpec((TM, d), lambda i: (i, 0))


def _pair_spec():
    return pl.BlockSpec((N_PAIR, TM, LANES), lambda i: (0, i, 0))


def _head_norm(t, gain_tiled, seg_ref):
    hi, lo = _split_bf16(t * t)
    groups = []
    for g in range(0, t.shape[1], LANES):
        both = jnp.concatenate([hi[:, g:g + LANES], lo[:, g:g + LANES]], axis=1)
        groups.append(_dot(both, seg_ref[...]))
    ss = jnp.concatenate(groups, axis=1)
    r = lax.rsqrt(ss * (1.0 / HEAD_DIM) + EPS)
    return t * r * gain_tiled


def _ffn_in_kernel(x_ref, g1_ref, wg_ref, wu_ref, wd_ref, g_ref, w_ref, qg_ref, kg_ref, seg_ref,
                   o_ref, xr_ref, gate_ref, q_ref, k_ref, v_ref, h_ref):
    x1 = _ffn_tile(x_ref[...], g1_ref, wg_ref, wu_ref, wd_ref, h_ref)
    o_ref[...] = x1
    h = _rms(x1, g_ref[...]).astype(bf16)
    xr_ref[...] = _dot(h, w_ref[:, 0:D_RNN])
    gate_ref[...] = _dot(h, w_ref[:, D_RNN:2 * D_RNN])
    o = 2 * D_RNN
    q = _dot(h, w_ref[:, o:o + D_ATT])
    k = _dot(h, w_ref[:, o + D_ATT:o + 2 * D_ATT])
    v = _dot(h, w_ref[:, o + 2 * D_ATT:o + 3 * D_ATT])
    scale = 1.0 / math.sqrt(HEAD_DIM)
    qn = (_head_norm(q, qg_ref[...], seg_ref) * scale).astype(bf16)
    kn = _head_norm(k, kg_ref[...], seg_ref).astype(bf16)
    vb = v.astype(bf16)
    for p in range(N_PAIR):
        lanes = slice(p * LANES, (p + 1) * LANES)
        q_ref[p] = qn[:, lanes]
        k_ref[p] = kn[:, lanes]
        v_ref[p] = vb[:, lanes]


def _ffn_in(x2d, g1, wg, wu, wd, g, w_in, qg, kg, seg):
    n_tok = x2d.shape[0]
    return pl.pallas_call(
        _ffn_in_kernel,
        out_shape=(jax.ShapeDtypeStruct((n_tok, D_MODEL), f32),
                   jax.ShapeDtypeStruct((n_tok, D_RNN), f32),
                   jax.ShapeDtypeStruct((n_tok, D_RNN), f32),
                   jax.ShapeDtypeStruct((N_PAIR, n_tok, LANES), bf16),
                   jax.ShapeDtypeStruct((N_PAIR, n_tok, LANES), bf16),
                   jax.ShapeDtypeStruct((N_PAIR, n_tok, LANES), bf16)),
        grid=(n_tok // TM,),
        in_specs=[_tok_spec(D_MODEL)] + _ffn_specs() + [
            _const_spec((1, D_MODEL)), _const_spec((D_MODEL, N_IN)),
            _const_spec((1, D_ATT)), _const_spec((1, D_ATT)), _const_spec((2 * LANES, LANES))],
        out_specs=(_tok_spec(D_MODEL), _tok_spec(D_RNN), _tok_spec(D_RNN),
                   _pair_spec(), _pair_spec(), _pair_spec()),
        scratch_shapes=[pltpu.VMEM((TM, D_FF), bf16)],
        compiler_params=pltpu.CompilerParams(
            dimension_semantics=("parallel",), vmem_limit_bytes=VMEM_LIMIT),
        name="ffn_in",
    )(x2d, g1, wg, wu, wd, g, w_in, qg, kg, seg)


HIST = 8


def _rglru_kernel(xr_ref, gate_ref, cw_ref, cb_ref, wa_ref, ba_ref, wx_ref, bx_ref,
                  lam_ref, g_ref, o_ref, xbuf_ref, hc_ref):
    @pl.when(pl.program_id(1) == 0)
    def _():
        xbuf_ref[0:HIST, :] = jnp.zeros((HIST, D_RNN), f32)
        hc_ref[...] = jnp.zeros((1, D_RNN), f32)

    x = xr_ref[0]
    xbuf_ref[HIST:HIST + TS, :] = x
    xc = cb_ref[...] + cw_ref[CONV_W - 1:CONV_W, :] * x
    for j in range(CONV_W - 1):
        off = HIST - (CONV_W - 1) + j
        xc = xc + cw_ref[j:j + 1, :] * xbuf_ref[off:off + TS, :]
    xbuf_ref[0:HIST, :] = x[TS - HIST:TS, :]

    xcb = xc.astype(bf16)
    r = jax.nn.sigmoid(_dot(xcb, wa_ref[...]) + ba_ref[...])
    i = jax.nn.sigmoid(_dot(xcb, wx_ref[...]) + bx_ref[...])
    lam = lam_ref[...]
    log_sig_lam = jnp.minimum(lam, 0.0) - jnp.log1p(jnp.exp(-jnp.abs(lam)))
    log_a = RG_C * r * log_sig_lam
    a = jnp.exp(log_a)
    th = jnp.tanh(log_a)
    mult = jnp.sqrt(-2.0 * th / (1.0 - th))
    b = mult * (i * xc)

    n_grp = TS // SUBLANES
    a = a.reshape(n_grp, SUBLANES, D_RNN)
    b = b.reshape(n_grp, SUBLANES, D_RNN)
    row = lax.broadcasted_iota(jnp.int32, (n_grp, SUBLANES, D_RNN), 1)
    d = 1
    while d < SUBLANES:
        a_sh = pltpu.roll(a, d, axis=1)
        b_sh = pltpu.roll(b, d, axis=1)
        m = row >= d
        b = jnp.where(m, a * b_sh, 0.0) + b
        a = jnp.where(m, a * a_sh, a)
        d *= 2
    carry = jnp.broadcast_to(hc_ref[...], (SUBLANES, D_RNN))
    groups = []
    for g in range(n_grp):
        h_g = a[g] * carry + b[g]
        groups.append(h_g)
        carry = jnp.broadcast_to(h_g[SUBLANES - 1:SUBLANES, :], (SUBLANES, D_RNN))
    h = jnp.concatenate(groups, axis=0)
    hc_ref[...] = groups[-1][SUBLANES - 1:SUBLANES, :]

    y = h * jax.nn.gelu(gate_ref[0])
    o_ref[0] = _rms(y, g_ref[...]).astype(bf16)


def _rglru(xr, gate, cw, cb, wa, ba, wx, bx, lam, g):
    bsz, seq, _ = xr.shape
    seq_spec = pl.BlockSpec((1, TS, D_RNN), lambda b, s: (b, s, 0))
    vec = _const_spec((1, D_RNN))
    return pl.pallas_call(
        _rglru_kernel,
        out_shape=jax.ShapeDtypeStruct((bsz, seq, D_RNN), bf16),
        grid=(bsz, seq // TS),
        in_specs=[seq_spec, seq_spec, _const_spec((CONV_W, D_RNN)), vec,
                  _const_spec((D_RNN, D_RNN)), vec, _const_spec((D_RNN, D_RNN)), vec, vec, vec],
        out_specs=seq_spec,
        scratch_shapes=[pltpu.VMEM((HIST + TS, D_RNN), f32), pltpu.VMEM((1, D_RNN), f32)],
        compiler_params=pltpu.CompilerParams(
            dimension_semantics=("parallel", "arbitrary"), vmem_limit_bytes=VMEM_LIMIT),
        name="rglru",
    )(xr, gate, cw, cb, wa, ba, wx, bx, lam, g)


PIPE_DEPTH = 2
BLK_PER_TILE = TK // LANES
KT_PER_QT = TQ // TK
N_BLK = HEADS_PER_STEP * BLK_PER_TILE
ITER_PER_TRIP = 4
LOG2E = 1.4426950408889634


MASKED = -1e30


def _attn_kernel(hp_ref, qt_ref, kt_ref, q_ref, k_ref, v_ref, tri_ref, o_ref,
                 c_ref, z_ref, split_ref, w_ref, pen_ref):
    n_items = qt_ref.shape[0]
    heads = range(HEADS_PER_STEP)
    for ref in (o_ref, c_ref, split_ref, w_ref):
        ref[...] = jnp.zeros_like(ref)
    z_ref[...] = jnp.full(z_ref.shape, MASKED, f32)
    pen_ref[0] = jnp.zeros_like(pen_ref[0])
    for j in range(KT_PER_QT):
        for b in range(BLK_PER_TILE):
            key = lax.broadcasted_iota(jnp.int32, (TQ, LANES), 1) + (j * TK + b * LANES)
            query = lax.broadcasted_iota(jnp.int32, (TQ, LANES), 0)
            pen_ref[j + 1, b] = jnp.where(key < query, 0.0, MASKED)
    lane_head = lax.broadcasted_iota(jnp.int32, (TK, LANES), 1) // HEAD_DIM

    def item(j):
        j = jnp.clip(j, 0, n_items - 1)
        return hp_ref[j], qt_ref[j], kt_ref[j]

    def diag_tile(qi):
        return (qi + 1) * KT_PER_QT - 1

    def per_head_rows(ref, hp, t):
        x = ref[hp, 0, pl.ds(pl.multiple_of(t * TK, TK), TK), :]
        return jnp.concatenate([jnp.where(lane_head == h, x, jnp.zeros_like(x)) for h in heads], axis=0)

    def stage_c2(hp, qi, t):
        rows = pl.ds(pl.multiple_of(qi * TQ, TQ), TQ)
        w = jnp.concatenate([w_ref[blk] for blk in range(N_BLK)], axis=1)
        o_ref[hp, 0, rows, :] += _dot(w, per_head_rows(v_ref, hp, t))

    def stage_b2c1(hp, qi, t, slot):
        for h in heads:
            first = h * BLK_PER_TILE
            parts = [split_ref[blk] for blk in range(first, first + BLK_PER_TILE)]
            r = _dot(jnp.concatenate(parts, axis=1), tri_ref[...])
            c = jnp.where(t == diag_tile(qi), 0.0, c_ref[h])
            for b in range(BLK_PER_TILE):
                s = z_ref[slot, first + b] + r[:, b * LANES:(b + 1) * LANES] + c
                w_ref[first + b] = jnp.exp(s).astype(bf16)
            c_ref[h] = c + jnp.broadcast_to(r[:, 0:1], (TQ, LANES))

    def stage_ab1(hp, qi, t, slot):
        q = q_ref[hp, 0, pl.ds(pl.multiple_of(qi * TQ, TQ), TQ), :]
        z = lax.dot_general(q, per_head_rows(k_ref, hp, t), (((1,), (1,)), ((), ())),
                            preferred_element_type=f32)
        kind = jnp.maximum(t - qi * KT_PER_QT + 1, 0)
        for b in range(BLK_PER_TILE):
            pen = pen_ref[kind, b]
            for h in heads:
                blk = h * BLK_PER_TILE + b
                zm = z[:, blk * LANES:(blk + 1) * LANES] + pen
                z_ref[slot, blk] = zm
                sp = jnp.log(1.0 + jnp.exp2(jnp.abs(zm) * (-LOG2E))) + jnp.maximum(zm, 0.0)
                split_ref[blk] = sp.astype(bf16)

    def iteration(i, u):
        stage_c2(*item(i - PIPE_DEPTH))
        stage_b2c1(*item(i - 1), (u - 1) % ITER_PER_TRIP)
        stage_ab1(*item(i), u)

    def body(m, carry):
        for u in range(ITER_PER_TRIP):
            iteration(ITER_PER_TRIP * m + u, u)
        return carry

    n_iter = n_items + PIPE_DEPTH
    full_trips = n_iter // ITER_PER_TRIP
    lax.fori_loop(0, full_trips, body, 0)
    for u in range(n_iter % ITER_PER_TRIP):
        iteration(full_trips * ITER_PER_TRIP + u, u)


def _attention(q, k, v, tri):
    _, bsz, seq, _ = q.shape
    n_qt = seq // TQ
    items = [(hp, qi, t) for hp in range(N_PAIR) for qi in range(n_qt)
             for t in range((qi + 1) * KT_PER_QT - 1, -1, -1)]
    hp_tab, qt, kt = (jnp.asarray([it[n] for it in items], jnp.int32) for n in range(3))
    seq_spec = pl.BlockSpec((N_PAIR, 1, seq, LANES), lambda b, *_: (0, b, 0, 0))
    tri_spec = pl.BlockSpec((TK, TK), lambda b, *_: (0, 0), pipeline_mode=pl.Buffered(1))
    return pl.pallas_call(
        _attn_kernel,
        out_shape=jax.ShapeDtypeStruct((N_PAIR, bsz, seq, LANES), f32),
        grid_spec=pltpu.PrefetchScalarGridSpec(
            num_scalar_prefetch=3,
            grid=(bsz,),
            in_specs=[seq_spec, seq_spec, seq_spec, tri_spec],
            out_specs=seq_spec,
            scratch_shapes=[
                pltpu.VMEM((HEADS_PER_STEP, TQ, LANES), f32),
                pltpu.VMEM((ITER_PER_TRIP, N_BLK, TQ, LANES), f32),
                pltpu.VMEM((N_BLK, TQ, LANES), bf16),
                pltpu.VMEM((N_BLK, TQ, LANES), bf16),
                pltpu.VMEM((KT_PER_QT + 1, BLK_PER_TILE, TQ, LANES), f32),
            ]),
        compiler_params=pltpu.CompilerParams(
            dimension_semantics=("parallel",), vmem_limit_bytes=VMEM_LIMIT),
        name="stickbreak",
    )(hp_tab, qt, kt, q, k, v, tri)


def _ffn_out_kernel(x_ref, yr_ref, ya_ref, ga_ref, wo_ref, g2_ref, wg_ref, wu_ref, wd_ref,
                    o_ref, h_ref):
    ya = jnp.concatenate([ya_ref[p] for p in range(N_PAIR)], axis=1)
    ya = _rms(ya, ga_ref[...]).astype(bf16)
    x2 = (x_ref[...] + _dot(yr_ref[...], wo_ref[0:D_RNN, :])
          + _dot(ya, wo_ref[D_RNN:D_RNN + D_ATT, :]))
    o_ref[...] = _ffn_tile(x2, g2_ref, wg_ref, wu_ref, wd_ref, h_ref)


def _ffn_out(x2d, yr, ya, ga, w_out, g2, wg, wu, wd):
    n_tok = x2d.shape[0]
    return pl.pallas_call(
        _ffn_out_kernel,
        out_shape=jax.ShapeDtypeStruct((n_tok, D_MODEL), f32),
        grid=(n_tok // TM,),
        in_specs=[_tok_spec(D_MODEL), _tok_spec(D_RNN), _pair_spec(), _const_spec((1, D_ATT)),
                  _const_spec((D_RNN + D_ATT, D_MODEL))] + _ffn_specs(),
        out_specs=_tok_spec(D_MODEL),
        scratch_shapes=[pltpu.VMEM((TM, D_FF), bf16)],
        compiler_params=pltpu.CompilerParams(
            dimension_semantics=("parallel",), vmem_limit_bytes=VMEM_LIMIT),
        name="ffn_out",
    )(x2d, yr, ya, ga, w_out, g2, wg, wu, wd)


def _block_diag(w):
    eye = jnp.eye(RNN_BLOCKS, dtype=w.dtype)
    return jnp.einsum("ncd,nm->ncmd", w, eye).reshape(D_RNN, D_RNN)


def _scan_matrix():
    j = jnp.arange(TK)
    return -(j[:, None] >= j[None, :]).astype(bf16)


def kernel(x, ffn1_norm, ffn1_w_gate, ffn1_w_up, ffn1_w_down, mix_norm, w_in, conv_w, conv_b,
           rg_w_a, rg_b_a, rg_w_x, rg_b_x, rg_lambda, q_norm, k_norm, rnn_out_norm,
           attn_out_norm, w_out, ffn2_norm, ffn2_w_gate, ffn2_w_up, ffn2_w_down):
    bsz, seq, _ = x.shape
    depth = ffn1_norm.shape[0]
    row_head = (jnp.arange(2 * LANES) % LANES) // HEAD_DIM
    seg = (row_head[:, None] == (jnp.arange(LANES) // HEAD_DIM)[None, :]).astype(bf16)
    tri = _scan_matrix()
    x2d = x.reshape(bsz * seq, D_MODEL)
    for l in range(depth):
        x2d, xr, gate, q, k, v = _ffn_in(
            x2d, ffn1_norm[l][None], ffn1_w_gate[l].astype(bf16), ffn1_w_up[l].astype(bf16),
            ffn1_w_down[l].astype(bf16), mix_norm[l][None], w_in[l].astype(bf16),
            jnp.tile(q_norm[l], N_HEADS)[None], jnp.tile(k_norm[l], N_HEADS)[None], seg)
        y_rnn = _rglru(
            xr.reshape(bsz, seq, D_RNN), gate.reshape(bsz, seq, D_RNN),
            conv_w[l], conv_b[l][None], _block_diag(rg_w_a[l]).astype(bf16), rg_b_a[l][None],
            _block_diag(rg_w_x[l]).astype(bf16), rg_b_x[l][None], rg_lambda[l][None],
            rnn_out_norm[l][None])
        pairs = (N_PAIR, bsz, seq, LANES)
        y_att = _attention(q.reshape(pairs), k.reshape(pairs), v.reshape(pairs), tri)
        x2d = _ffn_out(x2d, y_rnn.reshape(bsz * seq, D_RNN), y_att.reshape(N_PAIR, bsz * seq, LANES),
                       attn_out_norm[l][None], w_out[l].astype(bf16), ffn2_norm[l][None],
                       ffn2_w_gate[l].astype(bf16), ffn2_w_up[l].astype(bf16),
                       ffn2_w_down[l].astype(bf16))
    return x2d.reshape(bsz, seq, D_MODEL)
```

```python
import math

import jax
import jax.numpy as jnp
from jax import lax
from jax.experimental import pallas as pl
from jax.experimental.pallas import tpu as pltpu

D_MODEL = 1024
D_RNN = 512
RNN_BLOCKS = 8
RNN_BW = D_RNN // RNN_BLOCKS
CONV_W = 4
RG_C = 8.0
D_ATT = 512
HEAD_DIM = 64
N_HEADS = D_ATT // HEAD_DIM
D_FF = 2816
N_IN = 2 * D_RNN + 3 * D_ATT
EPS = 1e-6

LANES = 128
SUBLANES = 8
HEADS_PER_STEP = LANES // HEAD_DIM
N_PAIR = D_ATT // LANES
VMEM_LIMIT = 56 * 1024 * 1024

TM = 512
FF_CHUNK = 256
TS = 512
TQ = 512
TK = 256

f32 = jnp.float32
bf16 = jnp.bfloat16


def _rms(xf, g):
    r = lax.rsqrt(jnp.mean(xf * xf, axis=-1, keepdims=True) + EPS)
    return xf * r * g


def _dot(a, b):
    return jnp.dot(a, b, preferred_element_type=f32)


def _split_bf16(x):
    hi = x.astype(bf16)
    lo = (x - hi.astype(f32)).astype(bf16)
    return hi, lo


def _const_spec(shape):
    nd = len(shape)
    return pl.BlockSpec(shape, lambda *_: (0,) * nd, pipeline_mode=pl.Buffered(1))


def _ffn_tile(x, g_ref, wg_ref, wu_ref, wd_ref, h_ref):
    xn = _rms(x, g_ref[...]).astype(bf16)
    for c in range(0, D_FF, FF_CHUNK):
        gate = _dot(xn, wg_ref[:, c:c + FF_CHUNK])
        up = _dot(xn, wu_ref[:, c:c + FF_CHUNK])
        h_ref[:, c:c + FF_CHUNK] = (jax.nn.silu(gate) * up).astype(bf16)
    return x + 0.5 * _dot(h_ref[...], wd_ref[...])


def _ffn_specs():
    return [_const_spec((1, D_MODEL)), _const_spec((D_MODEL, D_FF)),
            _const_spec((D_MODEL, D_FF)), _const_spec((D_FF, D_MODEL))]


def _tok_spec(d):
    return pl.BlockSpec((TM, d), lambda i: (i, 0))


def _pair_spec():
    return pl.BlockSpec((N_PAIR, TM, LANES), lambda i: (0, i, 0))


def _head_norm(t, gain_tiled, seg_ref):
    hi, lo = _split_bf16(t * t)
    groups = []
    for g in range(0, t.shape[1], LANES):
        both = jnp.concatenate([hi[:, g:g + LANES], lo[:, g:g + LANES]], axis=1)
        groups.append(_dot(both, seg_ref[...]))
    ss = jnp.concatenate(groups, axis=1)
    r = lax.rsqrt(ss * (1.0 / HEAD_DIM) + EPS)
    return t * r * gain_tiled


def _ffn_in_kernel(x_ref, g1_ref, wg_ref, wu_ref, wd_ref, g_ref, w_ref, qg_ref, kg_ref, seg_ref,
                   o_ref, xr_ref, gate_ref, q_ref, k_ref, v_ref, h_ref):
    x1 = _ffn_tile(x_ref[...], g1_ref, wg_ref, wu_ref, wd_ref, h_ref)
    o_ref[...] = x1
    h = _rms(x1, g_ref[...]).astype(bf16)
    xr_ref[...] = _dot(h, w_ref[:, 0:D_RNN])
    gate_ref[...] = _dot(h, w_ref[:, D_RNN:2 * D_RNN])
    o = 2 * D_RNN
    q = _dot(h, w_ref[:, o:o + D_ATT])
    k = _dot(h, w_ref[:, o + D_ATT:o + 2 * D_ATT])
    v = _dot(h, w_ref[:, o + 2 * D_ATT:o + 3 * D_ATT])
    scale = 1.0 / math.sqrt(HEAD_DIM)
    qn = (_head_norm(q, qg_ref[...], seg_ref) * scale).astype(bf16)
    kn = _head_norm(k, kg_ref[...], seg_ref).astype(bf16)
    vb = v.astype(bf16)
    for p in range(N_PAIR):
        lanes = slice(p * LANES, (p + 1) * LANES)
        q_ref[p] = qn[:, lanes]
        k_ref[p] = kn[:, lanes]
        v_ref[p] = vb[:, lanes]


def _ffn_in(x2d, g1, wg, wu, wd, g, w_in, qg, kg, seg):
    n_tok = x2d.shape[0]
    return pl.pallas_call(
        _ffn_in_kernel,
        out_shape=(jax.ShapeDtypeStruct((n_tok, D_MODEL), f32),
                   jax.ShapeDtypeStruct((n_tok, D_RNN), f32),
                   jax.ShapeDtypeStruct((n_tok, D_RNN), f32),
                   jax.ShapeDtypeStruct((N_PAIR, n_tok, LANES), bf16),
                   jax.ShapeDtypeStruct((N_PAIR, n_tok, LANES), bf16),
                   jax.ShapeDtypeStruct((N_PAIR, n_tok, LANES), bf16)),
        grid=(n_tok // TM,),
        in_specs=[_tok_spec(D_MODEL)] + _ffn_specs() + [
            _const_spec((1, D_MODEL)), _const_spec((D_MODEL, N_IN)),
            _const_spec((1, D_ATT)), _const_spec((1, D_ATT)), _const_spec((2 * LANES, LANES))],
        out_specs=(_tok_spec(D_MODEL), _tok_spec(D_RNN), _tok_spec(D_RNN),
                   _pair_spec(), _pair_spec(), _pair_spec()),
        scratch_shapes=[pltpu.VMEM((TM, D_FF), bf16)],
        compiler_params=pltpu.CompilerParams(
            dimension_semantics=("parallel",), vmem_limit_bytes=VMEM_LIMIT),
        name="ffn_in",
    )(x2d, g1, wg, wu, wd, g, w_in, qg, kg, seg)


HIST = 8


def _rglru_kernel(xr_ref, gate_ref, cw_ref, cb_ref, wa_ref, ba_ref, wx_ref, bx_ref,
                  lam_ref, g_ref, o_ref, xbuf_ref, hc_ref):
    @pl.when(pl.program_id(1) == 0)
    def _():
        xbuf_ref[0:HIST, :] = jnp.zeros((HIST, D_RNN), f32)
        hc_ref[...] = jnp.zeros((1, D_RNN), f32)

    x = xr_ref[0]
    xbuf_ref[HIST:HIST + TS, :] = x
    xc = cb_ref[...] + cw_ref[CONV_W - 1:CONV_W, :] * x
    for j in range(CONV_W - 1):
        off = HIST - (CONV_W - 1) + j
        xc = xc + cw_ref[j:j + 1, :] * xbuf_ref[off:off + TS, :]
    xbuf_ref[0:HIST, :] = x[TS - HIST:TS, :]

    xcb = xc.astype(bf16)
    r = jax.nn.sigmoid(_dot(xcb, wa_ref[...]) + ba_ref[...])
    i = jax.nn.sigmoid(_dot(xcb, wx_ref[...]) + bx_ref[...])
    lam = lam_ref[...]
    log_sig_lam = jnp.minimum(lam, 0.0) - jnp.log1p(jnp.exp(-jnp.abs(lam)))
    log_a = RG_C * r * log_sig_lam
    a = jnp.exp(log_a)
    th = jnp.tanh(log_a)
    mult = jnp.sqrt(-2.0 * th / (1.0 - th))
    b = mult * (i * xc)

    n_grp = TS // SUBLANES
    a = a.reshape(n_grp, SUBLANES, D_RNN)
    b = b.reshape(n_grp, SUBLANES, D_RNN)
    row = lax.broadcasted_iota(jnp.int32, (n_grp, SUBLANES, D_RNN), 1)
    d = 1
    while d < SUBLANES:
        a_sh = pltpu.roll(a, d, axis=1)
        b_sh = pltpu.roll(b, d, axis=1)
        m = row >= d
        b = jnp.where(m, a * b_sh, 0.0) + b
        a = jnp.where(m, a * a_sh, a)
        d *= 2
    carry = jnp.broadcast_to(hc_ref[...], (SUBLANES, D_RNN))
    groups = []
    for g in range(n_grp):
        h_g = a[g] * carry + b[g]
        groups.append(h_g)
        carry = jnp.broadcast_to(h_g[SUBLANES - 1:SUBLANES, :], (SUBLANES, D_RNN))
    h = jnp.concatenate(groups, axis=0)
    hc_ref[...] = groups[-1][SUBLANES - 1:SUBLANES, :]

    y = h * jax.nn.gelu(gate_ref[0])
    o_ref[0] = _rms(y, g_ref[...]).astype(bf16)


def _rglru(xr, gate, cw, cb, wa, ba, wx, bx, lam, g):
    bsz, seq, _ = xr.shape
    seq_spec = pl.BlockSpec((1, TS, D_RNN), lambda b, s: (b, s, 0))
    vec = _const_spec((1, D_RNN))
    return pl.pallas_call(
        _rglru_kernel,
        out_shape=jax.ShapeDtypeStruct((bsz, seq, D_RNN), bf16),
        grid=(bsz, seq // TS),
        in_specs=[seq_spec, seq_spec, _const_spec((CONV_W, D_RNN)), vec,
                  _const_spec((D_RNN, D_RNN)), vec, _const_spec((D_RNN, D_RNN)), vec, vec, vec],
        out_specs=seq_spec,
        scratch_shapes=[pltpu.VMEM((HIST + TS, D_RNN), f32), pltpu.VMEM((1, D_RNN), f32)],
        compiler_params=pltpu.CompilerParams(
            dimension_semantics=("parallel", "arbitrary"), vmem_limit_bytes=VMEM_LIMIT),
        name="rglru",
    )(xr, gate, cw, cb, wa, ba, wx, bx, lam, g)


PIPE_DEPTH = 2
BLK_PER_TILE = TK // LANES
KT_PER_QT = TQ // TK
N_BLK = HEADS_PER_STEP * BLK_PER_TILE
ITER_PER_TRIP = 8
Z_SLOTS = 2
LOG2E = 1.4426950408889634


MASKED = -1e30


def _attn_kernel(hp_ref, qt_ref, kt_ref, q_ref, k_ref, v_ref, tri_ref, o_ref,
                 c_ref, z_ref, split_ref, w_ref, pen_ref):
    n_items = qt_ref.shape[0]
    heads = range(HEADS_PER_STEP)
    for ref in (o_ref, c_ref, z_ref, split_ref, w_ref):
        ref[...] = jnp.zeros_like(ref)
    pen_ref[0] = jnp.zeros_like(pen_ref[0])
    for j in range(KT_PER_QT):
        for b in range(BLK_PER_TILE):
            key = lax.broadcasted_iota(jnp.int32, (TQ, LANES), 1) + (j * TK + b * LANES)
            query = lax.broadcasted_iota(jnp.int32, (TQ, LANES), 0)
            pen_ref[j + 1, b] = jnp.where(key < query, 0.0, MASKED)
    lane_head = lax.broadcasted_iota(jnp.int32, (TK, LANES), 1) // HEAD_DIM

    def item(j):
        j = jnp.clip(j, 0, n_items - 1)
        return hp_ref[j], qt_ref[j], kt_ref[j]

    def diag_tile(qi):
        return (qi + 1) * KT_PER_QT - 1

    def per_head_rows(ref, hp, t):
        x = ref[hp, 0, pl.ds(pl.multiple_of(t * TK, TK), TK), :]
        return jnp.concatenate([jnp.where(lane_head == h, x, jnp.zeros_like(x)) for h in heads], axis=0)

    def stage_c2(hp, qi, t):
        rows = pl.ds(pl.multiple_of(qi * TQ, TQ), TQ)
        acc = jnp.where(t == diag_tile(qi), 0.0, o_ref[hp, 0, rows, :])
        w = jnp.concatenate([w_ref[blk] for blk in range(N_BLK)], axis=1)
        o_ref[hp, 0, rows, :] = acc + _dot(w, per_head_rows(v_ref, hp, t))

    def stage_b2c1(hp, qi, t, slot):
        for h in heads:
            first = h * BLK_PER_TILE
            parts = [split_ref[blk] for blk in range(first, first + BLK_PER_TILE)]
            r = _dot(jnp.concatenate(parts, axis=1), tri_ref[...])
            c = jnp.where(t == diag_tile(qi), 0.0, c_ref[h])
            for b in range(BLK_PER_TILE):
                s = z_ref[slot, first + b] + r[:, b * LANES:(b + 1) * LANES] + c
                w_ref[first + b] = jnp.exp(s).astype(bf16)
            c_ref[h] = c + jnp.broadcast_to(r[:, 0:1], (TQ, LANES))

    def stage_ab1(hp, qi, t, slot):
        q = q_ref[hp, 0, pl.ds(pl.multiple_of(qi * TQ, TQ), TQ), :]
        z = lax.dot_general(q, per_head_rows(k_ref, hp, t), (((1,), (1,)), ((), ())),
                            preferred_element_type=f32)
        kind = jnp.maximum(t - qi * KT_PER_QT + 1, 0)
        for b in range(BLK_PER_TILE):
            pen = pen_ref[kind, b]
            for h in heads:
                blk = h * BLK_PER_TILE + b
                zm = z[:, blk * LANES:(blk + 1) * LANES] + pen
                z_ref[slot, blk] = zm
                sp = jnp.log(1.0 + jnp.exp2(jnp.abs(zm) * (-LOG2E))) + jnp.maximum(zm, 0.0)
                split_ref[blk] = sp.astype(bf16)

    def iteration(i, u):
        stage_c2(*item(i - PIPE_DEPTH))
        stage_b2c1(*item(i - 1), (u - 1) % Z_SLOTS)
        stage_ab1(*item(i), u % Z_SLOTS)

    def body(m, carry):
        for u in range(ITER_PER_TRIP):
            iteration(ITER_PER_TRIP * m + u, u)
        return carry

    n_iter = n_items + PIPE_DEPTH
    full_trips = n_iter // ITER_PER_TRIP
    lax.fori_loop(0, full_trips, body, 0)
    for u in range(n_iter % ITER_PER_TRIP):
        iteration(full_trips * ITER_PER_TRIP + u, u)


def _attention(q, k, v, tri):
    _, bsz, seq, _ = q.shape
    n_qt = seq // TQ
    items = [(hp, qi, t) for hp in range(N_PAIR) for qi in range(n_qt)
             for t in range((qi + 1) * KT_PER_QT - 1, -1, -1)]
    hp_tab, qt, kt = (jnp.asarray([it[n] for it in items], jnp.int32) for n in range(3))
    seq_spec = pl.BlockSpec((N_PAIR, 1, seq, LANES), lambda b, *_: (0, b, 0, 0))
    tri_spec = pl.BlockSpec((TK, TK), lambda b, *_: (0, 0), pipeline_mode=pl.Buffered(1))
    return pl.pallas_call(
        _attn_kernel,
        out_shape=jax.ShapeDtypeStruct((N_PAIR, bsz, seq, LANES), f32),
        grid_spec=pltpu.PrefetchScalarGridSpec(
            num_scalar_prefetch=3,
            grid=(bsz,),
            in_specs=[seq_spec, seq_spec, seq_spec, tri_spec],
            out_specs=seq_spec,
            scratch_shapes=[
                pltpu.VMEM((HEADS_PER_STEP, TQ, LANES), f32),
                pltpu.VMEM((Z_SLOTS, N_BLK, TQ, LANES), f32),
                pltpu.VMEM((N_BLK, TQ, LANES), bf16),
                pltpu.VMEM((N_BLK, TQ, LANES), bf16),
                pltpu.VMEM((KT_PER_QT + 1, BLK_PER_TILE, TQ, LANES), f32),
            ]),
        compiler_params=pltpu.CompilerParams(
            dimension_semantics=("parallel",), vmem_limit_bytes=VMEM_LIMIT),
        name="stickbreak",
    )(hp_tab, qt, kt, q, k, v, tri)


def _ffn_out_kernel(x_ref, yr_ref, ya_ref, ga_ref, wo_ref, g2_ref, wg_ref, wu_ref, wd_ref,
                    o_ref, h_ref):
    ya = jnp.concatenate([ya_ref[p] for p in range(N_PAIR)], axis=1)
    ya = _rms(ya, ga_ref[...]).astype(bf16)
    x2 = (x_ref[...] + _dot(yr_ref[...], wo_ref[0:D_RNN, :])
          + _dot(ya, wo_ref[D_RNN:D_RNN + D_ATT, :]))
    o_ref[...] = _ffn_tile(x2, g2_ref, wg_ref, wu_ref, wd_ref, h_ref)


def _ffn_out(x2d, yr, ya, ga, w_out, g2, wg, wu, wd):
    n_tok = x2d.shape[0]
    return pl.pallas_call(
        _ffn_out_kernel,
        out_shape=jax.ShapeDtypeStruct((n_tok, D_MODEL), f32),
        grid=(n_tok // TM,),
        in_specs=[_tok_spec(D_MODEL), _tok_spec(D_RNN), _pair_spec(), _const_spec((1, D_ATT)),
                  _const_spec((D_RNN + D_ATT, D_MODEL))] + _ffn_specs(),
        out_specs=_tok_spec(D_MODEL),
        scratch_shapes=[pltpu.VMEM((TM, D_FF), bf16)],
        compiler_params=pltpu.CompilerParams(
            dimension_semantics=("parallel",), vmem_limit_bytes=VMEM_LIMIT),
        name="ffn_out",
    )(x2d, yr, ya, ga, w_out, g2, wg, wu, wd)


def _block_diag(w):
    eye = jnp.eye(RNN_BLOCKS, dtype=w.dtype)
    return jnp.einsum("ncd,nm->ncmd", w, eye).reshape(D_RNN, D_RNN)


def _scan_matrix():
    j = jnp.arange(TK)
    return -(j[:, None] >= j[None, :]).astype(bf16)


def kernel(x, ffn1_norm, ffn1_w_gate, ffn1_w_up, ffn1_w_down, mix_norm, w_in, conv_w, conv_b,
           rg_w_a, rg_b_a, rg_w_x, rg_b_x, rg_lambda, q_norm, k_norm, rnn_out_norm,
           attn_out_norm, w_out, ffn2_norm, ffn2_w_gate, ffn2_w_up, ffn2_w_down):
    bsz, seq, _ = x.shape
    depth = ffn1_norm.shape[0]
    row_head = (jnp.arange(2 * LANES) % LANES) // HEAD_DIM
    seg = (row_head[:, None] == (jnp.arange(LANES) // HEAD_DIM)[None, :]).astype(bf16)
    tri = _scan_matrix()
    x2d = x.reshape(bsz * seq, D_MODEL)
    for l in range(depth):
        x2d, xr, gate, q, k, v = _ffn_in(
            x2d, ffn1_norm[l][None], ffn1_w_gate[l].astype(bf16), ffn1_w_up[l].astype(bf16),
            ffn1_w_down[l].astype(bf16), mix_norm[l][None], w_in[l].astype(bf16),
            jnp.tile(q_norm[l], N_HEADS)[None], jnp.tile(k_norm[l], N_HEADS)[None], seg)
        y_rnn = _rglru(
            xr.reshape(bsz, seq, D_RNN), gate.reshape(bsz, seq, D_RNN),
            conv_w[l], conv_b[l][None], _block_diag(rg_w_a[l]).astype(bf16), rg_b_a[l][None],
            _block_diag(rg_w_x[l]).astype(bf16), rg_b_x[l][None], rg_lambda[l][None],
            rnn_out_norm[l][None])
        pairs = (N_PAIR, bsz, seq, LANES)
        y_att = _attention(q.reshape(pairs), k.reshape(pairs), v.reshape(pairs), tri)
        x2d = _ffn_out(x2d, y_rnn.reshape(bsz * seq, D_RNN), y_att.reshape(N_PAIR, bsz * seq, LANES),
                       attn_out_norm[l][None], w_out[l].astype(bf16), ffn2_norm[l][None],
                       ffn2_w_gate[l].astype(bf16), ffn2_w_up[l].astype(bf16),
                       ffn2_w_down[l].astype(bf16))
    return x2d.reshape(bsz, seq, D_MODEL)
```

```python
import math

import jax
import jax.numpy as jnp
from jax import lax
from jax.experimental import pallas as pl
from jax.experimental.pallas import tpu as pltpu

D_MODEL = 1024
D_RNN = 512
RNN_BLOCKS = 8
RNN_BW = D_RNN // RNN_BLOCKS
CONV_W = 4
RG_C = 8.0
D_ATT = 512
HEAD_DIM = 64
N_HEADS = D_ATT // HEAD_DIM
D_FF = 2816
N_IN = 2 * D_RNN + 3 * D_ATT
EPS = 1e-6

LANES = 128
SUBLANES = 8
HEADS_PER_STEP = LANES // HEAD_DIM
N_PAIR = D_ATT // LANES
VMEM_LIMIT = 56 * 1024 * 1024

TM = 512
FF_CHUNK = 256
TS = 512
TQ = 512
TK = 256

f32 = jnp.float32
bf16 = jnp.bfloat16


def _rms(xf, g):
    r = lax.rsqrt(jnp.mean(xf * xf, axis=-1, keepdims=True) + EPS)
    return xf * r * g


def _dot(a, b):
    return jnp.dot(a, b, preferred_element_type=f32)


def _split_bf16(x):
    hi = x.astype(bf16)
    lo = (x - hi.astype(f32)).astype(bf16)
    return hi, lo


def _const_spec(shape):
    nd = len(shape)
    return pl.BlockSpec(shape, lambda *_: (0,) * nd, pipeline_mode=pl.Buffered(1))


def _ffn_tile(x, g_ref, wg_ref, wu_ref, wd_ref, h_ref):
    xn = _rms(x, g_ref[...]).astype(bf16)
    for c in range(0, D_FF, FF_CHUNK):
        gate = _dot(xn, wg_ref[:, c:c + FF_CHUNK])
        up = _dot(xn, wu_ref[:, c:c + FF_CHUNK])
        h_ref[:, c:c + FF_CHUNK] = (jax.nn.silu(gate) * up).astype(bf16)
    return x + 0.5 * _dot(h_ref[...], wd_ref[...])


def _ffn_specs():
    return [_const_spec((1, D_MODEL)), _const_spec((D_MODEL, D_FF)),
            _const_spec((D_MODEL, D_FF)), _const_spec((D_FF, D_MODEL))]


def _tok_spec(d):
    return pl.BlockSpec((TM, d), lambda i: (i, 0))


def _pair_spec():
    return pl.BlockSpec((N_PAIR, TM, LANES), lambda i: (0, i, 0))


def _head_norm(t, gain_tiled, seg_ref):
    hi, lo = _split_bf16(t * t)
    groups = []
    for g in range(0, t.shape[1], LANES):
        both = jnp.concatenate([hi[:, g:g + LANES], lo[:, g:g + LANES]], axis=1)
        groups.append(_dot(both, seg_ref[...]))
    ss = jnp.concatenate(groups, axis=1)
    r = lax.rsqrt(ss * (1.0 / HEAD_DIM) + EPS)
    return t * r * gain_tiled


def _ffn_in_kernel(x_ref, g1_ref, wg_ref, wu_ref, wd_ref, g_ref, w_ref, qg_ref, kg_ref, seg_ref,
                   o_ref, xr_ref, gate_ref, q_ref, k_ref, v_ref, h_ref):
    x1 = _ffn_tile(x_ref[...], g1_ref, wg_ref, wu_ref, wd_ref, h_ref)
    o_ref[...] = x1
    h = _rms(x1, g_ref[...]).astype(bf16)
    xr_ref[...] = _dot(h, w_ref[:, 0:D_RNN])
    gate_ref[...] = _dot(h, w_ref[:, D_RNN:2 * D_RNN])
    o = 2 * D_RNN
    q = _dot(h, w_ref[:, o:o + D_ATT])
    k = _dot(h, w_ref[:, o + D_ATT:o + 2 * D_ATT])
    v = _dot(h, w_ref[:, o + 2 * D_ATT:o + 3 * D_ATT])
    scale = 1.0 / math.sqrt(HEAD_DIM)
    qn = (_head_norm(q, qg_ref[...], seg_ref) * scale).astype(bf16)
    kn = _head_norm(k, kg_ref[...], seg_ref).astype(bf16)
    vb = v.astype(bf16)
    for p in range(N_PAIR):
        lanes = slice(p * LANES, (p + 1) * LANES)
        q_ref[p] = qn[:, lanes]
        k_ref[p] = kn[:, lanes]
        v_ref[p] = vb[:, lanes]


def _ffn_in(x2d, g1, wg, wu, wd, g, w_in, qg, kg, seg):
    n_tok = x2d.shape[0]
    return pl.pallas_call(
        _ffn_in_kernel,
        out_shape=(jax.ShapeDtypeStruct((n_tok, D_MODEL), f32),
                   jax.ShapeDtypeStruct((n_tok, D_RNN), f32),
                   jax.ShapeDtypeStruct((n_tok, D_RNN), f32),
                   jax.ShapeDtypeStruct((N_PAIR, n_tok, LANES), bf16),
                   jax.ShapeDtypeStruct((N_PAIR, n_tok, LANES), bf16),
                   jax.ShapeDtypeStruct((N_PAIR, n_tok, LANES), bf16)),
        grid=(n_tok // TM,),
        in_specs=[_tok_spec(D_MODEL)] + _ffn_specs() + [
            _const_spec((1, D_MODEL)), _const_spec((D_MODEL, N_IN)),
            _const_spec((1, D_ATT)), _const_spec((1, D_ATT)), _const_spec((2 * LANES, LANES))],
        out_specs=(_tok_spec(D_MODEL), _tok_spec(D_RNN), _tok_spec(D_RNN),
                   _pair_spec(), _pair_spec(), _pair_spec()),
        scratch_shapes=[pltpu.VMEM((TM, D_FF), bf16)],
        compiler_params=pltpu.CompilerParams(
            dimension_semantics=("parallel",), vmem_limit_bytes=VMEM_LIMIT),
        name="ffn_in",
    )(x2d, g1, wg, wu, wd, g, w_in, qg, kg, seg)


HIST = 8


def _rglru_kernel(xr_ref, gate_ref, cw_ref, cb_ref, wa_ref, ba_ref, wx_ref, bx_ref,
                  lam_ref, g_ref, o_ref, xbuf_ref, hc_ref):
    @pl.when(pl.program_id(1) == 0)
    def _():
        xbuf_ref[0:HIST, :] = jnp.zeros((HIST, D_RNN), f32)
        hc_ref[...] = jnp.zeros((1, D_RNN), f32)

    x = xr_ref[0]
    xbuf_ref[HIST:HIST + TS, :] = x
    xc = cb_ref[...] + cw_ref[CONV_W - 1:CONV_W, :] * x
    for j in range(CONV_W - 1):
        off = HIST - (CONV_W - 1) + j
        xc = xc + cw_ref[j:j + 1, :] * xbuf_ref[off:off + TS, :]
    xbuf_ref[0:HIST, :] = x[TS - HIST:TS, :]

    xcb = xc.astype(bf16)
    r = jax.nn.sigmoid(_dot(xcb, wa_ref[...]) + ba_ref[...])
    i = jax.nn.sigmoid(_dot(xcb, wx_ref[...]) + bx_ref[...])
    lam = lam_ref[...]
    log_sig_lam = jnp.minimum(lam, 0.0) - jnp.log1p(jnp.exp(-jnp.abs(lam)))
    log_a = RG_C * r * log_sig_lam
    a = jnp.exp(log_a)
    th = jnp.tanh(log_a)
    mult = jnp.sqrt(-2.0 * th / (1.0 - th))
    b = mult * (i * xc)

    n_grp = TS // SUBLANES
    a = a.reshape(n_grp, SUBLANES, D_RNN)
    b = b.reshape(n_grp, SUBLANES, D_RNN)
    row = lax.broadcasted_iota(jnp.int32, (n_grp, SUBLANES, D_RNN), 1)
    d = 1
    while d < SUBLANES:
        a_sh = pltpu.roll(a, d, axis=1)
        b_sh = pltpu.roll(b, d, axis=1)
        m = row >= d
        b = jnp.where(m, a * b_sh, 0.0) + b
        a = jnp.where(m, a * a_sh, a)
        d *= 2
    carry = jnp.broadcast_to(hc_ref[...], (SUBLANES, D_RNN))
    groups = []
    for g in range(n_grp):
        h_g = a[g] * carry + b[g]
        groups.append(h_g)
        carry = jnp.broadcast_to(h_g[SUBLANES - 1:SUBLANES, :], (SUBLANES, D_RNN))
    h = jnp.concatenate(groups, axis=0)
    hc_ref[...] = groups[-1][SUBLANES - 1:SUBLANES, :]

    y = h * jax.nn.gelu(gate_ref[0])
    o_ref[0] = _rms(y, g_ref[...]).astype(bf16)


def _rglru(xr, gate, cw, cb, wa, ba, wx, bx, lam, g):
    bsz, seq, _ = xr.shape
    seq_spec = pl.BlockSpec((1, TS, D_RNN), lambda b, s: (b, s, 0))
    vec = _const_spec((1, D_RNN))
    return pl.pallas_call(
        _rglru_kernel,
        out_shape=jax.ShapeDtypeStruct((bsz, seq, D_RNN), bf16),
        grid=(bsz, seq // TS),
        in_specs=[seq_spec, seq_spec, _const_spec((CONV_W, D_RNN)), vec,
                  _const_spec((D_RNN, D_RNN)), vec, _const_spec((D_RNN, D_RNN)), vec, vec, vec],
        out_specs=seq_spec,
        scratch_shapes=[pltpu.VMEM((HIST + TS, D_RNN), f32), pltpu.VMEM((1, D_RNN), f32)],
        compiler_params=pltpu.CompilerParams(
            dimension_semantics=("parallel", "arbitrary"), vmem_limit_bytes=VMEM_LIMIT),
        name="rglru",
    )(xr, gate, cw, cb, wa, ba, wx, bx, lam, g)


PIPE_DEPTH = 2
BLK_PER_TILE = TK // LANES
KT_PER_QT = TQ // TK
N_BLK = HEADS_PER_STEP * BLK_PER_TILE
ITER_PER_TRIP = 16
Z_SLOTS = 2
LOG2E = 1.4426950408889634


MASKED = -1e30


def _attn_kernel(hp_ref, qt_ref, kt_ref, q_ref, k_ref, v_ref, tri_ref, o_ref,
                 c_ref, z_ref, split_ref, w_ref, pen_ref):
    n_items = qt_ref.shape[0]
    heads = range(HEADS_PER_STEP)
    for ref in (o_ref, c_ref, z_ref, split_ref, w_ref):
        ref[...] = jnp.zeros_like(ref)
    pen_ref[0] = jnp.zeros_like(pen_ref[0])
    for j in range(KT_PER_QT):
        for b in range(BLK_PER_TILE):
            key = lax.broadcasted_iota(jnp.int32, (TQ, LANES), 1) + (j * TK + b * LANES)
            query = lax.broadcasted_iota(jnp.int32, (TQ, LANES), 0)
            pen_ref[j + 1, b] = jnp.where(key < query, 0.0, MASKED)
    lane_head = lax.broadcasted_iota(jnp.int32, (TK, LANES), 1) // HEAD_DIM

    def item(j):
        j = jnp.clip(j, 0, n_items - 1)
        return hp_ref[j], qt_ref[j], kt_ref[j]

    def diag_tile(qi):
        return (qi + 1) * KT_PER_QT - 1

    def per_head_rows(ref, hp, t):
        x = ref[hp, 0, pl.ds(pl.multiple_of(t * TK, TK), TK), :]
        return jnp.concatenate([jnp.where(lane_head == h, x, jnp.zeros_like(x)) for h in heads], axis=0)

    def stage_c2(hp, qi, t):
        rows = pl.ds(pl.multiple_of(qi * TQ, TQ), TQ)
        acc = jnp.where(t == diag_tile(qi), 0.0, o_ref[hp, 0, rows, :])
        w = jnp.concatenate([w_ref[blk] for blk in range(N_BLK)], axis=1)
        o_ref[hp, 0, rows, :] = acc + _dot(w, per_head_rows(v_ref, hp, t))

    def stage_b2c1(hp, qi, t, slot):
        for h in heads:
            first = h * BLK_PER_TILE
            parts = [split_ref[blk] for blk in range(first, first + BLK_PER_TILE)]
            r = _dot(jnp.concatenate(parts, axis=1), tri_ref[...])
            c = jnp.where(t == diag_tile(qi), 0.0, c_ref[h])
            for b in range(BLK_PER_TILE):
                s = z_ref[slot, first + b] + r[:, b * LANES:(b + 1) * LANES] + c
                w_ref[first + b] = jnp.exp(s).astype(bf16)
            c_ref[h] = c + jnp.broadcast_to(r[:, 0:1], (TQ, LANES))

    def stage_ab1(hp, qi, t, slot):
        q = q_ref[hp, 0, pl.ds(pl.multiple_of(qi * TQ, TQ), TQ), :]
        z = lax.dot_general(q, per_head_rows(k_ref, hp, t), (((1,), (1,)), ((), ())),
                            preferred_element_type=f32)
        kind = jnp.maximum(t - qi * KT_PER_QT + 1, 0)
        for b in range(BLK_PER_TILE):
            pen = pen_ref[kind, b]
            for h in heads:
                blk = h * BLK_PER_TILE + b
                zm = z[:, blk * LANES:(blk + 1) * LANES] + pen
                z_ref[slot, blk] = zm
                sp = jnp.log(1.0 + jnp.exp2(jnp.abs(zm) * (-LOG2E))) + jnp.maximum(zm, 0.0)
                split_ref[blk] = sp.astype(bf16)

    def iteration(i, u):
        stage_c2(*item(i - PIPE_DEPTH))
        stage_b2c1(*item(i - 1), (u - 1) % Z_SLOTS)
        stage_ab1(*item(i), u % Z_SLOTS)

    def body(m, carry):
        for u in range(ITER_PER_TRIP):
            iteration(ITER_PER_TRIP * m + u, u)
        return carry

    n_iter = n_items + PIPE_DEPTH
    full_trips = n_iter // ITER_PER_TRIP
    lax.fori_loop(0, full_trips, body, 0)
    for u in range(n_iter % ITER_PER_TRIP):
        iteration(full_trips * ITER_PER_TRIP + u, u)


def _attention(q, k, v, tri):
    _, bsz, seq, _ = q.shape
    n_qt = seq // TQ
    items = [(hp, qi, t) for hp in range(N_PAIR) for qi in range(n_qt)
             for t in range((qi + 1) * KT_PER_QT - 1, -1, -1)]
    hp_tab, qt, kt = (jnp.asarray([it[n] for it in items], jnp.int32) for n in range(3))
    seq_spec = pl.BlockSpec((N_PAIR, 1, seq, LANES), lambda b, *_: (0, b, 0, 0))
    tri_spec = pl.BlockSpec((TK, TK), lambda b, *_: (0, 0), pipeline_mode=pl.Buffered(1))
    return pl.pallas_call(
        _attn_kernel,
        out_shape=jax.ShapeDtypeStruct((N_PAIR, bsz, seq, LANES), f32),
        grid_spec=pltpu.PrefetchScalarGridSpec(
            num_scalar_prefetch=3,
            grid=(bsz,),
            in_specs=[seq_spec, seq_spec, seq_spec, tri_spec],
            out_specs=seq_spec,
            scratch_shapes=[
                pltpu.VMEM((HEADS_PER_STEP, TQ, LANES), f32),
                pltpu.VMEM((Z_SLOTS, N_BLK, TQ, LANES), f32),
                pltpu.VMEM((N_BLK, TQ, LANES), bf16),
                pltpu.VMEM((N_BLK, TQ, LANES), bf16),
                pltpu.VMEM((KT_PER_QT + 1, BLK_PER_TILE, TQ, LANES), f32),
            ]),
        compiler_params=pltpu.CompilerParams(
            dimension_semantics=("parallel",), vmem_limit_bytes=VMEM_LIMIT),
        name="stickbreak",
    )(hp_tab, qt, kt, q, k, v, tri)


def _ffn_out_kernel(x_ref, yr_ref, ya_ref, ga_ref, wo_ref, g2_ref, wg_ref, wu_ref, wd_ref,
                    o_ref, h_ref):
    ya = jnp.concatenate([ya_ref[p] for p in range(N_PAIR)], axis=1)
    ya = _rms(ya, ga_ref[...]).astype(bf16)
    x2 = (x_ref[...] + _dot(yr_ref[...], wo_ref[0:D_RNN, :])
          + _dot(ya, wo_ref[D_RNN:D_RNN + D_ATT, :]))
    o_ref[...] = _ffn_tile(x2, g2_ref, wg_ref, wu_ref, wd_ref, h_ref)


def _ffn_out(x2d, yr, ya, ga, w_out, g2, wg, wu, wd):
    n_tok = x2d.shape[0]
    return pl.pallas_call(
        _ffn_out_kernel,
        out_shape=jax.ShapeDtypeStruct((n_tok, D_MODEL), f32),
        grid=(n_tok // TM,),
        in_specs=[_tok_spec(D_MODEL), _tok_spec(D_RNN), _pair_spec(), _const_spec((1, D_ATT)),
                  _const_spec((D_RNN + D_ATT, D_MODEL))] + _ffn_specs(),
        out_specs=_tok_spec(D_MODEL),
        scratch_shapes=[pltpu.VMEM((TM, D_FF), bf16)],
        compiler_params=pltpu.CompilerParams(
            dimension_semantics=("parallel",), vmem_limit_bytes=VMEM_LIMIT),
        name="ffn_out",
    )(x2d, yr, ya, ga, w_out, g2, wg, wu, wd)


def _block_diag(w):
    eye = jnp.eye(RNN_BLOCKS, dtype=w.dtype)
    return jnp.einsum("ncd,nm->ncmd", w, eye).reshape(D_RNN, D_RNN)


def _scan_matrix():
    j = jnp.arange(TK)
    return -(j[:, None] >= j[None, :]).astype(bf16)


def kernel(x, ffn1_norm, ffn1_w_gate, ffn1_w_up, ffn1_w_down, mix_norm, w_in, conv_w, conv_b,
           rg_w_a, rg_b_a, rg_w_x, rg_b_x, rg_lambda, q_norm, k_norm, rnn_out_norm,
           attn_out_norm, w_out, ffn2_norm, ffn2_w_gate, ffn2_w_up, ffn2_w_down):
    bsz, seq, _ = x.shape
    depth = ffn1_norm.shape[0]
    row_head = (jnp.arange(2 * LANES) % LANES) // HEAD_DIM
    seg = (row_head[:, None] == (jnp.arange(LANES) // HEAD_DIM)[None, :]).astype(bf16)
    tri = _scan_matrix()
    x2d = x.reshape(bsz * seq, D_MODEL)
    for l in range(depth):
        x2d, xr, gate, q, k, v = _ffn_in(
            x2d, ffn1_norm[l][None], ffn1_w_gate[l].astype(bf16), ffn1_w_up[l].astype(bf16),
            ffn1_w_down[l].astype(bf16), mix_norm[l][None], w_in[l].astype(bf16),
            jnp.tile(q_norm[l], N_HEADS)[None], jnp.tile(k_norm[l], N_HEADS)[None], seg)
        y_rnn = _rglru(
            xr.reshape(bsz, seq, D_RNN), gate.reshape(bsz, seq, D_RNN),
            conv_w[l], conv_b[l][None], _block_diag(rg_w_a[l]).astype(bf16), rg_b_a[l][None],
            _block_diag(rg_w_x[l]).astype(bf16), rg_b_x[l][None], rg_lambda[l][None],
            rnn_out_norm[l][None])
        pairs = (N_PAIR, bsz, seq, LANES)
        y_att = _attention(q.reshape(pairs), k.reshape(pairs), v.reshape(pairs), tri)
        x2d = _ffn_out(x2d, y_rnn.reshape(bsz * seq, D_RNN), y_att.reshape(N_PAIR, bsz * seq, LANES),
                       attn_out_norm[l][None], w_out[l].astype(bf16), ffn2_norm[l][None],
                       ffn2_w_gate[l].astype(bf16), ffn2_w_up[l].astype(bf16),
                       ffn2_w_down[l].astype(bf16))
    return x2d.reshape(bsz, seq, D_MODEL)
```

```python
import math

import jax
import jax.numpy as jnp
from jax import lax
from jax.experimental import pallas as pl
from jax.experimental.pallas import tpu as pltpu

D_MODEL = 1024
D_RNN = 512
RNN_BLOCKS = 8
RNN_BW = D_RNN // RNN_BLOCKS
CONV_W = 4
RG_C = 8.0
D_ATT = 512
HEAD_DIM = 64
N_HEADS = D_ATT // HEAD_DIM
D_FF = 2816
N_IN = 2 * D_RNN + 3 * D_ATT
EPS = 1e-6

LANES = 128
SUBLANES = 8
HEADS_PER_STEP = LANES // HEAD_DIM
N_PAIR = D_ATT // LANES
VMEM_LIMIT = 56 * 1024 * 1024

TM = 512
FF_CHUNK = 256
TS = 512
TQ = 512
TK = 256

f32 = jnp.float32
bf16 = jnp.bfloat16


def _rms(xf, g):
    r = lax.rsqrt(jnp.mean(xf * xf, axis=-1, keepdims=True) + EPS)
    return xf * r * g


def _dot(a, b):
    return jnp.dot(a, b, preferred_element_type=f32)


def _split_bf16(x):
    hi = x.astype(bf16)
    lo = (x - hi.astype(f32)).astype(bf16)
    return hi, lo


def _const_spec(shape):
    nd = len(shape)
    return pl.BlockSpec(shape, lambda *_: (0,) * nd, pipeline_mode=pl.Buffered(1))


def _ffn_tile(x, g_ref, wg_ref, wu_ref, wd_ref, h_ref):
    xn = _rms(x, g_ref[...]).astype(bf16)
    for c in range(0, D_FF, FF_CHUNK):
        gate = _dot(xn, wg_ref[:, c:c + FF_CHUNK])
        up = _dot(xn, wu_ref[:, c:c + FF_CHUNK])
        h_ref[:, c:c + FF_CHUNK] = (jax.nn.silu(gate) * up).astype(bf16)
    return x + 0.5 * _dot(h_ref[...], wd_ref[...])


def _ffn_specs():
    return [_const_spec((1, D_MODEL)), _const_spec((D_MODEL, D_FF)),
            _const_spec((D_MODEL, D_FF)), _const_spec((D_FF, D_MODEL))]


def _tok_spec(d):
    return pl.BlockSpec((TM, d), lambda i: (i, 0))


def _pair_spec():
    return pl.BlockSpec((N_PAIR, TM, LANES), lambda i: (0, i, 0))


def _head_norm(t, gain_tiled, seg_ref):
    hi, lo = _split_bf16(t * t)
    groups = []
    for g in range(0, t.shape[1], LANES):
        both = jnp.concatenate([hi[:, g:g + LANES], lo[:, g:g + LANES]], axis=1)
        groups.append(_dot(both, seg_ref[...]))
    ss = jnp.concatenate(groups, axis=1)
    r = lax.rsqrt(ss * (1.0 / HEAD_DIM) + EPS)
    return t * r * gain_tiled


def _ffn_in_kernel(x_ref, g1_ref, wg_ref, wu_ref, wd_ref, g_ref, w_ref, qg_ref, kg_ref, seg_ref,
                   o_ref, xr_ref, gate_ref, q_ref, k_ref, v_ref, h_ref):
    x1 = _ffn_tile(x_ref[...], g1_ref, wg_ref, wu_ref, wd_ref, h_ref)
    o_ref[...] = x1
    h = _rms(x1, g_ref[...]).astype(bf16)
    xr_ref[...] = _dot(h, w_ref[:, 0:D_RNN])
    gate_ref[...] = _dot(h, w_ref[:, D_RNN:2 * D_RNN])
    o = 2 * D_RNN
    q = _dot(h, w_ref[:, o:o + D_ATT])
    k = _dot(h, w_ref[:, o + D_ATT:o + 2 * D_ATT])
    v = _dot(h, w_ref[:, o + 2 * D_ATT:o + 3 * D_ATT])
    scale = 1.0 / math.sqrt(HEAD_DIM)
    qn = (_head_norm(q, qg_ref[...], seg_ref) * scale).astype(bf16)
    kn = _head_norm(k, kg_ref[...], seg_ref).astype(bf16)
    vb = v.astype(bf16)
    for p in range(N_PAIR):
        lanes = slice(p * LANES, (p + 1) * LANES)
        q_ref[p] = qn[:, lanes]
        k_ref[p] = kn[:, lanes]
        v_ref[p] = vb[:, lanes]


def _ffn_in(x2d, g1, wg, wu, wd, g, w_in, qg, kg, seg):
    n_tok = x2d.shape[0]
    return pl.pallas_call(
        _ffn_in_kernel,
        out_shape=(jax.ShapeDtypeStruct((n_tok, D_MODEL), f32),
                   jax.ShapeDtypeStruct((n_tok, D_RNN), f32),
                   jax.ShapeDtypeStruct((n_tok, D_RNN), f32),
                   jax.ShapeDtypeStruct((N_PAIR, n_tok, LANES), bf16),
                   jax.ShapeDtypeStruct((N_PAIR, n_tok, LANES), bf16),
                   jax.ShapeDtypeStruct((N_PAIR, n_tok, LANES), bf16)),
        grid=(n_tok // TM,),
        in_specs=[_tok_spec(D_MODEL)] + _ffn_specs() + [
            _const_spec((1, D_MODEL)), _const_spec((D_MODEL, N_IN)),
            _const_spec((1, D_ATT)), _const_spec((1, D_ATT)), _const_spec((2 * LANES, LANES))],
        out_specs=(_tok_spec(D_MODEL), _tok_spec(D_RNN), _tok_spec(D_RNN),
                   _pair_spec(), _pair_spec(), _pair_spec()),
        scratch_shapes=[pltpu.VMEM((TM, D_FF), bf16)],
        compiler_params=pltpu.CompilerParams(
            dimension_semantics=("parallel",), vmem_limit_bytes=VMEM_LIMIT),
        name="ffn_in",
    )(x2d, g1, wg, wu, wd, g, w_in, qg, kg, seg)


HIST = 8


def _rglru_kernel(xr_ref, gate_ref, cw_ref, cb_ref, wa_ref, ba_ref, wx_ref, bx_ref,
                  lam_ref, g_ref, o_ref, xbuf_ref, hc_ref):
    @pl.when(pl.program_id(1) == 0)
    def _():
        xbuf_ref[0:HIST, :] = jnp.zeros((HIST, D_RNN), f32)
        hc_ref[...] = jnp.zeros((1, D_RNN), f32)

    x = xr_ref[0]
    xbuf_ref[HIST:HIST + TS, :] = x
    xc = cb_ref[...] + cw_ref[CONV_W - 1:CONV_W, :] * x
    for j in range(CONV_W - 1):
        off = HIST - (CONV_W - 1) + j
        xc = xc + cw_ref[j:j + 1, :] * xbuf_ref[off:off + TS, :]
    xbuf_ref[0:HIST, :] = x[TS - HIST:TS, :]

    xcb = xc.astype(bf16)
    r = jax.nn.sigmoid(_dot(xcb, wa_ref[...]) + ba_ref[...])
    i = jax.nn.sigmoid(_dot(xcb, wx_ref[...]) + bx_ref[...])
    lam = lam_ref[...]
    log_sig_lam = jnp.minimum(lam, 0.0) - jnp.log1p(jnp.exp(-jnp.abs(lam)))
    log_a = RG_C * r * log_sig_lam
    a = jnp.exp(log_a)
    th = jnp.tanh(log_a)
    mult = jnp.sqrt(-2.0 * th / (1.0 - th))
    b = mult * (i * xc)

    n_grp = TS // SUBLANES
    a = a.reshape(n_grp, SUBLANES, D_RNN)
    b = b.reshape(n_grp, SUBLANES, D_RNN)
    row = lax.broadcasted_iota(jnp.int32, (n_grp, SUBLANES, D_RNN), 1)
    d = 1
    while d < SUBLANES:
        a_sh = pltpu.roll(a, d, axis=1)
        b_sh = pltpu.roll(b, d, axis=1)
        m = row >= d
        b = jnp.where(m, a * b_sh, 0.0) + b
        a = jnp.where(m, a * a_sh, a)
        d *= 2
    carry = jnp.broadcast_to(hc_ref[...], (SUBLANES, D_RNN))
    groups = []
    for g in range(n_grp):
        h_g = a[g] * carry + b[g]
        groups.append(h_g)
        carry = jnp.broadcast_to(h_g[SUBLANES - 1:SUBLANES, :], (SUBLANES, D_RNN))
    h = jnp.concatenate(groups, axis=0)
    hc_ref[...] = groups[-1][SUBLANES - 1:SUBLANES, :]

    y = h * jax.nn.gelu(gate_ref[0])
    o_ref[0] = _rms(y, g_ref[...]).astype(bf16)


def _rglru(xr, gate, cw, cb, wa, ba, wx, bx, lam, g):
    bsz, seq, _ = xr.shape
    seq_spec = pl.BlockSpec((1, TS, D_RNN), lambda b, s: (b, s, 0))
    vec = _const_spec((1, D_RNN))
    return pl.pallas_call(
        _rglru_kernel,
        out_shape=jax.ShapeDtypeStruct((bsz, seq, D_RNN), bf16),
        grid=(bsz, seq // TS),
        in_specs=[seq_spec, seq_spec, _const_spec((CONV_W, D_RNN)), vec,
                  _const_spec((D_RNN, D_RNN)), vec, _const_spec((D_RNN, D_RNN)), vec, vec, vec],
        out_specs=seq_spec,
        scratch_shapes=[pltpu.VMEM((HIST + TS, D_RNN), f32), pltpu.VMEM((1, D_RNN), f32)],
        compiler_params=pltpu.CompilerParams(
            dimension_semantics=("parallel", "arbitrary"), vmem_limit_bytes=VMEM_LIMIT),
        name="rglru",
    )(xr, gate, cw, cb, wa, ba, wx, bx, lam, g)


PIPE_DEPTH = 2
BLK_PER_TILE = TK // LANES
KT_PER_QT = TQ // TK
N_BLK = HEADS_PER_STEP * BLK_PER_TILE
ITER_PER_TRIP = 32
SLOTS = 2
LOG2E = 1.4426950408889634


MASKED = -1e30


def _attn_kernel(hp_ref, qt_ref, kt_ref, q_ref, k_ref, v_ref, tri_ref, o_ref,
                 c_ref, z_ref, split_ref, w_ref, pen_ref):
    n_items = qt_ref.shape[0]
    heads = range(HEADS_PER_STEP)
    for ref in (o_ref, c_ref, z_ref, split_ref, w_ref):
        ref[...] = jnp.zeros_like(ref)
    pen_ref[0] = jnp.zeros_like(pen_ref[0])
    for j in range(KT_PER_QT):
        for b in range(BLK_PER_TILE):
            key = lax.broadcasted_iota(jnp.int32, (TQ, LANES), 1) + (j * TK + b * LANES)
            query = lax.broadcasted_iota(jnp.int32, (TQ, LANES), 0)
            pen_ref[j + 1, b] = jnp.where(key < query, 0.0, MASKED)
    lane_head = lax.broadcasted_iota(jnp.int32, (TK, LANES), 1) // HEAD_DIM

    def item(j):
        j = jnp.clip(j, 0, n_items - 1)
        return hp_ref[j], qt_ref[j], kt_ref[j]

    def diag_tile(qi):
        return (qi + 1) * KT_PER_QT - 1

    def per_head_rows(ref, hp, t):
        x = ref[hp, 0, pl.ds(pl.multiple_of(t * TK, TK), TK), :]
        return jnp.concatenate([jnp.where(lane_head == h, x, jnp.zeros_like(x)) for h in heads], axis=0)

    def stage_c2(hp, qi, t, slot):
        rows = pl.ds(pl.multiple_of(qi * TQ, TQ), TQ)
        acc = jnp.where(t == diag_tile(qi), 0.0, o_ref[hp, 0, rows, :])
        w = jnp.concatenate([w_ref[slot, blk] for blk in range(N_BLK)], axis=1)
        o_ref[hp, 0, rows, :] = acc + _dot(w, per_head_rows(v_ref, hp, t))

    def stage_b2c1(hp, qi, t, slot):
        for h in heads:
            first = h * BLK_PER_TILE
            parts = [split_ref[blk] for blk in range(first, first + BLK_PER_TILE)]
            r = _dot(jnp.concatenate(parts, axis=1), tri_ref[...])
            c = jnp.where(t == diag_tile(qi), 0.0, c_ref[h])
            for b in range(BLK_PER_TILE):
                s = z_ref[slot, first + b] + r[:, b * LANES:(b + 1) * LANES] + c
                w_ref[slot, first + b] = jnp.exp(s).astype(bf16)
            c_ref[h] = c + jnp.broadcast_to(r[:, 0:1], (TQ, LANES))

    def stage_ab1(hp, qi, t, slot):
        q = q_ref[hp, 0, pl.ds(pl.multiple_of(qi * TQ, TQ), TQ), :]
        z = lax.dot_general(q, per_head_rows(k_ref, hp, t), (((1,), (1,)), ((), ())),
                            preferred_element_type=f32)
        kind = jnp.maximum(t - qi * KT_PER_QT + 1, 0)
        for b in range(BLK_PER_TILE):
            pen = pen_ref[kind, b]
            for h in heads:
                blk = h * BLK_PER_TILE + b
                zm = z[:, blk * LANES:(blk + 1) * LANES] + pen
                z_ref[slot, blk] = zm
                sp = jnp.log(1.0 + jnp.exp2(jnp.abs(zm) * (-LOG2E))) + jnp.maximum(zm, 0.0)
                split_ref[blk] = sp.astype(bf16)

    def iteration(i, u):
        stage_b2c1(*item(i - 1), (u - 1) % SLOTS)
        stage_ab1(*item(i), u % SLOTS)
        stage_c2(*item(i - PIPE_DEPTH), (u - 2) % SLOTS)

    def body(m, carry):
        for u in range(ITER_PER_TRIP):
            iteration(ITER_PER_TRIP * m + u, u)
        return carry

    n_iter = n_items + PIPE_DEPTH
    full_trips = n_iter // ITER_PER_TRIP
    lax.fori_loop(0, full_trips, body, 0)
    for u in range(n_iter % ITER_PER_TRIP):
        iteration(full_trips * ITER_PER_TRIP + u, u)


def _attention(q, k, v, tri):
    _, bsz, seq, _ = q.shape
    n_qt = seq // TQ
    items = [(hp, qi, t) for hp in range(N_PAIR) for qi in range(n_qt)
             for t in range((qi + 1) * KT_PER_QT - 1, -1, -1)]
    hp_tab, qt, kt = (jnp.asarray([it[n] for it in items], jnp.int32) for n in range(3))
    seq_spec = pl.BlockSpec((N_PAIR, 1, seq, LANES), lambda b, *_: (0, b, 0, 0))
    tri_spec = pl.BlockSpec((TK, TK), lambda b, *_: (0, 0), pipeline_mode=pl.Buffered(1))
    return pl.pallas_call(
        _attn_kernel,
        out_shape=jax.ShapeDtypeStruct((N_PAIR, bsz, seq, LANES), f32),
        grid_spec=pltpu.PrefetchScalarGridSpec(
            num_scalar_prefetch=3,
            grid=(bsz,),
            in_specs=[seq_spec, seq_spec, seq_spec, tri_spec],
            out_specs=seq_spec,
            scratch_shapes=[
                pltpu.VMEM((HEADS_PER_STEP, TQ, LANES), f32),
                pltpu.VMEM((SLOTS, N_BLK, TQ, LANES), f32),
                pltpu.VMEM((N_BLK, TQ, LANES), bf16),
                pltpu.VMEM((SLOTS, N_BLK, TQ, LANES), bf16),
                pltpu.VMEM((KT_PER_QT + 1, BLK_PER_TILE, TQ, LANES), f32),
            ]),
        compiler_params=pltpu.CompilerParams(
            dimension_semantics=("parallel",), vmem_limit_bytes=VMEM_LIMIT),
        name="stickbreak",
    )(hp_tab, qt, kt, q, k, v, tri)


def _ffn_out_kernel(x_ref, yr_ref, ya_ref, ga_ref, wo_ref, g2_ref, wg_ref, wu_ref, wd_ref,
                    o_ref, h_ref):
    ya = jnp.concatenate([ya_ref[p] for p in range(N_PAIR)], axis=1)
    ya = _rms(ya, ga_ref[...]).astype(bf16)
    x2 = (x_ref[...] + _dot(yr_ref[...], wo_ref[0:D_RNN, :])
          + _dot(ya, wo_ref[D_RNN:D_RNN + D_ATT, :]))
    o_ref[...] = _ffn_tile(x2, g2_ref, wg_ref, wu_ref, wd_ref, h_ref)


def _ffn_out(x2d, yr, ya, ga, w_out, g2, wg, wu, wd):
    n_tok = x2d.shape[0]
    return pl.pallas_call(
        _ffn_out_kernel,
        out_shape=jax.ShapeDtypeStruct((n_tok, D_MODEL), f32),
        grid=(n_tok // TM,),
        in_specs=[_tok_spec(D_MODEL), _tok_spec(D_RNN), _pair_spec(), _const_spec((1, D_ATT)),
                  _const_spec((D_RNN + D_ATT, D_MODEL))] + _ffn_specs(),
        out_specs=_tok_spec(D_MODEL),
        scratch_shapes=[pltpu.VMEM((TM, D_FF), bf16)],
        compiler_params=pltpu.CompilerParams(
            dimension_semantics=("parallel",), vmem_limit_bytes=VMEM_LIMIT),
        name="ffn_out",
    )(x2d, yr, ya, ga, w_out, g2, wg, wu, wd)


def _block_diag(w):
    eye = jnp.eye(RNN_BLOCKS, dtype=w.dtype)
    return jnp.einsum("ncd,nm->ncmd", w, eye).reshape(D_RNN, D_RNN)


def _scan_matrix():
    j = jnp.arange(TK)
    return -(j[:, None] >= j[None, :]).astype(bf16)


def kernel(x, ffn1_norm, ffn1_w_gate, ffn1_w_up, ffn1_w_down, mix_norm, w_in, conv_w, conv_b,
           rg_w_a, rg_b_a, rg_w_x, rg_b_x, rg_lambda, q_norm, k_norm, rnn_out_norm,
           attn_out_norm, w_out, ffn2_norm, ffn2_w_gate, ffn2_w_up, ffn2_w_down):
    bsz, seq, _ = x.shape
    depth = ffn1_norm.shape[0]
    row_head = (jnp.arange(2 * LANES) % LANES) // HEAD_DIM
    seg = (row_head[:, None] == (jnp.arange(LANES) // HEAD_DIM)[None, :]).astype(bf16)
    tri = _scan_matrix()
    x2d = x.reshape(bsz * seq, D_MODEL)
    for l in range(depth):
        x2d, xr, gate, q, k, v = _ffn_in(
            x2d, ffn1_norm[l][None], ffn1_w_gate[l].astype(bf16), ffn1_w_up[l].astype(bf16),
            ffn1_w_down[l].astype(bf16), mix_norm[l][None], w_in[l].astype(bf16),
            jnp.tile(q_norm[l], N_HEADS)[None], jnp.tile(k_norm[l], N_HEADS)[None], seg)
        y_rnn = _rglru(
            xr.reshape(bsz, seq, D_RNN), gate.reshape(bsz, seq, D_RNN),
            conv_w[l], conv_b[l][None], _block_diag(rg_w_a[l]).astype(bf16), rg_b_a[l][None],
            _block_diag(rg_w_x[l]).astype(bf16), rg_b_x[l][None], rg_lambda[l][None],
            rnn_out_norm[l][None])
        pairs = (N_PAIR, bsz, seq, LANES)
        y_att = _attention(q.reshape(pairs), k.reshape(pairs), v.reshape(pairs), tri)
        x2d = _ffn_out(x2d, y_rnn.reshape(bsz * seq, D_RNN), y_att.reshape(N_PAIR, bsz * seq, LANES),
                       attn_out_norm[l][None], w_out[l].astype(bf16), ffn2_norm[l][None],
                       ffn2_w_gate[l].astype(bf16), ffn2_w_up[l].astype(bf16),
                       ffn2_w_down[l].astype(bf16))
    return x2d.reshape(bsz, seq, D_MODEL)
```

```python
import math

import jax
import jax.numpy as jnp
from jax import lax
from jax.experimental import pallas as pl
from jax.experimental.pallas import tpu as pltpu

D_MODEL = 1024
D_RNN = 512
RNN_BLOCKS = 8
RNN_BW = D_RNN // RNN_BLOCKS
CONV_W = 4
RG_C = 8.0
D_ATT = 512
HEAD_DIM = 64
N_HEADS = D_ATT // HEAD_DIM
D_FF = 2816
N_IN = 2 * D_RNN + 3 * D_ATT
EPS = 1e-6

LANES = 128
SUBLANES = 8
HEADS_PER_STEP = LANES // HEAD_DIM
N_PAIR = D_ATT // LANES
VMEM_LIMIT = 56 * 1024 * 1024

TM = 512
FF_CHUNK = 256
TS = 512
TQ = 512
TK = 256

f32 = jnp.float32
bf16 = jnp.bfloat16


def _rms(xf, g):
    r = lax.rsqrt(jnp.mean(xf * xf, axis=-1, keepdims=True) + EPS)
    return xf * r * g


def _dot(a, b):
    return jnp.dot(a, b, preferred_element_type=f32)


def _split_bf16(x):
    hi = x.astype(bf16)
    lo = (x - hi.astype(f32)).astype(bf16)
    return hi, lo


def _const_spec(shape):
    nd = len(shape)
    return pl.BlockSpec(shape, lambda *_: (0,) * nd, pipeline_mode=pl.Buffered(1))


def _ffn_tile(x, g_ref, wg_ref, wu_ref, wd_ref, h_ref):
    xn = _rms(x, g_ref[...]).astype(bf16)
    for c in range(0, D_FF, FF_CHUNK):
        gate = _dot(xn, wg_ref[:, c:c + FF_CHUNK])
        up = _dot(xn, wu_ref[:, c:c + FF_CHUNK])
        h_ref[:, c:c + FF_CHUNK] = (jax.nn.silu(gate) * up).astype(bf16)
    return x + 0.5 * _dot(h_ref[...], wd_ref[...])


CAST_ROWS = 128


def _load_as_bf16(src_hbm, dst_ref, stage_ref, sem_ref):
    rows, cols = dst_ref.shape
    n_chunks = rows // CAST_ROWS
    assert n_chunks * CAST_ROWS == rows and cols <= stage_ref.shape[2]

    def chunk_copy(c, slot):
        return pltpu.make_async_copy(src_hbm.at[pl.ds(c * CAST_ROWS, CAST_ROWS), :],
                                     stage_ref.at[slot, :, pl.ds(0, cols)], sem_ref.at[slot])

    chunk_copy(0, 0).start()

    def body(c, carry):
        slot = c % 2

        @pl.when(c + 1 < n_chunks)
        def _():
            chunk_copy(c + 1, 1 - slot).start()

        chunk_copy(c, slot).wait()
        dst_rows = pl.ds(pl.multiple_of(c * CAST_ROWS, CAST_ROWS), CAST_ROWS)
        dst_ref[dst_rows, :] = stage_ref[slot, :, pl.ds(0, cols)].astype(bf16)
        return carry

    lax.fori_loop(0, n_chunks, body, 0)


def _hbm_spec():
    return pl.BlockSpec(memory_space=pl.ANY)


def _ffn_specs():
    return [_const_spec((1, D_MODEL)), _hbm_spec(), _hbm_spec(), _hbm_spec()]


def _ffn_scratch():
    return [pltpu.VMEM((TM, D_FF), bf16), pltpu.VMEM((D_MODEL, D_FF), bf16),
            pltpu.VMEM((D_MODEL, D_FF), bf16), pltpu.VMEM((D_FF, D_MODEL), bf16),
            pltpu.VMEM((2, CAST_ROWS, D_FF), f32), pltpu.SemaphoreType.DMA((2,))]


def _tok_spec(d):
    return pl.BlockSpec((TM, d), lambda i: (i, 0))


def _pair_spec():
    return pl.BlockSpec((N_PAIR, TM, LANES), lambda i: (0, i, 0))


def _head_norm(t, gain_tiled, seg_ref):
    hi, lo = _split_bf16(t * t)
    groups = []
    for g in range(0, t.shape[1], LANES):
        both = jnp.concatenate([hi[:, g:g + LANES], lo[:, g:g + LANES]], axis=1)
        groups.append(_dot(both, seg_ref[...]))
    ss = jnp.concatenate(groups, axis=1)
    r = lax.rsqrt(ss * (1.0 / HEAD_DIM) + EPS)
    return t * r * gain_tiled


def _ffn_in_kernel(x_ref, g1_ref, wg_hbm, wu_hbm, wd_hbm, g_ref, w_hbm, qg_ref, kg_ref, seg_ref,
                   o_ref, xr_ref, gate_ref, q_ref, k_ref, v_ref,
                   h_ref, wg_ref, wu_ref, wd_ref, stage_ref, sem_ref, w_ref):
    @pl.when(pl.program_id(0) == 0)
    def _():
        for src, dst in ((wg_hbm, wg_ref), (wu_hbm, wu_ref), (wd_hbm, wd_ref), (w_hbm, w_ref)):
            _load_as_bf16(src, dst, stage_ref, sem_ref)

    x1 = _ffn_tile(x_ref[...], g1_ref, wg_ref, wu_ref, wd_ref, h_ref)
    o_ref[...] = x1
    h = _rms(x1, g_ref[...]).astype(bf16)
    xr_ref[...] = _dot(h, w_ref[:, 0:D_RNN])
    gate_ref[...] = _dot(h, w_ref[:, D_RNN:2 * D_RNN])
    o = 2 * D_RNN
    q = _dot(h, w_ref[:, o:o + D_ATT])
    k = _dot(h, w_ref[:, o + D_ATT:o + 2 * D_ATT])
    v = _dot(h, w_ref[:, o + 2 * D_ATT:o + 3 * D_ATT])
    scale = 1.0 / math.sqrt(HEAD_DIM)
    qn = (_head_norm(q, qg_ref[...], seg_ref) * scale).astype(bf16)
    kn = _head_norm(k, kg_ref[...], seg_ref).astype(bf16)
    vb = v.astype(bf16)
    for p in range(N_PAIR):
        lanes = slice(p * LANES, (p + 1) * LANES)
        q_ref[p] = qn[:, lanes]
        k_ref[p] = kn[:, lanes]
        v_ref[p] = vb[:, lanes]


def _ffn_in(x2d, g1, wg, wu, wd, g, w_in, qg, kg, seg):
    n_tok = x2d.shape[0]
    return pl.pallas_call(
        _ffn_in_kernel,
        out_shape=(jax.ShapeDtypeStruct((n_tok, D_MODEL), f32),
                   jax.ShapeDtypeStruct((n_tok, D_RNN), f32),
                   jax.ShapeDtypeStruct((n_tok, D_RNN), f32),
                   jax.ShapeDtypeStruct((N_PAIR, n_tok, LANES), bf16),
                   jax.ShapeDtypeStruct((N_PAIR, n_tok, LANES), bf16),
                   jax.ShapeDtypeStruct((N_PAIR, n_tok, LANES), bf16)),
        grid=(n_tok // TM,),
        in_specs=[_tok_spec(D_MODEL)] + _ffn_specs() + [
            _const_spec((1, D_MODEL)), _hbm_spec(),
            _const_spec((1, D_ATT)), _const_spec((1, D_ATT)), _const_spec((2 * LANES, LANES))],
        out_specs=(_tok_spec(D_MODEL), _tok_spec(D_RNN), _tok_spec(D_RNN),
                   _pair_spec(), _pair_spec(), _pair_spec()),
        scratch_shapes=_ffn_scratch() + [pltpu.VMEM((D_MODEL, N_IN), bf16)],
        compiler_params=pltpu.CompilerParams(
            dimension_semantics=("arbitrary",), vmem_limit_bytes=VMEM_LIMIT),
        name="ffn_in",
    )(x2d, g1, wg, wu, wd, g, w_in, qg, kg, seg)


HIST = 8


def _rglru_kernel(xr_ref, gate_ref, cw_ref, cb_ref, wa_ref, ba_ref, wx_ref, bx_ref,
                  lam_ref, g_ref, o_ref, xbuf_ref, hc_ref):
    @pl.when(pl.program_id(1) == 0)
    def _():
        xbuf_ref[0:HIST, :] = jnp.zeros((HIST, D_RNN), f32)
        hc_ref[...] = jnp.zeros((1, D_RNN), f32)

    x = xr_ref[0]
    xbuf_ref[HIST:HIST + TS, :] = x
    xc = cb_ref[...] + cw_ref[CONV_W - 1:CONV_W, :] * x
    for j in range(CONV_W - 1):
        off = HIST - (CONV_W - 1) + j
        xc = xc + cw_ref[j:j + 1, :] * xbuf_ref[off:off + TS, :]
    xbuf_ref[0:HIST, :] = x[TS - HIST:TS, :]

    xcb = xc.astype(bf16)
    r = jax.nn.sigmoid(_dot(xcb, wa_ref[...]) + ba_ref[...])
    i = jax.nn.sigmoid(_dot(xcb, wx_ref[...]) + bx_ref[...])
    lam = lam_ref[...]
    log_sig_lam = jnp.minimum(lam, 0.0) - jnp.log1p(jnp.exp(-jnp.abs(lam)))
    log_a = RG_C * r * log_sig_lam
    a = jnp.exp(log_a)
    th = jnp.tanh(log_a)
    mult = jnp.sqrt(-2.0 * th / (1.0 - th))
    b = mult * (i * xc)

    n_grp = TS // SUBLANES
    a = a.reshape(n_grp, SUBLANES, D_RNN)
    b = b.reshape(n_grp, SUBLANES, D_RNN)
    row = lax.broadcasted_iota(jnp.int32, (n_grp, SUBLANES, D_RNN), 1)
    d = 1
    while d < SUBLANES:
        a_sh = pltpu.roll(a, d, axis=1)
        b_sh = pltpu.roll(b, d, axis=1)
        m = row >= d
        b = jnp.where(m, a * b_sh, 0.0) + b
        a = jnp.where(m, a * a_sh, a)
        d *= 2
    carry = jnp.broadcast_to(hc_ref[...], (SUBLANES, D_RNN))
    groups = []
    for g in range(n_grp):
        h_g = a[g] * carry + b[g]
        groups.append(h_g)
        carry = jnp.broadcast_to(h_g[SUBLANES - 1:SUBLANES, :], (SUBLANES, D_RNN))
    h = jnp.concatenate(groups, axis=0)
    hc_ref[...] = groups[-1][SUBLANES - 1:SUBLANES, :]

    y = h * jax.nn.gelu(gate_ref[0])
    o_ref[0] = _rms(y, g_ref[...]).astype(bf16)


def _rglru(xr, gate, cw, cb, wa, ba, wx, bx, lam, g):
    bsz, seq, _ = xr.shape
    seq_spec = pl.BlockSpec((1, TS, D_RNN), lambda b, s: (b, s, 0))
    vec = _const_spec((1, D_RNN))
    return pl.pallas_call(
        _rglru_kernel,
        out_shape=jax.ShapeDtypeStruct((bsz, seq, D_RNN), bf16),
        grid=(bsz, seq // TS),
        in_specs=[seq_spec, seq_spec, _const_spec((CONV_W, D_RNN)), vec,
                  _const_spec((D_RNN, D_RNN)), vec, _const_spec((D_RNN, D_RNN)), vec, vec, vec],
        out_specs=seq_spec,
        scratch_shapes=[pltpu.VMEM((HIST + TS, D_RNN), f32), pltpu.VMEM((1, D_RNN), f32)],
        compiler_params=pltpu.CompilerParams(
            dimension_semantics=("parallel", "arbitrary"), vmem_limit_bytes=VMEM_LIMIT),
        name="rglru",
    )(xr, gate, cw, cb, wa, ba, wx, bx, lam, g)


PIPE_DEPTH = 2
BLK_PER_TILE = TK // LANES
KT_PER_QT = TQ // TK
N_BLK = HEADS_PER_STEP * BLK_PER_TILE
ITER_PER_TRIP = 16
Z_SLOTS = 2
LOG2E = 1.4426950408889634


MASKED = -1e30


def _attn_kernel(hp_ref, qt_ref, kt_ref, q_ref, k_ref, v_ref, tri_ref, o_ref,
                 c_ref, z_ref, split_ref, w_ref, pen_ref):
    n_items = qt_ref.shape[0]
    heads = range(HEADS_PER_STEP)
    for ref in (o_ref, c_ref, z_ref, split_ref, w_ref):
        ref[...] = jnp.zeros_like(ref)
    pen_ref[0] = jnp.zeros_like(pen_ref[0])
    for j in range(KT_PER_QT):
        for b in range(BLK_PER_TILE):
            key = lax.broadcasted_iota(jnp.int32, (TQ, LANES), 1) + (j * TK + b * LANES)
            query = lax.broadcasted_iota(jnp.int32, (TQ, LANES), 0)
            pen_ref[j + 1, b] = jnp.where(key < query, 0.0, MASKED)
    lane_head = lax.broadcasted_iota(jnp.int32, (TK, LANES), 1) // HEAD_DIM

    def item(j):
        j = jnp.clip(j, 0, n_items - 1)
        return hp_ref[j], qt_ref[j], kt_ref[j]

    def diag_tile(qi):
        return (qi + 1) * KT_PER_QT - 1

    def per_head_rows(ref, hp, t):
        x = ref[hp, 0, pl.ds(pl.multiple_of(t * TK, TK), TK), :]
        return jnp.concatenate([jnp.where(lane_head == h, x, jnp.zeros_like(x)) for h in heads], axis=0)

    def stage_c2(hp, qi, t):
        rows = pl.ds(pl.multiple_of(qi * TQ, TQ), TQ)
        acc = jnp.where(t == diag_tile(qi), 0.0, o_ref[hp, 0, rows, :])
        w = jnp.concatenate([w_ref[blk] for blk in range(N_BLK)], axis=1)
        o_ref[hp, 0, rows, :] = acc + _dot(w, per_head_rows(v_ref, hp, t))

    def stage_b2c1(hp, qi, t, slot):
        for h in heads:
            first = h * BLK_PER_TILE
            parts = [split_ref[blk] for blk in range(first, first + BLK_PER_TILE)]
            r = _dot(jnp.concatenate(parts, axis=1), tri_ref[...])
            c = jnp.where(t == diag_tile(qi), 0.0, c_ref[h])
            for b in range(BLK_PER_TILE):
                s = z_ref[slot, first + b] + r[:, b * LANES:(b + 1) * LANES] + c
                w_ref[first + b] = jnp.exp(s).astype(bf16)
            c_ref[h] = c + jnp.broadcast_to(r[:, 0:1], (TQ, LANES))

    def stage_ab1(hp, qi, t, slot):
        q = q_ref[hp, 0, pl.ds(pl.multiple_of(qi * TQ, TQ), TQ), :]
        z = lax.dot_general(q, per_head_rows(k_ref, hp, t), (((1,), (1,)), ((), ())),
                            preferred_element_type=f32)
        kind = jnp.maximum(t - qi * KT_PER_QT + 1, 0)
        for b in range(BLK_PER_TILE):
            pen = pen_ref[kind, b]
            for h in heads:
                blk = h * BLK_PER_TILE + b
                zm = z[:, blk * LANES:(blk + 1) * LANES] + pen
                z_ref[slot, blk] = zm
                sp = jnp.log(1.0 + jnp.exp2(jnp.abs(zm) * (-LOG2E))) + jnp.maximum(zm, 0.0)
                split_ref[blk] = sp.astype(bf16)

    def iteration(i, u):
        stage_c2(*item(i - PIPE_DEPTH))
        stage_b2c1(*item(i - 1), (u - 1) % Z_SLOTS)
        stage_ab1(*item(i), u % Z_SLOTS)

    def body(m, carry):
        for u in range(ITER_PER_TRIP):
            iteration(ITER_PER_TRIP * m + u, u)
        return carry

    n_iter = n_items + PIPE_DEPTH
    full_trips = n_iter // ITER_PER_TRIP
    lax.fori_loop(0, full_trips, body, 0)
    for u in range(n_iter % ITER_PER_TRIP):
        iteration(full_trips * ITER_PER_TRIP + u, u)


def _attention(q, k, v, tri):
    _, bsz, seq, _ = q.shape
    n_qt = seq // TQ
    items = [(hp, qi, t) for hp in range(N_PAIR) for qi in range(n_qt)
             for t in range((qi + 1) * KT_PER_QT - 1, -1, -1)]
    hp_tab, qt, kt = (jnp.asarray([it[n] for it in items], jnp.int32) for n in range(3))
    seq_spec = pl.BlockSpec((N_PAIR, 1, seq, LANES), lambda b, *_: (0, b, 0, 0))
    tri_spec = pl.BlockSpec((TK, TK), lambda b, *_: (0, 0), pipeline_mode=pl.Buffered(1))
    return pl.pallas_call(
        _attn_kernel,
        out_shape=jax.ShapeDtypeStruct((N_PAIR, bsz, seq, LANES), f32),
        grid_spec=pltpu.PrefetchScalarGridSpec(
            num_scalar_prefetch=3,
            grid=(bsz,),
            in_specs=[seq_spec, seq_spec, seq_spec, tri_spec],
            out_specs=seq_spec,
            scratch_shapes=[
                pltpu.VMEM((HEADS_PER_STEP, TQ, LANES), f32),
                pltpu.VMEM((Z_SLOTS, N_BLK, TQ, LANES), f32),
                pltpu.VMEM((N_BLK, TQ, LANES), bf16),
                pltpu.VMEM((N_BLK, TQ, LANES), bf16),
                pltpu.VMEM((KT_PER_QT + 1, BLK_PER_TILE, TQ, LANES), f32),
            ]),
        compiler_params=pltpu.CompilerParams(
            dimension_semantics=("parallel",), vmem_limit_bytes=VMEM_LIMIT),
        name="stickbreak",
    )(hp_tab, qt, kt, q, k, v, tri)


def _ffn_out_kernel(x_ref, yr_ref, ya_ref, ga_ref, wo_hbm, g2_ref, wg_hbm, wu_hbm, wd_hbm,
                    o_ref, h_ref, wg_ref, wu_ref, wd_ref, stage_ref, sem_ref, wo_ref):
    @pl.when(pl.program_id(0) == 0)
    def _():
        for src, dst in ((wo_hbm, wo_ref), (wg_hbm, wg_ref), (wu_hbm, wu_ref), (wd_hbm, wd_ref)):
            _load_as_bf16(src, dst, stage_ref, sem_ref)

    ya = jnp.concatenate([ya_ref[p] for p in range(N_PAIR)], axis=1)
    ya = _rms(ya, ga_ref[...]).astype(bf16)
    x2 = (x_ref[...] + _dot(yr_ref[...], wo_ref[0:D_RNN, :])
          + _dot(ya, wo_ref[D_RNN:D_RNN + D_ATT, :]))
    o_ref[...] = _ffn_tile(x2, g2_ref, wg_ref, wu_ref, wd_ref, h_ref)


def _ffn_out(x2d, yr, ya, ga, w_out, g2, wg, wu, wd):
    n_tok = x2d.shape[0]
    return pl.pallas_call(
        _ffn_out_kernel,
        out_shape=jax.ShapeDtypeStruct((n_tok, D_MODEL), f32),
        grid=(n_tok // TM,),
        in_specs=[_tok_spec(D_MODEL), _tok_spec(D_RNN), _pair_spec(), _const_spec((1, D_ATT)),
                  _hbm_spec()] + _ffn_specs(),
        out_specs=_tok_spec(D_MODEL),
        scratch_shapes=_ffn_scratch() + [pltpu.VMEM((D_RNN + D_ATT, D_MODEL), bf16)],
        compiler_params=pltpu.CompilerParams(
            dimension_semantics=("arbitrary",), vmem_limit_bytes=VMEM_LIMIT),
        name="ffn_out",
    )(x2d, yr, ya, ga, w_out, g2, wg, wu, wd)


def _block_diag(w):
    eye = jnp.eye(RNN_BLOCKS, dtype=w.dtype)
    return jnp.einsum("ncd,nm->ncmd", w, eye).reshape(D_RNN, D_RNN)


def _scan_matrix():
    j = jnp.arange(TK)
    return -(j[:, None] >= j[None, :]).astype(bf16)


def kernel(x, ffn1_norm, ffn1_w_gate, ffn1_w_up, ffn1_w_down, mix_norm, w_in, conv_w, conv_b,
           rg_w_a, rg_b_a, rg_w_x, rg_b_x, rg_lambda, q_norm, k_norm, rnn_out_norm,
           attn_out_norm, w_out, ffn2_norm, ffn2_w_gate, ffn2_w_up, ffn2_w_down):
    bsz, seq, _ = x.shape
    depth = ffn1_norm.shape[0]
    row_head = (jnp.arange(2 * LANES) % LANES) // HEAD_DIM
    seg = (row_head[:, None] == (jnp.arange(LANES) // HEAD_DIM)[None, :]).astype(bf16)
    tri = _scan_matrix()
    x2d = x.reshape(bsz * seq, D_MODEL)
    for l in range(depth):
        x2d, xr, gate, q, k, v = _ffn_in(
            x2d, ffn1_norm[l][None], ffn1_w_gate[l], ffn1_w_up[l], ffn1_w_down[l],
            mix_norm[l][None], w_in[l],
            jnp.tile(q_norm[l], N_HEADS)[None], jnp.tile(k_norm[l], N_HEADS)[None], seg)
        y_rnn = _rglru(
            xr.reshape(bsz, seq, D_RNN), gate.reshape(bsz, seq, D_RNN),
            conv_w[l], conv_b[l][None], _block_diag(rg_w_a[l]).astype(bf16), rg_b_a[l][None],
            _block_diag(rg_w_x[l]).astype(bf16), rg_b_x[l][None], rg_lambda[l][None],
            rnn_out_norm[l][None])
        pairs = (N_PAIR, bsz, seq, LANES)
        y_att = _attention(q.reshape(pairs), k.reshape(pairs), v.reshape(pairs), tri)
        x2d = _ffn_out(x2d, y_rnn.reshape(bsz * seq, D_RNN), y_att.reshape(N_PAIR, bsz * seq, LANES),
                       attn_out_norm[l][None], w_out[l], ffn2_norm[l][None],
                       ffn2_w_gate[l], ffn2_w_up[l], ffn2_w_down[l])
    return x2d.reshape(bsz, seq, D_MODEL)
```

```python
import math

import jax
import jax.numpy as jnp
from jax import lax
from jax.experimental import pallas as pl
from jax.experimental.pallas import tpu as pltpu

D_MODEL = 1024
D_RNN = 512
RNN_BLOCKS = 8
RNN_BW = D_RNN // RNN_BLOCKS
CONV_W = 4
RG_C = 8.0
D_ATT = 512
HEAD_DIM = 64
N_HEADS = D_ATT // HEAD_DIM
D_FF = 2816
N_IN = 2 * D_RNN + 3 * D_ATT
EPS = 1e-6

LANES = 128
SUBLANES = 8
HEADS_PER_STEP = LANES // HEAD_DIM
N_PAIR = D_ATT // LANES
VMEM_LIMIT = 56 * 1024 * 1024

TM = 512
FF_CHUNK = 256
TS = 512
TQ = 512
TK = 256

f32 = jnp.float32
bf16 = jnp.bfloat16


def _rms(xf, g):
    r = lax.rsqrt(jnp.mean(xf * xf, axis=-1, keepdims=True) + EPS)
    return xf * r * g


def _dot(a, b):
    return jnp.dot(a, b, preferred_element_type=f32)


def _split_bf16(x):
    hi = x.astype(bf16)
    lo = (x - hi.astype(f32)).astype(bf16)
    return hi, lo


def _const_spec(shape):
    nd = len(shape)
    return pl.BlockSpec(shape, lambda *_: (0,) * nd, pipeline_mode=pl.Buffered(1))


def _ffn_tile(x, g_ref, wg_ref, wu_ref, wd_ref, h_ref):
    xn = _rms(x, g_ref[...]).astype(bf16)
    for c in range(0, D_FF, FF_CHUNK):
        gate = _dot(xn, wg_ref[:, c:c + FF_CHUNK])
        up = _dot(xn, wu_ref[:, c:c + FF_CHUNK])
        h_ref[:, c:c + FF_CHUNK] = (jax.nn.silu(gate) * up).astype(bf16)
    return x + 0.5 * _dot(h_ref[...], wd_ref[...])


CAST_ROWS = 128
N_STAGE = 4


def _load_as_bf16(src_hbm, dst_ref, stage_ref, sem_ref):
    rows, cols = dst_ref.shape
    n_chunks = rows // CAST_ROWS
    assert n_chunks * CAST_ROWS == rows and cols <= stage_ref.shape[2]

    def chunk_copy(c, slot):
        return pltpu.make_async_copy(src_hbm.at[pl.ds(c * CAST_ROWS, CAST_ROWS), :],
                                     stage_ref.at[slot, :, pl.ds(0, cols)], sem_ref.at[slot])

    for c in range(min(N_STAGE - 1, n_chunks)):
        chunk_copy(c, c).start()

    def body(c, carry):
        slot = c % N_STAGE
        ahead = c + N_STAGE - 1

        @pl.when(ahead < n_chunks)
        def _():
            chunk_copy(ahead, ahead % N_STAGE).start()

        chunk_copy(c, slot).wait()
        dst_rows = pl.ds(pl.multiple_of(c * CAST_ROWS, CAST_ROWS), CAST_ROWS)
        dst_ref[dst_rows, :] = stage_ref[slot, :, pl.ds(0, cols)].astype(bf16)
        return carry

    lax.fori_loop(0, n_chunks, body, 0)


def _hbm_spec():
    return pl.BlockSpec(memory_space=pl.ANY)


def _ffn_specs():
    return [_const_spec((1, D_MODEL)), _hbm_spec(), _hbm_spec(), _hbm_spec()]


def _ffn_scratch():
    return [pltpu.VMEM((TM, D_FF), bf16), pltpu.VMEM((D_MODEL, D_FF), bf16),
            pltpu.VMEM((D_MODEL, D_FF), bf16), pltpu.VMEM((D_FF, D_MODEL), bf16),
            pltpu.VMEM((N_STAGE, CAST_ROWS, D_FF), f32), pltpu.SemaphoreType.DMA((N_STAGE,))]


def _tok_spec(d):
    return pl.BlockSpec((TM, d), lambda i: (i, 0))


def _pair_spec():
    return pl.BlockSpec((N_PAIR, TM, LANES), lambda i: (0, i, 0))


def _head_norm(t, gain_tiled, seg_ref):
    hi, lo = _split_bf16(t * t)
    groups = []
    for g in range(0, t.shape[1], LANES):
        both = jnp.concatenate([hi[:, g:g + LANES], lo[:, g:g + LANES]], axis=1)
        groups.append(_dot(both, seg_ref[...]))
    ss = jnp.concatenate(groups, axis=1)
    r = lax.rsqrt(ss * (1.0 / HEAD_DIM) + EPS)
    return t * r * gain_tiled


def _ffn_in_kernel(x_ref, g1_ref, wg_hbm, wu_hbm, wd_hbm, g_ref, w_hbm, qg_ref, kg_ref, seg_ref,
                   o_ref, xr_ref, gate_ref, q_ref, k_ref, v_ref,
                   h_ref, wg_ref, wu_ref, wd_ref, stage_ref, sem_ref, w_ref):
    @pl.when(pl.program_id(0) == 0)
    def _():
        for src, dst in ((wg_hbm, wg_ref), (wu_hbm, wu_ref), (wd_hbm, wd_ref), (w_hbm, w_ref)):
            _load_as_bf16(src, dst, stage_ref, sem_ref)

    x1 = _ffn_tile(x_ref[...], g1_ref, wg_ref, wu_ref, wd_ref, h_ref)
    o_ref[...] = x1
    h = _rms(x1, g_ref[...]).astype(bf16)
    xr_ref[...] = _dot(h, w_ref[:, 0:D_RNN])
    gate_ref[...] = _dot(h, w_ref[:, D_RNN:2 * D_RNN])
    o = 2 * D_RNN
    q = _dot(h, w_ref[:, o:o + D_ATT])
    k = _dot(h, w_ref[:, o + D_ATT:o + 2 * D_ATT])
    v = _dot(h, w_ref[:, o + 2 * D_ATT:o + 3 * D_ATT])
    scale = 1.0 / math.sqrt(HEAD_DIM)
    qn = (_head_norm(q, qg_ref[...], seg_ref) * scale).astype(bf16)
    kn = _head_norm(k, kg_ref[...], seg_ref).astype(bf16)
    vb = v.astype(bf16)
    for p in range(N_PAIR):
        lanes = slice(p * LANES, (p + 1) * LANES)
        q_ref[p] = qn[:, lanes]
        k_ref[p] = kn[:, lanes]
        v_ref[p] = vb[:, lanes]


def _ffn_in(x2d, g1, wg, wu, wd, g, w_in, qg, kg, seg):
    n_tok = x2d.shape[0]
    return pl.pallas_call(
        _ffn_in_kernel,
        out_shape=(jax.ShapeDtypeStruct((n_tok, D_MODEL), f32),
                   jax.ShapeDtypeStruct((n_tok, D_RNN), f32),
                   jax.ShapeDtypeStruct((n_tok, D_RNN), f32),
                   jax.ShapeDtypeStruct((N_PAIR, n_tok, LANES), bf16),
                   jax.ShapeDtypeStruct((N_PAIR, n_tok, LANES), bf16),
                   jax.ShapeDtypeStruct((N_PAIR, n_tok, LANES), bf16)),
        grid=(n_tok // TM,),
        in_specs=[_tok_spec(D_MODEL)] + _ffn_specs() + [
            _const_spec((1, D_MODEL)), _hbm_spec(),
            _const_spec((1, D_ATT)), _const_spec((1, D_ATT)), _const_spec((2 * LANES, LANES))],
        out_specs=(_tok_spec(D_MODEL), _tok_spec(D_RNN), _tok_spec(D_RNN),
                   _pair_spec(), _pair_spec(), _pair_spec()),
        scratch_shapes=_ffn_scratch() + [pltpu.VMEM((D_MODEL, N_IN), bf16)],
        compiler_params=pltpu.CompilerParams(
            dimension_semantics=("arbitrary",), vmem_limit_bytes=VMEM_LIMIT),
        name="ffn_in",
    )(x2d, g1, wg, wu, wd, g, w_in, qg, kg, seg)


HIST = 8


def _rglru_kernel(xr_ref, gate_ref, cw_ref, cb_ref, wa_ref, ba_ref, wx_ref, bx_ref,
                  lam_ref, g_ref, o_ref, xbuf_ref, hc_ref):
    @pl.when(pl.program_id(1) == 0)
    def _():
        xbuf_ref[0:HIST, :] = jnp.zeros((HIST, D_RNN), f32)
        hc_ref[...] = jnp.zeros((1, D_RNN), f32)

    x = xr_ref[0]
    xbuf_ref[HIST:HIST + TS, :] = x
    xc = cb_ref[...] + cw_ref[CONV_W - 1:CONV_W, :] * x
    for j in range(CONV_W - 1):
        off = HIST - (CONV_W - 1) + j
        xc = xc + cw_ref[j:j + 1, :] * xbuf_ref[off:off + TS, :]
    xbuf_ref[0:HIST, :] = x[TS - HIST:TS, :]

    xcb = xc.astype(bf16)
    r = jax.nn.sigmoid(_dot(xcb, wa_ref[...]) + ba_ref[...])
    i = jax.nn.sigmoid(_dot(xcb, wx_ref[...]) + bx_ref[...])
    lam = lam_ref[...]
    log_sig_lam = jnp.minimum(lam, 0.0) - jnp.log1p(jnp.exp(-jnp.abs(lam)))
    log_a = RG_C * r * log_sig_lam
    a = jnp.exp(log_a)
    th = jnp.tanh(log_a)
    mult = jnp.sqrt(-2.0 * th / (1.0 - th))
    b = mult * (i * xc)

    n_grp = TS // SUBLANES
    a = a.reshape(n_grp, SUBLANES, D_RNN)
    b = b.reshape(n_grp, SUBLANES, D_RNN)
    row = lax.broadcasted_iota(jnp.int32, (n_grp, SUBLANES, D_RNN), 1)
    d = 1
    while d < SUBLANES:
        a_sh = pltpu.roll(a, d, axis=1)
        b_sh = pltpu.roll(b, d, axis=1)
        m = row >= d
        b = jnp.where(m, a * b_sh, 0.0) + b
        a = jnp.where(m, a * a_sh, a)
        d *= 2
    carry = jnp.broadcast_to(hc_ref[...], (SUBLANES, D_RNN))
    groups = []
    for g in range(n_grp):
        h_g = a[g] * carry + b[g]
        groups.append(h_g)
        carry = jnp.broadcast_to(h_g[SUBLANES - 1:SUBLANES, :], (SUBLANES, D_RNN))
    h = jnp.concatenate(groups, axis=0)
    hc_ref[...] = groups[-1][SUBLANES - 1:SUBLANES, :]

    y = h * jax.nn.gelu(gate_ref[0])
    o_ref[0] = _rms(y, g_ref[...]).astype(bf16)


def _rglru(xr, gate, cw, cb, wa, ba, wx, bx, lam, g):
    bsz, seq, _ = xr.shape
    seq_spec = pl.BlockSpec((1, TS, D_RNN), lambda b, s: (b, s, 0))
    vec = _const_spec((1, D_RNN))
    return pl.pallas_call(
        _rglru_kernel,
        out_shape=jax.ShapeDtypeStruct((bsz, seq, D_RNN), bf16),
        grid=(bsz, seq // TS),
        in_specs=[seq_spec, seq_spec, _const_spec((CONV_W, D_RNN)), vec,
                  _const_spec((D_RNN, D_RNN)), vec, _const_spec((D_RNN, D_RNN)), vec, vec, vec],
        out_specs=seq_spec,
        scratch_shapes=[pltpu.VMEM((HIST + TS, D_RNN), f32), pltpu.VMEM((1, D_RNN), f32)],
        compiler_params=pltpu.CompilerParams(
            dimension_semantics=("parallel", "arbitrary"), vmem_limit_bytes=VMEM_LIMIT),
        name="rglru",
    )(xr, gate, cw, cb, wa, ba, wx, bx, lam, g)


PIPE_DEPTH = 2
BLK_PER_TILE = TK // LANES
KT_PER_QT = TQ // TK
N_BLK = HEADS_PER_STEP * BLK_PER_TILE
ITER_PER_TRIP = 16
Z_SLOTS = 2
LOG2E = 1.4426950408889634


MASKED = -1e30


def _attn_kernel(hp_ref, qt_ref, kt_ref, q_ref, k_ref, v_ref, tri_ref, o_ref,
                 c_ref, z_ref, split_ref, w_ref, pen_ref):
    n_items = qt_ref.shape[0]
    heads = range(HEADS_PER_STEP)
    for ref in (o_ref, c_ref, z_ref, split_ref, w_ref):
        ref[...] = jnp.zeros_like(ref)
    pen_ref[0] = jnp.zeros_like(pen_ref[0])
    for j in range(KT_PER_QT):
        for b in range(BLK_PER_TILE):
            key = lax.broadcasted_iota(jnp.int32, (TQ, LANES), 1) + (j * TK + b * LANES)
            query = lax.broadcasted_iota(jnp.int32, (TQ, LANES), 0)
            pen_ref[j + 1, b] = jnp.where(key < query, 0.0, MASKED)
    lane_head = lax.broadcasted_iota(jnp.int32, (TK, LANES), 1) // HEAD_DIM

    def item(j):
        j = jnp.clip(j, 0, n_items - 1)
        return hp_ref[j], qt_ref[j], kt_ref[j]

    def diag_tile(qi):
        return (qi + 1) * KT_PER_QT - 1

    def per_head_rows(ref, hp, t):
        x = ref[hp, 0, pl.ds(pl.multiple_of(t * TK, TK), TK), :]
        return jnp.concatenate([jnp.where(lane_head == h, x, jnp.zeros_like(x)) for h in heads], axis=0)

    def stage_c2(hp, qi, t):
        rows = pl.ds(pl.multiple_of(qi * TQ, TQ), TQ)
        acc = jnp.where(t == diag_tile(qi), 0.0, o_ref[hp, 0, rows, :])
        w = jnp.concatenate([w_ref[blk] for blk in range(N_BLK)], axis=1)
        o_ref[hp, 0, rows, :] = acc + _dot(w, per_head_rows(v_ref, hp, t))

    def stage_b2c1(hp, qi, t, slot):
        for h in heads:
            first = h * BLK_PER_TILE
            parts = [split_ref[blk] for blk in range(first, first + BLK_PER_TILE)]
            r = _dot(jnp.concatenate(parts, axis=1), tri_ref[...])
            c = jnp.where(t == diag_tile(qi), 0.0, c_ref[h])
            for b in range(BLK_PER_TILE):
                s = z_ref[slot, first + b] + r[:, b * LANES:(b + 1) * LANES] + c
                w_ref[first + b] = jnp.exp(s).astype(bf16)
            c_ref[h] = c + jnp.broadcast_to(r[:, 0:1], (TQ, LANES))

    def stage_ab1(hp, qi, t, slot):
        q = q_ref[hp, 0, pl.ds(pl.multiple_of(qi * TQ, TQ), TQ), :]
        z = lax.dot_general(q, per_head_rows(k_ref, hp, t), (((1,), (1,)), ((), ())),
                            preferred_element_type=f32)
        kind = jnp.maximum(t - qi * KT_PER_QT + 1, 0)
        for b in range(BLK_PER_TILE):
            pen = pen_ref[kind, b]
            for h in heads:
                blk = h * BLK_PER_TILE + b
                zm = z[:, blk * LANES:(blk + 1) * LANES] + pen
                z_ref[slot, blk] = zm
                sp = jnp.log(1.0 + jnp.exp2(jnp.abs(zm) * (-LOG2E))) + jnp.maximum(zm, 0.0)
                split_ref[blk] = sp.astype(bf16)

    def iteration(i, u):
        stage_c2(*item(i - PIPE_DEPTH))
        stage_b2c1(*item(i - 1), (u - 1) % Z_SLOTS)
        stage_ab1(*item(i), u % Z_SLOTS)

    def body(m, carry):
        for u in range(ITER_PER_TRIP):
            iteration(ITER_PER_TRIP * m + u, u)
        return carry

    n_iter = n_items + PIPE_DEPTH
    full_trips = n_iter // ITER_PER_TRIP
    lax.fori_loop(0, full_trips, body, 0)
    for u in range(n_iter % ITER_PER_TRIP):
        iteration(full_trips * ITER_PER_TRIP + u, u)


def _attention(q, k, v, tri):
    _, bsz, seq, _ = q.shape
    n_qt = seq // TQ
    items = [(hp, qi, t) for hp in range(N_PAIR) for qi in range(n_qt)
             for t in range((qi + 1) * KT_PER_QT - 1, -1, -1)]
    hp_tab, qt, kt = (jnp.asarray([it[n] for it in items], jnp.int32) for n in range(3))
    seq_spec = pl.BlockSpec((N_PAIR, 1, seq, LANES), lambda b, *_: (0, b, 0, 0))
    tri_spec = pl.BlockSpec((TK, TK), lambda b, *_: (0, 0), pipeline_mode=pl.Buffered(1))
    return pl.pallas_call(
        _attn_kernel,
        out_shape=jax.ShapeDtypeStruct((N_PAIR, bsz, seq, LANES), f32),
        grid_spec=pltpu.PrefetchScalarGridSpec(
            num_scalar_prefetch=3,
            grid=(bsz,),
            in_specs=[seq_spec, seq_spec, seq_spec, tri_spec],
            out_specs=seq_spec,
            scratch_shapes=[
                pltpu.VMEM((HEADS_PER_STEP, TQ, LANES), f32),
                pltpu.VMEM((Z_SLOTS, N_BLK, TQ, LANES), f32),
                pltpu.VMEM((N_BLK, TQ, LANES), bf16),
                pltpu.VMEM((N_BLK, TQ, LANES), bf16),
                pltpu.VMEM((KT_PER_QT + 1, BLK_PER_TILE, TQ, LANES), f32),
            ]),
        compiler_params=pltpu.CompilerParams(
            dimension_semantics=("parallel",), vmem_limit_bytes=VMEM_LIMIT),
        name="stickbreak",
    )(hp_tab, qt, kt, q, k, v, tri)


def _ffn_out_kernel(x_ref, yr_ref, ya_ref, ga_ref, wo_hbm, g2_ref, wg_hbm, wu_hbm, wd_hbm,
                    o_ref, h_ref, wg_ref, wu_ref, wd_ref, stage_ref, sem_ref, wo_ref):
    @pl.when(pl.program_id(0) == 0)
    def _():
        for src, dst in ((wo_hbm, wo_ref), (wg_hbm, wg_ref), (wu_hbm, wu_ref), (wd_hbm, wd_ref)):
            _load_as_bf16(src, dst, stage_ref, sem_ref)

    ya = jnp.concatenate([ya_ref[p] for p in range(N_PAIR)], axis=1)
    ya = _rms(ya, ga_ref[...]).astype(bf16)
    x2 = (x_ref[...] + _dot(yr_ref[...], wo_ref[0:D_RNN, :])
          + _dot(ya, wo_ref[D_RNN:D_RNN + D_ATT, :]))
    o_ref[...] = _ffn_tile(x2, g2_ref, wg_ref, wu_ref, wd_ref, h_ref)


def _ffn_out(x2d, yr, ya, ga, w_out, g2, wg, wu, wd):
    n_tok = x2d.shape[0]
    return pl.pallas_call(
        _ffn_out_kernel,
        out_shape=jax.ShapeDtypeStruct((n_tok, D_MODEL), f32),
        grid=(n_tok // TM,),
        in_specs=[_tok_spec(D_MODEL), _tok_spec(D_RNN), _pair_spec(), _const_spec((1, D_ATT)),
                  _hbm_spec()] + _ffn_specs(),
        out_specs=_tok_spec(D_MODEL),
        scratch_shapes=_ffn_scratch() + [pltpu.VMEM((D_RNN + D_ATT, D_MODEL), bf16)],
        compiler_params=pltpu.CompilerParams(
            dimension_semantics=("arbitrary",), vmem_limit_bytes=VMEM_LIMIT),
        name="ffn_out",
    )(x2d, yr, ya, ga, w_out, g2, wg, wu, wd)


def _block_diag(w):
    eye = jnp.eye(RNN_BLOCKS, dtype=w.dtype)
    return jnp.einsum("ncd,nm->ncmd", w, eye).reshape(D_RNN, D_RNN)


def _scan_matrix():
    j = jnp.arange(TK)
    return -(j[:, None] >= j[None, :]).astype(bf16)


def kernel(x, ffn1_norm, ffn1_w_gate, ffn1_w_up, ffn1_w_down, mix_norm, w_in, conv_w, conv_b,
           rg_w_a, rg_b_a, rg_w_x, rg_b_x, rg_lambda, q_norm, k_norm, rnn_out_norm,
           attn_out_norm, w_out, ffn2_norm, ffn2_w_gate, ffn2_w_up, ffn2_w_down):
    bsz, seq, _ = x.shape
    depth = ffn1_norm.shape[0]
    row_head = (jnp.arange(2 * LANES) % LANES) // HEAD_DIM
    seg = (row_head[:, None] == (jnp.arange(LANES) // HEAD_DIM)[None, :]).astype(bf16)
    tri = _scan_matrix()
    x2d = x.reshape(bsz * seq, D_MODEL)
    for l in range(depth):
        x2d, xr, gate, q, k, v = _ffn_in(
            x2d, ffn1_norm[l][None], ffn1_w_gate[l], ffn1_w_up[l], ffn1_w_down[l],
            mix_norm[l][None], w_in[l],
            jnp.tile(q_norm[l], N_HEADS)[None], jnp.tile(k_norm[l], N_HEADS)[None], seg)
        y_rnn = _rglru(
            xr.reshape(bsz, seq, D_RNN), gate.reshape(bsz, seq, D_RNN),
            conv_w[l], conv_b[l][None], _block_diag(rg_w_a[l]).astype(bf16), rg_b_a[l][None],
            _block_diag(rg_w_x[l]).astype(bf16), rg_b_x[l][None], rg_lambda[l][None],
            rnn_out_norm[l][None])
        pairs = (N_PAIR, bsz, seq, LANES)
        y_att = _attention(q.reshape(pairs), k.reshape(pairs), v.reshape(pairs), tri)
        x2d = _ffn_out(x2d, y_rnn.reshape(bsz * seq, D_RNN), y_att.reshape(N_PAIR, bsz * seq, LANES),
                       attn_out_norm[l][None], w_out[l], ffn2_norm[l][None],
                       ffn2_w_gate[l], ffn2_w_up[l], ffn2_w_down[l])
    return x2d.reshape(bsz, seq, D_MODEL)
```

```python
import math

import jax
import jax.numpy as jnp
from jax import lax
from jax.experimental import pallas as pl
from jax.experimental.pallas import tpu as pltpu

D_MODEL = 1024
D_RNN = 512
RNN_BLOCKS = 8
RNN_BW = D_RNN // RNN_BLOCKS
CONV_W = 4
RG_C = 8.0
D_ATT = 512
HEAD_DIM = 64
N_HEADS = D_ATT // HEAD_DIM
D_FF = 2816
N_IN = 2 * D_RNN + 3 * D_ATT
EPS = 1e-6

LANES = 128
SUBLANES = 8
HEADS_PER_STEP = LANES // HEAD_DIM
N_PAIR = D_ATT // LANES
VMEM_LIMIT = 56 * 1024 * 1024

TM = 512
FF_CHUNK = 256
TS = 512
TQ = 512
TK = 256

f32 = jnp.float32
bf16 = jnp.bfloat16


def _rms(xf, g):
    r = lax.rsqrt(jnp.mean(xf * xf, axis=-1, keepdims=True) + EPS)
    return xf * r * g


def _dot(a, b):
    return jnp.dot(a, b, preferred_element_type=f32)


def _split_bf16(x):
    hi = x.astype(bf16)
    lo = (x - hi.astype(f32)).astype(bf16)
    return hi, lo


def _const_spec(shape):
    nd = len(shape)
    return pl.BlockSpec(shape, lambda *_: (0,) * nd, pipeline_mode=pl.Buffered(1))


def _ffn_tile(x, g_ref, wg_ref, wu_ref, wd_ref, h_ref):
    xn = _rms(x, g_ref[...]).astype(bf16)
    for c in range(0, D_FF, FF_CHUNK):
        gate = _dot(xn, wg_ref[:, c:c + FF_CHUNK])
        up = _dot(xn, wu_ref[:, c:c + FF_CHUNK])
        h_ref[:, c:c + FF_CHUNK] = (jax.nn.silu(gate) * up).astype(bf16)
    return x + 0.5 * _dot(h_ref[...], wd_ref[...])


CAST_ROWS = 128
N_STAGE = 6


def _load_as_bf16(src_hbm, dst_ref, stage_ref, sem_ref):
    rows, cols = dst_ref.shape
    n_chunks = rows // CAST_ROWS
    assert n_chunks * CAST_ROWS == rows and cols <= stage_ref.shape[2]

    def chunk_copy(c, slot):
        return pltpu.make_async_copy(src_hbm.at[pl.ds(c * CAST_ROWS, CAST_ROWS), :],
                                     stage_ref.at[slot, :, pl.ds(0, cols)], sem_ref.at[slot])

    for c in range(min(N_STAGE - 1, n_chunks)):
        chunk_copy(c, c).start()

    def body(c, carry):
        slot = c % N_STAGE
        ahead = c + N_STAGE - 1

        @pl.when(ahead < n_chunks)
        def _():
            chunk_copy(ahead, ahead % N_STAGE).start()

        chunk_copy(c, slot).wait()
        dst_rows = pl.ds(pl.multiple_of(c * CAST_ROWS, CAST_ROWS), CAST_ROWS)
        dst_ref[dst_rows, :] = stage_ref[slot, :, pl.ds(0, cols)].astype(bf16)
        return carry

    lax.fori_loop(0, n_chunks, body, 0)


def _hbm_spec():
    return pl.BlockSpec(memory_space=pl.ANY)


def _ffn_specs():
    return [_const_spec((1, D_MODEL)), _hbm_spec(), _hbm_spec(), _hbm_spec()]


def _ffn_scratch():
    return [pltpu.VMEM((TM, D_FF), bf16), pltpu.VMEM((D_MODEL, D_FF), bf16),
            pltpu.VMEM((D_MODEL, D_FF), bf16), pltpu.VMEM((D_FF, D_MODEL), bf16),
            pltpu.VMEM((N_STAGE, CAST_ROWS, D_FF), f32), pltpu.SemaphoreType.DMA((N_STAGE,))]


def _tok_spec(d):
    return pl.BlockSpec((TM, d), lambda i: (i, 0))


def _pair_spec():
    return pl.BlockSpec((N_PAIR, TM, LANES), lambda i: (0, i, 0))


def _head_norm(t, gain_tiled, seg_ref):
    hi, lo = _split_bf16(t * t)
    groups = []
    for g in range(0, t.shape[1], LANES):
        both = jnp.concatenate([hi[:, g:g + LANES], lo[:, g:g + LANES]], axis=1)
        groups.append(_dot(both, seg_ref[...]))
    ss = jnp.concatenate(groups, axis=1)
    r = lax.rsqrt(ss * (1.0 / HEAD_DIM) + EPS)
    return t * r * gain_tiled


def _ffn_in_kernel(x_ref, g1_ref, wg_hbm, wu_hbm, wd_hbm, g_ref, w_hbm, qg_ref, kg_ref, seg_ref,
                   o_ref, xr_ref, gate_ref, q_ref, k_ref, v_ref,
                   h_ref, wg_ref, wu_ref, wd_ref, stage_ref, sem_ref, w_ref):
    @pl.when(pl.program_id(0) == 0)
    def _():
        for src, dst in ((wg_hbm, wg_ref), (wu_hbm, wu_ref), (wd_hbm, wd_ref), (w_hbm, w_ref)):
            _load_as_bf16(src, dst, stage_ref, sem_ref)

    x1 = _ffn_tile(x_ref[...], g1_ref, wg_ref, wu_ref, wd_ref, h_ref)
    o_ref[...] = x1
    h = _rms(x1, g_ref[...]).astype(bf16)
    xr_ref[...] = _dot(h, w_ref[:, 0:D_RNN])
    gate_ref[...] = _dot(h, w_ref[:, D_RNN:2 * D_RNN])
    o = 2 * D_RNN
    q = _dot(h, w_ref[:, o:o + D_ATT])
    k = _dot(h, w_ref[:, o + D_ATT:o + 2 * D_ATT])
    v = _dot(h, w_ref[:, o + 2 * D_ATT:o + 3 * D_ATT])
    scale = 1.0 / math.sqrt(HEAD_DIM)
    qn = (_head_norm(q, qg_ref[...], seg_ref) * scale).astype(bf16)
    kn = _head_norm(k, kg_ref[...], seg_ref).astype(bf16)
    vb = v.astype(bf16)
    for p in range(N_PAIR):
        lanes = slice(p * LANES, (p + 1) * LANES)
        q_ref[p] = qn[:, lanes]
        k_ref[p] = kn[:, lanes]
        v_ref[p] = vb[:, lanes]


def _ffn_in(x2d, g1, wg, wu, wd, g, w_in, qg, kg, seg):
    n_tok = x2d.shape[0]
    return pl.pallas_call(
        _ffn_in_kernel,
        out_shape=(jax.ShapeDtypeStruct((n_tok, D_MODEL), f32),
                   jax.ShapeDtypeStruct((n_tok, D_RNN), f32),
                   jax.ShapeDtypeStruct((n_tok, D_RNN), f32),
                   jax.ShapeDtypeStruct((N_PAIR, n_tok, LANES), bf16),
                   jax.ShapeDtypeStruct((N_PAIR, n_tok, LANES), bf16),
                   jax.ShapeDtypeStruct((N_PAIR, n_tok, LANES), bf16)),
        grid=(n_tok // TM,),
        in_specs=[_tok_spec(D_MODEL)] + _ffn_specs() + [
            _const_spec((1, D_MODEL)), _hbm_spec(),
            _const_spec((1, D_ATT)), _const_spec((1, D_ATT)), _const_spec((2 * LANES, LANES))],
        out_specs=(_tok_spec(D_MODEL), _tok_spec(D_RNN), _tok_spec(D_RNN),
                   _pair_spec(), _pair_spec(), _pair_spec()),
        scratch_shapes=_ffn_scratch() + [pltpu.VMEM((D_MODEL, N_IN), bf16)],
        compiler_params=pltpu.CompilerParams(
            dimension_semantics=("arbitrary",), vmem_limit_bytes=VMEM_LIMIT),
        name="ffn_in",
    )(x2d, g1, wg, wu, wd, g, w_in, qg, kg, seg)


HIST = 8


def _rglru_kernel(xr_ref, gate_ref, cw_ref, cb_ref, wa_ref, ba_ref, wx_ref, bx_ref,
                  lam_ref, g_ref, o_ref, xbuf_ref, hc_ref):
    @pl.when(pl.program_id(1) == 0)
    def _():
        xbuf_ref[0:HIST, :] = jnp.zeros((HIST, D_RNN), f32)
        hc_ref[...] = jnp.zeros((1, D_RNN), f32)

    x = xr_ref[0]
    xbuf_ref[HIST:HIST + TS, :] = x
    xc = cb_ref[...] + cw_ref[CONV_W - 1:CONV_W, :] * x
    for j in range(CONV_W - 1):
        off = HIST - (CONV_W - 1) + j
        xc = xc + cw_ref[j:j + 1, :] * xbuf_ref[off:off + TS, :]
    xbuf_ref[0:HIST, :] = x[TS - HIST:TS, :]

    xcb = xc.astype(bf16)
    r = jax.nn.sigmoid(_dot(xcb, wa_ref[...]) + ba_ref[...])
    i = jax.nn.sigmoid(_dot(xcb, wx_ref[...]) + bx_ref[...])
    lam = lam_ref[...]
    log_sig_lam = jnp.minimum(lam, 0.0) - jnp.log1p(jnp.exp(-jnp.abs(lam)))
    log_a = RG_C * r * log_sig_lam
    a = jnp.exp(log_a)
    th = jnp.tanh(log_a)
    mult = jnp.sqrt(-2.0 * th / (1.0 - th))
    b = mult * (i * xc)

    n_grp = TS // SUBLANES
    a = a.reshape(n_grp, SUBLANES, D_RNN)
    b = b.reshape(n_grp, SUBLANES, D_RNN)
    row = lax.broadcasted_iota(jnp.int32, (n_grp, SUBLANES, D_RNN), 1)
    d = 1
    while d < SUBLANES:
        a_sh = pltpu.roll(a, d, axis=1)
        b_sh = pltpu.roll(b, d, axis=1)
        m = row >= d
        b = jnp.where(m, a * b_sh, 0.0) + b
        a = jnp.where(m, a * a_sh, a)
        d *= 2
    carry = jnp.broadcast_to(hc_ref[...], (SUBLANES, D_RNN))
    groups = []
    for g in range(n_grp):
        h_g = a[g] * carry + b[g]
        groups.append(h_g)
        carry = jnp.broadcast_to(h_g[SUBLANES - 1:SUBLANES, :], (SUBLANES, D_RNN))
    h = jnp.concatenate(groups, axis=0)
    hc_ref[...] = groups[-1][SUBLANES - 1:SUBLANES, :]

    y = h * jax.nn.gelu(gate_ref[0])
    o_ref[0] = _rms(y, g_ref[...]).astype(bf16)


def _rglru(xr, gate, cw, cb, wa, ba, wx, bx, lam, g):
    bsz, seq, _ = xr.shape
    seq_spec = pl.BlockSpec((1, TS, D_RNN), lambda b, s: (b, s, 0))
    vec = _const_spec((1, D_RNN))
    return pl.pallas_call(
        _rglru_kernel,
        out_shape=jax.ShapeDtypeStruct((bsz, seq, D_RNN), bf16),
        grid=(bsz, seq // TS),
        in_specs=[seq_spec, seq_spec, _const_spec((CONV_W, D_RNN)), vec,
                  _const_spec((D_RNN, D_RNN)), vec, _const_spec((D_RNN, D_RNN)), vec, vec, vec],
        out_specs=seq_spec,
        scratch_shapes=[pltpu.VMEM((HIST + TS, D_RNN), f32), pltpu.VMEM((1, D_RNN), f32)],
        compiler_params=pltpu.CompilerParams(
            dimension_semantics=("parallel", "arbitrary"), vmem_limit_bytes=VMEM_LIMIT),
        name="rglru",
    )(xr, gate, cw, cb, wa, ba, wx, bx, lam, g)


PIPE_DEPTH = 2
BLK_PER_TILE = TK // LANES
KT_PER_QT = TQ // TK
N_BLK = HEADS_PER_STEP * BLK_PER_TILE
ITER_PER_TRIP = 16
Z_SLOTS = 2
LOG2E = 1.4426950408889634


MASKED = -1e30


def _attn_kernel(hp_ref, qt_ref, kt_ref, q_ref, k_ref, v_ref, tri_ref, o_ref,
                 c_ref, z_ref, split_ref, w_ref, pen_ref):
    n_items = qt_ref.shape[0]
    heads = range(HEADS_PER_STEP)
    for ref in (o_ref, c_ref, z_ref, split_ref, w_ref):
        ref[...] = jnp.zeros_like(ref)
    pen_ref[0] = jnp.zeros_like(pen_ref[0])
    for j in range(KT_PER_QT):
        for b in range(BLK_PER_TILE):
            key = lax.broadcasted_iota(jnp.int32, (TQ, LANES), 1) + (j * TK + b * LANES)
            query = lax.broadcasted_iota(jnp.int32, (TQ, LANES), 0)
            pen_ref[j + 1, b] = jnp.where(key < query, 0.0, MASKED)
    lane_head = lax.broadcasted_iota(jnp.int32, (TK, LANES), 1) // HEAD_DIM

    def item(j):
        j = jnp.clip(j, 0, n_items - 1)
        return hp_ref[j], qt_ref[j], kt_ref[j]

    def diag_tile(qi):
        return (qi + 1) * KT_PER_QT - 1

    def per_head_rows(ref, hp, t):
        x = ref[hp, 0, pl.ds(pl.multiple_of(t * TK, TK), TK), :]
        return jnp.concatenate([jnp.where(lane_head == h, x, jnp.zeros_like(x)) for h in heads], axis=0)

    def stage_c2(hp, qi, t):
        rows = pl.ds(pl.multiple_of(qi * TQ, TQ), TQ)
        acc = jnp.where(t == diag_tile(qi), 0.0, o_ref[hp, 0, rows, :])
        w = jnp.concatenate([w_ref[blk] for blk in range(N_BLK)], axis=1)
        o_ref[hp, 0, rows, :] = acc + _dot(w, per_head_rows(v_ref, hp, t))

    def stage_b2c1(hp, qi, t, slot):
        for h in heads:
            first = h * BLK_PER_TILE
            parts = [split_ref[blk] for blk in range(first, first + BLK_PER_TILE)]
            r = _dot(jnp.concatenate(parts, axis=1), tri_ref[...])
            c = jnp.where(t == diag_tile(qi), 0.0, c_ref[h])
            for b in range(BLK_PER_TILE):
                s = z_ref[slot, first + b] + r[:, b * LANES:(b + 1) * LANES] + c
                w_ref[first + b] = jnp.exp(s).astype(bf16)
            c_ref[h] = c + jnp.broadcast_to(r[:, 0:1], (TQ, LANES))

    def stage_ab1(hp, qi, t, slot):
        q = q_ref[hp, 0, pl.ds(pl.multiple_of(qi * TQ, TQ), TQ), :]
        z = lax.dot_general(q, per_head_rows(k_ref, hp, t), (((1,), (1,)), ((), ())),
                            preferred_element_type=f32)
        kind = jnp.maximum(t - qi * KT_PER_QT + 1, 0)
        for b in range(BLK_PER_TILE):
            pen = pen_ref[kind, b]
            for h in heads:
                blk = h * BLK_PER_TILE + b
                zm = z[:, blk * LANES:(blk + 1) * LANES] + pen
                z_ref[slot, blk] = zm
                sp = jnp.log(1.0 + jnp.exp2(jnp.abs(zm) * (-LOG2E))) + jnp.maximum(zm, 0.0)
                split_ref[blk] = sp.astype(bf16)

    def iteration(i, u):
        stage_c2(*item(i - PIPE_DEPTH))
        stage_b2c1(*item(i - 1), (u - 1) % Z_SLOTS)
        stage_ab1(*item(i), u % Z_SLOTS)

    def body(m, carry):
        for u in range(ITER_PER_TRIP):
            iteration(ITER_PER_TRIP * m + u, u)
        return carry

    n_iter = n_items + PIPE_DEPTH
    full_trips = n_iter // ITER_PER_TRIP
    lax.fori_loop(0, full_trips, body, 0)
    for u in range(n_iter % ITER_PER_TRIP):
        iteration(full_trips * ITER_PER_TRIP + u, u)


def _attention(q, k, v, tri):
    _, bsz, seq, _ = q.shape
    n_qt = seq // TQ
    items = [(hp, qi, t) for hp in range(N_PAIR) for qi in range(n_qt)
             for t in range((qi + 1) * KT_PER_QT - 1, -1, -1)]
    hp_tab, qt, kt = (jnp.asarray([it[n] for it in items], jnp.int32) for n in range(3))
    seq_spec = pl.BlockSpec((N_PAIR, 1, seq, LANES), lambda b, *_: (0, b, 0, 0))
    tri_spec = pl.BlockSpec((TK, TK), lambda b, *_: (0, 0), pipeline_mode=pl.Buffered(1))
    return pl.pallas_call(
        _attn_kernel,
        out_shape=jax.ShapeDtypeStruct((N_PAIR, bsz, seq, LANES), f32),
        grid_spec=pltpu.PrefetchScalarGridSpec(
            num_scalar_prefetch=3,
            grid=(bsz,),
            in_specs=[seq_spec, seq_spec, seq_spec, tri_spec],
            out_specs=seq_spec,
            scratch_shapes=[
                pltpu.VMEM((HEADS_PER_STEP, TQ, LANES), f32),
                pltpu.VMEM((Z_SLOTS, N_BLK, TQ, LANES), f32),
                pltpu.VMEM((N_BLK, TQ, LANES), bf16),
                pltpu.VMEM((N_BLK, TQ, LANES), bf16),
                pltpu.VMEM((KT_PER_QT + 1, BLK_PER_TILE, TQ, LANES), f32),
            ]),
        compiler_params=pltpu.CompilerParams(
            dimension_semantics=("parallel",), vmem_limit_bytes=VMEM_LIMIT),
        name="stickbreak",
    )(hp_tab, qt, kt, q, k, v, tri)


def _ffn_out_kernel(x_ref, yr_ref, ya_ref, ga_ref, wo_hbm, g2_ref, wg_hbm, wu_hbm, wd_hbm,
                    o_ref, h_ref, wg_ref, wu_ref, wd_ref, stage_ref, sem_ref, wo_ref):
    @pl.when(pl.program_id(0) == 0)
    def _():
        for src, dst in ((wo_hbm, wo_ref), (wg_hbm, wg_ref), (wu_hbm, wu_ref), (wd_hbm, wd_ref)):
            _load_as_bf16(src, dst, stage_ref, sem_ref)

    ya = jnp.concatenate([ya_ref[p] for p in range(N_PAIR)], axis=1)
    ya = _rms(ya, ga_ref[...]).astype(bf16)
    x2 = (x_ref[...] + _dot(yr_ref[...], wo_ref[0:D_RNN, :])
          + _dot(ya, wo_ref[D_RNN:D_RNN + D_ATT, :]))
    o_ref[...] = _ffn_tile(x2, g2_ref, wg_ref, wu_ref, wd_ref, h_ref)


def _ffn_out(x2d, yr, ya, ga, w_out, g2, wg, wu, wd):
    n_tok = x2d.shape[0]
    return pl.pallas_call(
        _ffn_out_kernel,
        out_shape=jax.ShapeDtypeStruct((n_tok, D_MODEL), f32),
        grid=(n_tok // TM,),
        in_specs=[_tok_spec(D_MODEL), _tok_spec(D_RNN), _pair_spec(), _const_spec((1, D_ATT)),
                  _hbm_spec()] + _ffn_specs(),
        out_specs=_tok_spec(D_MODEL),
        scratch_shapes=_ffn_scratch() + [pltpu.VMEM((D_RNN + D_ATT, D_MODEL), bf16)],
        compiler_params=pltpu.CompilerParams(
            dimension_semantics=("arbitrary",), vmem_limit_bytes=VMEM_LIMIT),
        name="ffn_out",
    )(x2d, yr, ya, ga, w_out, g2, wg, wu, wd)


def _block_diag(w):
    eye = jnp.eye(RNN_BLOCKS, dtype=w.dtype)
    return jnp.einsum("ncd,nm->ncmd", w, eye).reshape(D_RNN, D_RNN)


def _scan_matrix():
    j = jnp.arange(TK)
    return -(j[:, None] >= j[None, :]).astype(bf16)


def kernel(x, ffn1_norm, ffn1_w_gate, ffn1_w_up, ffn1_w_down, mix_norm, w_in, conv_w, conv_b,
           rg_w_a, rg_b_a, rg_w_x, rg_b_x, rg_lambda, q_norm, k_norm, rnn_out_norm,
           attn_out_norm, w_out, ffn2_norm, ffn2_w_gate, ffn2_w_up, ffn2_w_down):
    bsz, seq, _ = x.shape
    depth = ffn1_norm.shape[0]
    row_head = (jnp.arange(2 * LANES) % LANES) // HEAD_DIM
    seg = (row_head[:, None] == (jnp.arange(LANES) // HEAD_DIM)[None, :]).astype(bf16)
    tri = _scan_matrix()
    x2d = x.reshape(bsz * seq, D_MODEL)
    for l in range(depth):
        x2d, xr, gate, q, k, v = _ffn_in(
            x2d, ffn1_norm[l][None], ffn1_w_gate[l], ffn1_w_up[l], ffn1_w_down[l],
            mix_norm[l][None], w_in[l],
            jnp.tile(q_norm[l], N_HEADS)[None], jnp.tile(k_norm[l], N_HEADS)[None], seg)
        y_rnn = _rglru(
            xr.reshape(bsz, seq, D_RNN), gate.reshape(bsz, seq, D_RNN),
            conv_w[l], conv_b[l][None], _block_diag(rg_w_a[l]).astype(bf16), rg_b_a[l][None],
            _block_diag(rg_w_x[l]).astype(bf16), rg_b_x[l][None], rg_lambda[l][None],
            rnn_out_norm[l][None])
        pairs = (N_PAIR, bsz, seq, LANES)
        y_att = _attention(q.reshape(pairs), k.reshape(pairs), v.reshape(pairs), tri)
        x2d = _ffn_out(x2d, y_rnn.reshape(bsz * seq, D_RNN), y_att.reshape(N_PAIR, bsz * seq, LANES),
                       attn_out_norm[l][None], w_out[l], ffn2_norm[l][None],
                       ffn2_w_gate[l], ffn2_w_up[l], ffn2_w_down[l])
    return x2d.reshape(bsz, seq, D_MODEL)
```

```python
import math

import jax
import jax.numpy as jnp
from jax import lax
from jax.experimental import pallas as pl
from jax.experimental.pallas import tpu as pltpu

D_MODEL = 1024
D_RNN = 512
RNN_BLOCKS = 8
RNN_BW = D_RNN // RNN_BLOCKS
CONV_W = 4
RG_C = 8.0
D_ATT = 512
HEAD_DIM = 64
N_HEADS = D_ATT // HEAD_DIM
D_FF = 2816
N_IN = 2 * D_RNN + 3 * D_ATT
EPS = 1e-6

LANES = 128
SUBLANES = 8
HEADS_PER_STEP = LANES // HEAD_DIM
N_PAIR = D_ATT // LANES
VMEM_LIMIT = 56 * 1024 * 1024

TM = 512
FF_CHUNK = 256
TS = 512
TQ = 512
TK = 256

f32 = jnp.float32
bf16 = jnp.bfloat16


def _rms(xf, g):
    r = lax.rsqrt(jnp.mean(xf * xf, axis=-1, keepdims=True) + EPS)
    return xf * r * g


def _dot(a, b):
    return jnp.dot(a, b, preferred_element_type=f32)


def _split_bf16(x):
    hi = x.astype(bf16)
    lo = (x - hi.astype(f32)).astype(bf16)
    return hi, lo


def _const_spec(shape):
    nd = len(shape)
    return pl.BlockSpec(shape, lambda *_: (0,) * nd, pipeline_mode=pl.Buffered(1))


def _ffn_tile(x, g_ref, wg_ref, wu_ref, wd_ref, h_ref):
    xn = _rms(x, g_ref[...]).astype(bf16)
    for c in range(0, D_FF, FF_CHUNK):
        gate = _dot(xn, wg_ref[:, c:c + FF_CHUNK])
        up = _dot(xn, wu_ref[:, c:c + FF_CHUNK])
        h_ref[:, c:c + FF_CHUNK] = (jax.nn.silu(gate) * up).astype(bf16)
    return x + 0.5 * _dot(h_ref[...], wd_ref[...])


CAST_ROWS = 128
N_STAGE = 6


def _load_as_bf16(src_hbm, dst_ref, stage_ref, sem_ref):
    rows, cols = dst_ref.shape
    n_chunks = rows // CAST_ROWS
    assert n_chunks * CAST_ROWS == rows and cols <= stage_ref.shape[2]

    def chunk_copy(c, slot):
        return pltpu.make_async_copy(src_hbm.at[pl.ds(c * CAST_ROWS, CAST_ROWS), :],
                                     stage_ref.at[slot, :, pl.ds(0, cols)], sem_ref.at[slot])

    for c in range(min(N_STAGE - 1, n_chunks)):
        chunk_copy(c, c).start()

    def body(c, carry):
        slot = c % N_STAGE
        ahead = c + N_STAGE - 1

        @pl.when(ahead < n_chunks)
        def _():
            chunk_copy(ahead, ahead % N_STAGE).start()

        chunk_copy(c, slot).wait()
        dst_rows = pl.ds(pl.multiple_of(c * CAST_ROWS, CAST_ROWS), CAST_ROWS)
        dst_ref[dst_rows, :] = stage_ref[slot, :, pl.ds(0, cols)].astype(bf16)
        return carry

    lax.fori_loop(0, n_chunks, body, 0)


def _hbm_spec():
    return pl.BlockSpec(memory_space=pl.ANY)


def _ffn_specs():
    return [_const_spec((1, D_MODEL)), _hbm_spec(), _hbm_spec(), _hbm_spec()]


def _ffn_scratch():
    return [pltpu.VMEM((TM, D_FF), bf16), pltpu.VMEM((D_MODEL, D_FF), bf16),
            pltpu.VMEM((D_MODEL, D_FF), bf16), pltpu.VMEM((D_FF, D_MODEL), bf16),
            pltpu.VMEM((N_STAGE, CAST_ROWS, D_FF), f32), pltpu.SemaphoreType.DMA((N_STAGE,))]


def _tok_spec(d):
    return pl.BlockSpec((TM, d), lambda i: (i, 0))


def _pair_spec():
    return pl.BlockSpec((N_PAIR, TM, LANES), lambda i: (0, i, 0))


def _head_norm(t, gain_tiled, seg_ref):
    hi, lo = _split_bf16(t * t)
    groups = []
    for g in range(0, t.shape[1], LANES):
        both = jnp.concatenate([hi[:, g:g + LANES], lo[:, g:g + LANES]], axis=1)
        groups.append(_dot(both, seg_ref[...]))
    ss = jnp.concatenate(groups, axis=1)
    r = lax.rsqrt(ss * (1.0 / HEAD_DIM) + EPS)
    return t * r * gain_tiled


def _ffn_in_kernel(x_ref, g1_ref, wg_hbm, wu_hbm, wd_hbm, g_ref, w_hbm, qg_ref, kg_ref, seg_ref,
                   o_ref, xr_ref, gate_ref, q_ref, k_ref, v_ref,
                   h_ref, wg_ref, wu_ref, wd_ref, stage_ref, sem_ref, w_ref):
    @pl.when(pl.program_id(0) == 0)
    def _():
        for src, dst in ((wg_hbm, wg_ref), (wu_hbm, wu_ref), (wd_hbm, wd_ref), (w_hbm, w_ref)):
            _load_as_bf16(src, dst, stage_ref, sem_ref)

    x1 = _ffn_tile(x_ref[...], g1_ref, wg_ref, wu_ref, wd_ref, h_ref)
    o_ref[...] = x1
    h = _rms(x1, g_ref[...]).astype(bf16)
    xr_ref[...] = _dot(h, w_ref[:, 0:D_RNN])
    gate_ref[...] = _dot(h, w_ref[:, D_RNN:2 * D_RNN])
    o = 2 * D_RNN
    q = _dot(h, w_ref[:, o:o + D_ATT])
    k = _dot(h, w_ref[:, o + D_ATT:o + 2 * D_ATT])
    v = _dot(h, w_ref[:, o + 2 * D_ATT:o + 3 * D_ATT])
    scale = 1.0 / math.sqrt(HEAD_DIM)
    qn = (_head_norm(q, qg_ref[...], seg_ref) * scale).astype(bf16)
    kn = _head_norm(k, kg_ref[...], seg_ref).astype(bf16)
    vb = v.astype(bf16)
    for p in range(N_PAIR):
        lanes = slice(p * LANES, (p + 1) * LANES)
        q_ref[p] = qn[:, lanes]
        k_ref[p] = kn[:, lanes]
        v_ref[p] = vb[:, lanes]


def _ffn_in(x2d, g1, wg, wu, wd, g, w_in, qg, kg, seg):
    n_tok = x2d.shape[0]
    return pl.pallas_call(
        _ffn_in_kernel,
        out_shape=(jax.ShapeDtypeStruct((n_tok, D_MODEL), f32),
                   jax.ShapeDtypeStruct((n_tok, D_RNN), f32),
                   jax.ShapeDtypeStruct((n_tok, D_RNN), f32),
                   jax.ShapeDtypeStruct((N_PAIR, n_tok, LANES), bf16),
                   jax.ShapeDtypeStruct((N_PAIR, n_tok, LANES), bf16),
                   jax.ShapeDtypeStruct((N_PAIR, n_tok, LANES), bf16)),
        grid=(n_tok // TM,),
        in_specs=[_tok_spec(D_MODEL)] + _ffn_specs() + [
            _const_spec((1, D_MODEL)), _hbm_spec(),
            _const_spec((1, D_ATT)), _const_spec((1, D_ATT)), _const_spec((2 * LANES, LANES))],
        out_specs=(_tok_spec(D_MODEL), _tok_spec(D_RNN), _tok_spec(D_RNN),
                   _pair_spec(), _pair_spec(), _pair_spec()),
        scratch_shapes=_ffn_scratch() + [pltpu.VMEM((D_MODEL, N_IN), bf16)],
        compiler_params=pltpu.CompilerParams(
            dimension_semantics=("arbitrary",), vmem_limit_bytes=VMEM_LIMIT),
        name="ffn_in",
    )(x2d, g1, wg, wu, wd, g, w_in, qg, kg, seg)


HIST = 8


def _rglru_kernel(xr_ref, gate_ref, cw_ref, cb_ref, wa_ref, ba_ref, wx_ref, bx_ref,
                  lam_ref, g_ref, o_ref, xbuf_ref, hc_ref):
    @pl.when(pl.program_id(1) == 0)
    def _():
        xbuf_ref[0:HIST, :] = jnp.zeros((HIST, D_RNN), f32)
        hc_ref[...] = jnp.zeros((1, D_RNN), f32)

    x = xr_ref[0]
    xbuf_ref[HIST:HIST + TS, :] = x
    xc = cb_ref[...] + cw_ref[CONV_W - 1:CONV_W, :] * x
    for j in range(CONV_W - 1):
        off = HIST - (CONV_W - 1) + j
        xc = xc + cw_ref[j:j + 1, :] * xbuf_ref[off:off + TS, :]
    xbuf_ref[0:HIST, :] = x[TS - HIST:TS, :]

    xcb = xc.astype(bf16)
    r = jax.nn.sigmoid(_dot(xcb, wa_ref[...]) + ba_ref[...])
    i = jax.nn.sigmoid(_dot(xcb, wx_ref[...]) + bx_ref[...])
    lam = lam_ref[...]
    log_sig_lam = jnp.minimum(lam, 0.0) - jnp.log1p(jnp.exp(-jnp.abs(lam)))
    log_a = RG_C * r * log_sig_lam
    a = jnp.exp(log_a)
    th = jnp.tanh(log_a)
    mult = jnp.sqrt(-2.0 * th / (1.0 - th))
    b = mult * (i * xc)

    n_grp = TS // SUBLANES
    a = a.reshape(n_grp, SUBLANES, D_RNN)
    b = b.reshape(n_grp, SUBLANES, D_RNN)
    row = lax.broadcasted_iota(jnp.int32, (n_grp, SUBLANES, D_RNN), 1)
    d = 1
    while d < SUBLANES:
        a_sh = pltpu.roll(a, d, axis=1)
        b_sh = pltpu.roll(b, d, axis=1)
        m = row >= d
        b = jnp.where(m, a * b_sh, 0.0) + b
        a = jnp.where(m, a * a_sh, a)
        d *= 2
    carry = jnp.broadcast_to(hc_ref[...], (SUBLANES, D_RNN))
    groups = []
    for g in range(n_grp):
        h_g = a[g] * carry + b[g]
        groups.append(h_g)
        carry = jnp.broadcast_to(h_g[SUBLANES - 1:SUBLANES, :], (SUBLANES, D_RNN))
    h = jnp.concatenate(groups, axis=0)
    hc_ref[...] = groups[-1][SUBLANES - 1:SUBLANES, :]

    y = h * jax.nn.gelu(gate_ref[0])
    o_ref[0] = _rms(y, g_ref[...]).astype(bf16)


def _rglru(xr, gate, cw, cb, wa, ba, wx, bx, lam, g):
    bsz, seq, _ = xr.shape
    seq_spec = pl.BlockSpec((1, TS, D_RNN), lambda b, s: (b, s, 0))
    vec = _const_spec((1, D_RNN))
    return pl.pallas_call(
        _rglru_kernel,
        out_shape=jax.ShapeDtypeStruct((bsz, seq, D_RNN), bf16),
        grid=(bsz, seq // TS),
        in_specs=[seq_spec, seq_spec, _const_spec((CONV_W, D_RNN)), vec,
                  _const_spec((D_RNN, D_RNN)), vec, _const_spec((D_RNN, D_RNN)), vec, vec, vec],
        out_specs=seq_spec,
        scratch_shapes=[pltpu.VMEM((HIST + TS, D_RNN), f32), pltpu.VMEM((1, D_RNN), f32)],
        compiler_params=pltpu.CompilerParams(
            dimension_semantics=("parallel", "arbitrary"), vmem_limit_bytes=VMEM_LIMIT),
        name="rglru",
    )(xr, gate, cw, cb, wa, ba, wx, bx, lam, g)


PIPE_DEPTH = 2
BLK_PER_TILE = TK // LANES
KT_PER_QT = TQ // TK
N_BLK = HEADS_PER_STEP * BLK_PER_TILE
ITER_PER_TRIP = 16
Z_SLOTS = 2
LOG2E = 1.4426950408889634


MASKED = -1e30


def _attn_kernel(hp_ref, qt_ref, kt_ref, q_ref, k_ref, v_ref, tri_ref, o_ref,
                 c_ref, z_ref, sp_ref, w_ref, pen_ref):
    n_items = qt_ref.shape[0]
    heads = range(HEADS_PER_STEP)
    for ref in (o_ref, c_ref, z_ref, sp_ref, w_ref):
        ref[...] = jnp.zeros_like(ref)
    pen_ref[0] = jnp.zeros_like(pen_ref[0])
    for j in range(KT_PER_QT):
        for b in range(BLK_PER_TILE):
            key = lax.broadcasted_iota(jnp.int32, (TQ, LANES), 1) + (j * TK + b * LANES)
            query = lax.broadcasted_iota(jnp.int32, (TQ, LANES), 0)
            pen_ref[j + 1, b] = jnp.where(key < query, 0.0, MASKED)
    lane_head = lax.broadcasted_iota(jnp.int32, (TK, LANES), 1) // HEAD_DIM

    def item(j):
        j = jnp.clip(j, 0, n_items - 1)
        return hp_ref[j], qt_ref[j], kt_ref[j]

    def diag_tile(qi):
        return (qi + 1) * KT_PER_QT - 1

    def per_head_rows(ref, hp, t):
        x = ref[hp, 0, pl.ds(pl.multiple_of(t * TK, TK), TK), :]
        return jnp.concatenate([jnp.where(lane_head == h, x, jnp.zeros_like(x)) for h in heads], axis=0)

    def stage_c2(hp, qi, t):
        rows = pl.ds(pl.multiple_of(qi * TQ, TQ), TQ)
        acc = jnp.where(t == diag_tile(qi), 0.0, o_ref[hp, 0, rows, :])
        w = jnp.concatenate([w_ref[blk] for blk in range(N_BLK)], axis=1)
        o_ref[hp, 0, rows, :] = acc + _dot(w, per_head_rows(v_ref, hp, t))

    def stage_b2c1(hp, qi, t, slot):
        for h in heads:
            first = h * BLK_PER_TILE
            parts = [sp_ref[blk] for blk in range(first, first + BLK_PER_TILE)]
            r = _dot(jnp.concatenate(parts, axis=1), tri_ref[...])
            c = jnp.where(t == diag_tile(qi), 0.0, c_ref[h])
            for b in range(BLK_PER_TILE):
                s = z_ref[slot, first + b] + r[:, b * LANES:(b + 1) * LANES] + c
                w_ref[first + b] = jnp.exp(s).astype(bf16)
            c_ref[h] = c + jnp.broadcast_to(r[:, 0:1], (TQ, LANES))

    def stage_ab1(hp, qi, t, slot):
        q = q_ref[hp, 0, pl.ds(pl.multiple_of(qi * TQ, TQ), TQ), :]
        z = lax.dot_general(q, per_head_rows(k_ref, hp, t), (((1,), (1,)), ((), ())),
                            preferred_element_type=f32)
        kind = jnp.maximum(t - qi * KT_PER_QT + 1, 0)
        for b in range(BLK_PER_TILE):
            pen = pen_ref[kind, b]
            for h in heads:
                blk = h * BLK_PER_TILE + b
                zm = z[:, blk * LANES:(blk + 1) * LANES] + pen
                z_ref[slot, blk] = zm
                sp = jnp.log(1.0 + jnp.exp2(jnp.abs(zm) * (-LOG2E))) + jnp.maximum(zm, 0.0)
                sp_ref[blk] = sp.astype(bf16)

    def iteration(i, u):
        stage_c2(*item(i - PIPE_DEPTH))
        stage_b2c1(*item(i - 1), (u - 1) % Z_SLOTS)
        stage_ab1(*item(i), u % Z_SLOTS)

    def body(m, carry):
        for u in range(ITER_PER_TRIP):
            iteration(ITER_PER_TRIP * m + u, u)
        return carry

    n_iter = n_items + PIPE_DEPTH
    full_trips = n_iter // ITER_PER_TRIP
    lax.fori_loop(0, full_trips, body, 0)
    for u in range(n_iter % ITER_PER_TRIP):
        iteration(full_trips * ITER_PER_TRIP + u, u)


def _attention(q, k, v, tri):
    _, bsz, seq, _ = q.shape
    n_qt = seq // TQ
    items = [(hp, qi, t) for hp in range(N_PAIR) for qi in range(n_qt)
             for t in range((qi + 1) * KT_PER_QT - 1, -1, -1)]
    hp_tab, qt, kt = (jnp.asarray([it[n] for it in items], jnp.int32) for n in range(3))
    seq_spec = pl.BlockSpec((N_PAIR, 1, seq, LANES), lambda b, *_: (0, b, 0, 0))
    tri_spec = pl.BlockSpec((TK, TK), lambda b, *_: (0, 0), pipeline_mode=pl.Buffered(1))
    return pl.pallas_call(
        _attn_kernel,
        out_shape=jax.ShapeDtypeStruct((N_PAIR, bsz, seq, LANES), f32),
        grid_spec=pltpu.PrefetchScalarGridSpec(
            num_scalar_prefetch=3,
            grid=(bsz,),
            in_specs=[seq_spec, seq_spec, seq_spec, tri_spec],
            out_specs=seq_spec,
            scratch_shapes=[
                pltpu.VMEM((HEADS_PER_STEP, TQ, LANES), f32),
                pltpu.VMEM((Z_SLOTS, N_BLK, TQ, LANES), f32),
                pltpu.VMEM((N_BLK, TQ, LANES), bf16),
                pltpu.VMEM((N_BLK, TQ, LANES), bf16),
                pltpu.VMEM((KT_PER_QT + 1, BLK_PER_TILE, TQ, LANES), f32),
            ]),
        compiler_params=pltpu.CompilerParams(
            dimension_semantics=("parallel",), vmem_limit_bytes=VMEM_LIMIT),
        name="stickbreak",
    )(hp_tab, qt, kt, q, k, v, tri)


def _ffn_out_kernel(x_ref, yr_ref, ya_ref, ga_ref, wo_hbm, g2_ref, wg_hbm, wu_hbm, wd_hbm,
                    o_ref, h_ref, wg_ref, wu_ref, wd_ref, stage_ref, sem_ref, wo_ref):
    @pl.when(pl.program_id(0) == 0)
    def _():
        for src, dst in ((wo_hbm, wo_ref), (wg_hbm, wg_ref), (wu_hbm, wu_ref), (wd_hbm, wd_ref)):
            _load_as_bf16(src, dst, stage_ref, sem_ref)

    ya = jnp.concatenate([ya_ref[p] for p in range(N_PAIR)], axis=1)
    ya = _rms(ya, ga_ref[...]).astype(bf16)
    x2 = (x_ref[...] + _dot(yr_ref[...], wo_ref[0:D_RNN, :])
          + _dot(ya, wo_ref[D_RNN:D_RNN + D_ATT, :]))
    o_ref[...] = _ffn_tile(x2, g2_ref, wg_ref, wu_ref, wd_ref, h_ref)


def _ffn_out(x2d, yr, ya, ga, w_out, g2, wg, wu, wd):
    n_tok = x2d.shape[0]
    return pl.pallas_call(
        _ffn_out_kernel,
        out_shape=jax.ShapeDtypeStruct((n_tok, D_MODEL), f32),
        grid=(n_tok // TM,),
        in_specs=[_tok_spec(D_MODEL), _tok_spec(D_RNN), _pair_spec(), _const_spec((1, D_ATT)),
                  _hbm_spec()] + _ffn_specs(),
        out_specs=_tok_spec(D_MODEL),
        scratch_shapes=_ffn_scratch() + [pltpu.VMEM((D_RNN + D_ATT, D_MODEL), bf16)],
        compiler_params=pltpu.CompilerParams(
            dimension_semantics=("arbitrary",), vmem_limit_bytes=VMEM_LIMIT),
        name="ffn_out",
    )(x2d, yr, ya, ga, w_out, g2, wg, wu, wd)


def _block_diag(w):
    eye = jnp.eye(RNN_BLOCKS, dtype=w.dtype)
    return jnp.einsum("ncd,nm->ncmd", w, eye).reshape(D_RNN, D_RNN)


def _scan_matrix():
    j = jnp.arange(TK)
    return -(j[:, None] >= j[None, :]).astype(bf16)


def kernel(x, ffn1_norm, ffn1_w_gate, ffn1_w_up, ffn1_w_down, mix_norm, w_in, conv_w, conv_b,
           rg_w_a, rg_b_a, rg_w_x, rg_b_x, rg_lambda, q_norm, k_norm, rnn_out_norm,
           attn_out_norm, w_out, ffn2_norm, ffn2_w_gate, ffn2_w_up, ffn2_w_down):
    bsz, seq, _ = x.shape
    depth = ffn1_norm.shape[0]
    row_head = (jnp.arange(2 * LANES) % LANES) // HEAD_DIM
    seg = (row_head[:, None] == (jnp.arange(LANES) // HEAD_DIM)[None, :]).astype(bf16)
    tri = _scan_matrix()
    x2d = x.reshape(bsz * seq, D_MODEL)
    for l in range(depth):
        x2d, xr, gate, q, k, v = _ffn_in(
            x2d, ffn1_norm[l][None], ffn1_w_gate[l], ffn1_w_up[l], ffn1_w_down[l],
            mix_norm[l][None], w_in[l],
            jnp.tile(q_norm[l], N_HEADS)[None], jnp.tile(k_norm[l], N_HEADS)[None], seg)
        y_rnn = _rglru(
            xr.reshape(bsz, seq, D_RNN), gate.reshape(bsz, seq, D_RNN),
            conv_w[l], conv_b[l][None], _block_diag(rg_w_a[l]).astype(bf16), rg_b_a[l][None],
            _block_diag(rg_w_x[l]).astype(bf16), rg_b_x[l][None], rg_lambda[l][None],
            rnn_out_norm[l][None])
        pairs = (N_PAIR, bsz, seq, LANES)
        y_att = _attention(q.reshape(pairs), k.reshape(pairs), v.reshape(pairs), tri)
        x2d = _ffn_out(x2d, y_rnn.reshape(bsz * seq, D_RNN), y_att.reshape(N_PAIR, bsz * seq, LANES),
                       attn_out_norm[l][None], w_out[l], ffn2_norm[l][None],
                       ffn2_w_gate[l], ffn2_w_up[l], ffn2_w_down[l])
    return x2d.reshape(bsz, seq, D_MODEL)
```

```python
import functools
import math

import jax
import jax.numpy as jnp
from jax import lax
from jax.experimental import pallas as pl
from jax.experimental.pallas import tpu as pltpu

D_MODEL = 1024
D_RNN = 512
RNN_BLOCKS = 8
RNN_BW = D_RNN // RNN_BLOCKS
CONV_W = 4
RG_C = 8.0
D_ATT = 512
HEAD_DIM = 64
N_HEADS = D_ATT // HEAD_DIM
D_FF = 2816
N_IN = 2 * D_RNN + 3 * D_ATT
EPS = 1e-6

LANES = 128
SUBLANES = 8
HEADS_PER_STEP = LANES // HEAD_DIM
N_PAIR = D_ATT // LANES
VMEM_LIMIT = 56 * 1024 * 1024

TM = 512
FF_CHUNK = 256
TS = TM
TQ = 512
TK = 256

f32 = jnp.float32
bf16 = jnp.bfloat16


def _rms(xf, g):
    r = lax.rsqrt(jnp.mean(xf * xf, axis=-1, keepdims=True) + EPS)
    return xf * r * g


def _dot(a, b):
    return jnp.dot(a, b, preferred_element_type=f32)


def _split_bf16(x):
    hi = x.astype(bf16)
    lo = (x - hi.astype(f32)).astype(bf16)
    return hi, lo


def _const_spec(shape):
    nd = len(shape)
    return pl.BlockSpec(shape, lambda *_: (0,) * nd, pipeline_mode=pl.Buffered(1))


def _ffn_tile(x, g_ref, wg_ref, wu_ref, wd_ref, h_ref):
    xn = _rms(x, g_ref[...]).astype(bf16)
    for c in range(0, D_FF, FF_CHUNK):
        gate = _dot(xn, wg_ref[:, c:c + FF_CHUNK])
        up = _dot(xn, wu_ref[:, c:c + FF_CHUNK])
        h_ref[:, c:c + FF_CHUNK] = (jax.nn.silu(gate) * up).astype(bf16)
    return x + 0.5 * _dot(h_ref[...], wd_ref[...])


CAST_ROWS = 128
N_STAGE = 6


def _load_as_bf16(src_hbm, dst_ref, stage_ref, sem_ref):
    rows, cols = dst_ref.shape
    n_chunks = rows // CAST_ROWS
    assert n_chunks * CAST_ROWS == rows and cols <= stage_ref.shape[2]

    def chunk_copy(c, slot):
        return pltpu.make_async_copy(src_hbm.at[pl.ds(c * CAST_ROWS, CAST_ROWS), :],
                                     stage_ref.at[slot, :, pl.ds(0, cols)], sem_ref.at[slot])

    for c in range(min(N_STAGE - 1, n_chunks)):
        chunk_copy(c, c).start()

    def body(c, carry):
        slot = c % N_STAGE
        ahead = c + N_STAGE - 1

        @pl.when(ahead < n_chunks)
        def _():
            chunk_copy(ahead, ahead % N_STAGE).start()

        chunk_copy(c, slot).wait()
        dst_rows = pl.ds(pl.multiple_of(c * CAST_ROWS, CAST_ROWS), CAST_ROWS)
        dst_ref[dst_rows, :] = stage_ref[slot, :, pl.ds(0, cols)].astype(bf16)
        return carry

    lax.fori_loop(0, n_chunks, body, 0)


def _hbm_spec():
    return pl.BlockSpec(memory_space=pl.ANY)


def _ffn_specs():
    return [_const_spec((1, D_MODEL)), _hbm_spec(), _hbm_spec(), _hbm_spec()]


def _ffn_scratch():
    return [pltpu.VMEM((TM, D_FF), bf16), pltpu.VMEM((D_MODEL, D_FF), bf16),
            pltpu.VMEM((D_MODEL, D_FF), bf16), pltpu.VMEM((D_FF, D_MODEL), bf16),
            pltpu.VMEM((N_STAGE, CAST_ROWS, D_FF), f32), pltpu.SemaphoreType.DMA((N_STAGE,))]


def _tok_spec(d):
    return pl.BlockSpec((TM, d), lambda i: (i, 0))


def _pair_spec():
    return pl.BlockSpec((N_PAIR, TM, LANES), lambda i: (0, i, 0))


def _head_norm(t, gain_tiled, seg_ref):
    hi, lo = _split_bf16(t * t)
    groups = []
    for g in range(0, t.shape[1], LANES):
        both = jnp.concatenate([hi[:, g:g + LANES], lo[:, g:g + LANES]], axis=1)
        groups.append(_dot(both, seg_ref[...]))
    ss = jnp.concatenate(groups, axis=1)
    r = lax.rsqrt(ss * (1.0 / HEAD_DIM) + EPS)
    return t * r * gain_tiled


def _ffn_in_kernel(tiles_per_seq,
                   x_ref, g1_ref, wg_hbm, wu_hbm, wd_hbm, g_ref, w_hbm, qg_ref, kg_ref, seg_ref,
                   cw_ref, cb_ref, wa_ref, ba_ref, wx_ref, bx_ref, lam_ref, gr_ref,
                   o_ref, y_ref, q_ref, k_ref, v_ref,
                   h_ref, wg_ref, wu_ref, wd_ref, stage_ref, sem_ref, w_ref,
                   xr_ref, gate_ref, xbuf_ref, hc_ref):
    step = pl.program_id(0)

    @pl.when(step == 0)
    def _():
        for src, dst in ((wg_hbm, wg_ref), (wu_hbm, wu_ref), (wd_hbm, wd_ref), (w_hbm, w_ref)):
            _load_as_bf16(src, dst, stage_ref, sem_ref)
        for ref in (xr_ref, gate_ref, xbuf_ref, hc_ref):
            ref[...] = jnp.zeros_like(ref)

    first = (step + (tiles_per_seq - 1)) % tiles_per_seq == 0
    y_ref[...] = _rglru_tile(xr_ref[...], gate_ref[...], first, cw_ref, cb_ref, wa_ref, ba_ref,
                             wx_ref, bx_ref, lam_ref, gr_ref, xbuf_ref, hc_ref)

    x1 = _ffn_tile(x_ref[...], g1_ref, wg_ref, wu_ref, wd_ref, h_ref)
    o_ref[...] = x1
    h = _rms(x1, g_ref[...]).astype(bf16)
    xr_ref[...] = _dot(h, w_ref[:, 0:D_RNN])
    gate_ref[...] = _dot(h, w_ref[:, D_RNN:2 * D_RNN])
    o = 2 * D_RNN
    q = _dot(h, w_ref[:, o:o + D_ATT])
    k = _dot(h, w_ref[:, o + D_ATT:o + 2 * D_ATT])
    v = _dot(h, w_ref[:, o + 2 * D_ATT:o + 3 * D_ATT])
    scale = 1.0 / math.sqrt(HEAD_DIM)
    qn = (_head_norm(q, qg_ref[...], seg_ref) * scale).astype(bf16)
    kn = _head_norm(k, kg_ref[...], seg_ref).astype(bf16)
    vb = v.astype(bf16)
    for p in range(N_PAIR):
        lanes = slice(p * LANES, (p + 1) * LANES)
        q_ref[p] = qn[:, lanes]
        k_ref[p] = kn[:, lanes]
        v_ref[p] = vb[:, lanes]


def _ffn_in(x2d, seq, g1, wg, wu, wd, g, w_in, qg, kg, seg, cw, cb, wa, ba, wx, bx, lam, gr):
    n_tok = x2d.shape[0]
    n_tiles = n_tok // TM
    cur = lambda i: jnp.minimum(i, n_tiles - 1)
    prev = lambda i: jnp.maximum(i - 1, 0)
    tok_cur = lambda d: pl.BlockSpec((TM, d), lambda i: (cur(i), 0))
    pair_cur = pl.BlockSpec((N_PAIR, TM, LANES), lambda i: (0, cur(i), 0))
    vec = _const_spec((1, D_RNN))
    return pl.pallas_call(
        functools.partial(_ffn_in_kernel, seq // TS),
        out_shape=(jax.ShapeDtypeStruct((n_tok, D_MODEL), f32),
                   jax.ShapeDtypeStruct((n_tok, D_RNN), bf16),
                   jax.ShapeDtypeStruct((N_PAIR, n_tok, LANES), bf16),
                   jax.ShapeDtypeStruct((N_PAIR, n_tok, LANES), bf16),
                   jax.ShapeDtypeStruct((N_PAIR, n_tok, LANES), bf16)),
        grid=(n_tiles + 1,),
        in_specs=[tok_cur(D_MODEL)] + _ffn_specs() + [
            _const_spec((1, D_MODEL)), _hbm_spec(),
            _const_spec((1, D_ATT)), _const_spec((1, D_ATT)), _const_spec((2 * LANES, LANES)),
            _const_spec((CONV_W, D_RNN)), vec, _const_spec((D_RNN, D_RNN)), vec,
            _const_spec((D_RNN, D_RNN)), vec, vec, vec],
        out_specs=(tok_cur(D_MODEL), pl.BlockSpec((TS, D_RNN), lambda i: (prev(i), 0)),
                   pair_cur, pair_cur, pair_cur),
        scratch_shapes=_ffn_scratch() + [
            pltpu.VMEM((D_MODEL, N_IN), bf16),
            pltpu.VMEM((TS, D_RNN), f32), pltpu.VMEM((TS, D_RNN), f32),
            pltpu.VMEM((HIST + TS, D_RNN), f32), pltpu.VMEM((1, D_RNN), f32)],
        compiler_params=pltpu.CompilerParams(
            dimension_semantics=("arbitrary",), vmem_limit_bytes=VMEM_LIMIT),
        name="ffn_in",
    )(x2d, g1, wg, wu, wd, g, w_in, qg, kg, seg, cw, cb, wa, ba, wx, bx, lam, gr)


HIST = 8


def _rglru_tile(x, gate, first, cw_ref, cb_ref, wa_ref, ba_ref, wx_ref, bx_ref, lam_ref, g_ref,
                xbuf_ref, hc_ref):
    xbuf_ref[0:HIST, :] = jnp.where(first, 0.0, xbuf_ref[0:HIST, :])
    xbuf_ref[HIST:HIST + TS, :] = x
    xc = cb_ref[...] + cw_ref[CONV_W - 1:CONV_W, :] * x
    for j in range(CONV_W - 1):
        off = HIST - (CONV_W - 1) + j
        xc = xc + cw_ref[j:j + 1, :] * xbuf_ref[off:off + TS, :]
    xbuf_ref[0:HIST, :] = x[TS - HIST:TS, :]

    xcb = xc.astype(bf16)
    r = jax.nn.sigmoid(_dot(xcb, wa_ref[...]) + ba_ref[...])
    i = jax.nn.sigmoid(_dot(xcb, wx_ref[...]) + bx_ref[...])
    lam = lam_ref[...]
    log_sig_lam = jnp.minimum(lam, 0.0) - jnp.log1p(jnp.exp(-jnp.abs(lam)))
    log_a = RG_C * r * log_sig_lam
    a = jnp.exp(log_a)
    th = jnp.tanh(log_a)
    mult = jnp.sqrt(-2.0 * th / (1.0 - th))
    b = mult * (i * xc)

    n_grp = TS // SUBLANES
    a = a.reshape(n_grp, SUBLANES, D_RNN)
    b = b.reshape(n_grp, SUBLANES, D_RNN)
    row = lax.broadcasted_iota(jnp.int32, (n_grp, SUBLANES, D_RNN), 1)
    d = 1
    while d < SUBLANES:
        a_sh = pltpu.roll(a, d, axis=1)
        b_sh = pltpu.roll(b, d, axis=1)
        m = row >= d
        b = jnp.where(m, a * b_sh, 0.0) + b
        a = jnp.where(m, a * a_sh, a)
        d *= 2
    carry = jnp.broadcast_to(jnp.where(first, 0.0, hc_ref[...]), (SUBLANES, D_RNN))
    groups = []
    for g in range(n_grp):
        h_g = a[g] * carry + b[g]
        groups.append(h_g)
        carry = jnp.broadcast_to(h_g[SUBLANES - 1:SUBLANES, :], (SUBLANES, D_RNN))
    h = jnp.concatenate(groups, axis=0)
    hc_ref[...] = groups[-1][SUBLANES - 1:SUBLANES, :]

    y = h * jax.nn.gelu(gate)
    return _rms(y, g_ref[...]).astype(bf16)


PIPE_DEPTH = 2
BLK_PER_TILE = TK // LANES
KT_PER_QT = TQ // TK
N_BLK = HEADS_PER_STEP * BLK_PER_TILE
ITER_PER_TRIP = 16
Z_SLOTS = 2
LOG2E = 1.4426950408889634


MASKED = -1e30


def _attn_kernel(hp_ref, qt_ref, kt_ref, q_ref, k_ref, v_ref, tri_ref, o_ref,
                 c_ref, z_ref, sp_ref, w_ref, pen_ref):
    n_items = qt_ref.shape[0]
    heads = range(HEADS_PER_STEP)
    for ref in (o_ref, c_ref, z_ref, sp_ref, w_ref):
        ref[...] = jnp.zeros_like(ref)
    pen_ref[0] = jnp.zeros_like(pen_ref[0])
    for j in range(KT_PER_QT):
        for b in range(BLK_PER_TILE):
            key = lax.broadcasted_iota(jnp.int32, (TQ, LANES), 1) + (j * TK + b * LANES)
            query = lax.broadcasted_iota(jnp.int32, (TQ, LANES), 0)
            pen_ref[j + 1, b] = jnp.where(key < query, 0.0, MASKED)
    lane_head = lax.broadcasted_iota(jnp.int32, (TK, LANES), 1) // HEAD_DIM

    def item(j):
        j = jnp.clip(j, 0, n_items - 1)
        return hp_ref[j], qt_ref[j], kt_ref[j]

    def diag_tile(qi):
        return (qi + 1) * KT_PER_QT - 1

    def per_head_rows(ref, hp, t):
        x = ref[hp, 0, pl.ds(pl.multiple_of(t * TK, TK), TK), :]
        return jnp.concatenate([jnp.where(lane_head == h, x, jnp.zeros_like(x)) for h in heads], axis=0)

    def stage_c2(hp, qi, t):
        rows = pl.ds(pl.multiple_of(qi * TQ, TQ), TQ)
        acc = jnp.where(t == diag_tile(qi), 0.0, o_ref[hp, 0, rows, :])
        w = jnp.concatenate([w_ref[blk] for blk in range(N_BLK)], axis=1)
        o_ref[hp, 0, rows, :] = acc + _dot(w, per_head_rows(v_ref, hp, t))

    def stage_b2c1(hp, qi, t, slot):
        for h in heads:
            first = h * BLK_PER_TILE
            parts = [sp_ref[blk] for blk in range(first, first + BLK_PER_TILE)]
            r = _dot(jnp.concatenate(parts, axis=1), tri_ref[...])
            c = jnp.where(t == diag_tile(qi), 0.0, c_ref[h])
            for b in range(BLK_PER_TILE):
                s = z_ref[slot, first + b] + r[:, b * LANES:(b + 1) * LANES] + c
                w_ref[first + b] = jnp.exp(s).astype(bf16)
            c_ref[h] = c + jnp.broadcast_to(r[:, 0:1], (TQ, LANES))

    def stage_ab1(hp, qi, t, slot):
        q = q_ref[hp, 0, pl.ds(pl.multiple_of(qi * TQ, TQ), TQ), :]
        z = lax.dot_general(q, per_head_rows(k_ref, hp, t), (((1,), (1,)), ((), ())),
                            preferred_element_type=f32)
        kind = jnp.maximum(t - qi * KT_PER_QT + 1, 0)
        for b in range(BLK_PER_TILE):
            pen = pen_ref[kind, b]
            for h in heads:
                blk = h * BLK_PER_TILE + b
                zm = z[:, blk * LANES:(blk + 1) * LANES] + pen
                z_ref[slot, blk] = zm
                sp = jnp.log(1.0 + jnp.exp2(jnp.abs(zm) * (-LOG2E))) + jnp.maximum(zm, 0.0)
                sp_ref[blk] = sp.astype(bf16)

    def iteration(i, u):
        stage_c2(*item(i - PIPE_DEPTH))
        stage_b2c1(*item(i - 1), (u - 1) % Z_SLOTS)
        stage_ab1(*item(i), u % Z_SLOTS)

    def body(m, carry):
        for u in range(ITER_PER_TRIP):
            iteration(ITER_PER_TRIP * m + u, u)
        return carry

    n_iter = n_items + PIPE_DEPTH
    full_trips = n_iter // ITER_PER_TRIP
    lax.fori_loop(0, full_trips, body, 0)
    for u in range(n_iter % ITER_PER_TRIP):
        iteration(full_trips * ITER_PER_TRIP + u, u)


def _attention(q, k, v, tri):
    _, bsz, seq, _ = q.shape
    n_qt = seq // TQ
    items = [(hp, qi, t) for hp in range(N_PAIR) for qi in range(n_qt)
             for t in range((qi + 1) * KT_PER_QT - 1, -1, -1)]
    hp_tab, qt, kt = (jnp.asarray([it[n] for it in items], jnp.int32) for n in range(3))
    seq_spec = pl.BlockSpec((N_PAIR, 1, seq, LANES), lambda b, *_: (0, b, 0, 0))
    tri_spec = pl.BlockSpec((TK, TK), lambda b, *_: (0, 0), pipeline_mode=pl.Buffered(1))
    return pl.pallas_call(
        _attn_kernel,
        out_shape=jax.ShapeDtypeStruct((N_PAIR, bsz, seq, LANES), f32),
        grid_spec=pltpu.PrefetchScalarGridSpec(
            num_scalar_prefetch=3,
            grid=(bsz,),
            in_specs=[seq_spec, seq_spec, seq_spec, tri_spec],
            out_specs=seq_spec,
            scratch_shapes=[
                pltpu.VMEM((HEADS_PER_STEP, TQ, LANES), f32),
                pltpu.VMEM((Z_SLOTS, N_BLK, TQ, LANES), f32),
                pltpu.VMEM((N_BLK, TQ, LANES), bf16),
                pltpu.VMEM((N_BLK, TQ, LANES), bf16),
                pltpu.VMEM((KT_PER_QT + 1, BLK_PER_TILE, TQ, LANES), f32),
            ]),
        compiler_params=pltpu.CompilerParams(
            dimension_semantics=("parallel",), vmem_limit_bytes=VMEM_LIMIT),
        name="stickbreak",
    )(hp_tab, qt, kt, q, k, v, tri)


def _ffn_out_kernel(x_ref, yr_ref, ya_ref, ga_ref, wo_hbm, g2_ref, wg_hbm, wu_hbm, wd_hbm,
                    o_ref, h_ref, wg_ref, wu_ref, wd_ref, stage_ref, sem_ref, wo_ref):
    @pl.when(pl.program_id(0) == 0)
    def _():
        for src, dst in ((wo_hbm, wo_ref), (wg_hbm, wg_ref), (wu_hbm, wu_ref), (wd_hbm, wd_ref)):
            _load_as_bf16(src, dst, stage_ref, sem_ref)

    ya = jnp.concatenate([ya_ref[p] for p in range(N_PAIR)], axis=1)
    ya = _rms(ya, ga_ref[...]).astype(bf16)
    x2 = (x_ref[...] + _dot(yr_ref[...], wo_ref[0:D_RNN, :])
          + _dot(ya, wo_ref[D_RNN:D_RNN + D_ATT, :]))
    o_ref[...] = _ffn_tile(x2, g2_ref, wg_ref, wu_ref, wd_ref, h_ref)


def _ffn_out(x2d, yr, ya, ga, w_out, g2, wg, wu, wd):
    n_tok = x2d.shape[0]
    return pl.pallas_call(
        _ffn_out_kernel,
        out_shape=jax.ShapeDtypeStruct((n_tok, D_MODEL), f32),
        grid=(n_tok // TM,),
        in_specs=[_tok_spec(D_MODEL), _tok_spec(D_RNN), _pair_spec(), _const_spec((1, D_ATT)),
                  _hbm_spec()] + _ffn_specs(),
        out_specs=_tok_spec(D_MODEL),
        scratch_shapes=_ffn_scratch() + [pltpu.VMEM((D_RNN + D_ATT, D_MODEL), bf16)],
        compiler_params=pltpu.CompilerParams(
            dimension_semantics=("arbitrary",), vmem_limit_bytes=VMEM_LIMIT),
        name="ffn_out",
    )(x2d, yr, ya, ga, w_out, g2, wg, wu, wd)


def _block_diag(w):
    eye = jnp.eye(RNN_BLOCKS, dtype=w.dtype)
    return jnp.einsum("ncd,nm->ncmd", w, eye).reshape(D_RNN, D_RNN)


def _scan_matrix():
    j = jnp.arange(TK)
    return -(j[:, None] >= j[None, :]).astype(bf16)


def kernel(x, ffn1_norm, ffn1_w_gate, ffn1_w_up, ffn1_w_down, mix_norm, w_in, conv_w, conv_b,
           rg_w_a, rg_b_a, rg_w_x, rg_b_x, rg_lambda, q_norm, k_norm, rnn_out_norm,
           attn_out_norm, w_out, ffn2_norm, ffn2_w_gate, ffn2_w_up, ffn2_w_down):
    bsz, seq, _ = x.shape
    depth = ffn1_norm.shape[0]
    row_head = (jnp.arange(2 * LANES) % LANES) // HEAD_DIM
    seg = (row_head[:, None] == (jnp.arange(LANES) // HEAD_DIM)[None, :]).astype(bf16)
    tri = _scan_matrix()
    x2d = x.reshape(bsz * seq, D_MODEL)
    for l in range(depth):
        x2d, y_rnn, q, k, v = _ffn_in(
            x2d, seq, ffn1_norm[l][None], ffn1_w_gate[l], ffn1_w_up[l], ffn1_w_down[l],
            mix_norm[l][None], w_in[l],
            jnp.tile(q_norm[l], N_HEADS)[None], jnp.tile(k_norm[l], N_HEADS)[None], seg,
            conv_w[l], conv_b[l][None], _block_diag(rg_w_a[l]).astype(bf16), rg_b_a[l][None],
            _block_diag(rg_w_x[l]).astype(bf16), rg_b_x[l][None], rg_lambda[l][None],
            rnn_out_norm[l][None])
        pairs = (N_PAIR, bsz, seq, LANES)
        y_att = _attention(q.reshape(pairs), k.reshape(pairs), v.reshape(pairs), tri)
        x2d = _ffn_out(x2d, y_rnn, y_att.reshape(N_PAIR, bsz * seq, LANES),
                       attn_out_norm[l][None], w_out[l], ffn2_norm[l][None],
                       ffn2_w_gate[l], ffn2_w_up[l], ffn2_w_down[l])
    return x2d.reshape(bsz, seq, D_MODEL)
```

```python
import functools
import math

import jax
import jax.numpy as jnp
from jax import lax
from jax.experimental import pallas as pl
from jax.experimental.pallas import tpu as pltpu

D_MODEL = 1024
D_RNN = 512
RNN_BLOCKS = 8
RNN_BW = D_RNN // RNN_BLOCKS
CONV_W = 4
RG_C = 8.0
D_ATT = 512
HEAD_DIM = 64
N_HEADS = D_ATT // HEAD_DIM
D_FF = 2816
N_IN = 2 * D_RNN + 3 * D_ATT
EPS = 1e-6

LANES = 128
SUBLANES = 8
HEADS_PER_STEP = LANES // HEAD_DIM
N_PAIR = D_ATT // LANES
VMEM_LIMIT = 56 * 1024 * 1024

TM = 512
FF_CHUNK = 256
TS = TM
TQ = 512
TK = 256

f32 = jnp.float32
bf16 = jnp.bfloat16


def _rms(xf, g):
    r = lax.rsqrt(jnp.mean(xf * xf, axis=-1, keepdims=True) + EPS)
    return xf * r * g


def _dot(a, b):
    return jnp.dot(a, b, preferred_element_type=f32)


def _split_bf16(x):
    hi = x.astype(bf16)
    lo = (x - hi.astype(f32)).astype(bf16)
    return hi, lo


def _const_spec(shape):
    nd = len(shape)
    return pl.BlockSpec(shape, lambda *_: (0,) * nd, pipeline_mode=pl.Buffered(1))


def _ffn_tile(x, g_ref, wg_ref, wu_ref, wd_ref, h_ref):
    xn = _rms(x, g_ref[...]).astype(bf16)
    for c in range(0, D_FF, FF_CHUNK):
        gate = _dot(xn, wg_ref[:, c:c + FF_CHUNK])
        up = _dot(xn, wu_ref[:, c:c + FF_CHUNK])
        h_ref[:, c:c + FF_CHUNK] = (jax.nn.silu(gate) * up).astype(bf16)
    return x + 0.5 * _dot(h_ref[...], wd_ref[...])


CAST_ROWS = 128
N_STAGE = 6


def _load_as_bf16(src_hbm, dst_ref, stage_ref, sem_ref):
    rows, cols = dst_ref.shape
    n_chunks = rows // CAST_ROWS
    assert n_chunks * CAST_ROWS == rows and cols <= stage_ref.shape[2]

    def chunk_copy(c, slot):
        return pltpu.make_async_copy(src_hbm.at[pl.ds(c * CAST_ROWS, CAST_ROWS), :],
                                     stage_ref.at[slot, :, pl.ds(0, cols)], sem_ref.at[slot])

    for c in range(min(N_STAGE - 1, n_chunks)):
        chunk_copy(c, c).start()

    def body(c, carry):
        slot = c % N_STAGE
        ahead = c + N_STAGE - 1

        @pl.when(ahead < n_chunks)
        def _():
            chunk_copy(ahead, ahead % N_STAGE).start()

        chunk_copy(c, slot).wait()
        dst_rows = pl.ds(pl.multiple_of(c * CAST_ROWS, CAST_ROWS), CAST_ROWS)
        dst_ref[dst_rows, :] = stage_ref[slot, :, pl.ds(0, cols)].astype(bf16)
        return carry

    lax.fori_loop(0, n_chunks, body, 0)


def _hbm_spec():
    return pl.BlockSpec(memory_space=pl.ANY)


def _ffn_specs():
    return [_const_spec((1, D_MODEL)), _hbm_spec(), _hbm_spec(), _hbm_spec()]


def _ffn_scratch():
    return [pltpu.VMEM((TM, D_FF), bf16), pltpu.VMEM((D_MODEL, D_FF), bf16),
            pltpu.VMEM((D_MODEL, D_FF), bf16), pltpu.VMEM((D_FF, D_MODEL), bf16),
            pltpu.VMEM((N_STAGE, CAST_ROWS, D_FF), f32), pltpu.SemaphoreType.DMA((N_STAGE,))]


def _tok_spec(d):
    return pl.BlockSpec((TM, d), lambda i: (i, 0))


def _pair_spec():
    return pl.BlockSpec((N_PAIR, TM, LANES), lambda i: (0, i, 0))


def _head_norm(t, gain_tiled, seg_ref):
    hi, lo = _split_bf16(t * t)
    groups = []
    for g in range(0, t.shape[1], LANES):
        both = jnp.concatenate([hi[:, g:g + LANES], lo[:, g:g + LANES]], axis=1)
        groups.append(_dot(both, seg_ref[...]))
    ss = jnp.concatenate(groups, axis=1)
    r = lax.rsqrt(ss * (1.0 / HEAD_DIM) + EPS)
    return t * r * gain_tiled


def _ffn_in_kernel(tiles_per_seq,
                   x_ref, g1_ref, wg_hbm, wu_hbm, wd_hbm, g_ref, w_hbm, qg_ref, kg_ref, seg_ref,
                   cw_ref, cb_ref, wa_ref, ba_ref, wx_ref, bx_ref, lam_ref, gr_ref,
                   o_ref, y_ref, q_ref, k_ref, v_ref,
                   h_ref, wg_ref, wu_ref, wd_ref, stage_ref, sem_ref, w_ref,
                   xr_ref, gate_ref, xbuf_ref, hc_ref):
    step = pl.program_id(0)

    @pl.when(step == 0)
    def _():
        for src, dst in ((wg_hbm, wg_ref), (wu_hbm, wu_ref), (wd_hbm, wd_ref), (w_hbm, w_ref)):
            _load_as_bf16(src, dst, stage_ref, sem_ref)
        for ref in (xr_ref, gate_ref, xbuf_ref, hc_ref):
            ref[...] = jnp.zeros_like(ref)

    first = (step + (tiles_per_seq - 1)) % tiles_per_seq == 0
    y_ref[...] = _rglru_tile(xr_ref[...], gate_ref[...], first, cw_ref, cb_ref, wa_ref, ba_ref,
                             wx_ref, bx_ref, lam_ref, gr_ref, xbuf_ref, hc_ref)

    x1 = _ffn_tile(x_ref[...], g1_ref, wg_ref, wu_ref, wd_ref, h_ref)
    o_ref[...] = x1
    h = _rms(x1, g_ref[...]).astype(bf16)
    xr_ref[...] = _dot(h, w_ref[:, 0:D_RNN])
    gate_ref[...] = _dot(h, w_ref[:, D_RNN:2 * D_RNN])
    o = 2 * D_RNN
    q = _dot(h, w_ref[:, o:o + D_ATT])
    k = _dot(h, w_ref[:, o + D_ATT:o + 2 * D_ATT])
    v = _dot(h, w_ref[:, o + 2 * D_ATT:o + 3 * D_ATT])
    scale = 1.0 / math.sqrt(HEAD_DIM)
    qn = (_head_norm(q, qg_ref[...], seg_ref) * scale).astype(bf16)
    kn = _head_norm(k, kg_ref[...], seg_ref).astype(bf16)
    vb = v.astype(bf16)
    for p in range(N_PAIR):
        lanes = slice(p * LANES, (p + 1) * LANES)
        q_ref[p] = qn[:, lanes]
        k_ref[p] = kn[:, lanes]
        v_ref[p] = vb[:, lanes]


def _ffn_in(x2d, seq, g1, wg, wu, wd, g, w_in, qg, kg, seg, cw, cb, wa, ba, wx, bx, lam, gr):
    n_tok = x2d.shape[0]
    n_tiles = n_tok // TM
    cur = lambda i: jnp.minimum(i, n_tiles - 1)
    prev = lambda i: jnp.maximum(i - 1, 0)
    tok_cur = lambda d: pl.BlockSpec((TM, d), lambda i: (cur(i), 0))
    pair_cur = pl.BlockSpec((N_PAIR, TM, LANES), lambda i: (0, cur(i), 0))
    vec = _const_spec((1, D_RNN))
    return pl.pallas_call(
        functools.partial(_ffn_in_kernel, seq // TS),
        out_shape=(jax.ShapeDtypeStruct((n_tok, D_MODEL), f32),
                   jax.ShapeDtypeStruct((n_tok, D_RNN), bf16),
                   jax.ShapeDtypeStruct((N_PAIR, n_tok, LANES), bf16),
                   jax.ShapeDtypeStruct((N_PAIR, n_tok, LANES), bf16),
                   jax.ShapeDtypeStruct((N_PAIR, n_tok, LANES), bf16)),
        grid=(n_tiles + 1,),
        in_specs=[tok_cur(D_MODEL)] + _ffn_specs() + [
            _const_spec((1, D_MODEL)), _hbm_spec(),
            _const_spec((1, D_ATT)), _const_spec((1, D_ATT)), _const_spec((2 * LANES, LANES)),
            _const_spec((CONV_W, D_RNN)), vec, _const_spec((D_RNN, D_RNN)), vec,
            _const_spec((D_RNN, D_RNN)), vec, vec, vec],
        out_specs=(tok_cur(D_MODEL), pl.BlockSpec((TS, D_RNN), lambda i: (prev(i), 0)),
                   pair_cur, pair_cur, pair_cur),
        scratch_shapes=_ffn_scratch() + [
            pltpu.VMEM((D_MODEL, N_IN), bf16),
            pltpu.VMEM((TS, D_RNN), f32), pltpu.VMEM((TS, D_RNN), f32),
            pltpu.VMEM((HIST + TS, D_RNN), f32), pltpu.VMEM((1, D_RNN), f32)],
        compiler_params=pltpu.CompilerParams(
            dimension_semantics=("arbitrary",), vmem_limit_bytes=VMEM_LIMIT),
        name="ffn_in",
    )(x2d, g1, wg, wu, wd, g, w_in, qg, kg, seg, cw, cb, wa, ba, wx, bx, lam, gr)


HIST = 8


def _rglru_tile(x, gate, first, cw_ref, cb_ref, wa_ref, ba_ref, wx_ref, bx_ref, lam_ref, g_ref,
                xbuf_ref, hc_ref):
    xbuf_ref[0:HIST, :] = jnp.where(first, 0.0, xbuf_ref[0:HIST, :])
    xbuf_ref[HIST:HIST + TS, :] = x
    xc = cb_ref[...] + cw_ref[CONV_W - 1:CONV_W, :] * x
    for j in range(CONV_W - 1):
        off = HIST - (CONV_W - 1) + j
        xc = xc + cw_ref[j:j + 1, :] * xbuf_ref[off:off + TS, :]
    xbuf_ref[0:HIST, :] = x[TS - HIST:TS, :]

    xcb = xc.astype(bf16)
    r = jax.nn.sigmoid(_dot(xcb, wa_ref[...]) + ba_ref[...])
    i = jax.nn.sigmoid(_dot(xcb, wx_ref[...]) + bx_ref[...])
    lam = lam_ref[...]
    log_sig_lam = jnp.minimum(lam, 0.0) - jnp.log1p(jnp.exp(-jnp.abs(lam)))
    log_a = RG_C * r * log_sig_lam
    a = jnp.exp(log_a)
    th = jnp.tanh(log_a)
    mult = jnp.sqrt(-2.0 * th / (1.0 - th))
    b = mult * (i * xc)

    n_grp = TS // SUBLANES
    a = a.reshape(n_grp, SUBLANES, D_RNN)
    b = b.reshape(n_grp, SUBLANES, D_RNN)
    row = lax.broadcasted_iota(jnp.int32, (n_grp, SUBLANES, D_RNN), 1)
    d = 1
    while d < SUBLANES:
        a_sh = pltpu.roll(a, d, axis=1)
        b_sh = pltpu.roll(b, d, axis=1)
        m = row >= d
        b = jnp.where(m, a * b_sh, 0.0) + b
        a = jnp.where(m, a * a_sh, a)
        d *= 2
    carry = jnp.broadcast_to(jnp.where(first, 0.0, hc_ref[...]), (SUBLANES, D_RNN))
    groups = []
    for g in range(n_grp):
        h_g = a[g] * carry + b[g]
        groups.append(h_g)
        carry = jnp.broadcast_to(h_g[SUBLANES - 1:SUBLANES, :], (SUBLANES, D_RNN))
    h = jnp.concatenate(groups, axis=0)
    hc_ref[...] = groups[-1][SUBLANES - 1:SUBLANES, :]

    y = h * jax.nn.gelu(gate)
    return _rms(y, g_ref[...]).astype(bf16)


PIPE_DEPTH = 2
BLK_PER_TILE = TK // LANES
KT_PER_QT = TQ // TK
N_BLK = HEADS_PER_STEP * BLK_PER_TILE
ITER_PER_TRIP = 24
Z_SLOTS = 2
LOG2E = 1.4426950408889634


MASKED = -1e30


def _attn_kernel(hp_ref, qt_ref, kt_ref, q_ref, k_ref, v_ref, tri_ref, o_ref,
                 c_ref, z_ref, sp_ref, w_ref, pen_ref):
    n_items = qt_ref.shape[0]
    heads = range(HEADS_PER_STEP)
    for ref in (o_ref, c_ref, z_ref, sp_ref, w_ref):
        ref[...] = jnp.zeros_like(ref)
    pen_ref[0] = jnp.zeros_like(pen_ref[0])
    for j in range(KT_PER_QT):
        for b in range(BLK_PER_TILE):
            key = lax.broadcasted_iota(jnp.int32, (TQ, LANES), 1) + (j * TK + b * LANES)
            query = lax.broadcasted_iota(jnp.int32, (TQ, LANES), 0)
            pen_ref[j + 1, b] = jnp.where(key < query, 0.0, MASKED)
    lane_head = lax.broadcasted_iota(jnp.int32, (TK, LANES), 1) // HEAD_DIM

    def item(j):
        j = jnp.clip(j, 0, n_items - 1)
        return hp_ref[j], qt_ref[j], kt_ref[j]

    def diag_tile(qi):
        return (qi + 1) * KT_PER_QT - 1

    def per_head_rows(ref, hp, t):
        x = ref[hp, 0, pl.ds(pl.multiple_of(t * TK, TK), TK), :]
        return jnp.concatenate([jnp.where(lane_head == h, x, jnp.zeros_like(x)) for h in heads], axis=0)

    def stage_c2(hp, qi, t):
        rows = pl.ds(pl.multiple_of(qi * TQ, TQ), TQ)
        acc = jnp.where(t == diag_tile(qi), 0.0, o_ref[hp, 0, rows, :])
        w = jnp.concatenate([w_ref[blk] for blk in range(N_BLK)], axis=1)
        o_ref[hp, 0, rows, :] = acc + _dot(w, per_head_rows(v_ref, hp, t))

    def stage_b2c1(hp, qi, t, slot):
        for h in heads:
            first = h * BLK_PER_TILE
            parts = [sp_ref[blk] for blk in range(first, first + BLK_PER_TILE)]
            r = _dot(jnp.concatenate(parts, axis=1), tri_ref[...])
            c = jnp.where(t == diag_tile(qi), 0.0, c_ref[h])
            for b in range(BLK_PER_TILE):
                s = z_ref[slot, first + b] + r[:, b * LANES:(b + 1) * LANES] + c
                w_ref[first + b] = jnp.exp(s).astype(bf16)
            c_ref[h] = c + jnp.broadcast_to(r[:, 0:1], (TQ, LANES))

    def stage_ab1(hp, qi, t, slot):
        q = q_ref[hp, 0, pl.ds(pl.multiple_of(qi * TQ, TQ), TQ), :]
        z = lax.dot_general(q, per_head_rows(k_ref, hp, t), (((1,), (1,)), ((), ())),
                            preferred_element_type=f32)
        kind = jnp.maximum(t - qi * KT_PER_QT + 1, 0)
        for b in range(BLK_PER_TILE):
            pen = pen_ref[kind, b]
            for h in heads:
                blk = h * BLK_PER_TILE + b
                zm = z[:, blk * LANES:(blk + 1) * LANES] + pen
                z_ref[slot, blk] = zm
                sp = jnp.log(1.0 + jnp.exp2(jnp.abs(zm) * (-LOG2E))) + jnp.maximum(zm, 0.0)
                sp_ref[blk] = sp.astype(bf16)

    def iteration(i, u):
        stage_c2(*item(i - PIPE_DEPTH))
        stage_b2c1(*item(i - 1), (u - 1) % Z_SLOTS)
        stage_ab1(*item(i), u % Z_SLOTS)

    def body(m, carry):
        for u in range(ITER_PER_TRIP):
            iteration(ITER_PER_TRIP * m + u, u)
        return carry

    n_iter = n_items + PIPE_DEPTH
    full_trips = n_iter // ITER_PER_TRIP
    lax.fori_loop(0, full_trips, body, 0)
    for u in range(n_iter % ITER_PER_TRIP):
        iteration(full_trips * ITER_PER_TRIP + u, u)


def _attention(q, k, v, tri):
    _, bsz, seq, _ = q.shape
    n_qt = seq // TQ
    items = [(hp, qi, t) for hp in range(N_PAIR) for qi in range(n_qt)
             for t in range((qi + 1) * KT_PER_QT - 1, -1, -1)]
    hp_tab, qt, kt = (jnp.asarray([it[n] for it in items], jnp.int32) for n in range(3))
    seq_spec = pl.BlockSpec((N_PAIR, 1, seq, LANES), lambda b, *_: (0, b, 0, 0))
    tri_spec = pl.BlockSpec((TK, TK), lambda b, *_: (0, 0), pipeline_mode=pl.Buffered(1))
    return pl.pallas_call(
        _attn_kernel,
        out_shape=jax.ShapeDtypeStruct((N_PAIR, bsz, seq, LANES), f32),
        grid_spec=pltpu.PrefetchScalarGridSpec(
            num_scalar_prefetch=3,
            grid=(bsz,),
            in_specs=[seq_spec, seq_spec, seq_spec, tri_spec],
            out_specs=seq_spec,
            scratch_shapes=[
                pltpu.VMEM((HEADS_PER_STEP, TQ, LANES), f32),
                pltpu.VMEM((Z_SLOTS, N_BLK, TQ, LANES), f32),
                pltpu.VMEM((N_BLK, TQ, LANES), bf16),
                pltpu.VMEM((N_BLK, TQ, LANES), bf16),
                pltpu.VMEM((KT_PER_QT + 1, BLK_PER_TILE, TQ, LANES), f32),
            ]),
        compiler_params=pltpu.CompilerParams(
            dimension_semantics=("parallel",), vmem_limit_bytes=VMEM_LIMIT),
        name="stickbreak",
    )(hp_tab, qt, kt, q, k, v, tri)


def _ffn_out_kernel(x_ref, yr_ref, ya_ref, ga_ref, wo_hbm, g2_ref, wg_hbm, wu_hbm, wd_hbm,
                    o_ref, h_ref, wg_ref, wu_ref, wd_ref, stage_ref, sem_ref, wo_ref):
    @pl.when(pl.program_id(0) == 0)
    def _():
        for src, dst in ((wo_hbm, wo_ref), (wg_hbm, wg_ref), (wu_hbm, wu_ref), (wd_hbm, wd_ref)):
            _load_as_bf16(src, dst, stage_ref, sem_ref)

    ya = jnp.concatenate([ya_ref[p] for p in range(N_PAIR)], axis=1)
    ya = _rms(ya, ga_ref[...]).astype(bf16)
    x2 = (x_ref[...] + _dot(yr_ref[...], wo_ref[0:D_RNN, :])
          + _dot(ya, wo_ref[D_RNN:D_RNN + D_ATT, :]))
    o_ref[...] = _ffn_tile(x2, g2_ref, wg_ref, wu_ref, wd_ref, h_ref)


def _ffn_out(x2d, yr, ya, ga, w_out, g2, wg, wu, wd):
    n_tok = x2d.shape[0]
    return pl.pallas_call(
        _ffn_out_kernel,
        out_shape=jax.ShapeDtypeStruct((n_tok, D_MODEL), f32),
        grid=(n_tok // TM,),
        in_specs=[_tok_spec(D_MODEL), _tok_spec(D_RNN), _pair_spec(), _const_spec((1, D_ATT)),
                  _hbm_spec()] + _ffn_specs(),
        out_specs=_tok_spec(D_MODEL),
        scratch_shapes=_ffn_scratch() + [pltpu.VMEM((D_RNN + D_ATT, D_MODEL), bf16)],
        compiler_params=pltpu.CompilerParams(
            dimension_semantics=("arbitrary",), vmem_limit_bytes=VMEM_LIMIT),
        name="ffn_out",
    )(x2d, yr, ya, ga, w_out, g2, wg, wu, wd)


def _block_diag(w):
    eye = jnp.eye(RNN_BLOCKS, dtype=w.dtype)
    return jnp.einsum("ncd,nm->ncmd", w, eye).reshape(D_RNN, D_RNN)


def _scan_matrix():
    j = jnp.arange(TK)
    return -(j[:, None] >= j[None, :]).astype(bf16)


def kernel(x, ffn1_norm, ffn1_w_gate, ffn1_w_up, ffn1_w_down, mix_norm, w_in, conv_w, conv_b,
           rg_w_a, rg_b_a, rg_w_x, rg_b_x, rg_lambda, q_norm, k_norm, rnn_out_norm,
           attn_out_norm, w_out, ffn2_norm, ffn2_w_gate, ffn2_w_up, ffn2_w_down):
    bsz, seq, _ = x.shape
    depth = ffn1_norm.shape[0]
    row_head = (jnp.arange(2 * LANES) % LANES) // HEAD_DIM
    seg = (row_head[:, None] == (jnp.arange(LANES) // HEAD_DIM)[None, :]).astype(bf16)
    tri = _scan_matrix()
    x2d = x.reshape(bsz * seq, D_MODEL)
    for l in range(depth):
        x2d, y_rnn, q, k, v = _ffn_in(
            x2d, seq, ffn1_norm[l][None], ffn1_w_gate[l], ffn1_w_up[l], ffn1_w_down[l],
            mix_norm[l][None], w_in[l],
            jnp.tile(q_norm[l], N_HEADS)[None], jnp.tile(k_norm[l], N_HEADS)[None], seg,
            conv_w[l], conv_b[l][None], _block_diag(rg_w_a[l]).astype(bf16), rg_b_a[l][None],
            _block_diag(rg_w_x[l]).astype(bf16), rg_b_x[l][None], rg_lambda[l][None],
            rnn_out_norm[l][None])
        pairs = (N_PAIR, bsz, seq, LANES)
        y_att = _attention(q.reshape(pairs), k.reshape(pairs), v.reshape(pairs), tri)
        x2d = _ffn_out(x2d, y_rnn, y_att.reshape(N_PAIR, bsz * seq, LANES),
                       attn_out_norm[l][None], w_out[l], ffn2_norm[l][None],
                       ffn2_w_gate[l], ffn2_w_up[l], ffn2_w_down[l])
    return x2d.reshape(bsz, seq, D_MODEL)
```

```python
import functools
import math

import jax
import jax.numpy as jnp
from jax import lax
from jax.experimental import pallas as pl
from jax.experimental.pallas import tpu as pltpu

D_MODEL = 1024
D_RNN = 512
RNN_BLOCKS = 8
RNN_BW = D_RNN // RNN_BLOCKS
CONV_W = 4
RG_C = 8.0
D_ATT = 512
HEAD_DIM = 64
N_HEADS = D_ATT // HEAD_DIM
D_FF = 2816
N_IN = 2 * D_RNN + 3 * D_ATT
EPS = 1e-6

LANES = 128
SUBLANES = 8
HEADS_PER_STEP = LANES // HEAD_DIM
N_PAIR = D_ATT // LANES
VMEM_LIMIT = 56 * 1024 * 1024

TM = 512
FF_CHUNK = 256
TS = TM
TQ = 512
TK = 256

f32 = jnp.float32
bf16 = jnp.bfloat16


def _rms(xf, g):
    r = lax.rsqrt(jnp.mean(xf * xf, axis=-1, keepdims=True) + EPS)
    return xf * r * g


def _dot(a, b):
    return jnp.dot(a, b, preferred_element_type=f32)


def _split_bf16(x):
    hi = x.astype(bf16)
    lo = (x - hi.astype(f32)).astype(bf16)
    return hi, lo


def _const_spec(shape):
    nd = len(shape)
    return pl.BlockSpec(shape, lambda *_: (0,) * nd, pipeline_mode=pl.Buffered(1))


def _ffn_tile(x, g_ref, wg_ref, wu_ref, wd_ref, h_ref, between=()):
    xn = _rms(x, g_ref[...]).astype(bf16)
    between = list(between)
    for c in range(0, D_FF, FF_CHUNK):
        gate = _dot(xn, wg_ref[:, c:c + FF_CHUNK])
        up = _dot(xn, wu_ref[:, c:c + FF_CHUNK])
        h_ref[:, c:c + FF_CHUNK] = (jax.nn.silu(gate) * up).astype(bf16)
        if between and (c // FF_CHUNK) % 2 == 0:
            between.pop(0)()
    return x + 0.5 * _dot(h_ref[...], wd_ref[...])


CAST_ROWS = 128
N_STAGE = 6


def _load_as_bf16(src_hbm, dst_ref, stage_ref, sem_ref):
    rows, cols = dst_ref.shape
    n_chunks = rows // CAST_ROWS
    assert n_chunks * CAST_ROWS == rows and cols <= stage_ref.shape[2]

    def chunk_copy(c, slot):
        return pltpu.make_async_copy(src_hbm.at[pl.ds(c * CAST_ROWS, CAST_ROWS), :],
                                     stage_ref.at[slot, :, pl.ds(0, cols)], sem_ref.at[slot])

    for c in range(min(N_STAGE - 1, n_chunks)):
        chunk_copy(c, c).start()

    def body(c, carry):
        slot = c % N_STAGE
        ahead = c + N_STAGE - 1

        @pl.when(ahead < n_chunks)
        def _():
            chunk_copy(ahead, ahead % N_STAGE).start()

        chunk_copy(c, slot).wait()
        dst_rows = pl.ds(pl.multiple_of(c * CAST_ROWS, CAST_ROWS), CAST_ROWS)
        dst_ref[dst_rows, :] = stage_ref[slot, :, pl.ds(0, cols)].astype(bf16)
        return carry

    lax.fori_loop(0, n_chunks, body, 0)


def _hbm_spec():
    return pl.BlockSpec(memory_space=pl.ANY)


def _ffn_specs():
    return [_const_spec((1, D_MODEL)), _hbm_spec(), _hbm_spec(), _hbm_spec()]


def _ffn_scratch():
    return [pltpu.VMEM((TM, D_FF), bf16), pltpu.VMEM((D_MODEL, D_FF), bf16),
            pltpu.VMEM((D_MODEL, D_FF), bf16), pltpu.VMEM((D_FF, D_MODEL), bf16),
            pltpu.VMEM((N_STAGE, CAST_ROWS, D_FF), f32), pltpu.SemaphoreType.DMA((N_STAGE,))]


def _tok_spec(d):
    return pl.BlockSpec((TM, d), lambda i: (i, 0))


def _pair_spec():
    return pl.BlockSpec((N_PAIR, TM, LANES), lambda i: (0, i, 0))


def _head_norm(t, gain_tiled, seg_ref):
    hi, lo = _split_bf16(t * t)
    groups = []
    for g in range(0, t.shape[1], LANES):
        both = jnp.concatenate([hi[:, g:g + LANES], lo[:, g:g + LANES]], axis=1)
        groups.append(_dot(both, seg_ref[...]))
    ss = jnp.concatenate(groups, axis=1)
    r = lax.rsqrt(ss * (1.0 / HEAD_DIM) + EPS)
    return t * r * gain_tiled


def _ffn_in_kernel(tiles_per_seq,
                   x_ref, g1_ref, wg_hbm, wu_hbm, wd_hbm, g_ref, w_hbm, qg_ref, kg_ref, seg_ref,
                   cw_ref, cb_ref, wa_ref, ba_ref, wx_ref, bx_ref, lam_ref, gr_ref,
                   o_ref, y_ref, q_ref, k_ref, v_ref,
                   h_ref, wg_ref, wu_ref, wd_ref, stage_ref, sem_ref, w_ref,
                   xr_ref, gate_ref, xbuf_ref, hc_ref):
    step = pl.program_id(0)

    @pl.when(step == 0)
    def _():
        for src, dst in ((wg_hbm, wg_ref), (wu_hbm, wu_ref), (wd_hbm, wd_ref), (w_hbm, w_ref)):
            _load_as_bf16(src, dst, stage_ref, sem_ref)
        for ref in (xr_ref, gate_ref, xbuf_ref, hc_ref):
            ref[...] = jnp.zeros_like(ref)

    first = (step + (tiles_per_seq - 1)) % tiles_per_seq == 0
    y_groups = []

    def lane_group(g):
        cols = slice(g * LANES, (g + 1) * LANES)
        return lambda: y_groups.append(_rglru_lanes(
            xr_ref, gate_ref, cols, first, cw_ref, cb_ref, wa_ref, ba_ref, wx_ref, bx_ref,
            lam_ref, xbuf_ref, hc_ref))

    x1 = _ffn_tile(x_ref[...], g1_ref, wg_ref, wu_ref, wd_ref, h_ref,
                   between=[lane_group(g) for g in range(D_RNN // LANES)])
    y_ref[...] = _rms(jnp.concatenate(y_groups, axis=1), gr_ref[...]).astype(bf16)
    o_ref[...] = x1
    h = _rms(x1, g_ref[...]).astype(bf16)
    xr_ref[...] = _dot(h, w_ref[:, 0:D_RNN])
    gate_ref[...] = _dot(h, w_ref[:, D_RNN:2 * D_RNN])
    o = 2 * D_RNN
    q = _dot(h, w_ref[:, o:o + D_ATT])
    k = _dot(h, w_ref[:, o + D_ATT:o + 2 * D_ATT])
    v = _dot(h, w_ref[:, o + 2 * D_ATT:o + 3 * D_ATT])
    scale = 1.0 / math.sqrt(HEAD_DIM)
    qn = (_head_norm(q, qg_ref[...], seg_ref) * scale).astype(bf16)
    kn = _head_norm(k, kg_ref[...], seg_ref).astype(bf16)
    vb = v.astype(bf16)
    for p in range(N_PAIR):
        lanes = slice(p * LANES, (p + 1) * LANES)
        q_ref[p] = qn[:, lanes]
        k_ref[p] = kn[:, lanes]
        v_ref[p] = vb[:, lanes]


def _ffn_in(x2d, seq, g1, wg, wu, wd, g, w_in, qg, kg, seg, cw, cb, wa, ba, wx, bx, lam, gr):
    n_tok = x2d.shape[0]
    n_tiles = n_tok // TM
    cur = lambda i: jnp.minimum(i, n_tiles - 1)
    prev = lambda i: jnp.maximum(i - 1, 0)
    tok_cur = lambda d: pl.BlockSpec((TM, d), lambda i: (cur(i), 0))
    pair_cur = pl.BlockSpec((N_PAIR, TM, LANES), lambda i: (0, cur(i), 0))
    vec = _const_spec((1, D_RNN))
    return pl.pallas_call(
        functools.partial(_ffn_in_kernel, seq // TS),
        out_shape=(jax.ShapeDtypeStruct((n_tok, D_MODEL), f32),
                   jax.ShapeDtypeStruct((n_tok, D_RNN), bf16),
                   jax.ShapeDtypeStruct((N_PAIR, n_tok, LANES), bf16),
                   jax.ShapeDtypeStruct((N_PAIR, n_tok, LANES), bf16),
                   jax.ShapeDtypeStruct((N_PAIR, n_tok, LANES), bf16)),
        grid=(n_tiles + 1,),
        in_specs=[tok_cur(D_MODEL)] + _ffn_specs() + [
            _const_spec((1, D_MODEL)), _hbm_spec(),
            _const_spec((1, D_ATT)), _const_spec((1, D_ATT)), _const_spec((2 * LANES, LANES)),
            _const_spec((CONV_W, D_RNN)), vec, _const_spec((D_RNN, D_RNN)), vec,
            _const_spec((D_RNN, D_RNN)), vec, vec, vec],
        out_specs=(tok_cur(D_MODEL), pl.BlockSpec((TS, D_RNN), lambda i: (prev(i), 0)),
                   pair_cur, pair_cur, pair_cur),
        scratch_shapes=_ffn_scratch() + [
            pltpu.VMEM((D_MODEL, N_IN), bf16),
            pltpu.VMEM((TS, D_RNN), f32), pltpu.VMEM((TS, D_RNN), f32),
            pltpu.VMEM((HIST + TS, D_RNN), f32), pltpu.VMEM((1, D_RNN), f32)],
        compiler_params=pltpu.CompilerParams(
            dimension_semantics=("arbitrary",), vmem_limit_bytes=VMEM_LIMIT),
        name="ffn_in",
    )(x2d, g1, wg, wu, wd, g, w_in, qg, kg, seg, cw, cb, wa, ba, wx, bx, lam, gr)


HIST = 8


def _rglru_lanes(xr_ref, gate_ref, cols, first, cw_ref, cb_ref, wa_ref, ba_ref, wx_ref, bx_ref,
                 lam_ref, xbuf_ref, hc_ref):
    x = xr_ref[:, cols]
    xbuf_ref[0:HIST, cols] = jnp.where(first, 0.0, xbuf_ref[0:HIST, cols])
    xbuf_ref[HIST:HIST + TS, cols] = x
    xc = cb_ref[:, cols] + cw_ref[CONV_W - 1:CONV_W, cols] * x
    for j in range(CONV_W - 1):
        off = HIST - (CONV_W - 1) + j
        xc = xc + cw_ref[j:j + 1, cols] * xbuf_ref[off:off + TS, cols]
    xbuf_ref[0:HIST, cols] = x[TS - HIST:TS, :]

    xcb = xc.astype(bf16)
    r = jax.nn.sigmoid(_dot(xcb, wa_ref[cols, cols]) + ba_ref[:, cols])
    i = jax.nn.sigmoid(_dot(xcb, wx_ref[cols, cols]) + bx_ref[:, cols])
    lam = lam_ref[:, cols]
    log_sig_lam = jnp.minimum(lam, 0.0) - jnp.log1p(jnp.exp(-jnp.abs(lam)))
    log_a = RG_C * r * log_sig_lam
    a = jnp.exp(log_a)
    th = jnp.tanh(log_a)
    mult = jnp.sqrt(-2.0 * th / (1.0 - th))
    b = mult * (i * xc)

    n_grp = TS // SUBLANES
    a = a.reshape(n_grp, SUBLANES, LANES)
    b = b.reshape(n_grp, SUBLANES, LANES)
    row = lax.broadcasted_iota(jnp.int32, (n_grp, SUBLANES, LANES), 1)
    d = 1
    while d < SUBLANES:
        a_sh = pltpu.roll(a, d, axis=1)
        b_sh = pltpu.roll(b, d, axis=1)
        m = row >= d
        b = jnp.where(m, a * b_sh, 0.0) + b
        a = jnp.where(m, a * a_sh, a)
        d *= 2
    carry = jnp.broadcast_to(jnp.where(first, 0.0, hc_ref[:, cols]), (SUBLANES, LANES))
    groups = []
    for g in range(n_grp):
        h_g = a[g] * carry + b[g]
        groups.append(h_g)
        carry = jnp.broadcast_to(h_g[SUBLANES - 1:SUBLANES, :], (SUBLANES, LANES))
    h = jnp.concatenate(groups, axis=0)
    hc_ref[:, cols] = groups[-1][SUBLANES - 1:SUBLANES, :]
    return h * jax.nn.gelu(gate_ref[:, cols])


PIPE_DEPTH = 2
BLK_PER_TILE = TK // LANES
KT_PER_QT = TQ // TK
N_BLK = HEADS_PER_STEP * BLK_PER_TILE
ITER_PER_TRIP = 16
Z_SLOTS = 2
LOG2E = 1.4426950408889634


MASKED = -1e30


def _attn_kernel(hp_ref, qt_ref, kt_ref, q_ref, k_ref, v_ref, tri_ref, o_ref,
                 c_ref, z_ref, sp_ref, w_ref, pen_ref):
    n_items = qt_ref.shape[0]
    heads = range(HEADS_PER_STEP)
    for ref in (o_ref, c_ref, z_ref, sp_ref, w_ref):
        ref[...] = jnp.zeros_like(ref)
    pen_ref[0] = jnp.zeros_like(pen_ref[0])
    for j in range(KT_PER_QT):
        for b in range(BLK_PER_TILE):
            key = lax.broadcasted_iota(jnp.int32, (TQ, LANES), 1) + (j * TK + b * LANES)
            query = lax.broadcasted_iota(jnp.int32, (TQ, LANES), 0)
            pen_ref[j + 1, b] = jnp.where(key < query, 0.0, MASKED)
    lane_head = lax.broadcasted_iota(jnp.int32, (TK, LANES), 1) // HEAD_DIM

    def item(j):
        j = jnp.clip(j, 0, n_items - 1)
        return hp_ref[j], qt_ref[j], kt_ref[j]

    def diag_tile(qi):
        return (qi + 1) * KT_PER_QT - 1

    def per_head_rows(ref, hp, t):
        x = ref[hp, 0, pl.ds(pl.multiple_of(t * TK, TK), TK), :]
        return jnp.concatenate([jnp.where(lane_head == h, x, jnp.zeros_like(x)) for h in heads], axis=0)

    def stage_c2(hp, qi, t):
        rows = pl.ds(pl.multiple_of(qi * TQ, TQ), TQ)
        acc = jnp.where(t == diag_tile(qi), 0.0, o_ref[hp, 0, rows, :])
        w = jnp.concatenate([w_ref[blk] for blk in range(N_BLK)], axis=1)
        o_ref[hp, 0, rows, :] = acc + _dot(w, per_head_rows(v_ref, hp, t))

    def stage_b2c1(hp, qi, t, slot):
        for h in heads:
            first = h * BLK_PER_TILE
            parts = [sp_ref[blk] for blk in range(first, first + BLK_PER_TILE)]
            r = _dot(jnp.concatenate(parts, axis=1), tri_ref[...])
            c = jnp.where(t == diag_tile(qi), 0.0, c_ref[h])
            for b in range(BLK_PER_TILE):
                s = z_ref[slot, first + b] + r[:, b * LANES:(b + 1) * LANES] + c
                w_ref[first + b] = jnp.exp(s).astype(bf16)
            c_ref[h] = c + jnp.broadcast_to(r[:, 0:1], (TQ, LANES))

    def stage_ab1(hp, qi, t, slot):
        q = q_ref[hp, 0, pl.ds(pl.multiple_of(qi * TQ, TQ), TQ), :]
        z = lax.dot_general(q, per_head_rows(k_ref, hp, t), (((1,), (1,)), ((), ())),
                            preferred_element_type=f32)
        kind = jnp.maximum(t - qi * KT_PER_QT + 1, 0)
        for b in range(BLK_PER_TILE):
            pen = pen_ref[kind, b]
            for h in heads:
                blk = h * BLK_PER_TILE + b
                zm = z[:, blk * LANES:(blk + 1) * LANES] + pen
                z_ref[slot, blk] = zm
                sp = jnp.log(1.0 + jnp.exp2(jnp.abs(zm) * (-LOG2E))) + jnp.maximum(zm, 0.0)
                sp_ref[blk] = sp.astype(bf16)

    def iteration(i, u):
        stage_c2(*item(i - PIPE_DEPTH))
        stage_b2c1(*item(i - 1), (u - 1) % Z_SLOTS)
        stage_ab1(*item(i), u % Z_SLOTS)

    def body(m, carry):
        for u in range(ITER_PER_TRIP):
            iteration(ITER_PER_TRIP * m + u, u)
        return carry

    n_iter = n_items + PIPE_DEPTH
    full_trips = n_iter // ITER_PER_TRIP
    lax.fori_loop(0, full_trips, body, 0)
    for u in range(n_iter % ITER_PER_TRIP):
        iteration(full_trips * ITER_PER_TRIP + u, u)


def _attention(q, k, v, tri):
    _, bsz, seq, _ = q.shape
    n_qt = seq // TQ
    items = [(hp, qi, t) for hp in range(N_PAIR) for qi in range(n_qt)
             for t in range((qi + 1) * KT_PER_QT - 1, -1, -1)]
    hp_tab, qt, kt = (jnp.asarray([it[n] for it in items], jnp.int32) for n in range(3))
    seq_spec = pl.BlockSpec((N_PAIR, 1, seq, LANES), lambda b, *_: (0, b, 0, 0))
    tri_spec = pl.BlockSpec((TK, TK), lambda b, *_: (0, 0), pipeline_mode=pl.Buffered(1))
    return pl.pallas_call(
        _attn_kernel,
        out_shape=jax.ShapeDtypeStruct((N_PAIR, bsz, seq, LANES), f32),
        grid_spec=pltpu.PrefetchScalarGridSpec(
            num_scalar_prefetch=3,
            grid=(bsz,),
            in_specs=[seq_spec, seq_spec, seq_spec, tri_spec],
            out_specs=seq_spec,
            scratch_shapes=[
                pltpu.VMEM((HEADS_PER_STEP, TQ, LANES), f32),
                pltpu.VMEM((Z_SLOTS, N_BLK, TQ, LANES), f32),
                pltpu.VMEM((N_BLK, TQ, LANES), bf16),
                pltpu.VMEM((N_BLK, TQ, LANES), bf16),
                pltpu.VMEM((KT_PER_QT + 1, BLK_PER_TILE, TQ, LANES), f32),
            ]),
        compiler_params=pltpu.CompilerParams(
            dimension_semantics=("parallel",), vmem_limit_bytes=VMEM_LIMIT),
        name="stickbreak",
    )(hp_tab, qt, kt, q, k, v, tri)


def _ffn_out_kernel(x_ref, yr_ref, ya_ref, ga_ref, wo_hbm, g2_ref, wg_hbm, wu_hbm, wd_hbm,
                    o_ref, h_ref, wg_ref, wu_ref, wd_ref, stage_ref, sem_ref, wo_ref):
    @pl.when(pl.program_id(0) == 0)
    def _():
        for src, dst in ((wo_hbm, wo_ref), (wg_hbm, wg_ref), (wu_hbm, wu_ref), (wd_hbm, wd_ref)):
            _load_as_bf16(src, dst, stage_ref, sem_ref)

    ya = jnp.concatenate([ya_ref[p] for p in range(N_PAIR)], axis=1)
    ya = _rms(ya, ga_ref[...]).astype(bf16)
    x2 = (x_ref[...] + _dot(yr_ref[...], wo_ref[0:D_RNN, :])
          + _dot(ya, wo_ref[D_RNN:D_RNN + D_ATT, :]))
    o_ref[...] = _ffn_tile(x2, g2_ref, wg_ref, wu_ref, wd_ref, h_ref)


def _ffn_out(x2d, yr, ya, ga, w_out, g2, wg, wu, wd):
    n_tok = x2d.shape[0]
    return pl.pallas_call(
        _ffn_out_kernel,
        out_shape=jax.ShapeDtypeStruct((n_tok, D_MODEL), f32),
        grid=(n_tok // TM,),
        in_specs=[_tok_spec(D_MODEL), _tok_spec(D_RNN), _pair_spec(), _const_spec((1, D_ATT)),
                  _hbm_spec()] + _ffn_specs(),
        out_specs=_tok_spec(D_MODEL),
        scratch_shapes=_ffn_scratch() + [pltpu.VMEM((D_RNN + D_ATT, D_MODEL), bf16)],
        compiler_params=pltpu.CompilerParams(
            dimension_semantics=("arbitrary",), vmem_limit_bytes=VMEM_LIMIT),
        name="ffn_out",
    )(x2d, yr, ya, ga, w_out, g2, wg, wu, wd)


def _block_diag(w):
    eye = jnp.eye(RNN_BLOCKS, dtype=w.dtype)
    return jnp.einsum("ncd,nm->ncmd", w, eye).reshape(D_RNN, D_RNN)


def _scan_matrix():
    j = jnp.arange(TK)
    return -(j[:, None] >= j[None, :]).astype(bf16)


def kernel(x, ffn1_norm, ffn1_w_gate, ffn1_w_up, ffn1_w_down, mix_norm, w_in, conv_w, conv_b,
           rg_w_a, rg_b_a, rg_w_x, rg_b_x, rg_lambda, q_norm, k_norm, rnn_out_norm,
           attn_out_norm, w_out, ffn2_norm, ffn2_w_gate, ffn2_w_up, ffn2_w_down):
    bsz, seq, _ = x.shape
    depth = ffn1_norm.shape[0]
    row_head = (jnp.arange(2 * LANES) % LANES) // HEAD_DIM
    seg = (row_head[:, None] == (jnp.arange(LANES) // HEAD_DIM)[None, :]).astype(bf16)
    tri = _scan_matrix()
    x2d = x.reshape(bsz * seq, D_MODEL)
    for l in range(depth):
        x2d, y_rnn, q, k, v = _ffn_in(
            x2d, seq, ffn1_norm[l][None], ffn1_w_gate[l], ffn1_w_up[l], ffn1_w_down[l],
            mix_norm[l][None], w_in[l],
            jnp.tile(q_norm[l], N_HEADS)[None], jnp.tile(k_norm[l], N_HEADS)[None], seg,
            conv_w[l], conv_b[l][None], _block_diag(rg_w_a[l]).astype(bf16), rg_b_a[l][None],
            _block_diag(rg_w_x[l]).astype(bf16), rg_b_x[l][None], rg_lambda[l][None],
            rnn_out_norm[l][None])
        pairs = (N_PAIR, bsz, seq, LANES)
        y_att = _attention(q.reshape(pairs), k.reshape(pairs), v.reshape(pairs), tri)
        x2d = _ffn_out(x2d, y_rnn, y_att.reshape(N_PAIR, bsz * seq, LANES),
                       attn_out_norm[l][None], w_out[l], ffn2_norm[l][None],
                       ffn2_w_gate[l], ffn2_w_up[l], ffn2_w_down[l])
    return x2d.reshape(bsz, seq, D_MODEL)
```

```python
import functools
import math

import jax
import jax.numpy as jnp
from jax import lax
from jax.experimental import pallas as pl
from jax.experimental.pallas import tpu as pltpu

D_MODEL = 1024
D_RNN = 512
RNN_BLOCKS = 8
RNN_BW = D_RNN // RNN_BLOCKS
CONV_W = 4
RG_C = 8.0
D_ATT = 512
HEAD_DIM = 64
N_HEADS = D_ATT // HEAD_DIM
D_FF = 2816
N_IN = 2 * D_RNN + 3 * D_ATT
EPS = 1e-6

LANES = 128
SUBLANES = 8
HEADS_PER_STEP = LANES // HEAD_DIM
N_PAIR = D_ATT // LANES
VMEM_LIMIT = 56 * 1024 * 1024

TM = 512
FF_CHUNK = 256
TS = TM
TQ = 512
TK = 256

f32 = jnp.float32
bf16 = jnp.bfloat16


def _rms(xf, g):
    r = lax.rsqrt(jnp.mean(xf * xf, axis=-1, keepdims=True) + EPS)
    return xf * r * g


def _dot(a, b):
    return jnp.dot(a, b, preferred_element_type=f32)


def _split_bf16(x):
    hi = x.astype(bf16)
    lo = (x - hi.astype(f32)).astype(bf16)
    return hi, lo


def _const_spec(shape):
    nd = len(shape)
    return pl.BlockSpec(shape, lambda *_: (0,) * nd, pipeline_mode=pl.Buffered(1))


def _ffn_tile(x, g_ref, wg_ref, wu_ref, wd_ref, h_ref, between=()):
    xn = _rms(x, g_ref[...]).astype(bf16)
    between = list(between)
    for c in range(0, D_FF, FF_CHUNK):
        gate = _dot(xn, wg_ref[:, c:c + FF_CHUNK])
        up = _dot(xn, wu_ref[:, c:c + FF_CHUNK])
        h_ref[:, c:c + FF_CHUNK] = (jax.nn.silu(gate) * up).astype(bf16)
        if between and (c // FF_CHUNK) % 2 == 0:
            between.pop(0)()
    return x + 0.5 * _dot(h_ref[...], wd_ref[...])


CAST_ROWS = 128
N_STAGE = 6


def _load_as_bf16(src_hbm, dst_ref, stage_ref, sem_ref):
    rows, cols = dst_ref.shape
    n_chunks = rows // CAST_ROWS
    assert n_chunks * CAST_ROWS == rows and cols <= stage_ref.shape[2]

    def chunk_copy(c, slot):
        return pltpu.make_async_copy(src_hbm.at[pl.ds(c * CAST_ROWS, CAST_ROWS), :],
                                     stage_ref.at[slot, :, pl.ds(0, cols)], sem_ref.at[slot])

    for c in range(min(N_STAGE - 1, n_chunks)):
        chunk_copy(c, c).start()

    def body(c, carry):
        slot = c % N_STAGE
        ahead = c + N_STAGE - 1

        @pl.when(ahead < n_chunks)
        def _():
            chunk_copy(ahead, ahead % N_STAGE).start()

        chunk_copy(c, slot).wait()
        dst_rows = pl.ds(pl.multiple_of(c * CAST_ROWS, CAST_ROWS), CAST_ROWS)
        dst_ref[dst_rows, :] = stage_ref[slot, :, pl.ds(0, cols)].astype(bf16)
        return carry

    lax.fori_loop(0, n_chunks, body, 0)


def _hbm_spec():
    return pl.BlockSpec(memory_space=pl.ANY)


def _ffn_specs():
    return [_const_spec((1, D_MODEL)), _hbm_spec(), _hbm_spec(), _hbm_spec()]


def _ffn_scratch():
    return [pltpu.VMEM((TM, D_FF), bf16), pltpu.VMEM((D_MODEL, D_FF), bf16),
            pltpu.VMEM((D_MODEL, D_FF), bf16), pltpu.VMEM((D_FF, D_MODEL), bf16),
            pltpu.VMEM((N_STAGE, CAST_ROWS, D_FF), f32), pltpu.SemaphoreType.DMA((N_STAGE,))]


def _tok_spec(d):
    return pl.BlockSpec((TM, d), lambda i: (i, 0))


def _pair_spec():
    return pl.BlockSpec((N_PAIR, TM, LANES), lambda i: (0, i, 0))


def _head_norm(t, gain_tiled, seg_ref):
    hi, lo = _split_bf16(t * t)
    groups = []
    for g in range(0, t.shape[1], LANES):
        both = jnp.concatenate([hi[:, g:g + LANES], lo[:, g:g + LANES]], axis=1)
        groups.append(_dot(both, seg_ref[...]))
    ss = jnp.concatenate(groups, axis=1)
    r = lax.rsqrt(ss * (1.0 / HEAD_DIM) + EPS)
    return t * r * gain_tiled


def _ffn_in_kernel(tiles_per_seq,
                   x_ref, g1_ref, wg_hbm, wu_hbm, wd_hbm, g_ref, w_hbm, qg_ref, kg_ref, seg_ref,
                   cw_ref, cb_ref, wa_ref, ba_ref, wx_ref, bx_ref, lam_ref, gr_ref,
                   o_ref, y_ref, q_ref, k_ref, v_ref,
                   h_ref, wg_ref, wu_ref, wd_ref, stage_ref, sem_ref, w_ref,
                   xr_ref, gate_ref, xbuf_ref, hc_ref):
    step = pl.program_id(0)

    @pl.when(step == 0)
    def _():
        for src, dst in ((wg_hbm, wg_ref), (wu_hbm, wu_ref), (wd_hbm, wd_ref), (w_hbm, w_ref)):
            _load_as_bf16(src, dst, stage_ref, sem_ref)
        for ref in (xr_ref, gate_ref, xbuf_ref, hc_ref):
            ref[...] = jnp.zeros_like(ref)

    first = (step + (tiles_per_seq - 1)) % tiles_per_seq == 0
    y_groups = []

    def lane_group(g):
        cols = slice(g * LANES, (g + 1) * LANES)
        return lambda: y_groups.append(_rglru_lanes(
            xr_ref, gate_ref, cols, first, cw_ref, cb_ref, wa_ref, ba_ref, wx_ref, bx_ref,
            lam_ref, xbuf_ref, hc_ref))

    x1 = _ffn_tile(x_ref[...], g1_ref, wg_ref, wu_ref, wd_ref, h_ref,
                   between=[lane_group(g) for g in range(D_RNN // LANES)])
    y_ref[...] = _rms(jnp.concatenate(y_groups, axis=1), gr_ref[...]).astype(bf16)
    o_ref[...] = x1
    h = _rms(x1, g_ref[...]).astype(bf16)
    xr_ref[...] = _dot(h, w_ref[:, 0:D_RNN])
    gate_ref[...] = _dot(h, w_ref[:, D_RNN:2 * D_RNN])
    o = 2 * D_RNN
    q = _dot(h, w_ref[:, o:o + D_ATT])
    k = _dot(h, w_ref[:, o + D_ATT:o + 2 * D_ATT])
    v = _dot(h, w_ref[:, o + 2 * D_ATT:o + 3 * D_ATT])
    scale = 1.0 / math.sqrt(HEAD_DIM)
    qn = (_head_norm(q, qg_ref[...], seg_ref) * scale).astype(bf16)
    kn = _head_norm(k, kg_ref[...], seg_ref).astype(bf16)
    vb = v.astype(bf16)
    for p in range(N_PAIR):
        lanes = slice(p * LANES, (p + 1) * LANES)
        q_ref[p] = qn[:, lanes]
        k_ref[p] = kn[:, lanes]
        v_ref[p] = vb[:, lanes]


def _ffn_in(x2d, seq, g1, wg, wu, wd, g, w_in, qg, kg, seg, cw, cb, wa, ba, wx, bx, lam, gr):
    n_tok = x2d.shape[0]
    n_tiles = n_tok // TM
    cur = lambda i: jnp.minimum(i, n_tiles - 1)
    prev = lambda i: jnp.maximum(i - 1, 0)
    tok_cur = lambda d: pl.BlockSpec((TM, d), lambda i: (cur(i), 0))
    pair_cur = pl.BlockSpec((N_PAIR, TM, LANES), lambda i: (0, cur(i), 0))
    vec = _const_spec((1, D_RNN))
    return pl.pallas_call(
        functools.partial(_ffn_in_kernel, seq // TS),
        out_shape=(jax.ShapeDtypeStruct((n_tok, D_MODEL), f32),
                   jax.ShapeDtypeStruct((n_tok, D_RNN), bf16),
                   jax.ShapeDtypeStruct((N_PAIR, n_tok, LANES), bf16),
                   jax.ShapeDtypeStruct((N_PAIR, n_tok, LANES), bf16),
                   jax.ShapeDtypeStruct((N_PAIR, n_tok, LANES), bf16)),
        grid=(n_tiles + 1,),
        in_specs=[tok_cur(D_MODEL)] + _ffn_specs() + [
            _const_spec((1, D_MODEL)), _hbm_spec(),
            _const_spec((1, D_ATT)), _const_spec((1, D_ATT)), _const_spec((2 * LANES, LANES)),
            _const_spec((CONV_W, D_RNN)), vec, _const_spec((D_RNN, D_RNN)), vec,
            _const_spec((D_RNN, D_RNN)), vec, vec, vec],
        out_specs=(tok_cur(D_MODEL), pl.BlockSpec((TS, D_RNN), lambda i: (prev(i), 0)),
                   pair_cur, pair_cur, pair_cur),
        scratch_shapes=_ffn_scratch() + [
            pltpu.VMEM((D_MODEL, N_IN), bf16),
            pltpu.VMEM((TS, D_RNN), f32), pltpu.VMEM((TS, D_RNN), f32),
            pltpu.VMEM((HIST + TS, D_RNN), f32), pltpu.VMEM((1, D_RNN), f32)],
        compiler_params=pltpu.CompilerParams(
            dimension_semantics=("arbitrary",), vmem_limit_bytes=VMEM_LIMIT),
        name="ffn_in",
    )(x2d, g1, wg, wu, wd, g, w_in, qg, kg, seg, cw, cb, wa, ba, wx, bx, lam, gr)


HIST = 8


def _rglru_lanes(xr_ref, gate_ref, cols, first, cw_ref, cb_ref, wa_ref, ba_ref, wx_ref, bx_ref,
                 lam_ref, xbuf_ref, hc_ref):
    x = xr_ref[:, cols]
    xbuf_ref[0:HIST, cols] = jnp.where(first, 0.0, xbuf_ref[0:HIST, cols])
    xbuf_ref[HIST:HIST + TS, cols] = x
    xc = cb_ref[:, cols] + cw_ref[CONV_W - 1:CONV_W, cols] * x
    for j in range(CONV_W - 1):
        off = HIST - (CONV_W - 1) + j
        xc = xc + cw_ref[j:j + 1, cols] * xbuf_ref[off:off + TS, cols]
    xbuf_ref[0:HIST, cols] = x[TS - HIST:TS, :]

    xcb = xc.astype(bf16)
    r = jax.nn.sigmoid(_dot(xcb, wa_ref[cols, cols]) + ba_ref[:, cols])
    i = jax.nn.sigmoid(_dot(xcb, wx_ref[cols, cols]) + bx_ref[:, cols])
    lam = lam_ref[:, cols]
    log_sig_lam = jnp.minimum(lam, 0.0) - jnp.log1p(jnp.exp(-jnp.abs(lam)))
    log_a = RG_C * r * log_sig_lam
    a = jnp.exp(log_a)
    th = jnp.tanh(log_a)
    mult = jnp.sqrt(-2.0 * th / (1.0 - th))
    b = mult * (i * xc)

    n_grp = TS // SUBLANES
    a = a.reshape(n_grp, SUBLANES, LANES)
    b = b.reshape(n_grp, SUBLANES, LANES)
    row = lax.broadcasted_iota(jnp.int32, (n_grp, SUBLANES, LANES), 1)
    d = 1
    while d < SUBLANES:
        a_sh = pltpu.roll(a, d, axis=1)
        b_sh = pltpu.roll(b, d, axis=1)
        m = row >= d
        b = jnp.where(m, a * b_sh, 0.0) + b
        a = jnp.where(m, a * a_sh, a)
        d *= 2
    carry = jnp.broadcast_to(jnp.where(first, 0.0, hc_ref[:, cols]), (SUBLANES, LANES))
    groups = []
    for g in range(n_grp):
        h_g = a[g] * carry + b[g]
        groups.append(h_g)
        carry = jnp.broadcast_to(h_g[SUBLANES - 1:SUBLANES, :], (SUBLANES, LANES))
    h = jnp.concatenate(groups, axis=0)
    hc_ref[:, cols] = groups[-1][SUBLANES - 1:SUBLANES, :]
    return h * jax.nn.gelu(gate_ref[:, cols])


PIPE_DEPTH = 2
BLK_PER_TILE = TK // LANES
KT_PER_QT = TQ // TK
N_BLK = HEADS_PER_STEP * BLK_PER_TILE
ITER_PER_TRIP = 16
Z_SLOTS = 2
LOG2E = 1.4426950408889634


MASKED = -1e30


def _attn_kernel(hp_ref, qt_ref, kt_ref, q_ref, k_ref, v_ref, tri_ref, o_ref,
                 c_ref, z_ref, sp_ref, w_ref, pen_ref):
    n_items = qt_ref.shape[0]
    heads = range(HEADS_PER_STEP)
    for ref in (o_ref, c_ref, z_ref, sp_ref, w_ref):
        ref[...] = jnp.zeros_like(ref)
    pen_ref[0] = jnp.zeros_like(pen_ref[0])
    for j in range(KT_PER_QT):
        for b in range(BLK_PER_TILE):
            key = lax.broadcasted_iota(jnp.int32, (TQ, LANES), 1) + (j * TK + b * LANES)
            query = lax.broadcasted_iota(jnp.int32, (TQ, LANES), 0)
            pen_ref[j + 1, b] = jnp.where(key < query, 0.0, MASKED)
    lane_head = lax.broadcasted_iota(jnp.int32, (TK, LANES), 1) // HEAD_DIM

    def item(j):
        j = jnp.clip(j, 0, n_items - 1)
        return hp_ref[j], qt_ref[j], kt_ref[j]

    def diag_tile(qi):
        return (qi + 1) * KT_PER_QT - 1

    def per_head_rows(ref, hp, t):
        x = ref[hp, 0, pl.ds(pl.multiple_of(t * TK, TK), TK), :]
        return jnp.concatenate([jnp.where(lane_head == h, x, jnp.zeros_like(x)) for h in heads], axis=0)

    def stage_c2(hp, qi, t):
        rows = pl.ds(pl.multiple_of(qi * TQ, TQ), TQ)
        acc = jnp.where(t == diag_tile(qi), 0.0, o_ref[hp, 0, rows, :])
        w = jnp.concatenate([w_ref[blk] for blk in range(N_BLK)], axis=1)
        o_ref[hp, 0, rows, :] = acc + _dot(w, per_head_rows(v_ref, hp, t))

    def stage_b2c1(hp, qi, t, slot):
        for h in heads:
            first = h * BLK_PER_TILE
            parts = [sp_ref[blk] for blk in range(first, first + BLK_PER_TILE)]
            r = _dot(jnp.concatenate(parts, axis=1), tri_ref[...])
            c = jnp.where(t == diag_tile(qi), 0.0, c_ref[h])
            for b in range(BLK_PER_TILE):
                s = z_ref[slot, first + b] + r[:, b * LANES:(b + 1) * LANES] + c
                w_ref[first + b] = jnp.exp(s).astype(bf16)
            c_ref[h] = c + jnp.broadcast_to(r[:, 0:1], (TQ, LANES))

    def stage_ab1(hp, qi, t, slot):
        q = q_ref[hp, 0, pl.ds(pl.multiple_of(qi * TQ, TQ), TQ), :]
        z = lax.dot_general(q, per_head_rows(k_ref, hp, t), (((1,), (1,)), ((), ())),
                            preferred_element_type=f32)
        kind = jnp.maximum(t - qi * KT_PER_QT + 1, 0)
        for blk in range(N_BLK):
            zm = z[:, blk * LANES:(blk + 1) * LANES] + pen_ref[kind, blk % BLK_PER_TILE]
            z_ref[slot, blk] = zm
            sp = jnp.log(1.0 + jnp.exp2(jnp.abs(zm) * (-LOG2E))) + jnp.maximum(zm, 0.0)
            sp_ref[blk] = sp.astype(bf16)

    def iteration(i, u):
        stage_c2(*item(i - PIPE_DEPTH))
        stage_b2c1(*item(i - 1), (u - 1) % Z_SLOTS)
        stage_ab1(*item(i), u % Z_SLOTS)

    def body(m, carry):
        for u in range(ITER_PER_TRIP):
            iteration(ITER_PER_TRIP * m + u, u)
        return carry

    n_iter = n_items + PIPE_DEPTH
    full_trips = n_iter // ITER_PER_TRIP
    lax.fori_loop(0, full_trips, body, 0)
    for u in range(n_iter % ITER_PER_TRIP):
        iteration(full_trips * ITER_PER_TRIP + u, u)


def _attention(q, k, v, tri):
    _, bsz, seq, _ = q.shape
    n_qt = seq // TQ
    items = [(hp, qi, t) for hp in range(N_PAIR) for qi in range(n_qt)
             for t in range((qi + 1) * KT_PER_QT - 1, -1, -1)]
    hp_tab, qt, kt = (jnp.asarray([it[n] for it in items], jnp.int32) for n in range(3))
    seq_spec = pl.BlockSpec((N_PAIR, 1, seq, LANES), lambda b, *_: (0, b, 0, 0))
    tri_spec = pl.BlockSpec((TK, TK), lambda b, *_: (0, 0), pipeline_mode=pl.Buffered(1))
    return pl.pallas_call(
        _attn_kernel,
        out_shape=jax.ShapeDtypeStruct((N_PAIR, bsz, seq, LANES), f32),
        grid_spec=pltpu.PrefetchScalarGridSpec(
            num_scalar_prefetch=3,
            grid=(bsz,),
            in_specs=[seq_spec, seq_spec, seq_spec, tri_spec],
            out_specs=seq_spec,
            scratch_shapes=[
                pltpu.VMEM((HEADS_PER_STEP, TQ, LANES), f32),
                pltpu.VMEM((Z_SLOTS, N_BLK, TQ, LANES), f32),
                pltpu.VMEM((N_BLK, TQ, LANES), bf16),
                pltpu.VMEM((N_BLK, TQ, LANES), bf16),
                pltpu.VMEM((KT_PER_QT + 1, BLK_PER_TILE, TQ, LANES), f32),
            ]),
        compiler_params=pltpu.CompilerParams(
            dimension_semantics=("parallel",), vmem_limit_bytes=VMEM_LIMIT),
        name="stickbreak",
    )(hp_tab, qt, kt, q, k, v, tri)


def _ffn_out_kernel(x_ref, yr_ref, ya_ref, ga_ref, wo_hbm, g2_ref, wg_hbm, wu_hbm, wd_hbm,
                    o_ref, h_ref, wg_ref, wu_ref, wd_ref, stage_ref, sem_ref, wo_ref):
    @pl.when(pl.program_id(0) == 0)
    def _():
        for src, dst in ((wo_hbm, wo_ref), (wg_hbm, wg_ref), (wu_hbm, wu_ref), (wd_hbm, wd_ref)):
            _load_as_bf16(src, dst, stage_ref, sem_ref)

    ya = jnp.concatenate([ya_ref[p] for p in range(N_PAIR)], axis=1)
    ya = _rms(ya, ga_ref[...]).astype(bf16)
    x2 = (x_ref[...] + _dot(yr_ref[...], wo_ref[0:D_RNN, :])
          + _dot(ya, wo_ref[D_RNN:D_RNN + D_ATT, :]))
    o_ref[...] = _ffn_tile(x2, g2_ref, wg_ref, wu_ref, wd_ref, h_ref)


def _ffn_out(x2d, yr, ya, ga, w_out, g2, wg, wu, wd):
    n_tok = x2d.shape[0]
    return pl.pallas_call(
        _ffn_out_kernel,
        out_shape=jax.ShapeDtypeStruct((n_tok, D_MODEL), f32),
        grid=(n_tok // TM,),
        in_specs=[_tok_spec(D_MODEL), _tok_spec(D_RNN), _pair_spec(), _const_spec((1, D_ATT)),
                  _hbm_spec()] + _ffn_specs(),
        out_specs=_tok_spec(D_MODEL),
        scratch_shapes=_ffn_scratch() + [pltpu.VMEM((D_RNN + D_ATT, D_MODEL), bf16)],
        compiler_params=pltpu.CompilerParams(
            dimension_semantics=("arbitrary",), vmem_limit_bytes=VMEM_LIMIT),
        name="ffn_out",
    )(x2d, yr, ya, ga, w_out, g2, wg, wu, wd)


def _block_diag(w):
    eye = jnp.eye(RNN_BLOCKS, dtype=w.dtype)
    return jnp.einsum("ncd,nm->ncmd", w, eye).reshape(D_RNN, D_RNN)


def _scan_matrix():
    j = jnp.arange(TK)
    return -(j[:, None] >= j[None, :]).astype(bf16)


def kernel(x, ffn1_norm, ffn1_w_gate, ffn1_w_up, ffn1_w_down, mix_norm, w_in, conv_w, conv_b,
           rg_w_a, rg_b_a, rg_w_x, rg_b_x, rg_lambda, q_norm, k_norm, rnn_out_norm,
           attn_out_norm, w_out, ffn2_norm, ffn2_w_gate, ffn2_w_up, ffn2_w_down):
    bsz, seq, _ = x.shape
    depth = ffn1_norm.shape[0]
    row_head = (jnp.arange(2 * LANES) % LANES) // HEAD_DIM
    seg = (row_head[:, None] == (jnp.arange(LANES) // HEAD_DIM)[None, :]).astype(bf16)
    tri = _scan_matrix()
    x2d = x.reshape(bsz * seq, D_MODEL)
    for l in range(depth):
        x2d, y_rnn, q, k, v = _ffn_in(
            x2d, seq, ffn1_norm[l][None], ffn1_w_gate[l], ffn1_w_up[l], ffn1_w_down[l],
            mix_norm[l][None], w_in[l],
            jnp.tile(q_norm[l], N_HEADS)[None], jnp.tile(k_norm[l], N_HEADS)[None], seg,
            conv_w[l], conv_b[l][None], _block_diag(rg_w_a[l]).astype(bf16), rg_b_a[l][None],
            _block_diag(rg_w_x[l]).astype(bf16), rg_b_x[l][None], rg_lambda[l][None],
            rnn_out_norm[l][None])
        pairs = (N_PAIR, bsz, seq, LANES)
        y_att = _attention(q.reshape(pairs), k.reshape(pairs), v.reshape(pairs), tri)
        x2d = _ffn_out(x2d, y_rnn, y_att.reshape(N_PAIR, bsz * seq, LANES),
                       attn_out_norm[l][None], w_out[l], ffn2_norm[l][None],
                       ffn2_w_gate[l], ffn2_w_up[l], ffn2_w_down[l])
    return x2d.reshape(bsz, seq, D_MODEL)
```
